```python
import math
import jax
import jax.numpy as jnp
from jax import lax
import numpy as np

D_MODEL = 1024
BATCH = 16
SEQ = 4096
DEPTH = 2

HG_HEADS = 4
HG_DK = 128
HG_DV = 128
HG_K = HG_HEADS * HG_DK
HG_V = HG_HEADS * HG_DV
HG_CHUNK = 64
NSA_HEADS = 8
NSA_KV = 2
NSA_DH = 64
NSA_HPG = NSA_HEADS // NSA_KV
NSA_Q = NSA_HEADS * NSA_DH
NSA_KVW = NSA_KV * NSA_DH
CMP_BLOCK = 32
CMP_STRIDE = 16
CMP_HID = 128
SEL_BLOCK = 64
SEL_TOPK = 8
WINDOW = 512
Q_BLOCK = 128
FORCE_BONUS = 1000.0
REL_BUCKETS = 32
REL_MAX_DIST = 1024
D_FF = 4 * D_MODEL
EPS = 1e-6
NEG_BIG = -1e30

_SPLITS = (HG_K, HG_K, HG_V, HG_V, NSA_Q, NSA_KVW, NSA_KVW, NSA_KVW, NSA_KVW, NSA_KVW, NSA_KVW, 3 * NSA_HEADS, D_MODEL, D_MODEL)
IN_COLS = sum(_SPLITS)

kernel_name = "hybrid_hgrn2_nsa_block"


def rms_norm(x, gain):
    xf = x.astype(jnp.float32)
    xf = xf * lax.rsqrt(jnp.mean(xf * xf, axis=-1, keepdims=True) + EPS)
    return xf.astype(x.dtype) * gain


def rel_bucket(dist):
    max_exact = REL_BUCKETS // 2
    d = jnp.maximum(dist, 0)
    df = jnp.maximum(d, 1).astype(jnp.float32)
    large = max_exact + (jnp.log(df / max_exact) / math.log(REL_MAX_DIST / max_exact) * (REL_BUCKETS - max_exact)).astype(jnp.int32)
    return jnp.where(d < max_exact, d, jnp.minimum(large, REL_BUCKETS - 1))


def masked_softmax(s, valid, axis):
    s = jnp.where(valid, s.astype(jnp.float32), NEG_BIG)
    p = jax.nn.softmax(s, axis=axis)
    return jnp.where(valid, p, 0.0)


def hgrn2_mixer(q, f_logit, i_in, g, lb, out_gain):
    dt = q.dtype
    B, S, _ = q.shape
    nc = S // HG_CHUNK
    z = f_logit.astype(jnp.float32)
    lbf = lb.astype(jnp.float32)
    log_f = jnp.logaddexp(jnp.log(lbf), jnp.log1p(-lbf) + jax.nn.log_sigmoid(z))
    k = (1.0 - lbf) * jax.nn.sigmoid(-z)

    def chunks(t, d):
        return t.astype(jnp.float32).reshape(B, nc, HG_CHUNK, HG_HEADS, d).transpose(1, 0, 3, 2, 4)

    xs = (chunks(q, HG_DK), chunks(k, HG_DK), chunks(log_f, HG_DK), chunks(i_in, HG_DV))
    causal = jnp.tril(jnp.ones((HG_CHUNK, HG_CHUNK), dtype=bool))[:, :, None]

    def step(state, inp):
        qb, kb, gb, vb = inp
        b = jnp.cumsum(gb, axis=2)
        diff = b[:, :, :, None, :] - b[:, :, None, :, :]
        decay = jnp.exp(jnp.where(causal, diff, -jnp.inf))
        scores = jnp.einsum('bhtd,bhsd,bhtsd->bhts', qb, kb, decay)
        o = jnp.einsum('bhtd,bhdv->bhtv', qb * jnp.exp(b), state) + jnp.einsum('bhts,bhsv->bhtv', scores, vb)
        b_last = b[:, :, -1:, :]
        new_state = jnp.exp(b_last[:, :, 0, :, None]) * state + jnp.einsum('bhsd,bhsv->bhdv', kb * jnp.exp(b_last - b), vb)
        return new_state, o

    state0 = jnp.zeros((B, HG_HEADS, HG_DK, HG_DV), jnp.float32)
    _, o = lax.scan(step, state0, xs)
    o = o.transpose(1, 0, 3, 2, 4).reshape(B, S, HG_HEADS, HG_DV)
    o = rms_norm(o, out_gain.astype(jnp.float32)) * jax.nn.silu(g.astype(jnp.float32).reshape(B, S, HG_HEADS, HG_DV))
    return o.reshape(B, S, HG_V).astype(dt)


def nsa_mixer(q, k_c, v_c, k_s, v_s, k_w, v_w, gate_logit, qk_gain, cmp_pos, cmp_w1, cmp_b1, cmp_w2, rel_bias):
    dt = q.dtype
    B, S, _ = q.shape
    G, HPG, DH = NSA_KV, NSA_HPG, NSA_DH
    scale = DH ** -0.5
    q = rms_norm(q.reshape(B, S, G, HPG, DH), qk_gain[0]).transpose(0, 2, 3, 1, 4)
    gates = jax.nn.sigmoid(gate_logit.astype(jnp.float32)).reshape(B, S, G, HPG, 3).transpose(0, 2, 3, 1, 4)

    def heads(t):
        return t.reshape(B, S, G, DH).transpose(0, 2, 1, 3)

    k_s = rms_norm(heads(k_s), qk_gain[2])
    v_s = heads(v_s)
    k_w = rms_norm(heads(k_w), qk_gain[3])
    v_w = heads(v_w)

    n_cmp = (S - CMP_BLOCK) // CMP_STRIDE + 1
    cmp_tok = np.arange(n_cmp)[:, None] * CMP_STRIDE + np.arange(CMP_BLOCK)[None, :]

    def compress(t, j):
        blk = heads(t)[:, :, cmp_tok] + cmp_pos[j]
        blk = blk.reshape(B, G, n_cmp, CMP_BLOCK * DH)
        return jax.nn.gelu(blk @ cmp_w1[j] + cmp_b1[j]) @ cmp_w2[j]

    k_cmp = rms_norm(compress(k_c, 0), qk_gain[1])
    v_cmp = compress(v_c, 1)
    cmp_end = jnp.asarray(cmp_tok[:, -1], jnp.int32)

    n_sel = S // SEL_BLOCK
    sel_start = np.arange(n_sel) * SEL_BLOCK
    overlap = jnp.asarray(((cmp_tok[:, 0][:, None] < sel_start[None, :] + SEL_BLOCK) & (cmp_tok[:, -1][:, None] >= sel_start[None, :])).astype(np.float32))
    sel_start_j = jnp.asarray(sel_start, jnp.int32)
    k_blk = k_s.reshape(B, G, n_sel, SEL_BLOCK, DH)
    v_blk = v_s.reshape(B, G, n_sel, SEL_BLOCK, DH)
    n_top = min(SEL_TOPK, n_sel)
    bi = jnp.arange(B)[:, None, None, None]
    gi = jnp.arange(G)[None, :, None, None]
    g6 = jnp.arange(G)[None, :, None, None, None, None]
    h6 = jnp.arange(HPG)[None, None, :, None, None, None]
    jj = jnp.arange(n_sel, dtype=jnp.int32)

    k_wp = jnp.pad(k_w, ((0, 0), (0, 0), (WINDOW, 0), (0, 0)))
    v_wp = jnp.pad(v_w, ((0, 0), (0, 0), (WINDOW, 0), (0, 0)))
    bias_gh = rel_bias.reshape(REL_BUCKETS, G, HPG).transpose(1, 2, 0)

    def block(qi):
        start = qi * Q_BLOCK
        qb = lax.dynamic_slice_in_dim(q, start, Q_BLOCK, axis=3)
        gb = lax.dynamic_slice_in_dim(gates, start, Q_BLOCK, axis=3)
        t = start + jnp.arange(Q_BLOCK, dtype=jnp.int32)
        dist_c = t[:, None] - cmp_end[None, :]
        s_c = jnp.einsum('bghqd,bgcd->bghqc', qb, k_cmp) * scale + bias_gh[:, :, rel_bucket(dist_c)]
        p_c = masked_softmax(s_c, dist_c >= 0, -1)
        o_c = jnp.einsum('bghqc,bgcd->bghqd', p_c, v_cmp)
        imp = jnp.einsum('bghqc,cj->bgqj', p_c, overlap)
        j_cur = t // SEL_BLOCK
        forced = (jj[None, :] == 0) | (jj[None, :] == j_cur[:, None]) | (jj[None, :] == j_cur[:, None] - 1)
        imp = jnp.where(sel_start_j[None, :] <= t[:, None], imp + FORCE_BONUS * forced, -1.0)
        _, idx = lax.top_k(imp, n_top)
        k_g = k_blk[bi, gi, idx]
        v_g = v_blk[bi, gi, idx]
        pos = idx[..., None] * SEL_BLOCK + jnp.arange(SEL_BLOCK, dtype=jnp.int32)
        dist_s = t[None, None, :, None, None] - pos
        bias_s = bias_gh[g6, h6, rel_bucket(dist_s)[:, :, None]]
        s_s = jnp.einsum('bghqd,bgqnkd->bghqnk', qb, k_g) * scale + bias_s
        p_s = masked_softmax(s_s, (dist_s >= 0)[:, :, None], (-2, -1))
        o_s = jnp.einsum('bghqnk,bgqnkd->bghqd', p_s, v_g)
        kw = lax.dynamic_slice_in_dim(k_wp, start, WINDOW + Q_BLOCK, axis=2)
        vw = lax.dynamic_slice_in_dim(v_wp, start, WINDOW + Q_BLOCK, axis=2)
        key_pos = start - WINDOW + jnp.arange(WINDOW + Q_BLOCK, dtype=jnp.int32)
        dist_w = t[:, None] - key_pos[None, :]
        valid_w = (dist_w >= 0) & (dist_w < WINDOW) & (key_pos[None, :] >= 0)
        s_w = jnp.einsum('bghqd,bgkd->bghqk', qb, kw) * scale + bias_gh[:, :, rel_bucket(dist_w)]
        p_w = masked_softmax(s_w, valid_w, -1)
        o_w = jnp.einsum('bghqk,bgkd->bghqd', p_w, vw)
        return gb[..., 0:1] * o_c + gb[..., 1:2] * o_s + gb[..., 2:3] * o_w

    out = lax.map(block, jnp.arange(S // Q_BLOCK))
    return out.transpose(1, 0, 4, 2, 3, 5).reshape(B, S, NSA_Q).astype(dt)


def setup_inputs(seed: int = 0) -> dict:
    key = jax.random.key(seed)
    ks = jax.random.split(key, 17)
    f32 = jnp.float32

    def nrm(k, shape, scale):
        return jax.random.normal(k, shape, f32) * scale

    def gain(k, shape):
        return 1.0 + 0.02 * jax.random.normal(k, shape, f32)

    return {
        "x": nrm(ks[0], (BATCH, SEQ, D_MODEL), 1.0),
        "rel_bias": nrm(ks[1], (REL_BUCKETS, NSA_HEADS), 0.5),
        "hg_lb_logits": nrm(ks[2], (DEPTH, HG_K), 0.5),
        "norm_mix": gain(ks[3], (DEPTH, D_MODEL)),
        "w_in": nrm(ks[4], (DEPTH, D_MODEL, IN_COLS), D_MODEL ** -0.5),
        "hg_out_gain": gain(ks[5], (DEPTH, HG_DV)),
        "nsa_qk_gain": gain(ks[6], (DEPTH, 4, NSA_DH)),
        "cmp_pos": nrm(ks[7], (DEPTH, 2, CMP_BLOCK, NSA_DH), 0.02),
        "cmp_w1": nrm(ks[8], (DEPTH, 2, CMP_BLOCK * NSA_DH, CMP_HID), (CMP_BLOCK * NSA_DH) ** -0.5),
        "cmp_b1": nrm(ks[9], (DEPTH, 2, CMP_HID), 0.01),
        "cmp_w2": nrm(ks[10], (DEPTH, 2, CMP_HID, NSA_DH), CMP_HID ** -0.5),
        "w_branch_a": nrm(ks[11], (DEPTH, HG_V, D_MODEL), HG_V ** -0.5),
        "w_branch_b": nrm(ks[12], (DEPTH, NSA_Q, D_MODEL), NSA_Q ** -0.5),
        "w_out": nrm(ks[13], (DEPTH, D_MODEL, D_MODEL), D_MODEL ** -0.5),
        "norm_mlp": gain(ks[14], (DEPTH, D_MODEL)),
        "w_up": nrm(ks[15], (DEPTH, D_MODEL, D_FF), D_MODEL ** -0.5),
        "w_down": nrm(ks[16], (DEPTH, D_FF, D_MODEL), 0.5 * D_FF ** -0.5),
    }


def reference(x, rel_bias, hg_lb_logits, norm_mix, w_in, hg_out_gain, nsa_qk_gain, cmp_pos, cmp_w1, cmp_b1, cmp_w2, w_branch_a, w_branch_b, w_out, norm_mlp, w_up, w_down):
    lb_cum = jnp.cumsum(jax.nn.softmax(hg_lb_logits.astype(jnp.float32), axis=0), axis=0)
    lower_bounds = lb_cum - lb_cum[0:1]
    offsets = np.cumsum(_SPLITS)[:-1].tolist()
    for l in range(DEPTH):
        h = rms_norm(x, norm_mix[l])
        (hq, hf, hi, hg, nq, kc, vc, ksl, vsl, kwn, vwn, ngate, gate_a, gate_b) = jnp.split(h @ w_in[l], offsets, axis=-1)
        y_a = hgrn2_mixer(hq, hf, hi, hg, lower_bounds[l], hg_out_gain[l]) @ w_branch_a[l]
        y_b = nsa_mixer(nq, kc, vc, ksl, vsl, kwn, vwn, ngate, nsa_qk_gain[l], cmp_pos[l], cmp_w1[l], cmp_b1[l], cmp_w2[l], rel_bias) @ w_branch_b[l]
        mixed = jax.nn.sigmoid(gate_a) * y_a + jax.nn.sigmoid(gate_b) * y_b
        x = x + mixed @ w_out[l]
        h = rms_norm(x, norm_mlp[l])
        x = x + jnp.square(jax.nn.relu(h @ w_up[l])) @ w_down[l]
    return x
```

```python
import functools
import math

import numpy as np
import jax
import jax.numpy as jnp
from jax import lax
from jax.experimental import pallas as pl
from jax.experimental.pallas import tpu as pltpu

F32 = jnp.float32
BF16 = jnp.bfloat16

D_MODEL = 1024
HG_HEADS = 4
HG_DK = 128
HG_DV = 128
HG_K = HG_HEADS * HG_DK
HG_V = HG_HEADS * HG_DV
NSA_HEADS = 8
NSA_KV = 2
NSA_DH = 64
NSA_HPG = NSA_HEADS // NSA_KV
NSA_Q = NSA_HEADS * NSA_DH
NSA_KVW = NSA_KV * NSA_DH
CMP_BLOCK = 32
CMP_STRIDE = 16
CMP_HID = 128
SEL_BLOCK = 64
SEL_TOPK = 8
WINDOW = 512
FORCE_BONUS = 1000.0
REL_BUCKETS = 32
REL_MAX_DIST = 1024
D_FF = 4 * D_MODEL
EPS = 1e-6
NEG_BIG = -1e30

LANES = 128
VMEM_LIMIT = 56 * 1024 * 1024

P_HG = 0
P_GAB = 4 * HG_K
P_KV = P_GAB + 2 * D_MODEL
P_NQ = P_KV + 4 * NSA_KVW
P_GATE = P_NQ + NSA_Q
P_COLS = P_GATE + NSA_KV * LANES

HG_CHUNK = 64
HG_TS = 512
Q_BLOCK = 128
SEL_CHUNK = 512
WIN_BLOCKS = WINDOW // Q_BLOCK + 1


def _cparams(sem):
    return pltpu.CompilerParams(dimension_semantics=sem, vmem_limit_bytes=VMEM_LIMIT)


def _rms_rows(x, gain):
    ms = jnp.mean(x * x, axis=-1, keepdims=True)
    return x * lax.rsqrt(ms + EPS) * gain


def _proj_kernel(x_ref, g_ref, w_ref, o_ref, h_ref):
    @pl.when(pl.program_id(1) == 0)
    def _():
        h_ref[...] = _rms_rows(x_ref[...], g_ref[...]).astype(BF16)

    o_ref[...] = jnp.dot(h_ref[...], w_ref[...], preferred_element_type=F32)


def _rms_proj(x2, gain, w, tm, tn):
    T, D = x2.shape
    N = w.shape[1]
    return pl.pallas_call(
        _proj_kernel,
        grid=(T // tm, N // tn),
        in_specs=[
            pl.BlockSpec((tm, D), lambda i, j: (i, 0)),
            pl.BlockSpec((1, D), lambda i, j: (0, 0)),
            pl.BlockSpec((D, tn), lambda i, j: (0, j)),
        ],
        out_specs=pl.BlockSpec((tm, tn), lambda i, j: (i, j)),
        out_shape=jax.ShapeDtypeStruct((T, N), F32),
        scratch_shapes=[pltpu.VMEM((tm, D), BF16)],
        compiler_params=_cparams(("parallel", "arbitrary")),
        name="rms_proj",
    )(x2, gain, w)


def _mlp_kernel(x_ref, g_ref, wu_ref, wd_ref, o_ref, h_ref):
    f = pl.program_id(1)

    @pl.when(f == 0)
    def _():
        x = x_ref[...]
        h_ref[...] = _rms_rows(x, g_ref[...]).astype(BF16)
        o_ref[...] = x

    u = jnp.dot(h_ref[...], wu_ref[...], preferred_element_type=F32)
    u = jnp.square(jnp.maximum(u, 0.0)).astype(BF16)
    o_ref[...] += jnp.dot(u, wd_ref[...], preferred_element_type=F32)


def _mlp(x2, gain, wu, wd, tm, tf):
    T, D = x2.shape
    FF = wu.shape[1]
    return pl.pallas_call(
        _mlp_kernel,
        grid=(T // tm, FF // tf),
        in_specs=[
            pl.BlockSpec((tm, D), lambda i, f: (i, 0)),
            pl.BlockSpec((1, D), lambda i, f: (0, 0)),
            pl.BlockSpec((D, tf), lambda i, f: (0, f)),
            pl.BlockSpec((tf, D), lambda i, f: (f, 0)),
        ],
        out_specs=pl.BlockSpec((tm, D), lambda i, f: (i, 0)),
        out_shape=jax.ShapeDtypeStruct((T, D), F32),
        scratch_shapes=[pltpu.VMEM((tm, D), BF16)],
        compiler_params=_cparams(("parallel", "arbitrary")),
        name="mlp",
    )(x2, gain, wu, wd)


def _merge_kernel(x_ref, oa_ref, ob_ref, ga_ref, gb_ref, wa_ref, wb_ref, wo_ref, o_ref):
    ya = jnp.dot(oa_ref[...].astype(BF16), wa_ref[...], preferred_element_type=F32)
    yb = jnp.dot(ob_ref[...].astype(BF16), wb_ref[...], preferred_element_type=F32)
    mixed = jax.nn.sigmoid(ga_ref[...]) * ya + jax.nn.sigmoid(gb_ref[...]) * yb
    o_ref[...] = x_ref[...] + jnp.dot(mixed.astype(BF16), wo_ref[...], preferred_element_type=F32)


def _merge(x2, oa, ob, proj, wa, wb, wo, tm):
    T, D = x2.shape
    ga_blk = P_GAB // D
    return pl.pallas_call(
        _merge_kernel,
        grid=(T // tm,),
        in_specs=[
            pl.BlockSpec((tm, D), lambda i: (i, 0)),
            pl.BlockSpec((tm, HG_V), lambda i: (i, 0)),
            pl.BlockSpec((tm, NSA_Q), lambda i: (i, 0)),
            pl.BlockSpec((tm, D), lambda i: (i, ga_blk)),
            pl.BlockSpec((tm, D), lambda i: (i, ga_blk + 1)),
            pl.BlockSpec((HG_V, D), lambda i: (0, 0)),
            pl.BlockSpec((NSA_Q, D), lambda i: (0, 0)),
            pl.BlockSpec((D, D), lambda i: (0, 0)),
        ],
        out_specs=pl.BlockSpec((tm, D), lambda i: (i, 0)),
        out_shape=jax.ShapeDtypeStruct((T, D), F32),
        compiler_params=_cparams(("parallel",)),
        name="merge",
    )(x2, oa, ob, proj, proj, wa, wb, wo)


def _hgrn_consts(C):
    L = int(math.log2(C))
    idx = np.arange(C)
    mats = [(idx[None, :] <= idx[:, None]),
            (idx[None, :] > idx[:, None])]
    bmask = []
    for l in range(L):
        m = 1 << l
        r = (idx & ~(2 * m - 1)) + m
        lo = np.minimum(idx, r)[:, None]
        hi = np.maximum(idx, r)[:, None]
        mats.append((idx[None, :] > lo) & (idx[None, :] <= hi))
        blk = idx >> (l + 1)
        bmask.append(blk[:, None] == blk[None, :])
    bmask.append(idx[:, None] == idx[None, :])
    seg = np.concatenate(mats, axis=0).astype(np.float32)
    return seg, np.stack(bmask).astype(np.float32), L


def _hgrn_kernel(q_ref, f_ref, i_ref, g_ref, lbp_ref, gain_ref, seg_ref, bm_ref, o_ref, st_ref,
                 *, C, L, nchunk):
    @pl.when(pl.program_id(1) == 0)
    def _():
        st_ref[...] = jnp.zeros_like(st_ref)

    rowi = lax.broadcasted_iota(jnp.int32, (C, HG_DK), 0)
    nt = (((1,), (1,)), ((), ()))
    tn = (((0,), (0,)), ((), ()))

    def chunk(c, carry):
        r0 = pl.multiple_of(c * C, C)
        for h in range(HG_HEADS):
            cs = slice(h * HG_DK, (h + 1) * HG_DK)
            q = q_ref[0, pl.ds(r0, C), cs]
            z = f_ref[0, pl.ds(r0, C), cs]
            v = i_ref[0, pl.ds(r0, C), cs]
            gg = g_ref[0, pl.ds(r0, C), cs]
            log_lb = lbp_ref[0:1, cs]
            log1m_lb = lbp_ref[1:2, cs]
            one_m_lb = lbp_ref[2:3, cs]
            y = log1m_lb + (jnp.minimum(z, 0.0) - jnp.log1p(jnp.exp(-jnp.abs(z))))
            logf = jnp.maximum(log_lb, y) + jnp.log1p(jnp.exp(-jnp.abs(log_lb - y)))
            k = one_m_lb * jax.nn.sigmoid(-z)
            g_hi = logf.astype(BF16)
            g_lo = (logf - g_hi.astype(F32)).astype(BF16)
            seg = jnp.dot(seg_ref[...], jnp.concatenate([g_hi, g_lo], axis=1),
                          preferred_element_type=F32)
            seg = seg[:, :HG_DK] + seg[:, HG_DK:]
            b = seg[0:C]
            brev = seg[C:2 * C]
            qd = (q * jnp.exp(b)).astype(BF16)
            kd = (k * jnp.exp(brev)).astype(BF16)
            vb = v.astype(BF16)
            st = st_ref[h]
            o = lax.dot_general(qd, st.astype(BF16), nt, preferred_element_type=F32)
            s = lax.dot_general(q.astype(BF16), k.astype(BF16), nt,
                                preferred_element_type=F32) * bm_ref[L]
            for l in range(L):
                e = jnp.exp(seg[(2 + l) * C:(3 + l) * C])
                second = (rowi & (1 << l)) != 0
                ql = jnp.where(second, q * e, 0.0).astype(BF16)
                kl = jnp.where(second, 0.0, k * e).astype(BF16)
                s = s + lax.dot_general(ql, kl, nt, preferred_element_type=F32) * bm_ref[l]
            o = o + jnp.dot(s.astype(BF16), vb, preferred_element_type=F32)
            st_ref[h] = (st * jnp.exp(b[C - 1:C, :])
                         + lax.dot_general(vb, kd, tn, preferred_element_type=F32))
            o = _rms_rows(o, gain_ref[...]) * (gg * jax.nn.sigmoid(gg))
            o_ref[0, pl.ds(r0, C), cs] = o
        return carry

    lax.fori_loop(0, nchunk, chunk, 0)


def _hgrn(proj3, lbp, gain, C=HG_CHUNK, ts=HG_TS):
    B, S, _ = proj3.shape
    ts = min(ts, S)
    seg, bm, L = _hgrn_consts(C)
    blk = lambda k: pl.BlockSpec((1, ts, HG_K), lambda b, s: (b, s, P_HG // HG_K + k))
    return pl.pallas_call(
        functools.partial(_hgrn_kernel, C=C, L=L, nchunk=ts // C),
        grid=(B, S // ts),
        in_specs=[
            blk(0), blk(1), blk(2), blk(3),
            pl.BlockSpec((3, HG_K), lambda b, s: (0, 0)),
            pl.BlockSpec((1, HG_DV), lambda b, s: (0, 0)),
            pl.BlockSpec(seg.shape, lambda b, s: (0, 0)),
            pl.BlockSpec(bm.shape, lambda b, s: (0, 0, 0)),
        ],
        out_specs=pl.BlockSpec((1, ts, HG_V), lambda b, s: (b, s, 0)),
        out_shape=jax.ShapeDtypeStruct((B, S, HG_V), F32),
        scratch_shapes=[pltpu.VMEM((HG_HEADS, HG_DV, HG_DK), F32)],
        compiler_params=_cparams(("parallel", "arbitrary")),
        name="hgrn2",
    )(proj3, proj3, proj3, proj3, lbp, gain, jnp.asarray(seg, BF16), jnp.asarray(bm))


def _half_rms(x, gain2, lane):
    sq = x * x
    s_lo = jnp.sum(jnp.where(lane < NSA_DH, sq, 0.0), axis=-1, keepdims=True)
    s_hi = jnp.sum(jnp.where(lane < NSA_DH, 0.0, sq), axis=-1, keepdims=True)
    ms = jnp.where(lane < NSA_DH, s_lo, s_hi) * (1.0 / NSA_DH)
    return x * lax.rsqrt(ms + EPS) * gain2


def _gelu_tanh(x):
    return 0.5 * x * (1.0 + jnp.tanh(math.sqrt(2.0 / math.pi) * (x + 0.044715 * (x * x * x))))


def _prep_kernel(kv_ref, kc_ref, vc_ref, gsel_ref, gwin_ref, gcmp_ref, pos_ref, w1_ref, w1c_ref,
                 b1_ref, w2_ref, kvs_ref, kvw_ref, kvc_ref, *, S, rows):
    lane = lax.broadcasted_iota(jnp.int32, (rows, LANES), 1)

    def body(i, carry):
        r0 = pl.multiple_of(i * rows, rows)
        for src, gain_ref, dst in ((0, gsel_ref, kvs_ref), (2, gwin_ref, kvw_ref)):
            kk = kv_ref[0, pl.ds(r0, rows), src * LANES:(src + 1) * LANES]
            vv = kv_ref[0, pl.ds(r0, rows), (src + 1) * LANES:(src + 2) * LANES]
            kn = _half_rms(kk, gain_ref[...], lane)
            dst[0, 0, pl.ds(r0, rows), :] = jnp.where(
                lane < NSA_DH, kn, pltpu.roll(vv, NSA_DH, 1)).astype(BF16)
            dst[0, 1, pl.ds(r0, rows), :] = jnp.where(
                lane < NSA_DH, pltpu.roll(kn, NSA_DH, 1), vv).astype(BF16)
        return carry

    lax.fori_loop(0, S // rows, body, 0)

    nc = S // CMP_STRIDE
    outs = []
    for j, src_ref in ((0, kc_ref), (1, vc_ref)):
        bias = jnp.dot(pos_ref[j].astype(BF16), w1_ref[j], preferred_element_type=F32)[0:1] + b1_ref[j]
        pj = jnp.dot(src_ref[0].astype(BF16), w1c_ref[j], preferred_element_type=F32)
        per_g = []
        for g in range(NSA_KV):
            first = pj[:, (2 * g) * CMP_HID:(2 * g + 1) * CMP_HID]
            second = pj[:, (2 * g + 1) * CMP_HID:(2 * g + 2) * CMP_HID]
            hid = _gelu_tanh(first + pltpu.roll(second, nc - 1, 0) + bias)
            per_g.append(jnp.dot(hid.astype(BF16), w2_ref[j], preferred_element_type=F32))
        outs.append(per_g)
    for g in range(NSA_KV):
        kc = outs[0][g]
        ms = jnp.sum(kc * kc, axis=-1, keepdims=True) * (1.0 / NSA_DH)
        kn = kc * lax.rsqrt(ms + EPS) * gcmp_ref[...]
        kvc_ref[0, g] = (kn + outs[1][g]).astype(BF16)


def _nsa_prep(proj3, kc2, vc2, gsel, gwin, gcmp, pos8, w1, w1c, b1, w2p):
    B, S, _ = proj3.shape
    nc = S // CMP_STRIDE
    rows = min(512, S)
    full = lambda shp: pl.BlockSpec(shp, lambda b: (0,) * len(shp))
    kv_sds = jax.ShapeDtypeStruct((B, NSA_KV, S, LANES), BF16)
    return pl.pallas_call(
        functools.partial(_prep_kernel, S=S, rows=rows),
        grid=(B,),
        in_specs=[
            pl.BlockSpec((1, S, 4 * NSA_KVW), lambda b: (b, 0, P_KV // (4 * NSA_KVW))),
            pl.BlockSpec((1, nc, CMP_STRIDE * NSA_KVW), lambda b: (b, 0, 0)),
            pl.BlockSpec((1, nc, CMP_STRIDE * NSA_KVW), lambda b: (b, 0, 0)),
            full((1, LANES)), full((1, LANES)), full((1, LANES)),
            full(pos8.shape), full(w1.shape), full(w1c.shape), full(b1.shape), full(w2p.shape),
        ],
        out_specs=[
            pl.BlockSpec((1, NSA_KV, S, LANES), lambda b: (b, 0, 0, 0)),
            pl.BlockSpec((1, NSA_KV, S, LANES), lambda b: (b, 0, 0, 0)),
            pl.BlockSpec((1, NSA_KV, nc, LANES), lambda b: (b, 0, 0, 0)),
        ],
        out_shape=[kv_sds, kv_sds, jax.ShapeDtypeStruct((B, NSA_KV, nc, LANES), BF16)],
        compiler_params=_cparams(("parallel",)),
        name="nsa_prep",
    )(proj3, kc2, vc2, gsel, gwin, gcmp, pos8, w1, w1c, b1, w2p)


def _sel_table_geometry():
    last_start = (REL_BUCKETS // 2) * (REL_MAX_DIST / (REL_BUCKETS // 2)) ** (
        (REL_BUCKETS - REL_BUCKETS // 2 - 1) / (REL_BUCKETS - REL_BUCKETS // 2))
    dsat = int(math.ceil((last_start + 16 + SEL_CHUNK) / LANES)) * LANES
    return dsat, (dsat + SEL_CHUNK) // LANES


def _attn_kernel(q_ref, gate_ref, kvc_ref, kvs_ref, kvw_ref, tabc_ref, tabw_ref, tabs_ref, ovl_ref,
                 exp_ref, qg_ref, o_ref, m_ref, l_ref, acc_ref, *, nsel, dsat):
    qi = pl.program_id(2)
    QB = Q_BLOCK
    R = NSA_HPG * QB
    nt = (((1,), (1,)), ((), ()))

    xq = q_ref[0]
    q4 = jnp.concatenate([xq[:, h * NSA_DH:(h + 1) * NSA_DH] for h in range(NSA_HPG)], axis=0)
    qn = _rms_rows(q4, qg_ref[...]) * (NSA_DH ** -0.5)
    qp = jnp.concatenate([qn, jnp.zeros_like(qn)], axis=1).astype(BF16)

    kvc = kvc_ref[0, 0]
    s = lax.dot_general(qp, kvc, nt, preferred_element_type=F32) + tabc_ref[0, 0]
    mx = jnp.max(s, axis=-1, keepdims=True)
    e = jnp.exp(s - mx)
    p = jnp.where(s > 0.5 * NEG_BIG, e / jnp.sum(e, axis=-1, keepdims=True), 0.0)
    o_c = jnp.dot(p.astype(BF16), kvc, preferred_element_type=F32)

    psum = p[0:QB] + p[QB:2 * QB] + p[2 * QB:3 * QB] + p[3 * QB:4 * QB]
    p_hi = psum.astype(BF16)
    p_lo = (psum - p_hi.astype(F32)).astype(BF16)
    imp = (jnp.dot(p_hi, ovl_ref[...], preferred_element_type=F32)
           + jnp.dot(p_lo, ovl_ref[...], preferred_element_type=F32))
    t = qi * QB + lax.broadcasted_iota(jnp.int32, (QB, nsel), 0)
    jcol = lax.broadcasted_iota(jnp.int32, (QB, nsel), 1)
    jcur = t // SEL_BLOCK
    forced = (jcol == 0) | (jcol == jcur) | (jcol == jcur - 1)
    work = jnp.where(jcol * SEL_BLOCK <= t, imp + jnp.where(forced, FORCE_BONUS, 0.0), -1.0)
    jf = jcol.astype(F32)
    sel = jnp.zeros((QB, nsel), F32)
    for _ in range(min(SEL_TOPK, nsel)):
        best = jnp.max(work, axis=-1, keepdims=True)
        first = jnp.min(jnp.where(work == best, jf, float(nsel)), axis=-1, keepdims=True)
        hit = jf == first
        sel = jnp.where(hit, 1.0, sel)
        work = jnp.where(hit, -3.0, work)
    selneg = jnp.where(sel > 0.5, 0.0, NEG_BIG).astype(BF16)

    m_ref[...] = jnp.full_like(m_ref, NEG_BIG)
    l_ref[...] = jnp.zeros_like(l_ref)
    acc_ref[...] = jnp.zeros_like(acc_ref)
    nslab = SEL_CHUNK // LANES

    def sel_chunk(c, carry):
        k0 = pl.multiple_of(c * SEL_CHUNK, SEL_CHUNK)
        kv = kvs_ref[0, 0, pl.ds(k0, SEL_CHUNK), :]
        sc = lax.dot_general(qp, kv, nt, preferred_element_type=F32)
        delta = qi * QB - c * SEL_CHUNK
        w0 = (dsat - jnp.minimum(delta, dsat)) // LANES
        bias = jnp.concatenate([tabs_ref[0, w0 + u] for u in range(nslab)], axis=1)
        neg = jnp.dot(selneg, exp_ref[c], preferred_element_type=F32)
        sc = sc + bias + jnp.concatenate([neg] * NSA_HPG, axis=0)
        m_old = m_ref[...]
        m_new = jnp.maximum(m_old, jnp.max(sc, axis=-1, keepdims=True))
        alpha = jnp.exp(m_old - m_new)
        pe = jnp.exp(sc - m_new)
        l_ref[...] = alpha * l_ref[...] + jnp.sum(pe, axis=-1, keepdims=True)
        acc_ref[...] = alpha * acc_ref[...] + jnp.dot(pe.astype(BF16), kv, preferred_element_type=F32)
        m_ref[...] = m_new
        return carry

    lax.fori_loop(0, (qi * QB + QB - 1) // SEL_CHUNK + 1, sel_chunk, 0)
    o_s = acc_ref[...] / l_ref[...]

    tiles, kvs = [], []
    for r in range(WIN_BLOCKS):
        kb = qi - (WIN_BLOCKS - 1) + r
        k0 = pl.multiple_of(jnp.maximum(kb, 0) * QB, QB)
        kv = kvw_ref[0, 0, pl.ds(k0, QB), :]
        kvs.append(kv)
        sw = lax.dot_general(qp, kv, nt, preferred_element_type=F32)
        tiles.append(sw + jnp.where(kb >= 0, 0.0, NEG_BIG))
    sw = jnp.concatenate(tiles, axis=1) + tabw_ref[0]
    mw = jnp.max(sw, axis=-1, keepdims=True)
    ew = jnp.exp(sw - mw)
    lw = jnp.sum(ew, axis=-1, keepdims=True)
    ewb = ew.astype(BF16)
    o_w = jnp.dot(ewb[:, 0:QB], kvs[0], preferred_element_type=F32)
    for r in range(1, WIN_BLOCKS):
        o_w = o_w + jnp.dot(ewb[:, r * QB:(r + 1) * QB], kvs[r], preferred_element_type=F32)
    o_w = o_w / lw

    gt = jax.nn.sigmoid(gate_ref[0])
    comb = []
    for h in range(NSA_HPG):
        rs = slice(h * QB, (h + 1) * QB)
        comb.append(gt[:, 3 * h:3 * h + 1] * o_c[rs] + gt[:, 3 * h + 1:3 * h + 2] * o_s[rs]
                    + gt[:, 3 * h + 2:3 * h + 3] * o_w[rs])
    lane = lax.broadcasted_iota(jnp.int32, (QB, LANES), 1)
    for u in range(NSA_HPG // 2):
        o_ref[0, :, u * LANES:(u + 1) * LANES] = jnp.where(
            lane < NSA_DH, pltpu.roll(comb[2 * u], NSA_DH, 1), comb[2 * u + 1])


def _nsa_attn(proj3, kvc, kvs, kvw, tabc, tabw, tabs, ovl, expand, qgain, dsat):
    B, S, _ = proj3.shape
    nq = S // Q_BLOCK
    nc = S // CMP_STRIDE
    nsel = S // SEL_BLOCK
    R = NSA_HPG * Q_BLOCK
    gw = NSA_HPG * NSA_DH
    return pl.pallas_call(
        functools.partial(_attn_kernel, nsel=nsel, dsat=dsat),
        grid=(B, NSA_KV, nq),
        in_specs=[
            pl.BlockSpec((1, Q_BLOCK, gw), lambda b, g, i: (b, i, P_NQ // gw + g)),
            pl.BlockSpec((1, Q_BLOCK, LANES), lambda b, g, i: (b, i, P_GATE // LANES + g)),
            pl.BlockSpec((1, 1, nc, LANES), lambda b, g, i: (b, g, 0, 0)),
            pl.BlockSpec((1, 1, S, LANES), lambda b, g, i: (b, g, 0, 0)),
            pl.BlockSpec((1, 1, S, LANES), lambda b, g, i: (b, g, 0, 0)),
            pl.BlockSpec((1, 1, R, nc), lambda b, g, i: (g, i, 0, 0)),
            pl.BlockSpec((1, R, WIN_BLOCKS * Q_BLOCK), lambda b, g, i: (g, 0, 0)),
            pl.BlockSpec((1,) + tabs.shape[1:], lambda b, g, i: (g, 0, 0, 0)),
            pl.BlockSpec(ovl.shape, lambda b, g, i: (0, 0)),
            pl.BlockSpec(expand.shape, lambda b, g, i: (0, 0, 0)),
            pl.BlockSpec((1, NSA_DH), lambda b, g, i: (0, 0)),
        ],
        out_specs=pl.BlockSpec((1, Q_BLOCK, gw), lambda b, g, i: (b, i, g)),
        out_shape=jax.ShapeDtypeStruct((B, S, NSA_Q), F32),
        scratch_shapes=[pltpu.VMEM((R, 1), F32), pltpu.VMEM((R, 1), F32), pltpu.VMEM((R, LANES), F32)],
        compiler_params=_cparams(("parallel", "parallel", "arbitrary")),
        name="nsa_attn",
    )(proj3, proj3, kvc, kvs, kvw, tabc, tabw, tabs, ovl, expand, qgain)


def _rel_bucket(d):
    max_exact = REL_BUCKETS // 2
    d = jnp.maximum(d, 0)
    df = jnp.maximum(d, 1).astype(F32)
    large = max_exact + (jnp.log(df / max_exact) / math.log(REL_MAX_DIST / max_exact)
                         * (REL_BUCKETS - max_exact)).astype(jnp.int32)
    return jnp.where(d < max_exact, d, jnp.minimum(large, REL_BUCKETS - 1))


def _bias_tables(rel_bias, S, dsat, nslab_tab):
    nq = S // Q_BLOCK
    nc = S // CMP_STRIDE
    n_cmp = (S - CMP_BLOCK) // CMP_STRIDE + 1
    dmax = S + WINDOW
    bias1d = rel_bias[_rel_bucket(jnp.arange(dmax, dtype=jnp.int32))]
    bias1d = bias1d.T.reshape(NSA_KV, NSA_HPG, dmax)

    def lookup(dist, valid):
        tab = bias1d[:, :, np.clip(dist, 0, dmax - 1)]
        tab = jnp.where(jnp.asarray(valid), tab, NEG_BIG)
        return tab.reshape(NSA_KV, NSA_HPG * dist.shape[0], dist.shape[1])

    i = np.arange(Q_BLOCK)
    c = np.arange(nc)
    d_c = (np.arange(nq)[:, None, None] * Q_BLOCK + i[None, :, None]) - (c[None, None, :] * CMP_STRIDE + CMP_BLOCK - 1)
    v_c = (d_c >= 0) & (c[None, None, :] < n_cmp)
    tabc = jnp.stack([lookup(d_c[n], v_c[n]) for n in range(nq)], axis=1)
    j = np.arange(WIN_BLOCKS * Q_BLOCK)
    d_w = i[:, None] + WINDOW - j[None, :]
    tabw = lookup(d_w, (d_w >= 0) & (d_w < WINDOW))
    u = np.arange(nslab_tab * LANES)
    d_s = i[:, None] - u[None, :] + dsat
    strip = lookup(d_s, d_s >= 0)
    tabs = strip.reshape(NSA_KV, NSA_HPG * Q_BLOCK, nslab_tab, LANES).transpose(0, 2, 1, 3)
    return tabc, tabw, tabs


def _static_mats(S):
    nc = S // CMP_STRIDE
    nsel = S // SEL_BLOCK
    n_cmp = (S - CMP_BLOCK) // CMP_STRIDE + 1
    c = np.arange(nc)
    sel_start = np.arange(nsel) * SEL_BLOCK
    ovl = ((c[:, None] * CMP_STRIDE < sel_start[None, :] + SEL_BLOCK)
           & (c[:, None] * CMP_STRIDE + CMP_BLOCK - 1 >= sel_start[None, :]) & (c[:, None] < n_cmp))
    key = np.arange(S)
    expand = (np.arange(nsel)[:, None] == key[None, :] // SEL_BLOCK)
    nch = max(S // SEL_CHUNK, 1)
    expand = expand.reshape(nsel, nch, S // nch).transpose(1, 0, 2)
    return jnp.asarray(ovl, BF16), jnp.asarray(expand, BF16)


def _layer_params(l, lower_bounds, norm_mix, w_in, hg_out_gain, nsa_qk_gain, cmp_pos, cmp_w1, cmp_b1,
                  cmp_w2, w_branch_a, w_branch_b, w_out, norm_mlp, w_up, w_down):
    splits = (HG_K, HG_K, HG_V, HG_V, NSA_Q, NSA_KVW, NSA_KVW, NSA_KVW, NSA_KVW, NSA_KVW, NSA_KVW,
              3 * NSA_HEADS, D_MODEL, D_MODEL)
    offs = np.concatenate([[0], np.cumsum(splits)])
    col = lambda k: w_in[l][:, offs[k]:offs[k + 1]]
    gate_w = col(11)
    zpad = jnp.zeros((D_MODEL, LANES - 3 * NSA_HPG), F32)
    gate_cols = []
    for g in range(NSA_KV):
        gate_cols += [gate_w[:, g * 3 * NSA_HPG:(g + 1) * 3 * NSA_HPG], zpad]
    w_main = jnp.concatenate([col(0), col(1), col(2), col(3), col(12), col(13),
                              col(7), col(8), col(9), col(10), col(4)] + gate_cols, axis=1).astype(BF16)
    w_cmp = jnp.concatenate([col(5), col(6)], axis=1).astype(BF16)

    lb = lower_bounds[l]
    lbp = jnp.stack([jnp.log(lb), jnp.log1p(-lb), 1.0 - lb])

    w1 = cmp_w1[l].reshape(2, 2, CMP_STRIDE, NSA_DH, CMP_HID)
    w1c = jnp.zeros((2, CMP_STRIDE, NSA_KV, NSA_DH, NSA_KV, 2, CMP_HID), F32)
    for g in range(NSA_KV):
        w1c = w1c.at[:, :, g, :, g, :, :].set(w1.transpose(0, 2, 3, 1, 4))
    w1c = w1c.reshape(2, CMP_STRIDE * NSA_KVW, NSA_KV * 2 * CMP_HID).astype(BF16)
    pos8 = jnp.zeros((2, 8, CMP_BLOCK * NSA_DH), F32).at[:, 0, :].set(cmp_pos[l].reshape(2, -1))
    w2p = jnp.zeros((2, CMP_HID, LANES), F32)
    w2p = w2p.at[0, :, :NSA_DH].set(cmp_w2[l, 0]).at[1, :, NSA_DH:].set(cmp_w2[l, 1]).astype(BF16)
    qk = nsa_qk_gain[l]
    z64 = jnp.zeros((NSA_DH,), F32)
    return dict(
        norm_mix=norm_mix[l][None, :], w_main=w_main, w_cmp=w_cmp, lbp=lbp,
        hg_gain=hg_out_gain[l][None, :],
        gsel=jnp.concatenate([qk[2], qk[2]])[None, :], gwin=jnp.concatenate([qk[3], qk[3]])[None, :],
        gcmp=jnp.concatenate([qk[1], z64])[None, :], qgain=qk[0][None, :],
        pos8=pos8, w1=cmp_w1[l].astype(BF16), w1c=w1c, b1=cmp_b1[l][:, None, :], w2p=w2p,
        wa=w_branch_a[l].astype(BF16), wb=w_branch_b[l].astype(BF16), wo=w_out[l].astype(BF16),
        norm_mlp=norm_mlp[l][None, :], wu=w_up[l].astype(BF16), wd=w_down[l].astype(BF16))


def kernel(x, rel_bias, hg_lb_logits, norm_mix, w_in, hg_out_gain, nsa_qk_gain, cmp_pos, cmp_w1, cmp_b1, cmp_w2, w_branch_a, w_branch_b, w_out, norm_mlp, w_up, w_down):
    B, S, D = x.shape
    T = B * S
    depth = w_in.shape[0]
    assert D == D_MODEL and S % SEL_CHUNK == 0 and S >= WINDOW
    lb_cum = jnp.cumsum(jax.nn.softmax(hg_lb_logits.astype(F32), axis=0), axis=0)
    lower_bounds = lb_cum - lb_cum[0:1]

    dsat, nslab_tab = _sel_table_geometry()
    tabc, tabw, tabs = _bias_tables(rel_bias, S, dsat, nslab_tab)
    ovl, expand = _static_mats(S)
    nc = S // CMP_STRIDE
    tm = min(1024, T)

    x2 = x.reshape(T, D)
    for l in range(depth):
        p = _layer_params(l, lower_bounds, norm_mix, w_in, hg_out_gain, nsa_qk_gain, cmp_pos, cmp_w1,
                          cmp_b1, cmp_w2, w_branch_a, w_branch_b, w_out, norm_mlp, w_up, w_down)
        proj = _rms_proj(x2, p["norm_mix"], p["w_main"], tm, P_COLS // 7)
        pc = _rms_proj(x2, p["norm_mix"], p["w_cmp"], tm, 2 * NSA_KVW)
        proj3 = proj.reshape(B, S, P_COLS)
        kc2 = pc[:, :NSA_KVW].reshape(B, nc, CMP_STRIDE * NSA_KVW)
        vc2 = pc[:, NSA_KVW:].reshape(B, nc, CMP_STRIDE * NSA_KVW)
        o_a = _hgrn(proj3, p["lbp"], p["hg_gain"])
        kvs, kvw, kvc = _nsa_prep(proj3, kc2, vc2, p["gsel"], p["gwin"], p["gcmp"], p["pos8"], p["w1"],
                                  p["w1c"], p["b1"], p["w2p"])
        o_b = _nsa_attn(proj3, kvc, kvs, kvw, tabc, tabw, tabs, ovl, expand, p["qgain"], dsat)
        x2 = _merge(x2, o_a.reshape(T, HG_V), o_b.reshape(T, NSA_Q), proj, p["wa"], p["wb"], p["wo"],
                    min(512, T))
        x2 = _mlp(x2, p["norm_mlp"], p["wu"], p["wd"], tm, 1024)
    return x2.reshape(B, S, D)
```

```python
import functools
import math

import numpy as np
import jax
import jax.numpy as jnp
from jax import lax
from jax.experimental import pallas as pl
from jax.experimental.pallas import tpu as pltpu

F32 = jnp.float32
BF16 = jnp.bfloat16

D_MODEL = 1024
HG_HEADS = 4
HG_DK = 128
HG_DV = 128
HG_K = HG_HEADS * HG_DK
HG_V = HG_HEADS * HG_DV
NSA_HEADS = 8
NSA_KV = 2
NSA_DH = 64
NSA_HPG = NSA_HEADS // NSA_KV
NSA_Q = NSA_HEADS * NSA_DH
NSA_KVW = NSA_KV * NSA_DH
CMP_BLOCK = 32
CMP_STRIDE = 16
CMP_HID = 128
SEL_BLOCK = 64
SEL_TOPK = 8
WINDOW = 512
FORCE_BONUS = 1000.0
REL_BUCKETS = 32
REL_MAX_DIST = 1024
D_FF = 4 * D_MODEL
EPS = 1e-6
NEG_BIG = -1e30

LANES = 128
VMEM_LIMIT = 56 * 1024 * 1024

P_HG = 0
P_GAB = 4 * HG_K
P_KV = P_GAB + 2 * D_MODEL
P_NQ = P_KV + 4 * NSA_KVW
P_GATE = P_NQ + NSA_Q
P_COLS = P_GATE + NSA_KV * LANES

HG_CHUNK = 64
HG_TS = 512
Q_BLOCK = 128
SEL_CHUNK = 512
WIN_BLOCKS = WINDOW // Q_BLOCK + 1


def _cparams(sem):
    return pltpu.CompilerParams(dimension_semantics=sem, vmem_limit_bytes=VMEM_LIMIT)


def _rms_rows(x, gain):
    ms = jnp.mean(x * x, axis=-1, keepdims=True)
    return x * lax.rsqrt(ms + EPS) * gain


def _proj_kernel(x_ref, g_ref, w_ref, wc_ref, o_ref, kc_ref, vc_ref, h_ref):
    @pl.when(pl.program_id(1) == 0)
    def _():
        h = _rms_rows(x_ref[...], g_ref[...]).astype(BF16)
        h_ref[...] = h
        c = jnp.dot(h, wc_ref[...], preferred_element_type=F32)
        kc_ref[...] = c[:, :NSA_KVW]
        vc_ref[...] = c[:, NSA_KVW:]

    o_ref[...] = jnp.dot(h_ref[...], w_ref[...], preferred_element_type=F32)


def _rms_proj(x2, gain, w, wc, tm, tn):
    T, D = x2.shape
    N = w.shape[1]
    cmp_sds = jax.ShapeDtypeStruct((T, NSA_KVW), F32)
    return pl.pallas_call(
        _proj_kernel,
        grid=(T // tm, N // tn),
        in_specs=[
            pl.BlockSpec((tm, D), lambda i, j: (i, 0)),
            pl.BlockSpec((1, D), lambda i, j: (0, 0)),
            pl.BlockSpec((D, tn), lambda i, j: (0, j)),
            pl.BlockSpec((D, 2 * NSA_KVW), lambda i, j: (0, 0)),
        ],
        out_specs=[
            pl.BlockSpec((tm, tn), lambda i, j: (i, j)),
            pl.BlockSpec((tm, NSA_KVW), lambda i, j: (i, 0)),
            pl.BlockSpec((tm, NSA_KVW), lambda i, j: (i, 0)),
        ],
        out_shape=[jax.ShapeDtypeStruct((T, N), F32), cmp_sds, cmp_sds],
        scratch_shapes=[pltpu.VMEM((tm, D), BF16)],
        compiler_params=_cparams(("parallel", "arbitrary")),
        name="rms_proj",
    )(x2, gain, w, wc)


def _mlp_kernel(x_ref, g_ref, wu_ref, wd_ref, o_ref, h_ref):
    f = pl.program_id(1)

    @pl.when(f == 0)
    def _():
        x = x_ref[...]
        h_ref[...] = _rms_rows(x, g_ref[...]).astype(BF16)
        o_ref[...] = x

    u = jnp.dot(h_ref[...], wu_ref[...], preferred_element_type=F32)
    u = jnp.square(jnp.maximum(u, 0.0)).astype(BF16)
    o_ref[...] += jnp.dot(u, wd_ref[...], preferred_element_type=F32)


def _mlp(x2, gain, wu, wd, tm, tf):
    T, D = x2.shape
    FF = wu.shape[1]
    return pl.pallas_call(
        _mlp_kernel,
        grid=(T // tm, FF // tf),
        in_specs=[
            pl.BlockSpec((tm, D), lambda i, f: (i, 0)),
            pl.BlockSpec((1, D), lambda i, f: (0, 0)),
            pl.BlockSpec((D, tf), lambda i, f: (0, f)),
            pl.BlockSpec((tf, D), lambda i, f: (f, 0)),
        ],
        out_specs=pl.BlockSpec((tm, D), lambda i, f: (i, 0)),
        out_shape=jax.ShapeDtypeStruct((T, D), F32),
        scratch_shapes=[pltpu.VMEM((tm, D), BF16)],
        compiler_params=_cparams(("parallel", "arbitrary")),
        name="mlp",
    )(x2, gain, wu, wd)


def _merge_kernel(x_ref, oa_ref, ob_ref, ga_ref, gb_ref, wa_ref, wb_ref, wo_ref, o_ref):
    ya = jnp.dot(oa_ref[...].astype(BF16), wa_ref[...], preferred_element_type=F32)
    yb = jnp.dot(ob_ref[...].astype(BF16), wb_ref[...], preferred_element_type=F32)
    mixed = jax.nn.sigmoid(ga_ref[...]) * ya + jax.nn.sigmoid(gb_ref[...]) * yb
    o_ref[...] = x_ref[...] + jnp.dot(mixed.astype(BF16), wo_ref[...], preferred_element_type=F32)


def _merge(x2, oa, ob, proj, wa, wb, wo, tm):
    T, D = x2.shape
    ga_blk = P_GAB // D
    return pl.pallas_call(
        _merge_kernel,
        grid=(T // tm,),
        in_specs=[
            pl.BlockSpec((tm, D), lambda i: (i, 0)),
            pl.BlockSpec((tm, HG_V), lambda i: (i, 0)),
            pl.BlockSpec((tm, NSA_Q), lambda i: (i, 0)),
            pl.BlockSpec((tm, D), lambda i: (i, ga_blk)),
            pl.BlockSpec((tm, D), lambda i: (i, ga_blk + 1)),
            pl.BlockSpec((HG_V, D), lambda i: (0, 0)),
            pl.BlockSpec((NSA_Q, D), lambda i: (0, 0)),
            pl.BlockSpec((D, D), lambda i: (0, 0)),
        ],
        out_specs=pl.BlockSpec((tm, D), lambda i: (i, 0)),
        out_shape=jax.ShapeDtypeStruct((T, D), F32),
        compiler_params=_cparams(("parallel",)),
        name="merge",
    )(x2, oa, ob, proj, proj, wa, wb, wo)


def _hgrn_consts(C):
    L = int(math.log2(C))
    idx = np.arange(C)
    mats = [(idx[None, :] <= idx[:, None]),
            (idx[None, :] > idx[:, None])]
    bmask = []
    for l in range(L):
        m = 1 << l
        r = (idx & ~(2 * m - 1)) + m
        lo = np.minimum(idx, r)[:, None]
        hi = np.maximum(idx, r)[:, None]
        mats.append((idx[None, :] > lo) & (idx[None, :] <= hi))
        blk = idx >> (l + 1)
        bmask.append(blk[:, None] == blk[None, :])
    bmask.append(idx[:, None] == idx[None, :])
    seg = np.concatenate(mats, axis=0).astype(np.float32)
    return seg, np.stack(bmask).astype(np.float32), L


def _hgrn_kernel(q_ref, f_ref, i_ref, g_ref, lbp_ref, gain_ref, seg_ref, bm_ref, o_ref, st_ref,
                 *, C, L, nchunk):
    @pl.when(pl.program_id(1) == 0)
    def _():
        st_ref[...] = jnp.zeros_like(st_ref)

    rowi = lax.broadcasted_iota(jnp.int32, (C, HG_DK), 0)
    nt = (((1,), (1,)), ((), ()))
    tn = (((0,), (0,)), ((), ()))

    def chunk(c, carry):
        r0 = pl.multiple_of(c * C, C)
        for h in range(HG_HEADS):
            cs = slice(h * HG_DK, (h + 1) * HG_DK)
            q = q_ref[0, pl.ds(r0, C), cs]
            z = f_ref[0, pl.ds(r0, C), cs]
            v = i_ref[0, pl.ds(r0, C), cs]
            gg = g_ref[0, pl.ds(r0, C), cs]
            log_lb = lbp_ref[0:1, cs]
            log1m_lb = lbp_ref[1:2, cs]
            one_m_lb = lbp_ref[2:3, cs]
            y = log1m_lb + (jnp.minimum(z, 0.0) - jnp.log1p(jnp.exp(-jnp.abs(z))))
            logf = jnp.maximum(log_lb, y) + jnp.log1p(jnp.exp(-jnp.abs(log_lb - y)))
            k = one_m_lb * jax.nn.sigmoid(-z)
            g_hi = logf.astype(BF16)
            g_lo = (logf - g_hi.astype(F32)).astype(BF16)
            seg = jnp.dot(seg_ref[...], jnp.concatenate([g_hi, g_lo], axis=1),
                          preferred_element_type=F32)
            seg = seg[:, :HG_DK] + seg[:, HG_DK:]
            b = seg[0:C]
            brev = seg[C:2 * C]
            qd = (q * jnp.exp(b)).astype(BF16)
            kd = (k * jnp.exp(brev)).astype(BF16)
            vb = v.astype(BF16)
            st = st_ref[h]
            o = lax.dot_general(qd, st.astype(BF16), nt, preferred_element_type=F32)
            s = lax.dot_general(q.astype(BF16), k.astype(BF16), nt,
                                preferred_element_type=F32) * bm_ref[L]
            for l in range(L):
                e = jnp.exp(seg[(2 + l) * C:(3 + l) * C])
                second = (rowi & (1 << l)) != 0
                ql = jnp.where(second, q * e, 0.0).astype(BF16)
                kl = jnp.where(second, 0.0, k * e).astype(BF16)
                s = s + lax.dot_general(ql, kl, nt, preferred_element_type=F32) * bm_ref[l]
            o = o + jnp.dot(s.astype(BF16), vb, preferred_element_type=F32)
            st_ref[h] = (st * jnp.exp(b[C - 1:C, :])
                         + lax.dot_general(vb, kd, tn, preferred_element_type=F32))
            o = _rms_rows(o, gain_ref[...]) * (gg * jax.nn.sigmoid(gg))
            o_ref[0, pl.ds(r0, C), cs] = o
        return carry

    lax.fori_loop(0, nchunk, chunk, 0)


def _hgrn(proj3, lbp, gain, C=HG_CHUNK, ts=HG_TS):
    B, S, _ = proj3.shape
    ts = min(ts, S)
    seg, bm, L = _hgrn_consts(C)
    blk = lambda k: pl.BlockSpec((1, ts, HG_K), lambda b, s: (b, s, P_HG // HG_K + k))
    return pl.pallas_call(
        functools.partial(_hgrn_kernel, C=C, L=L, nchunk=ts // C),
        grid=(B, S // ts),
        in_specs=[
            blk(0), blk(1), blk(2), blk(3),
            pl.BlockSpec((3, HG_K), lambda b, s: (0, 0)),
            pl.BlockSpec((1, HG_DV), lambda b, s: (0, 0)),
            pl.BlockSpec(seg.shape, lambda b, s: (0, 0)),
            pl.BlockSpec(bm.shape, lambda b, s: (0, 0, 0)),
        ],
        out_specs=pl.BlockSpec((1, ts, HG_V), lambda b, s: (b, s, 0)),
        out_shape=jax.ShapeDtypeStruct((B, S, HG_V), F32),
        scratch_shapes=[pltpu.VMEM((HG_HEADS, HG_DV, HG_DK), F32)],
        compiler_params=_cparams(("parallel", "arbitrary")),
        name="hgrn2",
    )(proj3, proj3, proj3, proj3, lbp, gain, jnp.asarray(seg, BF16), jnp.asarray(bm))


def _half_rms(x, gain2, lane):
    sq = x * x
    s_lo = jnp.sum(jnp.where(lane < NSA_DH, sq, 0.0), axis=-1, keepdims=True)
    s_hi = jnp.sum(jnp.where(lane < NSA_DH, 0.0, sq), axis=-1, keepdims=True)
    ms = jnp.where(lane < NSA_DH, s_lo, s_hi) * (1.0 / NSA_DH)
    return x * lax.rsqrt(ms + EPS) * gain2


def _gelu_tanh(x):
    return 0.5 * x * (1.0 + jnp.tanh(math.sqrt(2.0 / math.pi) * (x + 0.044715 * (x * x * x))))


def _prep_kernel(kv_ref, kc_ref, vc_ref, gsel_ref, gwin_ref, gcmp_ref, pos_ref, w1_ref, w1c_ref,
                 b1_ref, w2_ref, kvs_ref, kvw_ref, kvc_ref, *, S, rows):
    lane = lax.broadcasted_iota(jnp.int32, (rows, LANES), 1)

    def body(i, carry):
        r0 = pl.multiple_of(i * rows, rows)
        for src, gain_ref, dst in ((0, gsel_ref, kvs_ref), (2, gwin_ref, kvw_ref)):
            kk = kv_ref[0, pl.ds(r0, rows), src * LANES:(src + 1) * LANES]
            vv = kv_ref[0, pl.ds(r0, rows), (src + 1) * LANES:(src + 2) * LANES]
            kn = _half_rms(kk, gain_ref[...], lane)
            dst[0, 0, pl.ds(r0, rows), :] = jnp.where(
                lane < NSA_DH, kn, pltpu.roll(vv, NSA_DH, 1)).astype(BF16)
            dst[0, 1, pl.ds(r0, rows), :] = jnp.where(
                lane < NSA_DH, pltpu.roll(kn, NSA_DH, 1), vv).astype(BF16)
        return carry

    lax.fori_loop(0, S // rows, body, 0)

    nc = S // CMP_STRIDE
    outs = []
    for j, src_ref in ((0, kc_ref), (1, vc_ref)):
        bias = jnp.dot(pos_ref[j].astype(BF16), w1_ref[j], preferred_element_type=F32)[0:1] + b1_ref[j]
        pj = jnp.dot(src_ref[0].astype(BF16), w1c_ref[j], preferred_element_type=F32)
        per_g = []
        for g in range(NSA_KV):
            first = pj[:, (2 * g) * CMP_HID:(2 * g + 1) * CMP_HID]
            second = pj[:, (2 * g + 1) * CMP_HID:(2 * g + 2) * CMP_HID]
            hid = _gelu_tanh(first + pltpu.roll(second, nc - 1, 0) + bias)
            per_g.append(jnp.dot(hid.astype(BF16), w2_ref[j], preferred_element_type=F32))
        outs.append(per_g)
    for g in range(NSA_KV):
        kc = outs[0][g]
        ms = jnp.sum(kc * kc, axis=-1, keepdims=True) * (1.0 / NSA_DH)
        kn = kc * lax.rsqrt(ms + EPS) * gcmp_ref[...]
        kvc_ref[0, g] = (kn + outs[1][g]).astype(BF16)


def _nsa_prep(proj3, kc2, vc2, gsel, gwin, gcmp, pos8, w1, w1c, b1, w2p):
    B, S, _ = proj3.shape
    nc = S // CMP_STRIDE
    rows = min(512, S)
    full = lambda shp: pl.BlockSpec(shp, lambda b: (0,) * len(shp))
    kv_sds = jax.ShapeDtypeStruct((B, NSA_KV, S, LANES), BF16)
    return pl.pallas_call(
        functools.partial(_prep_kernel, S=S, rows=rows),
        grid=(B,),
        in_specs=[
            pl.BlockSpec((1, S, 4 * NSA_KVW), lambda b: (b, 0, P_KV // (4 * NSA_KVW))),
            pl.BlockSpec((1, nc, CMP_STRIDE * NSA_KVW), lambda b: (b, 0, 0)),
            pl.BlockSpec((1, nc, CMP_STRIDE * NSA_KVW), lambda b: (b, 0, 0)),
            full((1, LANES)), full((1, LANES)), full((1, LANES)),
            full(pos8.shape), full(w1.shape), full(w1c.shape), full(b1.shape), full(w2p.shape),
        ],
        out_specs=[
            pl.BlockSpec((1, NSA_KV, S, LANES), lambda b: (b, 0, 0, 0)),
            pl.BlockSpec((1, NSA_KV, S, LANES), lambda b: (b, 0, 0, 0)),
            pl.BlockSpec((1, NSA_KV, nc, LANES), lambda b: (b, 0, 0, 0)),
        ],
        out_shape=[kv_sds, kv_sds, jax.ShapeDtypeStruct((B, NSA_KV, nc, LANES), BF16)],
        compiler_params=_cparams(("parallel",)),
        name="nsa_prep",
    )(proj3, kc2, vc2, gsel, gwin, gcmp, pos8, w1, w1c, b1, w2p)


def _sel_table_geometry():
    last_start = (REL_BUCKETS // 2) * (REL_MAX_DIST / (REL_BUCKETS // 2)) ** (
        (REL_BUCKETS - REL_BUCKETS // 2 - 1) / (REL_BUCKETS - REL_BUCKETS // 2))
    dsat = int(math.ceil((last_start + 16 + SEL_CHUNK) / LANES)) * LANES
    return dsat, (dsat + SEL_CHUNK) // LANES


def _attn_kernel(q_ref, gate_ref, kvc_ref, kvs_ref, kvw_ref, tabc_ref, tabw_ref, tabs_ref, ovl_ref,
                 exp_ref, qg_ref, o_ref, m_ref, l_ref, acc_ref, *, nsel, dsat):
    qi = pl.program_id(2)
    QB = Q_BLOCK
    R = NSA_HPG * QB
    nt = (((1,), (1,)), ((), ()))

    xq = q_ref[0]
    q4 = jnp.concatenate([xq[:, h * NSA_DH:(h + 1) * NSA_DH] for h in range(NSA_HPG)], axis=0)
    qn = _rms_rows(q4, qg_ref[...]) * (NSA_DH ** -0.5)
    qp = jnp.concatenate([qn, jnp.zeros_like(qn)], axis=1).astype(BF16)

    kvc = kvc_ref[0, 0]
    s = lax.dot_general(qp, kvc, nt, preferred_element_type=F32) + tabc_ref[0, 0]
    mx = jnp.max(s, axis=-1, keepdims=True)
    e = jnp.exp(s - mx)
    p = jnp.where(s > 0.5 * NEG_BIG, e / jnp.sum(e, axis=-1, keepdims=True), 0.0)
    o_c = jnp.dot(p.astype(BF16), kvc, preferred_element_type=F32)

    psum = p[0:QB] + p[QB:2 * QB] + p[2 * QB:3 * QB] + p[3 * QB:4 * QB]
    p_hi = psum.astype(BF16)
    p_lo = (psum - p_hi.astype(F32)).astype(BF16)
    imp = (jnp.dot(p_hi, ovl_ref[...], preferred_element_type=F32)
           + jnp.dot(p_lo, ovl_ref[...], preferred_element_type=F32))
    t = qi * QB + lax.broadcasted_iota(jnp.int32, (QB, nsel), 0)
    jcol = lax.broadcasted_iota(jnp.int32, (QB, nsel), 1)
    jcur = t // SEL_BLOCK
    forced = (jcol == 0) | (jcol == jcur) | (jcol == jcur - 1)
    work = jnp.where(jcol * SEL_BLOCK <= t, imp + jnp.where(forced, FORCE_BONUS, 0.0), -1.0)
    jf = jcol.astype(F32)
    sel = jnp.zeros((QB, nsel), F32)
    for _ in range(min(SEL_TOPK, nsel)):
        best = jnp.max(work, axis=-1, keepdims=True)
        first = jnp.min(jnp.where(work == best, jf, float(nsel)), axis=-1, keepdims=True)
        hit = jf == first
        sel = jnp.where(hit, 1.0, sel)
        work = jnp.where(hit, -3.0, work)
    selneg = jnp.where(sel > 0.5, 0.0, NEG_BIG).astype(BF16)

    m_ref[...] = jnp.full_like(m_ref, NEG_BIG)
    l_ref[...] = jnp.zeros_like(l_ref)
    acc_ref[...] = jnp.zeros_like(acc_ref)
    nslab = SEL_CHUNK // LANES

    def sel_chunk(c, carry):
        k0 = pl.multiple_of(c * SEL_CHUNK, SEL_CHUNK)
        kv = kvs_ref[0, 0, pl.ds(k0, SEL_CHUNK), :]
        sc = lax.dot_general(qp, kv, nt, preferred_element_type=F32)
        delta = qi * QB - c * SEL_CHUNK
        w0 = (dsat - jnp.minimum(delta, dsat)) // LANES
        bias = jnp.concatenate([tabs_ref[0, w0 + u] for u in range(nslab)], axis=1)
        neg = jnp.dot(selneg, exp_ref[c], preferred_element_type=F32)
        sc = sc + bias + jnp.concatenate([neg] * NSA_HPG, axis=0)
        m_old = m_ref[...]
        m_new = jnp.maximum(m_old, jnp.max(sc, axis=-1, keepdims=True))
        alpha = jnp.exp(m_old - m_new)
        pe = jnp.exp(sc - m_new)
        l_ref[...] = alpha * l_ref[...] + jnp.sum(pe, axis=-1, keepdims=True)
        acc_ref[...] = alpha * acc_ref[...] + jnp.dot(pe.astype(BF16), kv, preferred_element_type=F32)
        m_ref[...] = m_new
        return carry

    lax.fori_loop(0, (qi * QB + QB - 1) // SEL_CHUNK + 1, sel_chunk, 0)
    o_s = acc_ref[...] / l_ref[...]

    tiles, kvs = [], []
    for r in range(WIN_BLOCKS):
        kb = qi - (WIN_BLOCKS - 1) + r
        k0 = pl.multiple_of(jnp.maximum(kb, 0) * QB, QB)
        kv = kvw_ref[0, 0, pl.ds(k0, QB), :]
        kvs.append(kv)
        sw = lax.dot_general(qp, kv, nt, preferred_element_type=F32)
        tiles.append(sw + jnp.where(kb >= 0, 0.0, NEG_BIG))
    sw = jnp.concatenate(tiles, axis=1) + tabw_ref[0, 0]
    mw = jnp.max(sw, axis=-1, keepdims=True)
    ew = jnp.exp(sw - mw)
    lw = jnp.sum(ew, axis=-1, keepdims=True)
    ewb = ew.astype(BF16)
    o_w = jnp.dot(ewb[:, 0:QB], kvs[0], preferred_element_type=F32)
    for r in range(1, WIN_BLOCKS):
        o_w = o_w + jnp.dot(ewb[:, r * QB:(r + 1) * QB], kvs[r], preferred_element_type=F32)
    o_w = o_w / lw

    gt = jax.nn.sigmoid(gate_ref[0])
    comb = []
    for h in range(NSA_HPG):
        rs = slice(h * QB, (h + 1) * QB)
        comb.append(gt[:, 3 * h:3 * h + 1] * o_c[rs] + gt[:, 3 * h + 1:3 * h + 2] * o_s[rs]
                    + gt[:, 3 * h + 2:3 * h + 3] * o_w[rs])
    lane = lax.broadcasted_iota(jnp.int32, (QB, LANES), 1)
    for u in range(NSA_HPG // 2):
        o_ref[0, :, u * LANES:(u + 1) * LANES] = jnp.where(
            lane < NSA_DH, pltpu.roll(comb[2 * u], NSA_DH, 1), comb[2 * u + 1])


def _nsa_attn(proj3, kvc, kvs, kvw, tabc, tabw, tabs, ovl, expand, qgain, dsat):
    B, S, _ = proj3.shape
    nq = S // Q_BLOCK
    nc = S // CMP_STRIDE
    nsel = S // SEL_BLOCK
    R = NSA_HPG * Q_BLOCK
    gw = NSA_HPG * NSA_DH
    return pl.pallas_call(
        functools.partial(_attn_kernel, nsel=nsel, dsat=dsat),
        grid=(B, NSA_KV, nq),
        in_specs=[
            pl.BlockSpec((1, Q_BLOCK, gw), lambda b, g, i: (b, i, P_NQ // gw + g)),
            pl.BlockSpec((1, Q_BLOCK, LANES), lambda b, g, i: (b, i, P_GATE // LANES + g)),
            pl.BlockSpec((1, 1, nc, LANES), lambda b, g, i: (b, g, 0, 0)),
            pl.BlockSpec((1, 1, S, LANES), lambda b, g, i: (b, g, 0, 0)),
            pl.BlockSpec((1, 1, S, LANES), lambda b, g, i: (b, g, 0, 0)),
            pl.BlockSpec((1, 1, R, nc), lambda b, g, i: (g, i, 0, 0)),
            pl.BlockSpec((1, 1, R, WIN_BLOCKS * Q_BLOCK), lambda b, g, i: (g, 0, 0, 0)),
            pl.BlockSpec((1,) + tabs.shape[1:], lambda b, g, i: (g, 0, 0, 0)),
            pl.BlockSpec(ovl.shape, lambda b, g, i: (0, 0)),
            pl.BlockSpec(expand.shape, lambda b, g, i: (0, 0, 0)),
            pl.BlockSpec((1, NSA_DH), lambda b, g, i: (0, 0)),
        ],
        out_specs=pl.BlockSpec((1, Q_BLOCK, gw), lambda b, g, i: (b, i, g)),
        out_shape=jax.ShapeDtypeStruct((B, S, NSA_Q), F32),
        scratch_shapes=[pltpu.VMEM((R, 1), F32), pltpu.VMEM((R, 1), F32), pltpu.VMEM((R, LANES), F32)],
        compiler_params=_cparams(("parallel", "parallel", "arbitrary")),
        name="nsa_attn",
    )(proj3, proj3, kvc, kvs, kvw, tabc, tabw, tabs, ovl, expand, qgain)


def _rel_bucket(d):
    max_exact = REL_BUCKETS // 2
    d = jnp.maximum(d, 0)
    df = jnp.maximum(d, 1).astype(F32)
    large = max_exact + (jnp.log(df / max_exact) / math.log(REL_MAX_DIST / max_exact)
                         * (REL_BUCKETS - max_exact)).astype(jnp.int32)
    return jnp.where(d < max_exact, d, jnp.minimum(large, REL_BUCKETS - 1))


def _bias_kernel(rb_ref, bkt_ref, o_ref, *, rows):
    g = pl.program_id(0)
    step = 32
    for r0 in range(0, rows, step):
        bkt = bkt_ref[0, r0:r0 + step, :]
        acc = [jnp.full(bkt.shape, NEG_BIG, F32) for _ in range(NSA_HPG)]
        for b in range(REL_BUCKETS):
            hit = bkt == b
            for h in range(NSA_HPG):
                acc[h] = jnp.where(hit, rb_ref[b, g * NSA_HPG + h], acc[h])
        for h in range(NSA_HPG):
            o_ref[0, 0, h * rows + r0:h * rows + r0 + step, :] = acc[h]


def _bias_table(rel_bias, bucket):
    nt, rows, cols = bucket.shape
    return pl.pallas_call(
        functools.partial(_bias_kernel, rows=rows),
        grid=(NSA_KV, nt),
        in_specs=[
            pl.BlockSpec(memory_space=pltpu.SMEM),
            pl.BlockSpec((1, rows, cols), lambda g, t: (t, 0, 0)),
        ],
        out_specs=pl.BlockSpec((1, 1, NSA_HPG * rows, cols), lambda g, t: (g, t, 0, 0)),
        out_shape=jax.ShapeDtypeStruct((NSA_KV, nt, NSA_HPG * rows, cols), F32),
        compiler_params=_cparams(("parallel", "parallel")),
        name="bias_table",
    )(rel_bias, bucket)


def _bias_tables(rel_bias, S, dsat, nslab_tab):
    nq = S // Q_BLOCK
    nc = S // CMP_STRIDE
    n_cmp = (S - CMP_BLOCK) // CMP_STRIDE + 1
    ar = lambda n: jnp.arange(n, dtype=jnp.int32)
    bucket = lambda d, valid: jnp.where(valid, _rel_bucket(d), -1)
    i = ar(Q_BLOCK)[None, :, None]
    c = ar(nc)[None, None, :]
    d_c = ar(nq)[:, None, None] * Q_BLOCK + i - (c * CMP_STRIDE + CMP_BLOCK - 1)
    tabc = _bias_table(rel_bias, bucket(d_c, (d_c >= 0) & (c < n_cmp)))
    d_w = i + WINDOW - ar(WIN_BLOCKS * Q_BLOCK)[None, None, :]
    tabw = _bias_table(rel_bias, bucket(d_w, (d_w >= 0) & (d_w < WINDOW)))
    d_s = i - (ar(nslab_tab)[:, None, None] * LANES + ar(LANES)[None, None, :]) + dsat
    tabs = _bias_table(rel_bias, bucket(d_s, d_s >= 0))
    return tabc, tabw, tabs


def _static_mats(S):
    nc = S // CMP_STRIDE
    nsel = S // SEL_BLOCK
    n_cmp = (S - CMP_BLOCK) // CMP_STRIDE + 1
    c = np.arange(nc)
    sel_start = np.arange(nsel) * SEL_BLOCK
    ovl = ((c[:, None] * CMP_STRIDE < sel_start[None, :] + SEL_BLOCK)
           & (c[:, None] * CMP_STRIDE + CMP_BLOCK - 1 >= sel_start[None, :]) & (c[:, None] < n_cmp))
    key = np.arange(S)
    expand = (np.arange(nsel)[:, None] == key[None, :] // SEL_BLOCK)
    nch = max(S // SEL_CHUNK, 1)
    expand = expand.reshape(nsel, nch, S // nch).transpose(1, 0, 2)
    return jnp.asarray(ovl, BF16), jnp.asarray(expand, BF16)


def _layer_params(l, lower_bounds, norm_mix, w_in, hg_out_gain, nsa_qk_gain, cmp_pos, cmp_w1, cmp_b1,
                  cmp_w2, w_branch_a, w_branch_b, w_out, norm_mlp, w_up, w_down):
    splits = (HG_K, HG_K, HG_V, HG_V, NSA_Q, NSA_KVW, NSA_KVW, NSA_KVW, NSA_KVW, NSA_KVW, NSA_KVW,
              3 * NSA_HEADS, D_MODEL, D_MODEL)
    offs = np.concatenate([[0], np.cumsum(splits)])
    col = lambda k: w_in[l][:, offs[k]:offs[k + 1]]
    gate_w = col(11)
    zpad = jnp.zeros((D_MODEL, LANES - 3 * NSA_HPG), F32)
    gate_cols = []
    for g in range(NSA_KV):
        gate_cols += [gate_w[:, g * 3 * NSA_HPG:(g + 1) * 3 * NSA_HPG], zpad]
    w_main = jnp.concatenate([col(0), col(1), col(2), col(3), col(12), col(13),
                              col(7), col(8), col(9), col(10), col(4)] + gate_cols, axis=1).astype(BF16)
    w_cmp = jnp.concatenate([col(5), col(6)], axis=1).astype(BF16)

    lb = lower_bounds[l]
    lbp = jnp.stack([jnp.log(lb), jnp.log1p(-lb), 1.0 - lb])

    w1 = cmp_w1[l].reshape(2, 2, CMP_STRIDE, NSA_DH, CMP_HID)
    w1c = jnp.zeros((2, CMP_STRIDE, NSA_KV, NSA_DH, NSA_KV, 2, CMP_HID), F32)
    for g in range(NSA_KV):
        w1c = w1c.at[:, :, g, :, g, :, :].set(w1.transpose(0, 2, 3, 1, 4))
    w1c = w1c.reshape(2, CMP_STRIDE * NSA_KVW, NSA_KV * 2 * CMP_HID).astype(BF16)
    pos8 = jnp.zeros((2, 8, CMP_BLOCK * NSA_DH), F32).at[:, 0, :].set(cmp_pos[l].reshape(2, -1))
    w2p = jnp.zeros((2, CMP_HID, LANES), F32)
    w2p = w2p.at[0, :, :NSA_DH].set(cmp_w2[l, 0]).at[1, :, NSA_DH:].set(cmp_w2[l, 1]).astype(BF16)
    qk = nsa_qk_gain[l]
    z64 = jnp.zeros((NSA_DH,), F32)
    return dict(
        norm_mix=norm_mix[l][None, :], w_main=w_main, w_cmp=w_cmp, lbp=lbp,
        hg_gain=hg_out_gain[l][None, :],
        gsel=jnp.concatenate([qk[2], qk[2]])[None, :], gwin=jnp.concatenate([qk[3], qk[3]])[None, :],
        gcmp=jnp.concatenate([qk[1], z64])[None, :], qgain=qk[0][None, :],
        pos8=pos8, w1=cmp_w1[l].astype(BF16), w1c=w1c, b1=cmp_b1[l][:, None, :], w2p=w2p,
        wa=w_branch_a[l].astype(BF16), wb=w_branch_b[l].astype(BF16), wo=w_out[l].astype(BF16),
        norm_mlp=norm_mlp[l][None, :], wu=w_up[l].astype(BF16), wd=w_down[l].astype(BF16))


def kernel(x, rel_bias, hg_lb_logits, norm_mix, w_in, hg_out_gain, nsa_qk_gain, cmp_pos, cmp_w1, cmp_b1, cmp_w2, w_branch_a, w_branch_b, w_out, norm_mlp, w_up, w_down):
    B, S, D = x.shape
    T = B * S
    depth = w_in.shape[0]
    assert D == D_MODEL and S % SEL_CHUNK == 0 and S >= WINDOW
    lb_cum = jnp.cumsum(jax.nn.softmax(hg_lb_logits.astype(F32), axis=0), axis=0)
    lower_bounds = lb_cum - lb_cum[0:1]

    dsat, nslab_tab = _sel_table_geometry()
    tabc, tabw, tabs = _bias_tables(rel_bias, S, dsat, nslab_tab)
    ovl, expand = _static_mats(S)
    nc = S // CMP_STRIDE
    tm = min(1024, T)

    x2 = x.reshape(T, D)
    for l in range(depth):
        p = _layer_params(l, lower_bounds, norm_mix, w_in, hg_out_gain, nsa_qk_gain, cmp_pos, cmp_w1,
                          cmp_b1, cmp_w2, w_branch_a, w_branch_b, w_out, norm_mlp, w_up, w_down)
        proj, kc, vc = _rms_proj(x2, p["norm_mix"], p["w_main"], p["w_cmp"], tm, P_COLS // 7)
        proj3 = proj.reshape(B, S, P_COLS)
        kc2 = kc.reshape(B, nc, CMP_STRIDE * NSA_KVW)
        vc2 = vc.reshape(B, nc, CMP_STRIDE * NSA_KVW)
        o_a = _hgrn(proj3, p["lbp"], p["hg_gain"])
        kvs, kvw, kvc = _nsa_prep(proj3, kc2, vc2, p["gsel"], p["gwin"], p["gcmp"], p["pos8"], p["w1"],
                                  p["w1c"], p["b1"], p["w2p"])
        o_b = _nsa_attn(proj3, kvc, kvs, kvw, tabc, tabw, tabs, ovl, expand, p["qgain"], dsat)
        x2 = _merge(x2, o_a.reshape(T, HG_V), o_b.reshape(T, NSA_Q), proj, p["wa"], p["wb"], p["wo"],
                    min(512, T))
        x2 = _mlp(x2, p["norm_mlp"], p["wu"], p["wd"], tm, 1024)
    return x2.reshape(B, S, D)
```

```python
import functools
import math

import numpy as np
import jax
import jax.numpy as jnp
from jax import lax
from jax.experimental import pallas as pl
from jax.experimental.pallas import tpu as pltpu

F32 = jnp.float32
BF16 = jnp.bfloat16

D_MODEL = 1024
HG_HEADS = 4
HG_DK = 128
HG_DV = 128
HG_K = HG_HEADS * HG_DK
HG_V = HG_HEADS * HG_DV
NSA_HEADS = 8
NSA_KV = 2
NSA_DH = 64
NSA_HPG = NSA_HEADS // NSA_KV
NSA_Q = NSA_HEADS * NSA_DH
NSA_KVW = NSA_KV * NSA_DH
CMP_BLOCK = 32
CMP_STRIDE = 16
CMP_HID = 128
SEL_BLOCK = 64
SEL_TOPK = 8
WINDOW = 512
FORCE_BONUS = 1000.0
REL_BUCKETS = 32
REL_MAX_DIST = 1024
D_FF = 4 * D_MODEL
EPS = 1e-6
NEG_BIG = -1e30

LANES = 128
VMEM_LIMIT = 56 * 1024 * 1024

P_HG = 0
P_GAB = 4 * HG_K
P_KV = P_GAB + 2 * D_MODEL
P_NQ = P_KV + 4 * NSA_KVW
P_GATE = P_NQ + NSA_Q
P_COLS = P_GATE + NSA_KV * LANES

HG_CHUNK = 64
HG_TS = 512
Q_BLOCK = 128
SEL_CHUNK = 512
WIN_BLOCKS = WINDOW // Q_BLOCK + 1
SEL_SHIFT = int(math.log2(SEL_BLOCK))
SEL_ROWS = LANES - NSA_DH


def _cparams(sem):
    return pltpu.CompilerParams(dimension_semantics=sem, vmem_limit_bytes=VMEM_LIMIT)


def _rms_rows(x, gain):
    ms = jnp.mean(x * x, axis=-1, keepdims=True)
    return x * lax.rsqrt(ms + EPS) * gain


def _proj_kernel(x_ref, g_ref, w_ref, wc_ref, o_ref, kc_ref, vc_ref, h_ref):
    @pl.when(pl.program_id(1) == 0)
    def _():
        h = _rms_rows(x_ref[...], g_ref[...]).astype(BF16)
        h_ref[...] = h
        c = jnp.dot(h, wc_ref[...], preferred_element_type=F32)
        kc_ref[...] = c[:, :NSA_KVW]
        vc_ref[...] = c[:, NSA_KVW:]

    o_ref[...] = jnp.dot(h_ref[...], w_ref[...], preferred_element_type=F32)


def _rms_proj(x2, gain, w, wc, tm, tn):
    T, D = x2.shape
    N = w.shape[1]
    cmp_sds = jax.ShapeDtypeStruct((T, NSA_KVW), F32)
    return pl.pallas_call(
        _proj_kernel,
        grid=(T // tm, N // tn),
        in_specs=[
            pl.BlockSpec((tm, D), lambda i, j: (i, 0)),
            pl.BlockSpec((1, D), lambda i, j: (0, 0)),
            pl.BlockSpec((D, tn), lambda i, j: (0, j)),
            pl.BlockSpec((D, 2 * NSA_KVW), lambda i, j: (0, 0)),
        ],
        out_specs=[
            pl.BlockSpec((tm, tn), lambda i, j: (i, j)),
            pl.BlockSpec((tm, NSA_KVW), lambda i, j: (i, 0)),
            pl.BlockSpec((tm, NSA_KVW), lambda i, j: (i, 0)),
        ],
        out_shape=[jax.ShapeDtypeStruct((T, N), F32), cmp_sds, cmp_sds],
        scratch_shapes=[pltpu.VMEM((tm, D), BF16)],
        compiler_params=_cparams(("parallel", "arbitrary")),
        name="rms_proj",
    )(x2, gain, w, wc)


def _mlp_kernel(x_ref, g_ref, wu_ref, wd_ref, o_ref, h_ref):
    f = pl.program_id(1)

    @pl.when(f == 0)
    def _():
        x = x_ref[...]
        h_ref[...] = _rms_rows(x, g_ref[...]).astype(BF16)
        o_ref[...] = x

    u = jnp.dot(h_ref[...], wu_ref[...], preferred_element_type=F32)
    u = jnp.square(jnp.maximum(u, 0.0)).astype(BF16)
    o_ref[...] += jnp.dot(u, wd_ref[...], preferred_element_type=F32)


def _mlp(x2, gain, wu, wd, tm, tf):
    T, D = x2.shape
    FF = wu.shape[1]
    return pl.pallas_call(
        _mlp_kernel,
        grid=(T // tm, FF // tf),
        in_specs=[
            pl.BlockSpec((tm, D), lambda i, f: (i, 0)),
            pl.BlockSpec((1, D), lambda i, f: (0, 0)),
            pl.BlockSpec((D, tf), lambda i, f: (0, f)),
            pl.BlockSpec((tf, D), lambda i, f: (f, 0)),
        ],
        out_specs=pl.BlockSpec((tm, D), lambda i, f: (i, 0)),
        out_shape=jax.ShapeDtypeStruct((T, D), F32),
        scratch_shapes=[pltpu.VMEM((tm, D), BF16)],
        compiler_params=_cparams(("parallel", "arbitrary")),
        name="mlp",
    )(x2, gain, wu, wd)


def _merge_kernel(x_ref, oa_ref, ob_ref, ga_ref, gb_ref, wa_ref, wb_ref, wo_ref, o_ref):
    ya = jnp.dot(oa_ref[...].astype(BF16), wa_ref[...], preferred_element_type=F32)
    yb = jnp.dot(ob_ref[...].astype(BF16), wb_ref[...], preferred_element_type=F32)
    mixed = jax.nn.sigmoid(ga_ref[...]) * ya + jax.nn.sigmoid(gb_ref[...]) * yb
    o_ref[...] = x_ref[...] + jnp.dot(mixed.astype(BF16), wo_ref[...], preferred_element_type=F32)


def _merge(x2, oa, ob, proj, wa, wb, wo, tm):
    T, D = x2.shape
    ga_blk = P_GAB // D
    return pl.pallas_call(
        _merge_kernel,
        grid=(T // tm,),
        in_specs=[
            pl.BlockSpec((tm, D), lambda i: (i, 0)),
            pl.BlockSpec((tm, HG_V), lambda i: (i, 0)),
            pl.BlockSpec((tm, NSA_Q), lambda i: (i, 0)),
            pl.BlockSpec((tm, D), lambda i: (i, ga_blk)),
            pl.BlockSpec((tm, D), lambda i: (i, ga_blk + 1)),
            pl.BlockSpec((HG_V, D), lambda i: (0, 0)),
            pl.BlockSpec((NSA_Q, D), lambda i: (0, 0)),
            pl.BlockSpec((D, D), lambda i: (0, 0)),
        ],
        out_specs=pl.BlockSpec((tm, D), lambda i: (i, 0)),
        out_shape=jax.ShapeDtypeStruct((T, D), F32),
        compiler_params=_cparams(("parallel",)),
        name="merge",
    )(x2, oa, ob, proj, proj, wa, wb, wo)


def _hgrn_consts(C):
    L = int(math.log2(C))
    idx = np.arange(C)
    mats = [(idx[None, :] <= idx[:, None]),
            (idx[None, :] > idx[:, None])]
    bmask = []
    for l in range(L):
        m = 1 << l
        r = (idx & ~(2 * m - 1)) + m
        lo = np.minimum(idx, r)[:, None]
        hi = np.maximum(idx, r)[:, None]
        mats.append((idx[None, :] > lo) & (idx[None, :] <= hi))
        blk = idx >> (l + 1)
        bmask.append(blk[:, None] == blk[None, :])
    bmask.append(idx[:, None] == idx[None, :])
    seg = np.concatenate(mats, axis=0).astype(np.float32)
    return seg, np.stack(bmask).astype(np.float32), L


def _hgrn_kernel(q_ref, f_ref, i_ref, g_ref, lbp_ref, gain_ref, seg_ref, bm_ref, o_ref, st_ref,
                 *, C, L, nchunk):
    @pl.when(pl.program_id(1) == 0)
    def _():
        st_ref[...] = jnp.zeros_like(st_ref)

    rowi = lax.broadcasted_iota(jnp.int32, (C, HG_DK), 0)
    nt = (((1,), (1,)), ((), ()))
    tn = (((0,), (0,)), ((), ()))

    def chunk(c, carry):
        r0 = pl.multiple_of(c * C, C)
        for h in range(HG_HEADS):
            cs = slice(h * HG_DK, (h + 1) * HG_DK)
            q = q_ref[0, pl.ds(r0, C), cs]
            z = f_ref[0, pl.ds(r0, C), cs]
            v = i_ref[0, pl.ds(r0, C), cs]
            gg = g_ref[0, pl.ds(r0, C), cs]
            log_lb = lbp_ref[0:1, cs]
            log1m_lb = lbp_ref[1:2, cs]
            one_m_lb = lbp_ref[2:3, cs]
            y = log1m_lb + (jnp.minimum(z, 0.0) - jnp.log1p(jnp.exp(-jnp.abs(z))))
            logf = jnp.maximum(log_lb, y) + jnp.log1p(jnp.exp(-jnp.abs(log_lb - y)))
            k = one_m_lb * jax.nn.sigmoid(-z)
            g_hi = logf.astype(BF16)
            g_lo = (logf - g_hi.astype(F32)).astype(BF16)
            seg = jnp.dot(seg_ref[...], jnp.concatenate([g_hi, g_lo], axis=1),
                          preferred_element_type=F32)
            seg = seg[:, :HG_DK] + seg[:, HG_DK:]
            b = seg[0:C]
            brev = seg[C:2 * C]
            qd = (q * jnp.exp(b)).astype(BF16)
            kd = (k * jnp.exp(brev)).astype(BF16)
            vb = v.astype(BF16)
            st = st_ref[h]
            o = lax.dot_general(qd, st.astype(BF16), nt, preferred_element_type=F32)
            s = lax.dot_general(q.astype(BF16), k.astype(BF16), nt,
                                preferred_element_type=F32) * bm_ref[L]
            for l in range(L):
                e = jnp.exp(seg[(2 + l) * C:(3 + l) * C])
                second = (rowi & (1 << l)) != 0
                ql = jnp.where(second, q * e, 0.0).astype(BF16)
                kl = jnp.where(second, 0.0, k * e).astype(BF16)
                s = s + lax.dot_general(ql, kl, nt, preferred_element_type=F32) * bm_ref[l]
            o = o + jnp.dot(s.astype(BF16), vb, preferred_element_type=F32)
            st_ref[h] = (st * jnp.exp(b[C - 1:C, :])
                         + lax.dot_general(vb, kd, tn, preferred_element_type=F32))
            o = _rms_rows(o, gain_ref[...]) * (gg * jax.nn.sigmoid(gg))
            o_ref[0, pl.ds(r0, C), cs] = o
        return carry

    lax.fori_loop(0, nchunk, chunk, 0)


def _hgrn(proj3, lbp, gain, C=HG_CHUNK, ts=HG_TS):
    B, S, _ = proj3.shape
    ts = min(ts, S)
    seg, bm, L = _hgrn_consts(C)
    blk = lambda k: pl.BlockSpec((1, ts, HG_K), lambda b, s: (b, s, P_HG // HG_K + k))
    return pl.pallas_call(
        functools.partial(_hgrn_kernel, C=C, L=L, nchunk=ts // C),
        grid=(B, S // ts),
        in_specs=[
            blk(0), blk(1), blk(2), blk(3),
            pl.BlockSpec((3, HG_K), lambda b, s: (0, 0)),
            pl.BlockSpec((1, HG_DV), lambda b, s: (0, 0)),
            pl.BlockSpec(seg.shape, lambda b, s: (0, 0)),
            pl.BlockSpec(bm.shape, lambda b, s: (0, 0, 0)),
        ],
        out_specs=pl.BlockSpec((1, ts, HG_V), lambda b, s: (b, s, 0)),
        out_shape=jax.ShapeDtypeStruct((B, S, HG_V), F32),
        scratch_shapes=[pltpu.VMEM((HG_HEADS, HG_DV, HG_DK), F32)],
        compiler_params=_cparams(("parallel", "arbitrary")),
        name="hgrn2",
    )(proj3, proj3, proj3, proj3, lbp, gain, jnp.asarray(seg, BF16), jnp.asarray(bm))


def _half_rms(x, gain2, lane):
    sq = x * x
    s_lo = jnp.sum(jnp.where(lane < NSA_DH, sq, 0.0), axis=-1, keepdims=True)
    s_hi = jnp.sum(jnp.where(lane < NSA_DH, 0.0, sq), axis=-1, keepdims=True)
    ms = jnp.where(lane < NSA_DH, s_lo, s_hi) * (1.0 / NSA_DH)
    return x * lax.rsqrt(ms + EPS) * gain2


def _gelu_tanh(x):
    return 0.5 * x * (1.0 + jnp.tanh(math.sqrt(2.0 / math.pi) * (x + 0.044715 * (x * x * x))))


def _prep_kernel(kv_ref, kc_ref, vc_ref, gsel_ref, gwin_ref, gcmp_ref, pos_ref, w1_ref, w1c_ref,
                 b1_ref, w2_ref, ksa_ref, vst_ref, kwa_ref, vwt_ref, kca_ref, vct_ref, *, S, rows):
    lane = lax.broadcasted_iota(jnp.int32, (rows, LANES), 1)
    rowi = lax.broadcasted_iota(jnp.int32, (rows, LANES), 0)
    ones_row = (lax.broadcasted_iota(jnp.int32, (NSA_DH, rows), 0) == 0).astype(F32)
    low = lane < NSA_DH

    def body(i, carry):
        r0 = pl.multiple_of(i * rows, rows)
        blk_hot = jnp.where(lane - NSA_DH == ((r0 + rowi) >> SEL_SHIFT), 1.0, 0.0)
        for src, gain_ref, aug, k_dst, v_dst, vrows in (
                (0, gsel_ref, blk_hot, ksa_ref, vst_ref, rows),
                (2, gwin_ref, jnp.zeros((rows, LANES), F32), kwa_ref, vwt_ref, Q_BLOCK)):
            kk = kv_ref[0, pl.ds(r0, rows), src * LANES:(src + 1) * LANES]
            vv = kv_ref[0, pl.ds(r0, rows), (src + 1) * LANES:(src + 2) * LANES]
            kn = _half_rms(kk, gain_ref[...], lane)
            k_dst[0, 0, pl.ds(r0, rows), :] = jnp.where(low, kn, aug).astype(BF16)
            k_dst[0, 1, pl.ds(r0, rows), :] = jnp.where(low, pltpu.roll(kn, NSA_DH, 1), aug).astype(BF16)
            vt = vv.T
            for g in range(NSA_KV):
                vg = jnp.concatenate([vt[g * NSA_DH:(g + 1) * NSA_DH], ones_row], axis=0).astype(BF16)
                for u in range(rows // vrows):
                    v_dst[0, g, i * (rows // vrows) + u] = vg[:, u * vrows:(u + 1) * vrows]
        return carry

    lax.fori_loop(0, S // rows, body, 0)

    nc = S // CMP_STRIDE
    outs = []
    for j, src_ref in ((0, kc_ref), (1, vc_ref)):
        bias = jnp.dot(pos_ref[j].astype(BF16), w1_ref[j], preferred_element_type=F32)[0:1] + b1_ref[j]
        pj = jnp.dot(src_ref[0].astype(BF16), w1c_ref[j], preferred_element_type=F32)
        per_g = []
        for g in range(NSA_KV):
            first = pj[:, (2 * g) * CMP_HID:(2 * g + 1) * CMP_HID]
            second = pj[:, (2 * g + 1) * CMP_HID:(2 * g + 2) * CMP_HID]
            hid = _gelu_tanh(first + pltpu.roll(second, nc - 1, 0) + bias).astype(BF16)
            if j == 0:
                per_g.append(jnp.dot(hid, w2_ref[0], preferred_element_type=F32))
            else:
                per_g.append(lax.dot_general(w2_ref[1], hid, (((1,), (1,)), ((), ())),
                                             preferred_element_type=F32))
        outs.append(per_g)
    for g in range(NSA_KV):
        kc = outs[0][g]
        ms = jnp.sum(kc * kc, axis=-1, keepdims=True) * (1.0 / NSA_DH)
        kca_ref[0, g] = (kc * lax.rsqrt(ms + EPS) * gcmp_ref[...]).astype(BF16)
        vct_ref[0, g] = outs[1][g].astype(BF16)


def _nsa_prep(proj3, kc2, vc2, gsel, gwin, gcmp, pos8, w1, w1c, b1, w2p):
    B, S, _ = proj3.shape
    nc = S // CMP_STRIDE
    rows = SEL_CHUNK
    full = lambda shp: pl.BlockSpec(shp, lambda b: (0,) * len(shp))
    whole = lambda shp: pl.BlockSpec((1,) + shp, lambda b: (b,) + (0,) * len(shp))
    shapes = [(NSA_KV, S, LANES), (NSA_KV, S // SEL_CHUNK, LANES, SEL_CHUNK),
              (NSA_KV, S, LANES), (NSA_KV, S // Q_BLOCK, LANES, Q_BLOCK),
              (NSA_KV, nc, LANES), (NSA_KV, LANES, nc)]
    return pl.pallas_call(
        functools.partial(_prep_kernel, S=S, rows=rows),
        grid=(B,),
        in_specs=[
            pl.BlockSpec((1, S, 4 * NSA_KVW), lambda b: (b, 0, P_KV // (4 * NSA_KVW))),
            pl.BlockSpec((1, nc, CMP_STRIDE * NSA_KVW), lambda b: (b, 0, 0)),
            pl.BlockSpec((1, nc, CMP_STRIDE * NSA_KVW), lambda b: (b, 0, 0)),
            full((1, LANES)), full((1, LANES)), full((1, LANES)),
            full(pos8.shape), full(w1.shape), full(w1c.shape), full(b1.shape), full(w2p.shape),
        ],
        out_specs=[whole(s) for s in shapes],
        out_shape=[jax.ShapeDtypeStruct((B,) + s, BF16) for s in shapes],
        compiler_params=_cparams(("parallel",)),
        name="nsa_prep",
    )(proj3, kc2, vc2, gsel, gwin, gcmp, pos8, w1, w1c, b1, w2p)


def _sel_table_geometry():
    last_start = (REL_BUCKETS // 2) * (REL_MAX_DIST / (REL_BUCKETS // 2)) ** (
        (REL_BUCKETS - REL_BUCKETS // 2 - 1) / (REL_BUCKETS - REL_BUCKETS // 2))
    dsat = int(math.ceil((last_start + 16 + SEL_CHUNK) / LANES)) * LANES
    return dsat, dsat + SEL_CHUNK


def _attn_kernel(q_ref, gate_ref, kca_ref, vct_ref, ksa_ref, vst_ref, kwa_ref, vwt_ref, tabc_ref,
                 tabw_ref, tabs_ref, ovl_ref, qg_ref, o_ref, m_ref, acc_ref, *, nsel, dsat):
    qi = pl.program_id(2)
    QB = Q_BLOCK
    R = NSA_HPG * QB
    DH = NSA_DH

    xt = q_ref[0].T
    cols = []
    for h in range(NSA_HPG):
        xh = xt[h * DH:(h + 1) * DH]
        ms = jnp.mean(xh * xh, axis=0, keepdims=True)
        cols.append(xh * lax.rsqrt(ms + EPS) * qg_ref[...])
    qt = jnp.concatenate(cols, axis=1).astype(BF16)
    qta = jnp.concatenate([qt, jnp.zeros_like(qt)], axis=0)

    s = jnp.dot(kca_ref[0, 0], qta, preferred_element_type=F32) + tabc_ref[0, 0]
    mx = jnp.max(s, axis=0, keepdims=True)
    e = jnp.exp(s - mx)
    p = jnp.where(s > 0.5 * NEG_BIG, e / jnp.sum(e, axis=0, keepdims=True), 0.0)
    o_c = jnp.dot(vct_ref[0, 0], p.astype(BF16), preferred_element_type=F32)

    psum = p[:, 0:QB] + p[:, QB:2 * QB] + p[:, 2 * QB:3 * QB] + p[:, 3 * QB:4 * QB]
    p_hi = psum.astype(BF16)
    p_lo = (psum - p_hi.astype(F32)).astype(BF16)
    imp = (jnp.dot(ovl_ref[...], p_hi, preferred_element_type=F32)
           + jnp.dot(ovl_ref[...], p_lo, preferred_element_type=F32))
    t = qi * QB + lax.broadcasted_iota(jnp.int32, (SEL_ROWS, QB), 1)
    jrow = lax.broadcasted_iota(jnp.int32, (SEL_ROWS, QB), 0)
    jcur = t >> SEL_SHIFT
    forced = (jrow == 0) | (jrow == jcur) | (jrow == jcur - 1)
    work = jnp.where(jrow * SEL_BLOCK <= t, imp + jnp.where(forced, FORCE_BONUS, 0.0), -1.0)
    work = jnp.where(jrow < nsel, work, -5.0)
    jf = jrow.astype(F32)
    sel = jnp.zeros((SEL_ROWS, QB), F32)
    for _ in range(min(SEL_TOPK, nsel)):
        best = jnp.max(work, axis=0, keepdims=True)
        first = jnp.min(jnp.where(work == best, jf, float(SEL_ROWS)), axis=0, keepdims=True)
        hit = jf == first
        sel = jnp.where(hit, 1.0, sel)
        work = jnp.where(hit, -3.0, work)
    selneg = jnp.where(sel > 0.5, 0.0, NEG_BIG).astype(BF16)
    qts = jnp.concatenate([qt, jnp.concatenate([selneg] * NSA_HPG, axis=1)], axis=0)

    tiles, vts = [], []
    for r in range(WIN_BLOCKS):
        kb = qi - (WIN_BLOCKS - 1) + r
        kbc = jnp.maximum(kb, 0)
        kw = kwa_ref[0, 0, pl.ds(pl.multiple_of(kbc * QB, QB), QB), :]
        vts.append(vwt_ref[0, 0, kbc])
        sw = jnp.dot(kw, qta, preferred_element_type=F32) + tabw_ref[0, 0, r * QB:(r + 1) * QB, :]
        tiles.append(sw + jnp.where(kb >= 0, 0.0, NEG_BIG))
    mw = tiles[0]
    for r in range(1, WIN_BLOCKS):
        mw = jnp.maximum(mw, tiles[r])
    mw = jnp.max(mw, axis=0, keepdims=True)
    o_w = jnp.zeros((LANES, R), F32)
    for r in range(WIN_BLOCKS):
        o_w = o_w + jnp.dot(vts[r], jnp.exp(tiles[r] - mw).astype(BF16), preferred_element_type=F32)

    m_ref[...] = jnp.full_like(m_ref, NEG_BIG)
    acc_ref[...] = jnp.zeros_like(acc_ref)

    def sel_chunk(c, carry):
        k0 = pl.multiple_of(c * SEL_CHUNK, SEL_CHUNK)
        delta = qi * QB - c * SEL_CHUNK
        start = pl.multiple_of(dsat - jnp.minimum(delta, dsat), LANES)
        sc = (jnp.dot(ksa_ref[0, 0, pl.ds(k0, SEL_CHUNK), :], qts, preferred_element_type=F32)
              + tabs_ref[0, 0, pl.ds(start, SEL_CHUNK), :])
        m_old = m_ref[...]
        m_new = jnp.maximum(m_old, jnp.max(sc, axis=0, keepdims=True))
        alpha = jnp.exp(m_old - m_new)
        pe = jnp.exp(sc - m_new).astype(BF16)
        acc_ref[...] = alpha * acc_ref[...] + jnp.dot(vst_ref[0, 0, c], pe, preferred_element_type=F32)
        m_ref[...] = m_new
        return carry

    lax.fori_loop(0, (qi * QB + QB - 1) // SEL_CHUNK + 1, sel_chunk, 0)
    acc = acc_ref[...]

    gt = jax.nn.sigmoid(gate_ref[0]).T
    o_s = acc[0:DH] / acc[DH:DH + 1]
    o_w = o_w[0:DH] / o_w[DH:DH + 1]
    heads = []
    for h in range(NSA_HPG):
        cs = slice(h * QB, (h + 1) * QB)
        heads.append(gt[3 * h:3 * h + 1] * o_c[0:DH, cs] + gt[3 * h + 1:3 * h + 2] * o_s[:, cs]
                     + gt[3 * h + 2:3 * h + 3] * o_w[:, cs])
    o_ref[0] = jnp.concatenate(heads, axis=0).T


def _nsa_attn(proj3, prep, tabc, tabw, tabs, ovl, qgain, dsat):
    B, S, _ = proj3.shape
    ksa, vst, kwa, vwt, kca, vct = prep
    nq = S // Q_BLOCK
    nc = S // CMP_STRIDE
    nsel = S // SEL_BLOCK
    R = NSA_HPG * Q_BLOCK
    gw = NSA_HPG * NSA_DH
    per_bg = lambda a: pl.BlockSpec((1, 1) + a.shape[2:], lambda b, g, i: (b, g) + (0,) * (a.ndim - 2))
    return pl.pallas_call(
        functools.partial(_attn_kernel, nsel=nsel, dsat=dsat),
        grid=(B, NSA_KV, nq),
        in_specs=[
            pl.BlockSpec((1, Q_BLOCK, gw), lambda b, g, i: (b, i, P_NQ // gw + g)),
            pl.BlockSpec((1, Q_BLOCK, LANES), lambda b, g, i: (b, i, P_GATE // LANES + g)),
            per_bg(kca), per_bg(vct), per_bg(ksa), per_bg(vst), per_bg(kwa), per_bg(vwt),
            pl.BlockSpec((1, 1, nc, R), lambda b, g, i: (g, i, 0, 0)),
            pl.BlockSpec((1, 1) + tabw.shape[2:], lambda b, g, i: (g, 0, 0, 0)),
            pl.BlockSpec((1, 1) + tabs.shape[2:], lambda b, g, i: (g, 0, 0, 0)),
            pl.BlockSpec(ovl.shape, lambda b, g, i: (0, 0)),
            pl.BlockSpec(qgain.shape, lambda b, g, i: (0, 0)),
        ],
        out_specs=pl.BlockSpec((1, Q_BLOCK, gw), lambda b, g, i: (b, i, g)),
        out_shape=jax.ShapeDtypeStruct((B, S, NSA_Q), F32),
        scratch_shapes=[pltpu.VMEM((1, R), F32), pltpu.VMEM((LANES, R), F32)],
        compiler_params=_cparams(("parallel", "parallel", "arbitrary")),
        name="nsa_attn",
    )(proj3, proj3, kca, vct, ksa, vst, kwa, vwt, tabc, tabw, tabs, ovl, qgain)


def _rel_bucket(d):
    max_exact = REL_BUCKETS // 2
    d = jnp.maximum(d, 0)
    df = jnp.maximum(d, 1).astype(F32)
    large = max_exact + (jnp.log(df / max_exact) / math.log(REL_MAX_DIST / max_exact)
                         * (REL_BUCKETS - max_exact)).astype(jnp.int32)
    return jnp.where(d < max_exact, d, jnp.minimum(large, REL_BUCKETS - 1))


def _bias_kernel(rb_ref, bkt_ref, o_ref, *, rows):
    g = pl.program_id(0)
    step = 32
    for r0 in range(0, rows, step):
        bkt = bkt_ref[0, r0:r0 + step, :]
        acc = [jnp.full(bkt.shape, NEG_BIG, F32) for _ in range(NSA_HPG)]
        for b in range(REL_BUCKETS):
            hit = bkt == b
            for h in range(NSA_HPG):
                acc[h] = jnp.where(hit, rb_ref[b, g * NSA_HPG + h], acc[h])
        for h in range(NSA_HPG):
            o_ref[0, 0, r0:r0 + step, h * Q_BLOCK:(h + 1) * Q_BLOCK] = acc[h]


def _bias_table(rel_bias, bucket):
    nt, rows, cols = bucket.shape
    return pl.pallas_call(
        functools.partial(_bias_kernel, rows=rows),
        grid=(NSA_KV, nt),
        in_specs=[
            pl.BlockSpec(memory_space=pltpu.SMEM),
            pl.BlockSpec((1, rows, cols), lambda g, t: (t, 0, 0)),
        ],
        out_specs=pl.BlockSpec((1, 1, rows, NSA_HPG * cols), lambda g, t: (g, t, 0, 0)),
        out_shape=jax.ShapeDtypeStruct((NSA_KV, nt, rows, NSA_HPG * cols), F32),
        compiler_params=_cparams(("parallel", "parallel")),
        name="bias_table",
    )(rel_bias, bucket)


def _bias_tables(rel_bias, S, dsat, strip_len):
    nq = S // Q_BLOCK
    nc = S // CMP_STRIDE
    n_cmp = (S - CMP_BLOCK) // CMP_STRIDE + 1
    ar = lambda n: jnp.arange(n, dtype=jnp.int32)
    bucket = lambda d, valid: jnp.where(valid, _rel_bucket(d), -1)
    i = ar(Q_BLOCK)[None, None, :]
    c = ar(nc)[None, :, None]
    d_c = ar(nq)[:, None, None] * Q_BLOCK + i - (c * CMP_STRIDE + CMP_BLOCK - 1)
    tabc = _bias_table(rel_bias, bucket(d_c, (d_c >= 0) & (c < n_cmp)))
    d_w = i + WINDOW - ar(WIN_BLOCKS * Q_BLOCK)[None, :, None]
    tabw = _bias_table(rel_bias, bucket(d_w, (d_w >= 0) & (d_w < WINDOW)))
    d_s = i - ar(strip_len)[None, :, None] + dsat
    tabs = _bias_table(rel_bias, bucket(d_s, d_s >= 0))
    return tabc, tabw, tabs


def _overlap_matrix(S):
    nc = S // CMP_STRIDE
    nsel = S // SEL_BLOCK
    n_cmp = (S - CMP_BLOCK) // CMP_STRIDE + 1
    c = np.arange(nc)[None, :]
    sel_start = (np.arange(SEL_ROWS) * SEL_BLOCK)[:, None]
    ovl = ((c * CMP_STRIDE < sel_start + SEL_BLOCK) & (c * CMP_STRIDE + CMP_BLOCK - 1 >= sel_start)
           & (c < n_cmp) & (np.arange(SEL_ROWS)[:, None] < nsel))
    return jnp.asarray(ovl, BF16)


def _layer_params(l, lower_bounds, norm_mix, w_in, hg_out_gain, nsa_qk_gain, cmp_pos, cmp_w1, cmp_b1,
                  cmp_w2, w_branch_a, w_branch_b, w_out, norm_mlp, w_up, w_down):
    splits = (HG_K, HG_K, HG_V, HG_V, NSA_Q, NSA_KVW, NSA_KVW, NSA_KVW, NSA_KVW, NSA_KVW, NSA_KVW,
              3 * NSA_HEADS, D_MODEL, D_MODEL)
    offs = np.concatenate([[0], np.cumsum(splits)])
    col = lambda k: w_in[l][:, offs[k]:offs[k + 1]]
    gate_w = col(11)
    zpad = jnp.zeros((D_MODEL, LANES - 3 * NSA_HPG), F32)
    gate_cols = []
    for g in range(NSA_KV):
        gate_cols += [gate_w[:, g * 3 * NSA_HPG:(g + 1) * 3 * NSA_HPG], zpad]
    w_main = jnp.concatenate([col(0), col(1), col(2), col(3), col(12), col(13),
                              col(7), col(8), col(9), col(10), col(4)] + gate_cols, axis=1).astype(BF16)
    w_cmp = jnp.concatenate([col(5), col(6)], axis=1).astype(BF16)

    lb = lower_bounds[l]
    lbp = jnp.stack([jnp.log(lb), jnp.log1p(-lb), 1.0 - lb])

    w1 = cmp_w1[l].reshape(2, 2, CMP_STRIDE, NSA_DH, CMP_HID)
    w1c = jnp.zeros((2, CMP_STRIDE, NSA_KV, NSA_DH, NSA_KV, 2, CMP_HID), F32)
    for g in range(NSA_KV):
        w1c = w1c.at[:, :, g, :, g, :, :].set(w1.transpose(0, 2, 3, 1, 4))
    w1c = w1c.reshape(2, CMP_STRIDE * NSA_KVW, NSA_KV * 2 * CMP_HID).astype(BF16)
    pos8 = jnp.zeros((2, 8, CMP_BLOCK * NSA_DH), F32).at[:, 0, :].set(cmp_pos[l].reshape(2, -1))
    w2p = jnp.zeros((2, CMP_HID, LANES), F32)
    w2p = w2p.at[0, :, :NSA_DH].set(cmp_w2[l, 0]).at[1, :NSA_DH, :].set(cmp_w2[l, 1].T).astype(BF16)
    qk = nsa_qk_gain[l]
    z64 = jnp.zeros((NSA_DH,), F32)
    return dict(
        norm_mix=norm_mix[l][None, :], w_main=w_main, w_cmp=w_cmp, lbp=lbp,
        hg_gain=hg_out_gain[l][None, :],
        gsel=jnp.concatenate([qk[2], qk[2]])[None, :], gwin=jnp.concatenate([qk[3], qk[3]])[None, :],
        gcmp=jnp.concatenate([qk[1], z64])[None, :],
        qgain=jnp.broadcast_to((qk[0] * NSA_DH ** -0.5)[:, None], (NSA_DH, Q_BLOCK)),
        pos8=pos8, w1=cmp_w1[l].astype(BF16), w1c=w1c, b1=cmp_b1[l][:, None, :], w2p=w2p,
        wa=w_branch_a[l].astype(BF16), wb=w_branch_b[l].astype(BF16), wo=w_out[l].astype(BF16),
        norm_mlp=norm_mlp[l][None, :], wu=w_up[l].astype(BF16), wd=w_down[l].astype(BF16))


def kernel(x, rel_bias, hg_lb_logits, norm_mix, w_in, hg_out_gain, nsa_qk_gain, cmp_pos, cmp_w1, cmp_b1, cmp_w2, w_branch_a, w_branch_b, w_out, norm_mlp, w_up, w_down):
    B, S, D = x.shape
    T = B * S
    depth = w_in.shape[0]
    assert D == D_MODEL and S % SEL_CHUNK == 0 and S >= WINDOW and S // SEL_BLOCK <= SEL_ROWS
    lb_cum = jnp.cumsum(jax.nn.softmax(hg_lb_logits.astype(F32), axis=0), axis=0)
    lower_bounds = lb_cum - lb_cum[0:1]

    dsat, strip_len = _sel_table_geometry()
    tabc, tabw, tabs = _bias_tables(rel_bias, S, dsat, strip_len)
    ovl = _overlap_matrix(S)
    nc = S // CMP_STRIDE
    tm = min(1024, T)

    x2 = x.reshape(T, D)
    for l in range(depth):
        p = _layer_params(l, lower_bounds, norm_mix, w_in, hg_out_gain, nsa_qk_gain, cmp_pos, cmp_w1,
                          cmp_b1, cmp_w2, w_branch_a, w_branch_b, w_out, norm_mlp, w_up, w_down)
        proj, kc, vc = _rms_proj(x2, p["norm_mix"], p["w_main"], p["w_cmp"], tm, P_COLS // 7)
        proj3 = proj.reshape(B, S, P_COLS)
        kc2 = kc.reshape(B, nc, CMP_STRIDE * NSA_KVW)
        vc2 = vc.reshape(B, nc, CMP_STRIDE * NSA_KVW)
        o_a = _hgrn(proj3, p["lbp"], p["hg_gain"])
        prep = _nsa_prep(proj3, kc2, vc2, p["gsel"], p["gwin"], p["gcmp"], p["pos8"], p["w1"],
                         p["w1c"], p["b1"], p["w2p"])
        o_b = _nsa_attn(proj3, prep, tabc, tabw, tabs, ovl, p["qgain"], dsat)
        x2 = _merge(x2, o_a.reshape(T, HG_V), o_b.reshape(T, NSA_Q), proj, p["wa"], p["wb"], p["wo"],
                    min(512, T))
        x2 = _mlp(x2, p["norm_mlp"], p["wu"], p["wd"], tm, 1024)
    return x2.reshape(B, S, D)
```

```python
import functools
import math

import numpy as np
import jax
import jax.numpy as jnp
from jax import lax
from jax.experimental import pallas as pl
from jax.experimental.pallas import tpu as pltpu

F32 = jnp.float32
BF16 = jnp.bfloat16

D_MODEL = 1024
HG_HEADS = 4
HG_DK = 128
HG_DV = 128
HG_K = HG_HEADS * HG_DK
HG_V = HG_HEADS * HG_DV
NSA_HEADS = 8
NSA_KV = 2
NSA_DH = 64
NSA_HPG = NSA_HEADS // NSA_KV
NSA_Q = NSA_HEADS * NSA_DH
NSA_KVW = NSA_KV * NSA_DH
CMP_BLOCK = 32
CMP_STRIDE = 16
CMP_HID = 128
SEL_BLOCK = 64
SEL_TOPK = 8
WINDOW = 512
FORCE_BONUS = 1000.0
REL_BUCKETS = 32
REL_MAX_DIST = 1024
D_FF = 4 * D_MODEL
EPS = 1e-6
NEG_BIG = -1e30
LOG2E = math.log2(math.e)

LANES = 128
VMEM_LIMIT = 56 * 1024 * 1024

P_HG = 0
P_GAB = 4 * HG_K
P_KV = P_GAB + 2 * D_MODEL
P_NQ = P_KV + 4 * NSA_KVW
P_GATE = P_NQ + NSA_Q
P_COLS = P_GATE + NSA_KV * LANES

HG_CHUNK = 64
HG_TS = 512
Q_BLOCK = 128
SEL_CHUNK = 512
WIN_BLOCKS = WINDOW // Q_BLOCK + 1
SEL_SHIFT = int(math.log2(SEL_BLOCK))
SEL_ROWS = LANES - NSA_DH


def _cparams(sem):
    return pltpu.CompilerParams(dimension_semantics=sem, vmem_limit_bytes=VMEM_LIMIT)


def _rms_rows(x, gain):
    ms = jnp.mean(x * x, axis=-1, keepdims=True)
    return x * lax.rsqrt(ms + EPS) * gain


def _proj_kernel(x_ref, g_ref, w_ref, wc_ref, o_ref, kc_ref, vc_ref, h_ref):
    @pl.when(pl.program_id(1) == 0)
    def _():
        h = _rms_rows(x_ref[...], g_ref[...]).astype(BF16)
        h_ref[...] = h
        c = jnp.dot(h, wc_ref[...], preferred_element_type=F32)
        kc_ref[...] = c[:, :NSA_KVW]
        vc_ref[...] = c[:, NSA_KVW:]

    o_ref[...] = jnp.dot(h_ref[...], w_ref[...], preferred_element_type=F32)


def _rms_proj(x2, gain, w, wc, tm, tn):
    T, D = x2.shape
    N = w.shape[1]
    cmp_sds = jax.ShapeDtypeStruct((T, NSA_KVW), F32)
    return pl.pallas_call(
        _proj_kernel,
        grid=(T // tm, N // tn),
        in_specs=[
            pl.BlockSpec((tm, D), lambda i, j: (i, 0)),
            pl.BlockSpec((1, D), lambda i, j: (0, 0)),
            pl.BlockSpec((D, tn), lambda i, j: (0, j)),
            pl.BlockSpec((D, 2 * NSA_KVW), lambda i, j: (0, 0)),
        ],
        out_specs=[
            pl.BlockSpec((tm, tn), lambda i, j: (i, j)),
            pl.BlockSpec((tm, NSA_KVW), lambda i, j: (i, 0)),
            pl.BlockSpec((tm, NSA_KVW), lambda i, j: (i, 0)),
        ],
        out_shape=[jax.ShapeDtypeStruct((T, N), F32), cmp_sds, cmp_sds],
        scratch_shapes=[pltpu.VMEM((tm, D), BF16)],
        compiler_params=_cparams(("parallel", "arbitrary")),
        name="rms_proj",
    )(x2, gain, w, wc)


def _mlp_kernel(x_ref, g_ref, wu_ref, wd_ref, o_ref, h_ref):
    f = pl.program_id(1)

    @pl.when(f == 0)
    def _():
        x = x_ref[...]
        h_ref[...] = _rms_rows(x, g_ref[...]).astype(BF16)
        o_ref[...] = x

    u = jnp.dot(h_ref[...], wu_ref[...], preferred_element_type=F32)
    u = jnp.square(jnp.maximum(u, 0.0)).astype(BF16)
    o_ref[...] += jnp.dot(u, wd_ref[...], preferred_element_type=F32)


def _mlp(x2, gain, wu, wd, tm, tf):
    T, D = x2.shape
    FF = wu.shape[1]
    return pl.pallas_call(
        _mlp_kernel,
        grid=(T // tm, FF // tf),
        in_specs=[
            pl.BlockSpec((tm, D), lambda i, f: (i, 0)),
            pl.BlockSpec((1, D), lambda i, f: (0, 0)),
            pl.BlockSpec((D, tf), lambda i, f: (0, f)),
            pl.BlockSpec((tf, D), lambda i, f: (f, 0)),
        ],
        out_specs=pl.BlockSpec((tm, D), lambda i, f: (i, 0)),
        out_shape=jax.ShapeDtypeStruct((T, D), F32),
        scratch_shapes=[pltpu.VMEM((tm, D), BF16)],
        compiler_params=_cparams(("parallel", "arbitrary")),
        name="mlp",
    )(x2, gain, wu, wd)


def _merge_kernel(x_ref, oa_ref, ob_ref, ga_ref, gb_ref, wa_ref, wb_ref, wo_ref, o_ref):
    ya = jnp.dot(oa_ref[...].astype(BF16), wa_ref[...], preferred_element_type=F32)
    yb = jnp.dot(ob_ref[...].astype(BF16), wb_ref[...], preferred_element_type=F32)
    mixed = jax.nn.sigmoid(ga_ref[...]) * ya + jax.nn.sigmoid(gb_ref[...]) * yb
    o_ref[...] = x_ref[...] + jnp.dot(mixed.astype(BF16), wo_ref[...], preferred_element_type=F32)


def _merge(x2, oa, ob, proj, wa, wb, wo, tm):
    T, D = x2.shape
    ga_blk = P_GAB // D
    return pl.pallas_call(
        _merge_kernel,
        grid=(T // tm,),
        in_specs=[
            pl.BlockSpec((tm, D), lambda i: (i, 0)),
            pl.BlockSpec((tm, HG_V), lambda i: (i, 0)),
            pl.BlockSpec((tm, NSA_Q), lambda i: (i, 0)),
            pl.BlockSpec((tm, D), lambda i: (i, ga_blk)),
            pl.BlockSpec((tm, D), lambda i: (i, ga_blk + 1)),
            pl.BlockSpec((HG_V, D), lambda i: (0, 0)),
            pl.BlockSpec((NSA_Q, D), lambda i: (0, 0)),
            pl.BlockSpec((D, D), lambda i: (0, 0)),
        ],
        out_specs=pl.BlockSpec((tm, D), lambda i: (i, 0)),
        out_shape=jax.ShapeDtypeStruct((T, D), F32),
        compiler_params=_cparams(("parallel",)),
        name="merge",
    )(x2, oa, ob, proj, proj, wa, wb, wo)


def _hgrn_consts(C):
    L = int(math.log2(C))
    idx = np.arange(C)
    mats = [(idx[None, :] <= idx[:, None]),
            (idx[None, :] > idx[:, None])]
    bmask = []
    for l in range(L):
        m = 1 << l
        r = (idx & ~(2 * m - 1)) + m
        lo = np.minimum(idx, r)[:, None]
        hi = np.maximum(idx, r)[:, None]
        mats.append((idx[None, :] > lo) & (idx[None, :] <= hi))
        blk = idx >> (l + 1)
        bmask.append(blk[:, None] == blk[None, :])
    bmask.append(idx[:, None] == idx[None, :])
    seg = np.concatenate(mats, axis=0).astype(np.float32)
    return seg, np.stack(bmask).astype(np.float32), L


def _hgrn_kernel(q_ref, f_ref, i_ref, g_ref, lbp_ref, gain_ref, seg_ref, bm_ref, o_ref, st_ref,
                 *, C, L, nchunk):
    @pl.when(pl.program_id(1) == 0)
    def _():
        st_ref[...] = jnp.zeros_like(st_ref)

    rowi = lax.broadcasted_iota(jnp.int32, (C, HG_DK), 0)
    nt = (((1,), (1,)), ((), ()))
    tn = (((0,), (0,)), ((), ()))

    def chunk(c, carry):
        r0 = pl.multiple_of(c * C, C)
        for h in range(HG_HEADS):
            cs = slice(h * HG_DK, (h + 1) * HG_DK)
            q = q_ref[0, pl.ds(r0, C), cs]
            z = f_ref[0, pl.ds(r0, C), cs]
            v = i_ref[0, pl.ds(r0, C), cs]
            gg = g_ref[0, pl.ds(r0, C), cs]
            log_lb = lbp_ref[0:1, cs]
            log1m_lb = lbp_ref[1:2, cs]
            one_m_lb = lbp_ref[2:3, cs]
            y = log1m_lb + (jnp.minimum(z, 0.0) - jnp.log1p(jnp.exp(-jnp.abs(z))))
            logf = jnp.maximum(log_lb, y) + jnp.log1p(jnp.exp(-jnp.abs(log_lb - y)))
            k = one_m_lb * jax.nn.sigmoid(-z)
            g_hi = logf.astype(BF16)
            g_lo = (logf - g_hi.astype(F32)).astype(BF16)
            seg = jnp.dot(seg_ref[...], jnp.concatenate([g_hi, g_lo], axis=1),
                          preferred_element_type=F32)
            seg = seg[:, :HG_DK] + seg[:, HG_DK:]
            b = seg[0:C]
            brev = seg[C:2 * C]
            qd = (q * jnp.exp(b)).astype(BF16)
            kd = (k * jnp.exp(brev)).astype(BF16)
            vb = v.astype(BF16)
            st = st_ref[h]
            o = lax.dot_general(qd, st.astype(BF16), nt, preferred_element_type=F32)
            s = lax.dot_general(q.astype(BF16), k.astype(BF16), nt,
                                preferred_element_type=F32) * bm_ref[L]
            for l in range(L):
                e = jnp.exp(seg[(2 + l) * C:(3 + l) * C])
                second = (rowi & (1 << l)) != 0
                ql = jnp.where(second, q * e, 0.0).astype(BF16)
                kl = jnp.where(second, 0.0, k * e).astype(BF16)
                s = s + lax.dot_general(ql, kl, nt, preferred_element_type=F32) * bm_ref[l]
            o = o + jnp.dot(s.astype(BF16), vb, preferred_element_type=F32)
            st_ref[h] = (st * jnp.exp(b[C - 1:C, :])
                         + lax.dot_general(vb, kd, tn, preferred_element_type=F32))
            o = _rms_rows(o, gain_ref[...]) * (gg * jax.nn.sigmoid(gg))
            o_ref[0, pl.ds(r0, C), cs] = o
        return carry

    lax.fori_loop(0, nchunk, chunk, 0)


def _hgrn(proj3, lbp, gain, C=HG_CHUNK, ts=HG_TS):
    B, S, _ = proj3.shape
    ts = min(ts, S)
    seg, bm, L = _hgrn_consts(C)
    blk = lambda k: pl.BlockSpec((1, ts, HG_K), lambda b, s: (b, s, P_HG // HG_K + k))
    return pl.pallas_call(
        functools.partial(_hgrn_kernel, C=C, L=L, nchunk=ts // C),
        grid=(B, S // ts),
        in_specs=[
            blk(0), blk(1), blk(2), blk(3),
            pl.BlockSpec((3, HG_K), lambda b, s: (0, 0)),
            pl.BlockSpec((1, HG_DV), lambda b, s: (0, 0)),
            pl.BlockSpec(seg.shape, lambda b, s: (0, 0)),
            pl.BlockSpec(bm.shape, lambda b, s: (0, 0, 0)),
        ],
        out_specs=pl.BlockSpec((1, ts, HG_V), lambda b, s: (b, s, 0)),
        out_shape=jax.ShapeDtypeStruct((B, S, HG_V), F32),
        scratch_shapes=[pltpu.VMEM((HG_HEADS, HG_DV, HG_DK), F32)],
        compiler_params=_cparams(("parallel", "arbitrary")),
        name="hgrn2",
    )(proj3, proj3, proj3, proj3, lbp, gain, jnp.asarray(seg, BF16), jnp.asarray(bm))


def _half_rms(x, gain2, lane):
    sq = x * x
    s_lo = jnp.sum(jnp.where(lane < NSA_DH, sq, 0.0), axis=-1, keepdims=True)
    s_hi = jnp.sum(jnp.where(lane < NSA_DH, 0.0, sq), axis=-1, keepdims=True)
    ms = jnp.where(lane < NSA_DH, s_lo, s_hi) * (1.0 / NSA_DH)
    return x * lax.rsqrt(ms + EPS) * gain2


def _gelu_tanh(x):
    return 0.5 * x * (1.0 + jnp.tanh(math.sqrt(2.0 / math.pi) * (x + 0.044715 * (x * x * x))))


def _prep_kernel(kv_ref, kc_ref, vc_ref, gsel_ref, gwin_ref, gcmp_ref, pos_ref, w1_ref, w1c_ref,
                 b1_ref, w2_ref, ksa_ref, vst_ref, kwa_ref, vwt_ref, kca_ref, vct_ref, *, S, rows):
    lane = lax.broadcasted_iota(jnp.int32, (rows, LANES), 1)
    rowi = lax.broadcasted_iota(jnp.int32, (rows, LANES), 0)
    ones_row = (lax.broadcasted_iota(jnp.int32, (NSA_DH, rows), 0) == 0).astype(F32)
    low = lane < NSA_DH

    def body(i, carry):
        r0 = pl.multiple_of(i * rows, rows)
        blk_hot = jnp.where(lane - NSA_DH == ((r0 + rowi) >> SEL_SHIFT), 1.0, 0.0)
        for src, gain_ref, aug, k_dst, v_dst, vrows in (
                (0, gsel_ref, blk_hot, ksa_ref, vst_ref, rows),
                (2, gwin_ref, jnp.zeros((rows, LANES), F32), kwa_ref, vwt_ref, Q_BLOCK)):
            kk = kv_ref[0, pl.ds(r0, rows), src * LANES:(src + 1) * LANES]
            vv = kv_ref[0, pl.ds(r0, rows), (src + 1) * LANES:(src + 2) * LANES]
            kn = _half_rms(kk, gain_ref[...], lane)
            k_dst[0, 0, pl.ds(r0, rows), :] = jnp.where(low, kn, aug).astype(BF16)
            k_dst[0, 1, pl.ds(r0, rows), :] = jnp.where(low, pltpu.roll(kn, NSA_DH, 1), aug).astype(BF16)
            vt = vv.T
            for g in range(NSA_KV):
                vg = jnp.concatenate([vt[g * NSA_DH:(g + 1) * NSA_DH], ones_row], axis=0).astype(BF16)
                for u in range(rows // vrows):
                    v_dst[0, g, i * (rows // vrows) + u] = vg[:, u * vrows:(u + 1) * vrows]
        return carry

    lax.fori_loop(0, S // rows, body, 0)

    nc = S // CMP_STRIDE
    outs = []
    for j, src_ref in ((0, kc_ref), (1, vc_ref)):
        bias = jnp.dot(pos_ref[j].astype(BF16), w1_ref[j], preferred_element_type=F32)[0:1] + b1_ref[j]
        pj = jnp.dot(src_ref[0].astype(BF16), w1c_ref[j], preferred_element_type=F32)
        per_g = []
        for g in range(NSA_KV):
            first = pj[:, (2 * g) * CMP_HID:(2 * g + 1) * CMP_HID]
            second = pj[:, (2 * g + 1) * CMP_HID:(2 * g + 2) * CMP_HID]
            hid = _gelu_tanh(first + pltpu.roll(second, nc - 1, 0) + bias).astype(BF16)
            if j == 0:
                per_g.append(jnp.dot(hid, w2_ref[0], preferred_element_type=F32))
            else:
                per_g.append(lax.dot_general(w2_ref[1], hid, (((1,), (1,)), ((), ())),
                                             preferred_element_type=F32))
        outs.append(per_g)
    for g in range(NSA_KV):
        kc = outs[0][g]
        ms = jnp.sum(kc * kc, axis=-1, keepdims=True) * (1.0 / NSA_DH)
        kca_ref[0, g] = (kc * lax.rsqrt(ms + EPS) * gcmp_ref[...]).astype(BF16)
        vct_ref[0, g] = outs[1][g].astype(BF16)


def _nsa_prep(proj3, kc2, vc2, gsel, gwin, gcmp, pos8, w1, w1c, b1, w2p):
    B, S, _ = proj3.shape
    nc = S // CMP_STRIDE
    rows = SEL_CHUNK
    full = lambda shp: pl.BlockSpec(shp, lambda b: (0,) * len(shp))
    whole = lambda shp: pl.BlockSpec((1,) + shp, lambda b: (b,) + (0,) * len(shp))
    shapes = [(NSA_KV, S, LANES), (NSA_KV, S // SEL_CHUNK, LANES, SEL_CHUNK),
              (NSA_KV, S, LANES), (NSA_KV, S // Q_BLOCK, LANES, Q_BLOCK),
              (NSA_KV, nc, LANES), (NSA_KV, LANES, nc)]
    return pl.pallas_call(
        functools.partial(_prep_kernel, S=S, rows=rows),
        grid=(B,),
        in_specs=[
            pl.BlockSpec((1, S, 4 * NSA_KVW), lambda b: (b, 0, P_KV // (4 * NSA_KVW))),
            pl.BlockSpec((1, nc, CMP_STRIDE * NSA_KVW), lambda b: (b, 0, 0)),
            pl.BlockSpec((1, nc, CMP_STRIDE * NSA_KVW), lambda b: (b, 0, 0)),
            full((1, LANES)), full((1, LANES)), full((1, LANES)),
            full(pos8.shape), full(w1.shape), full(w1c.shape), full(b1.shape), full(w2p.shape),
        ],
        out_specs=[whole(s) for s in shapes],
        out_shape=[jax.ShapeDtypeStruct((B,) + s, BF16) for s in shapes],
        compiler_params=_cparams(("parallel",)),
        name="nsa_prep",
    )(proj3, kc2, vc2, gsel, gwin, gcmp, pos8, w1, w1c, b1, w2p)


def _sel_table_geometry():
    last_start = (REL_BUCKETS // 2) * (REL_MAX_DIST / (REL_BUCKETS // 2)) ** (
        (REL_BUCKETS - REL_BUCKETS // 2 - 1) / (REL_BUCKETS - REL_BUCKETS // 2))
    dsat = int(math.ceil((last_start + 16 + SEL_CHUNK) / LANES)) * LANES
    return dsat, dsat + SEL_CHUNK


def _attn_kernel(q_ref, gate_ref, kca_ref, vct_ref, ksa_ref, vst_ref, kwa_ref, vwt_ref, tabc_ref,
                 tabw_ref, tabs_ref, ovl_ref, qg_ref, o_ref, m_ref, acc_ref, sa_ref, sb_ref, *, nsel, dsat):
    qi = pl.program_id(2)
    QB = Q_BLOCK
    R = NSA_HPG * QB
    DH = NSA_DH

    xt = q_ref[0].T
    cols = []
    for h in range(NSA_HPG):
        xh = xt[h * DH:(h + 1) * DH]
        ms = jnp.mean(xh * xh, axis=0, keepdims=True)
        cols.append(xh * lax.rsqrt(ms + EPS) * qg_ref[...])
    qt = jnp.concatenate(cols, axis=1).astype(BF16)
    qta = jnp.concatenate([qt, jnp.zeros_like(qt)], axis=0)

    s = jnp.dot(kca_ref[0, 0], qta, preferred_element_type=F32) + tabc_ref[0, 0]
    mx = jnp.max(s, axis=0, keepdims=True)
    e = jnp.exp2(s - mx)
    p = jnp.where(s > 0.5 * NEG_BIG, e / jnp.sum(e, axis=0, keepdims=True), 0.0)
    o_c = jnp.dot(vct_ref[0, 0], p.astype(BF16), preferred_element_type=F32)

    psum = p[:, 0:QB] + p[:, QB:2 * QB] + p[:, 2 * QB:3 * QB] + p[:, 3 * QB:4 * QB]
    p_hi = psum.astype(BF16)
    p_lo = (psum - p_hi.astype(F32)).astype(BF16)
    imp = (jnp.dot(ovl_ref[...], p_hi, preferred_element_type=F32)
           + jnp.dot(ovl_ref[...], p_lo, preferred_element_type=F32))
    t = qi * QB + lax.broadcasted_iota(jnp.int32, (SEL_ROWS, QB), 1)
    jrow = lax.broadcasted_iota(jnp.int32, (SEL_ROWS, QB), 0)
    jcur = t >> SEL_SHIFT
    forced = (jrow == 0) | (jrow == jcur) | (jrow == jcur - 1)
    work = jnp.where(jrow * SEL_BLOCK <= t, imp + jnp.where(forced, FORCE_BONUS, 0.0), -1.0)
    work = jnp.where(jrow < nsel, work, -5.0)
    jf = jrow.astype(F32)
    sel = jnp.zeros((SEL_ROWS, QB), F32)
    for _ in range(min(SEL_TOPK, nsel)):
        best = jnp.max(work, axis=0, keepdims=True)
        first = jnp.min(jnp.where(work == best, jf, float(SEL_ROWS)), axis=0, keepdims=True)
        hit = jf == first
        sel = jnp.where(hit, 1.0, sel)
        work = jnp.where(hit, -3.0, work)
    selneg = jnp.where(sel > 0.5, 0.0, NEG_BIG).astype(BF16)
    qts = jnp.concatenate([qt, jnp.concatenate([selneg] * NSA_HPG, axis=1)], axis=0)

    tiles, vts = [], []
    for r in range(WIN_BLOCKS):
        kb = qi - (WIN_BLOCKS - 1) + r
        kbc = jnp.maximum(kb, 0)
        kw = kwa_ref[0, 0, pl.ds(pl.multiple_of(kbc * QB, QB), QB), :]
        vts.append(vwt_ref[0, 0, kbc])
        sw = jnp.dot(kw, qta, preferred_element_type=F32) + tabw_ref[0, 0, r * QB:(r + 1) * QB, :]
        tiles.append(sw + jnp.where(kb >= 0, 0.0, NEG_BIG))
    mw = tiles[0]
    for r in range(1, WIN_BLOCKS):
        mw = jnp.maximum(mw, tiles[r])
    mw = jnp.max(mw, axis=0, keepdims=True)
    o_w = jnp.zeros((LANES, R), F32)
    for r in range(WIN_BLOCKS):
        o_w = o_w + jnp.dot(vts[r], jnp.exp2(tiles[r] - mw).astype(BF16), preferred_element_type=F32)

    m_ref[...] = jnp.full_like(m_ref, NEG_BIG)
    acc_ref[...] = jnp.zeros_like(acc_ref)

    def scores(c):
        k0 = pl.multiple_of(c * SEL_CHUNK, SEL_CHUNK)
        delta = qi * QB - c * SEL_CHUNK
        start = pl.multiple_of(dsat - jnp.minimum(delta, dsat), LANES)
        return (jnp.dot(ksa_ref[0, 0, pl.ds(k0, SEL_CHUNK), :], qts, preferred_element_type=F32)
                + tabs_ref[0, 0, pl.ds(start, SEL_CHUNK), :])

    def consume(s_ref, c):
        sc = s_ref[...]
        m_old = m_ref[...]
        m_new = jnp.maximum(m_old, jnp.max(sc, axis=0, keepdims=True))
        alpha = jnp.exp2(m_old - m_new)
        pe = jnp.exp2(sc - m_new).astype(BF16)
        acc_ref[...] = alpha * acc_ref[...] + jnp.dot(vst_ref[0, 0, c], pe, preferred_element_type=F32)
        m_ref[...] = m_new

    n_chunks = (qi * QB + QB - 1) // SEL_CHUNK + 1
    n_pairs = (n_chunks - 1) // 2
    sa_ref[...] = scores(0)

    def sel_pair(pr, carry):
        c = 2 * pr
        sb_ref[...] = scores(c + 1)
        consume(sa_ref, c)
        sa_ref[...] = scores(c + 2)
        consume(sb_ref, c + 1)
        return carry

    lax.fori_loop(0, n_pairs, sel_pair, 0)

    @pl.when(n_chunks % 2 == 0)
    def _():
        sb_ref[...] = scores(n_chunks - 1)
        consume(sa_ref, n_chunks - 2)
        consume(sb_ref, n_chunks - 1)

    @pl.when(n_chunks % 2 == 1)
    def _():
        consume(sa_ref, n_chunks - 1)

    acc = acc_ref[...]

    gt = jax.nn.sigmoid(gate_ref[0]).T
    o_s = acc[0:DH] / acc[DH:DH + 1]
    o_w = o_w[0:DH] / o_w[DH:DH + 1]
    heads = []
    for h in range(NSA_HPG):
        cs = slice(h * QB, (h + 1) * QB)
        heads.append(gt[3 * h:3 * h + 1] * o_c[0:DH, cs] + gt[3 * h + 1:3 * h + 2] * o_s[:, cs]
                     + gt[3 * h + 2:3 * h + 3] * o_w[:, cs])
    o_ref[0] = jnp.concatenate(heads, axis=0).T


def _nsa_attn(proj3, prep, tabc, tabw, tabs, ovl, qgain, dsat):
    B, S, _ = proj3.shape
    ksa, vst, kwa, vwt, kca, vct = prep
    nq = S // Q_BLOCK
    nc = S // CMP_STRIDE
    nsel = S // SEL_BLOCK
    R = NSA_HPG * Q_BLOCK
    gw = NSA_HPG * NSA_DH
    per_bg = lambda a: pl.BlockSpec((1, 1) + a.shape[2:], lambda b, g, i: (b, g) + (0,) * (a.ndim - 2))
    return pl.pallas_call(
        functools.partial(_attn_kernel, nsel=nsel, dsat=dsat),
        grid=(B, NSA_KV, nq),
        in_specs=[
            pl.BlockSpec((1, Q_BLOCK, gw), lambda b, g, i: (b, i, P_NQ // gw + g)),
            pl.BlockSpec((1, Q_BLOCK, LANES), lambda b, g, i: (b, i, P_GATE // LANES + g)),
            per_bg(kca), per_bg(vct), per_bg(ksa), per_bg(vst), per_bg(kwa), per_bg(vwt),
            pl.BlockSpec((1, 1, nc, R), lambda b, g, i: (g, i, 0, 0)),
            pl.BlockSpec((1, 1) + tabw.shape[2:], lambda b, g, i: (g, 0, 0, 0)),
            pl.BlockSpec((1, 1) + tabs.shape[2:], lambda b, g, i: (g, 0, 0, 0)),
            pl.BlockSpec(ovl.shape, lambda b, g, i: (0, 0)),
            pl.BlockSpec(qgain.shape, lambda b, g, i: (0, 0)),
        ],
        out_specs=pl.BlockSpec((1, Q_BLOCK, gw), lambda b, g, i: (b, i, g)),
        out_shape=jax.ShapeDtypeStruct((B, S, NSA_Q), F32),
        scratch_shapes=[pltpu.VMEM((1, R), F32), pltpu.VMEM((LANES, R), F32),
                        pltpu.VMEM((SEL_CHUNK, R), F32), pltpu.VMEM((SEL_CHUNK, R), F32)],
        compiler_params=_cparams(("parallel", "parallel", "arbitrary")),
        name="nsa_attn",
    )(proj3, proj3, kca, vct, ksa, vst, kwa, vwt, tabc, tabw, tabs, ovl, qgain)


def _rel_bucket(d):
    max_exact = REL_BUCKETS // 2
    d = jnp.maximum(d, 0)
    df = jnp.maximum(d, 1).astype(F32)
    large = max_exact + (jnp.log(df / max_exact) / math.log(REL_MAX_DIST / max_exact)
                         * (REL_BUCKETS - max_exact)).astype(jnp.int32)
    return jnp.where(d < max_exact, d, jnp.minimum(large, REL_BUCKETS - 1))


def _bias_kernel(rb_ref, bkt_ref, o_ref, *, rows):
    g = pl.program_id(0)
    step = 32
    for r0 in range(0, rows, step):
        bkt = bkt_ref[0, r0:r0 + step, :]
        acc = [jnp.full(bkt.shape, NEG_BIG, F32) for _ in range(NSA_HPG)]
        for b in range(REL_BUCKETS):
            hit = bkt == b
            for h in range(NSA_HPG):
                acc[h] = jnp.where(hit, rb_ref[b, g * NSA_HPG + h] * LOG2E, acc[h])
        for h in range(NSA_HPG):
            o_ref[0, 0, r0:r0 + step, h * Q_BLOCK:(h + 1) * Q_BLOCK] = acc[h]


def _bias_table(rel_bias, bucket):
    nt, rows, cols = bucket.shape
    return pl.pallas_call(
        functools.partial(_bias_kernel, rows=rows),
        grid=(NSA_KV, nt),
        in_specs=[
            pl.BlockSpec(memory_space=pltpu.SMEM),
            pl.BlockSpec((1, rows, cols), lambda g, t: (t, 0, 0)),
        ],
        out_specs=pl.BlockSpec((1, 1, rows, NSA_HPG * cols), lambda g, t: (g, t, 0, 0)),
        out_shape=jax.ShapeDtypeStruct((NSA_KV, nt, rows, NSA_HPG * cols), F32),
        compiler_params=_cparams(("parallel", "parallel")),
        name="bias_table",
    )(rel_bias, bucket)


def _bias_tables(rel_bias, S, dsat, strip_len):
    nq = S // Q_BLOCK
    nc = S // CMP_STRIDE
    n_cmp = (S - CMP_BLOCK) // CMP_STRIDE + 1
    ar = lambda n: jnp.arange(n, dtype=jnp.int32)
    bucket = lambda d, valid: jnp.where(valid, _rel_bucket(d), -1)
    i = ar(Q_BLOCK)[None, None, :]
    c = ar(nc)[None, :, None]
    d_c = ar(nq)[:, None, None] * Q_BLOCK + i - (c * CMP_STRIDE + CMP_BLOCK - 1)
    tabc = _bias_table(rel_bias, bucket(d_c, (d_c >= 0) & (c < n_cmp)))
    d_w = i + WINDOW - ar(WIN_BLOCKS * Q_BLOCK)[None, :, None]
    tabw = _bias_table(rel_bias, bucket(d_w, (d_w >= 0) & (d_w < WINDOW)))
    d_s = i - ar(strip_len)[None, :, None] + dsat
    tabs = _bias_table(rel_bias, bucket(d_s, d_s >= 0))
    return tabc, tabw, tabs


def _overlap_matrix(S):
    nc = S // CMP_STRIDE
    nsel = S // SEL_BLOCK
    n_cmp = (S - CMP_BLOCK) // CMP_STRIDE + 1
    c = np.arange(nc)[None, :]
    sel_start = (np.arange(SEL_ROWS) * SEL_BLOCK)[:, None]
    ovl = ((c * CMP_STRIDE < sel_start + SEL_BLOCK) & (c * CMP_STRIDE + CMP_BLOCK - 1 >= sel_start)
           & (c < n_cmp) & (np.arange(SEL_ROWS)[:, None] < nsel))
    return jnp.asarray(ovl, BF16)


def _layer_params(l, lower_bounds, norm_mix, w_in, hg_out_gain, nsa_qk_gain, cmp_pos, cmp_w1, cmp_b1,
                  cmp_w2, w_branch_a, w_branch_b, w_out, norm_mlp, w_up, w_down):
    splits = (HG_K, HG_K, HG_V, HG_V, NSA_Q, NSA_KVW, NSA_KVW, NSA_KVW, NSA_KVW, NSA_KVW, NSA_KVW,
              3 * NSA_HEADS, D_MODEL, D_MODEL)
    offs = np.concatenate([[0], np.cumsum(splits)])
    col = lambda k: w_in[l][:, offs[k]:offs[k + 1]]
    gate_w = col(11)
    zpad = jnp.zeros((D_MODEL, LANES - 3 * NSA_HPG), F32)
    gate_cols = []
    for g in range(NSA_KV):
        gate_cols += [gate_w[:, g * 3 * NSA_HPG:(g + 1) * 3 * NSA_HPG], zpad]
    w_main = jnp.concatenate([col(0), col(1), col(2), col(3), col(12), col(13),
                              col(7), col(8), col(9), col(10), col(4)] + gate_cols, axis=1).astype(BF16)
    w_cmp = jnp.concatenate([col(5), col(6)], axis=1).astype(BF16)

    lb = lower_bounds[l]
    lbp = jnp.stack([jnp.log(lb), jnp.log1p(-lb), 1.0 - lb])

    w1 = cmp_w1[l].reshape(2, 2, CMP_STRIDE, NSA_DH, CMP_HID)
    w1c = jnp.zeros((2, CMP_STRIDE, NSA_KV, NSA_DH, NSA_KV, 2, CMP_HID), F32)
    for g in range(NSA_KV):
        w1c = w1c.at[:, :, g, :, g, :, :].set(w1.transpose(0, 2, 3, 1, 4))
    w1c = w1c.reshape(2, CMP_STRIDE * NSA_KVW, NSA_KV * 2 * CMP_HID).astype(BF16)
    pos8 = jnp.zeros((2, 8, CMP_BLOCK * NSA_DH), F32).at[:, 0, :].set(cmp_pos[l].reshape(2, -1))
    w2p = jnp.zeros((2, CMP_HID, LANES), F32)
    w2p = w2p.at[0, :, :NSA_DH].set(cmp_w2[l, 0]).at[1, :NSA_DH, :].set(cmp_w2[l, 1].T).astype(BF16)
    qk = nsa_qk_gain[l]
    z64 = jnp.zeros((NSA_DH,), F32)
    return dict(
        norm_mix=norm_mix[l][None, :], w_main=w_main, w_cmp=w_cmp, lbp=lbp,
        hg_gain=hg_out_gain[l][None, :],
        gsel=jnp.concatenate([qk[2], qk[2]])[None, :], gwin=jnp.concatenate([qk[3], qk[3]])[None, :],
        gcmp=jnp.concatenate([qk[1], z64])[None, :],
        qgain=jnp.broadcast_to((qk[0] * (NSA_DH ** -0.5 * LOG2E))[:, None], (NSA_DH, Q_BLOCK)),
        pos8=pos8, w1=cmp_w1[l].astype(BF16), w1c=w1c, b1=cmp_b1[l][:, None, :], w2p=w2p,
        wa=w_branch_a[l].astype(BF16), wb=w_branch_b[l].astype(BF16), wo=w_out[l].astype(BF16),
        norm_mlp=norm_mlp[l][None, :], wu=w_up[l].astype(BF16), wd=w_down[l].astype(BF16))


def kernel(x, rel_bias, hg_lb_logits, norm_mix, w_in, hg_out_gain, nsa_qk_gain, cmp_pos, cmp_w1, cmp_b1, cmp_w2, w_branch_a, w_branch_b, w_out, norm_mlp, w_up, w_down):
    B, S, D = x.shape
    T = B * S
    depth = w_in.shape[0]
    assert D == D_MODEL and S % SEL_CHUNK == 0 and S >= WINDOW and S // SEL_BLOCK <= SEL_ROWS
    lb_cum = jnp.cumsum(jax.nn.softmax(hg_lb_logits.astype(F32), axis=0), axis=0)
    lower_bounds = lb_cum - lb_cum[0:1]

    dsat, strip_len = _sel_table_geometry()
    tabc, tabw, tabs = _bias_tables(rel_bias, S, dsat, strip_len)
    ovl = _overlap_matrix(S)
    nc = S // CMP_STRIDE
    tm = min(1024, T)

    x2 = x.reshape(T, D)
    for l in range(depth):
        p = _layer_params(l, lower_bounds, norm_mix, w_in, hg_out_gain, nsa_qk_gain, cmp_pos, cmp_w1,
                          cmp_b1, cmp_w2, w_branch_a, w_branch_b, w_out, norm_mlp, w_up, w_down)
        proj, kc, vc = _rms_proj(x2, p["norm_mix"], p["w_main"], p["w_cmp"], tm, P_COLS // 7)
        proj3 = proj.reshape(B, S, P_COLS)
        kc2 = kc.reshape(B, nc, CMP_STRIDE * NSA_KVW)
        vc2 = vc.reshape(B, nc, CMP_STRIDE * NSA_KVW)
        o_a = _hgrn(proj3, p["lbp"], p["hg_gain"])
        prep = _nsa_prep(proj3, kc2, vc2, p["gsel"], p["gwin"], p["gcmp"], p["pos8"], p["w1"],
                         p["w1c"], p["b1"], p["w2p"])
        o_b = _nsa_attn(proj3, prep, tabc, tabw, tabs, ovl, p["qgain"], dsat)
        x2 = _merge(x2, o_a.reshape(T, HG_V), o_b.reshape(T, NSA_Q), proj, p["wa"], p["wb"], p["wo"],
                    min(512, T))
        x2 = _mlp(x2, p["norm_mlp"], p["wu"], p["wd"], tm, 1024)
    return x2.reshape(B, S, D)
```

```python
import functools
import math

import numpy as np
import jax
import jax.numpy as jnp
from jax import lax
from jax.experimental import pallas as pl
from jax.experimental.pallas import tpu as pltpu

F32 = jnp.float32
BF16 = jnp.bfloat16

D_MODEL = 1024
HG_HEADS = 4
HG_DK = 128
HG_DV = 128
HG_K = HG_HEADS * HG_DK
HG_V = HG_HEADS * HG_DV
NSA_HEADS = 8
NSA_KV = 2
NSA_DH = 64
NSA_HPG = NSA_HEADS // NSA_KV
NSA_Q = NSA_HEADS * NSA_DH
NSA_KVW = NSA_KV * NSA_DH
CMP_BLOCK = 32
CMP_STRIDE = 16
CMP_HID = 128
SEL_BLOCK = 64
SEL_TOPK = 8
WINDOW = 512
FORCE_BONUS = 1000.0
REL_BUCKETS = 32
REL_MAX_DIST = 1024
D_FF = 4 * D_MODEL
EPS = 1e-6
NEG_BIG = -1e30
LOG2E = math.log2(math.e)

LANES = 128
VMEM_LIMIT = 56 * 1024 * 1024

P_HG = 0
P_GAB = 4 * HG_K
P_KV = P_GAB + 2 * D_MODEL
P_NQ = P_KV + 4 * NSA_KVW
P_GATE = P_NQ + NSA_Q
P_COLS = P_GATE + NSA_KV * LANES

HG_CHUNK = 128
HG_TS = 512
Q_BLOCK = 128
SEL_CHUNK = 512
WIN_BLOCKS = WINDOW // Q_BLOCK + 1
SEL_SHIFT = int(math.log2(SEL_BLOCK))
SEL_ROWS = LANES - NSA_DH


def _cparams(sem):
    return pltpu.CompilerParams(dimension_semantics=sem, vmem_limit_bytes=VMEM_LIMIT)


def _rms_rows(x, gain):
    ms = jnp.mean(x * x, axis=-1, keepdims=True)
    return x * lax.rsqrt(ms + EPS) * gain


def _proj_kernel(x_ref, g_ref, w_ref, wc_ref, o_ref, kc_ref, vc_ref, h_ref):
    @pl.when(pl.program_id(1) == 0)
    def _():
        h = _rms_rows(x_ref[...], g_ref[...]).astype(BF16)
        h_ref[...] = h
        c = jnp.dot(h, wc_ref[...], preferred_element_type=F32)
        kc_ref[...] = c[:, :NSA_KVW]
        vc_ref[...] = c[:, NSA_KVW:]

    o_ref[...] = jnp.dot(h_ref[...], w_ref[...], preferred_element_type=F32)


def _rms_proj(x2, gain, w, wc, tm, tn):
    T, D = x2.shape
    N = w.shape[1]
    cmp_sds = jax.ShapeDtypeStruct((T, NSA_KVW), F32)
    return pl.pallas_call(
        _proj_kernel,
        grid=(T // tm, N // tn),
        in_specs=[
            pl.BlockSpec((tm, D), lambda i, j: (i, 0)),
            pl.BlockSpec((1, D), lambda i, j: (0, 0)),
            pl.BlockSpec((D, tn), lambda i, j: (0, j)),
            pl.BlockSpec((D, 2 * NSA_KVW), lambda i, j: (0, 0)),
        ],
        out_specs=[
            pl.BlockSpec((tm, tn), lambda i, j: (i, j)),
            pl.BlockSpec((tm, NSA_KVW), lambda i, j: (i, 0)),
            pl.BlockSpec((tm, NSA_KVW), lambda i, j: (i, 0)),
        ],
        out_shape=[jax.ShapeDtypeStruct((T, N), F32), cmp_sds, cmp_sds],
        scratch_shapes=[pltpu.VMEM((tm, D), BF16)],
        compiler_params=_cparams(("parallel", "arbitrary")),
        name="rms_proj",
    )(x2, gain, w, wc)


def _mlp_kernel(x_ref, g_ref, wu_ref, wd_ref, o_ref, h_ref):
    f = pl.program_id(1)

    @pl.when(f == 0)
    def _():
        x = x_ref[...]
        h_ref[...] = _rms_rows(x, g_ref[...]).astype(BF16)
        o_ref[...] = x

    u = jnp.dot(h_ref[...], wu_ref[...], preferred_element_type=F32)
    u = jnp.square(jnp.maximum(u, 0.0)).astype(BF16)
    o_ref[...] += jnp.dot(u, wd_ref[...], preferred_element_type=F32)


def _mlp(x2, gain, wu, wd, tm, tf):
    T, D = x2.shape
    FF = wu.shape[1]
    return pl.pallas_call(
        _mlp_kernel,
        grid=(T // tm, FF // tf),
        in_specs=[
            pl.BlockSpec((tm, D), lambda i, f: (i, 0)),
            pl.BlockSpec((1, D), lambda i, f: (0, 0)),
            pl.BlockSpec((D, tf), lambda i, f: (0, f)),
            pl.BlockSpec((tf, D), lambda i, f: (f, 0)),
        ],
        out_specs=pl.BlockSpec((tm, D), lambda i, f: (i, 0)),
        out_shape=jax.ShapeDtypeStruct((T, D), F32),
        scratch_shapes=[pltpu.VMEM((tm, D), BF16)],
        compiler_params=_cparams(("parallel", "arbitrary")),
        name="mlp",
    )(x2, gain, wu, wd)


def _merge_kernel(x_ref, oa_ref, ob_ref, ga_ref, gb_ref, wa_ref, wb_ref, wo_ref, o_ref):
    ya = jnp.dot(oa_ref[...].astype(BF16), wa_ref[...], preferred_element_type=F32)
    yb = jnp.dot(ob_ref[...].astype(BF16), wb_ref[...], preferred_element_type=F32)
    mixed = jax.nn.sigmoid(ga_ref[...]) * ya + jax.nn.sigmoid(gb_ref[...]) * yb
    o_ref[...] = x_ref[...] + jnp.dot(mixed.astype(BF16), wo_ref[...], preferred_element_type=F32)


def _merge(x2, oa, ob, proj, wa, wb, wo, tm):
    T, D = x2.shape
    ga_blk = P_GAB // D
    return pl.pallas_call(
        _merge_kernel,
        grid=(T // tm,),
        in_specs=[
            pl.BlockSpec((tm, D), lambda i: (i, 0)),
            pl.BlockSpec((tm, HG_V), lambda i: (i, 0)),
            pl.BlockSpec((tm, NSA_Q), lambda i: (i, 0)),
            pl.BlockSpec((tm, D), lambda i: (i, ga_blk)),
            pl.BlockSpec((tm, D), lambda i: (i, ga_blk + 1)),
            pl.BlockSpec((HG_V, D), lambda i: (0, 0)),
            pl.BlockSpec((NSA_Q, D), lambda i: (0, 0)),
            pl.BlockSpec((D, D), lambda i: (0, 0)),
        ],
        out_specs=pl.BlockSpec((tm, D), lambda i: (i, 0)),
        out_shape=jax.ShapeDtypeStruct((T, D), F32),
        compiler_params=_cparams(("parallel",)),
        name="merge",
    )(x2, oa, ob, proj, proj, wa, wb, wo)


def _hgrn_consts(C):
    L = int(math.log2(C))
    idx = np.arange(C)
    mats = [(idx[None, :] <= idx[:, None]),
            (idx[None, :] > idx[:, None])]
    bmask = []
    for l in range(L):
        m = 1 << l
        r = (idx & ~(2 * m - 1)) + m
        lo = np.minimum(idx, r)[:, None]
        hi = np.maximum(idx, r)[:, None]
        mats.append((idx[None, :] > lo) & (idx[None, :] <= hi))
        blk = idx >> (l + 1)
        bmask.append(blk[:, None] == blk[None, :])
    bmask.append(idx[:, None] == idx[None, :])
    seg = np.concatenate(mats, axis=0).astype(np.float32)
    seg = np.concatenate([seg, seg], axis=1)
    return seg, np.stack(bmask).astype(np.float32), L


def _hgrn_kernel(q_ref, f_ref, i_ref, g_ref, lbp_ref, gain_ref, seg_ref, bm_ref, o_ref, st_ref,
                 *, C, L, nchunk):
    @pl.when(pl.program_id(1) == 0)
    def _():
        st_ref[...] = jnp.zeros_like(st_ref)

    rowi = lax.broadcasted_iota(jnp.int32, (C, HG_DK), 0)
    nt = (((1,), (1,)), ((), ()))
    tn = (((0,), (0,)), ((), ()))

    def chunk(c, carry):
        r0 = pl.multiple_of(c * C, C)
        for h in range(HG_HEADS):
            cs = slice(h * HG_DK, (h + 1) * HG_DK)
            q = q_ref[0, pl.ds(r0, C), cs]
            z = f_ref[0, pl.ds(r0, C), cs]
            v = i_ref[0, pl.ds(r0, C), cs]
            gg = g_ref[0, pl.ds(r0, C), cs]
            log_lb = lbp_ref[0:1, cs]
            log1m_lb = lbp_ref[1:2, cs]
            one_m_lb = lbp_ref[2:3, cs]
            y = log1m_lb + (jnp.minimum(z, 0.0) - jnp.log1p(jnp.exp(-jnp.abs(z))))
            logf = jnp.maximum(log_lb, y) + jnp.log1p(jnp.exp(-jnp.abs(log_lb - y)))
            k = one_m_lb * jax.nn.sigmoid(-z)
            g_hi = logf.astype(BF16)
            g_lo = (logf - g_hi.astype(F32)).astype(BF16)
            seg = jnp.dot(seg_ref[...], jnp.concatenate([g_hi, g_lo], axis=0),
                          preferred_element_type=F32)
            b = seg[0:C]
            brev = seg[C:2 * C]
            qd = (q * jnp.exp(b)).astype(BF16)
            kd = (k * jnp.exp(brev)).astype(BF16)
            vb = v.astype(BF16)
            st = st_ref[h]
            o = lax.dot_general(qd, st.astype(BF16), nt, preferred_element_type=F32)
            s = lax.dot_general(q.astype(BF16), k.astype(BF16), nt,
                                preferred_element_type=F32) * bm_ref[L]
            for l in range(L):
                e = jnp.exp(seg[(2 + l) * C:(3 + l) * C])
                second = (rowi & (1 << l)) != 0
                ql = jnp.where(second, q * e, 0.0).astype(BF16)
                kl = jnp.where(second, 0.0, k * e).astype(BF16)
                s = s + lax.dot_general(ql, kl, nt, preferred_element_type=F32) * bm_ref[l]
            o = o + jnp.dot(s.astype(BF16), vb, preferred_element_type=F32)
            st_ref[h] = (st * jnp.exp(b[C - 1:C, :])
                         + lax.dot_general(vb, kd, tn, preferred_element_type=F32))
            o = _rms_rows(o, gain_ref[...]) * (gg * jax.nn.sigmoid(gg))
            o_ref[0, pl.ds(r0, C), cs] = o
        return carry

    lax.fori_loop(0, nchunk, chunk, 0)


def _hgrn(proj3, lbp, gain, C=HG_CHUNK, ts=HG_TS):
    B, S, _ = proj3.shape
    ts = min(ts, S)
    seg, bm, L = _hgrn_consts(C)
    blk = lambda k: pl.BlockSpec((1, ts, HG_K), lambda b, s: (b, s, P_HG // HG_K + k))
    return pl.pallas_call(
        functools.partial(_hgrn_kernel, C=C, L=L, nchunk=ts // C),
        grid=(B, S // ts),
        in_specs=[
            blk(0), blk(1), blk(2), blk(3),
            pl.BlockSpec((3, HG_K), lambda b, s: (0, 0)),
            pl.BlockSpec((1, HG_DV), lambda b, s: (0, 0)),
            pl.BlockSpec(seg.shape, lambda b, s: (0, 0)),
            pl.BlockSpec(bm.shape, lambda b, s: (0, 0, 0)),
        ],
        out_specs=pl.BlockSpec((1, ts, HG_V), lambda b, s: (b, s, 0)),
        out_shape=jax.ShapeDtypeStruct((B, S, HG_V), F32),
        scratch_shapes=[pltpu.VMEM((HG_HEADS, HG_DV, HG_DK), F32)],
        compiler_params=_cparams(("parallel", "arbitrary")),
        name="hgrn2",
    )(proj3, proj3, proj3, proj3, lbp, gain, jnp.asarray(seg, BF16), jnp.asarray(bm))


def _half_rms(x, gain2, lane):
    sq = x * x
    s_lo = jnp.sum(jnp.where(lane < NSA_DH, sq, 0.0), axis=-1, keepdims=True)
    s_hi = jnp.sum(jnp.where(lane < NSA_DH, 0.0, sq), axis=-1, keepdims=True)
    ms = jnp.where(lane < NSA_DH, s_lo, s_hi) * (1.0 / NSA_DH)
    return x * lax.rsqrt(ms + EPS) * gain2


def _gelu_tanh(x):
    return 0.5 * x * (1.0 + jnp.tanh(math.sqrt(2.0 / math.pi) * (x + 0.044715 * (x * x * x))))


def _prep_kernel(kv_ref, kc_ref, vc_ref, gsel_ref, gwin_ref, gcmp_ref, pos_ref, w1_ref, w1c_ref,
                 b1_ref, w2_ref, ksa_ref, vst_ref, kwa_ref, vwt_ref, kca_ref, vct_ref, *, S, rows):
    lane = lax.broadcasted_iota(jnp.int32, (rows, LANES), 1)
    rowi = lax.broadcasted_iota(jnp.int32, (rows, LANES), 0)
    ones_row = (lax.broadcasted_iota(jnp.int32, (NSA_DH, rows), 0) == 0).astype(F32)
    low = lane < NSA_DH

    def body(i, carry):
        r0 = pl.multiple_of(i * rows, rows)
        blk_hot = jnp.where(lane - NSA_DH == ((r0 + rowi) >> SEL_SHIFT), 1.0, 0.0)
        for src, gain_ref, aug, k_dst, v_dst, vrows in (
                (0, gsel_ref, blk_hot, ksa_ref, vst_ref, rows),
                (2, gwin_ref, jnp.zeros((rows, LANES), F32), kwa_ref, vwt_ref, Q_BLOCK)):
            kk = kv_ref[0, pl.ds(r0, rows), src * LANES:(src + 1) * LANES]
            vv = kv_ref[0, pl.ds(r0, rows), (src + 1) * LANES:(src + 2) * LANES]
            kn = _half_rms(kk, gain_ref[...], lane)
            k_dst[0, 0, pl.ds(r0, rows), :] = jnp.where(low, kn, aug).astype(BF16)
            k_dst[0, 1, pl.ds(r0, rows), :] = jnp.where(low, pltpu.roll(kn, NSA_DH, 1), aug).astype(BF16)
            vt = vv.T
            for g in range(NSA_KV):
                vg = jnp.concatenate([vt[g * NSA_DH:(g + 1) * NSA_DH], ones_row], axis=0).astype(BF16)
                for u in range(rows // vrows):
                    v_dst[0, g, i * (rows // vrows) + u] = vg[:, u * vrows:(u + 1) * vrows]
        return carry

    lax.fori_loop(0, S // rows, body, 0)

    nc = S // CMP_STRIDE
    outs = []
    for j, src_ref in ((0, kc_ref), (1, vc_ref)):
        bias = jnp.dot(pos_ref[j].astype(BF16), w1_ref[j], preferred_element_type=F32)[0:1] + b1_ref[j]
        pj = jnp.dot(src_ref[0].astype(BF16), w1c_ref[j], preferred_element_type=F32)
        per_g = []
        for g in range(NSA_KV):
            first = pj[:, (2 * g) * CMP_HID:(2 * g + 1) * CMP_HID]
            second = pj[:, (2 * g + 1) * CMP_HID:(2 * g + 2) * CMP_HID]
            hid = _gelu_tanh(first + pltpu.roll(second, nc - 1, 0) + bias).astype(BF16)
            if j == 0:
                per_g.append(jnp.dot(hid, w2_ref[0], preferred_element_type=F32))
            else:
                per_g.append(lax.dot_general(w2_ref[1], hid, (((1,), (1,)), ((), ())),
                                             preferred_element_type=F32))
        outs.append(per_g)
    for g in range(NSA_KV):
        kc = outs[0][g]
        ms = jnp.sum(kc * kc, axis=-1, keepdims=True) * (1.0 / NSA_DH)
        kca_ref[0, g] = (kc * lax.rsqrt(ms + EPS) * gcmp_ref[...]).astype(BF16)
        vct_ref[0, g] = outs[1][g].astype(BF16)


def _nsa_prep(proj3, kc2, vc2, gsel, gwin, gcmp, pos8, w1, w1c, b1, w2p):
    B, S, _ = proj3.shape
    nc = S // CMP_STRIDE
    rows = SEL_CHUNK
    full = lambda shp: pl.BlockSpec(shp, lambda b: (0,) * len(shp))
    whole = lambda shp: pl.BlockSpec((1,) + shp, lambda b: (b,) + (0,) * len(shp))
    shapes = [(NSA_KV, S, LANES), (NSA_KV, S // SEL_CHUNK, LANES, SEL_CHUNK),
              (NSA_KV, S, LANES), (NSA_KV, S // Q_BLOCK, LANES, Q_BLOCK),
              (NSA_KV, nc, LANES), (NSA_KV, LANES, nc)]
    return pl.pallas_call(
        functools.partial(_prep_kernel, S=S, rows=rows),
        grid=(B,),
        in_specs=[
            pl.BlockSpec((1, S, 4 * NSA_KVW), lambda b: (b, 0, P_KV // (4 * NSA_KVW))),
            pl.BlockSpec((1, nc, CMP_STRIDE * NSA_KVW), lambda b: (b, 0, 0)),
            pl.BlockSpec((1, nc, CMP_STRIDE * NSA_KVW), lambda b: (b, 0, 0)),
            full((1, LANES)), full((1, LANES)), full((1, LANES)),
            full(pos8.shape), full(w1.shape), full(w1c.shape), full(b1.shape), full(w2p.shape),
        ],
        out_specs=[whole(s) for s in shapes],
        out_shape=[jax.ShapeDtypeStruct((B,) + s, BF16) for s in shapes],
        compiler_params=_cparams(("parallel",)),
        name="nsa_prep",
    )(proj3, kc2, vc2, gsel, gwin, gcmp, pos8, w1, w1c, b1, w2p)


def _sel_table_geometry():
    last_start = (REL_BUCKETS // 2) * (REL_MAX_DIST / (REL_BUCKETS // 2)) ** (
        (REL_BUCKETS - REL_BUCKETS // 2 - 1) / (REL_BUCKETS - REL_BUCKETS // 2))
    dsat = int(math.ceil((last_start + 16 + SEL_CHUNK) / LANES)) * LANES
    return dsat, dsat + SEL_CHUNK


def _attn_kernel(q_ref, gate_ref, kca_ref, vct_ref, ksa_ref, vst_ref, kwa_ref, vwt_ref, tabc_ref,
                 tabw_ref, tabs_ref, ovl_ref, qg_ref, o_ref, m_ref, acc_ref, sa_ref, sb_ref, *, nsel, dsat):
    qi = pl.program_id(2)
    QB = Q_BLOCK
    R = NSA_HPG * QB
    DH = NSA_DH

    xt = q_ref[0].T
    cols = []
    for h in range(NSA_HPG):
        xh = xt[h * DH:(h + 1) * DH]
        ms = jnp.mean(xh * xh, axis=0, keepdims=True)
        cols.append(xh * lax.rsqrt(ms + EPS) * qg_ref[...])
    qt = jnp.concatenate(cols, axis=1).astype(BF16)
    qta = jnp.concatenate([qt, jnp.zeros_like(qt)], axis=0)

    s = jnp.dot(kca_ref[0, 0], qta, preferred_element_type=F32) + tabc_ref[0, 0]
    mx = jnp.max(s, axis=0, keepdims=True)
    e = jnp.exp2(s - mx)
    p = jnp.where(s > 0.5 * NEG_BIG, e / jnp.sum(e, axis=0, keepdims=True), 0.0)
    o_c = jnp.dot(vct_ref[0, 0], p.astype(BF16), preferred_element_type=F32)

    psum = p[:, 0:QB] + p[:, QB:2 * QB] + p[:, 2 * QB:3 * QB] + p[:, 3 * QB:4 * QB]
    p_hi = psum.astype(BF16)
    p_lo = (psum - p_hi.astype(F32)).astype(BF16)
    imp = (jnp.dot(ovl_ref[...], p_hi, preferred_element_type=F32)
           + jnp.dot(ovl_ref[...], p_lo, preferred_element_type=F32))
    t = qi * QB + lax.broadcasted_iota(jnp.int32, (SEL_ROWS, QB), 1)
    jrow = lax.broadcasted_iota(jnp.int32, (SEL_ROWS, QB), 0)
    jcur = t >> SEL_SHIFT
    forced = (jrow == 0) | (jrow == jcur) | (jrow == jcur - 1)
    work = jnp.where(jrow * SEL_BLOCK <= t, imp + jnp.where(forced, FORCE_BONUS, 0.0), -1.0)
    work = jnp.where(jrow < nsel, work, -5.0)
    jf = jrow.astype(F32)
    sel = jnp.zeros((SEL_ROWS, QB), F32)
    for _ in range(min(SEL_TOPK, nsel)):
        best = jnp.max(work, axis=0, keepdims=True)
        first = jnp.min(jnp.where(work == best, jf, float(SEL_ROWS)), axis=0, keepdims=True)
        hit = jf == first
        sel = jnp.where(hit, 1.0, sel)
        work = jnp.where(hit, -3.0, work)
    selneg = jnp.where(sel > 0.5, 0.0, NEG_BIG).astype(BF16)
    qts = jnp.concatenate([qt, jnp.concatenate([selneg] * NSA_HPG, axis=1)], axis=0)

    tiles, vts = [], []
    for r in range(WIN_BLOCKS):
        kb = qi - (WIN_BLOCKS - 1) + r
        kbc = jnp.maximum(kb, 0)
        kw = kwa_ref[0, 0, pl.ds(pl.multiple_of(kbc * QB, QB), QB), :]
        vts.append(vwt_ref[0, 0, kbc])
        sw = jnp.dot(kw, qta, preferred_element_type=F32) + tabw_ref[0, 0, r * QB:(r + 1) * QB, :]
        tiles.append(sw + jnp.where(kb >= 0, 0.0, NEG_BIG))
    mw = tiles[0]
    for r in range(1, WIN_BLOCKS):
        mw = jnp.maximum(mw, tiles[r])
    mw = jnp.max(mw, axis=0, keepdims=True)
    o_w = jnp.zeros((LANES, R), F32)
    for r in range(WIN_BLOCKS):
        o_w = o_w + jnp.dot(vts[r], jnp.exp2(tiles[r] - mw).astype(BF16), preferred_element_type=F32)

    m_ref[...] = jnp.full_like(m_ref, NEG_BIG)
    acc_ref[...] = jnp.zeros_like(acc_ref)

    def scores(c):
        k0 = pl.multiple_of(c * SEL_CHUNK, SEL_CHUNK)
        delta = qi * QB - c * SEL_CHUNK
        start = pl.multiple_of(dsat - jnp.minimum(delta, dsat), LANES)
        return (jnp.dot(ksa_ref[0, 0, pl.ds(k0, SEL_CHUNK), :], qts, preferred_element_type=F32)
                + tabs_ref[0, 0, pl.ds(start, SEL_CHUNK), :])

    def consume(s_ref, c):
        sc = s_ref[...]
        m_old = m_ref[...]
        m_new = jnp.maximum(m_old, jnp.max(sc, axis=0, keepdims=True))
        alpha = jnp.exp2(m_old - m_new)
        pe = jnp.exp2(sc - m_new).astype(BF16)
        acc_ref[...] = alpha * acc_ref[...] + jnp.dot(vst_ref[0, 0, c], pe, preferred_element_type=F32)
        m_ref[...] = m_new

    n_chunks = (qi * QB + QB - 1) // SEL_CHUNK + 1
    n_pairs = (n_chunks - 1) // 2
    sa_ref[...] = scores(0)

    def sel_pair(pr, carry):
        c = 2 * pr
        sb_ref[...] = scores(c + 1)
        consume(sa_ref, c)
        sa_ref[...] = scores(c + 2)
        consume(sb_ref, c + 1)
        return carry

    lax.fori_loop(0, n_pairs, sel_pair, 0)

    @pl.when(n_chunks % 2 == 0)
    def _():
        sb_ref[...] = scores(n_chunks - 1)
        consume(sa_ref, n_chunks - 2)
        consume(sb_ref, n_chunks - 1)

    @pl.when(n_chunks % 2 == 1)
    def _():
        consume(sa_ref, n_chunks - 1)

    acc = acc_ref[...]

    gt = jax.nn.sigmoid(gate_ref[0]).T
    o_s = acc[0:DH] / acc[DH:DH + 1]
    o_w = o_w[0:DH] / o_w[DH:DH + 1]
    heads = []
    for h in range(NSA_HPG):
        cs = slice(h * QB, (h + 1) * QB)
        heads.append(gt[3 * h:3 * h + 1] * o_c[0:DH, cs] + gt[3 * h + 1:3 * h + 2] * o_s[:, cs]
                     + gt[3 * h + 2:3 * h + 3] * o_w[:, cs])
    o_ref[0] = jnp.concatenate(heads, axis=0).T


def _nsa_attn(proj3, prep, tabc, tabw, tabs, ovl, qgain, dsat):
    B, S, _ = proj3.shape
    ksa, vst, kwa, vwt, kca, vct = prep
    nq = S // Q_BLOCK
    nc = S // CMP_STRIDE
    nsel = S // SEL_BLOCK
    R = NSA_HPG * Q_BLOCK
    gw = NSA_HPG * NSA_DH
    per_bg = lambda a: pl.BlockSpec((1, 1) + a.shape[2:], lambda b, g, i: (b, g) + (0,) * (a.ndim - 2))
    return pl.pallas_call(
        functools.partial(_attn_kernel, nsel=nsel, dsat=dsat),
        grid=(B, NSA_KV, nq),
        in_specs=[
            pl.BlockSpec((1, Q_BLOCK, gw), lambda b, g, i: (b, i, P_NQ // gw + g)),
            pl.BlockSpec((1, Q_BLOCK, LANES), lambda b, g, i: (b, i, P_GATE // LANES + g)),
            per_bg(kca), per_bg(vct), per_bg(ksa), per_bg(vst), per_bg(kwa), per_bg(vwt),
            pl.BlockSpec((1, 1, nc, R), lambda b, g, i: (g, i, 0, 0)),
            pl.BlockSpec((1, 1) + tabw.shape[2:], lambda b, g, i: (g, 0, 0, 0)),
            pl.BlockSpec((1, 1) + tabs.shape[2:], lambda b, g, i: (g, 0, 0, 0)),
            pl.BlockSpec(ovl.shape, lambda b, g, i: (0, 0)),
            pl.BlockSpec(qgain.shape, lambda b, g, i: (0, 0)),
        ],
        out_specs=pl.BlockSpec((1, Q_BLOCK, gw), lambda b, g, i: (b, i, g)),
        out_shape=jax.ShapeDtypeStruct((B, S, NSA_Q), F32),
        scratch_shapes=[pltpu.VMEM((1, R), F32), pltpu.VMEM((LANES, R), F32),
                        pltpu.VMEM((SEL_CHUNK, R), F32), pltpu.VMEM((SEL_CHUNK, R), F32)],
        compiler_params=_cparams(("parallel", "parallel", "arbitrary")),
        name="nsa_attn",
    )(proj3, proj3, kca, vct, ksa, vst, kwa, vwt, tabc, tabw, tabs, ovl, qgain)


def _rel_bucket(d):
    max_exact = REL_BUCKETS // 2
    d = jnp.maximum(d, 0)
    df = jnp.maximum(d, 1).astype(F32)
    large = max_exact + (jnp.log(df / max_exact) / math.log(REL_MAX_DIST / max_exact)
                         * (REL_BUCKETS - max_exact)).astype(jnp.int32)
    return jnp.where(d < max_exact, d, jnp.minimum(large, REL_BUCKETS - 1))


def _bias_kernel(rb_ref, bkt_ref, o_ref, *, rows):
    g = pl.program_id(0)
    step = 32
    for r0 in range(0, rows, step):
        bkt = bkt_ref[0, r0:r0 + step, :]
        acc = [jnp.full(bkt.shape, NEG_BIG, F32) for _ in range(NSA_HPG)]
        for b in range(REL_BUCKETS):
            hit = bkt == b
            for h in range(NSA_HPG):
                acc[h] = jnp.where(hit, rb_ref[b, g * NSA_HPG + h] * LOG2E, acc[h])
        for h in range(NSA_HPG):
            o_ref[0, 0, r0:r0 + step, h * Q_BLOCK:(h + 1) * Q_BLOCK] = acc[h]


def _bias_table(rel_bias, bucket):
    nt, rows, cols = bucket.shape
    return pl.pallas_call(
        functools.partial(_bias_kernel, rows=rows),
        grid=(NSA_KV, nt),
        in_specs=[
            pl.BlockSpec(memory_space=pltpu.SMEM),
            pl.BlockSpec((1, rows, cols), lambda g, t: (t, 0, 0)),
        ],
        out_specs=pl.BlockSpec((1, 1, rows, NSA_HPG * cols), lambda g, t: (g, t, 0, 0)),
        out_shape=jax.ShapeDtypeStruct((NSA_KV, nt, rows, NSA_HPG * cols), F32),
        compiler_params=_cparams(("parallel", "parallel")),
        name="bias_table",
    )(rel_bias, bucket)


def _bias_tables(rel_bias, S, dsat, strip_len):
    nq = S // Q_BLOCK
    nc = S // CMP_STRIDE
    n_cmp = (S - CMP_BLOCK) // CMP_STRIDE + 1
    ar = lambda n: jnp.arange(n, dtype=jnp.int32)
    bucket = lambda d, valid: jnp.where(valid, _rel_bucket(d), -1)
    i = ar(Q_BLOCK)[None, None, :]
    c = ar(nc)[None, :, None]
    d_c = ar(nq)[:, None, None] * Q_BLOCK + i - (c * CMP_STRIDE + CMP_BLOCK - 1)
    tabc = _bias_table(rel_bias, bucket(d_c, (d_c >= 0) & (c < n_cmp)))
    d_w = i + WINDOW - ar(WIN_BLOCKS * Q_BLOCK)[None, :, None]
    tabw = _bias_table(rel_bias, bucket(d_w, (d_w >= 0) & (d_w < WINDOW)))
    d_s = i - ar(strip_len)[None, :, None] + dsat
    tabs = _bias_table(rel_bias, bucket(d_s, d_s >= 0))
    return tabc, tabw, tabs


def _overlap_matrix(S):
    nc = S // CMP_STRIDE
    nsel = S // SEL_BLOCK
    n_cmp = (S - CMP_BLOCK) // CMP_STRIDE + 1
    c = np.arange(nc)[None, :]
    sel_start = (np.arange(SEL_ROWS) * SEL_BLOCK)[:, None]
    ovl = ((c * CMP_STRIDE < sel_start + SEL_BLOCK) & (c * CMP_STRIDE + CMP_BLOCK - 1 >= sel_start)
           & (c < n_cmp) & (np.arange(SEL_ROWS)[:, None] < nsel))
    return jnp.asarray(ovl, BF16)


def _layer_params(l, lower_bounds, norm_mix, w_in, hg_out_gain, nsa_qk_gain, cmp_pos, cmp_w1, cmp_b1,
                  cmp_w2, w_branch_a, w_branch_b, w_out, norm_mlp, w_up, w_down):
    splits = (HG_K, HG_K, HG_V, HG_V, NSA_Q, NSA_KVW, NSA_KVW, NSA_KVW, NSA_KVW, NSA_KVW, NSA_KVW,
              3 * NSA_HEADS, D_MODEL, D_MODEL)
    offs = np.concatenate([[0], np.cumsum(splits)])
    col = lambda k: w_in[l][:, offs[k]:offs[k + 1]]
    gate_w = col(11)
    zpad = jnp.zeros((D_MODEL, LANES - 3 * NSA_HPG), F32)
    gate_cols = []
    for g in range(NSA_KV):
        gate_cols += [gate_w[:, g * 3 * NSA_HPG:(g + 1) * 3 * NSA_HPG], zpad]
    w_main = jnp.concatenate([col(0), col(1), col(2), col(3), col(12), col(13),
                              col(7), col(8), col(9), col(10), col(4)] + gate_cols, axis=1).astype(BF16)
    w_cmp = jnp.concatenate([col(5), col(6)], axis=1).astype(BF16)

    lb = lower_bounds[l]
    lbp = jnp.stack([jnp.log(lb), jnp.log1p(-lb), 1.0 - lb])

    w1 = cmp_w1[l].reshape(2, 2, CMP_STRIDE, NSA_DH, CMP_HID)
    w1c = jnp.zeros((2, CMP_STRIDE, NSA_KV, NSA_DH, NSA_KV, 2, CMP_HID), F32)
    for g in range(NSA_KV):
        w1c = w1c.at[:, :, g, :, g, :, :].set(w1.transpose(0, 2, 3, 1, 4))
    w1c = w1c.reshape(2, CMP_STRIDE * NSA_KVW, NSA_KV * 2 * CMP_HID).astype(BF16)
    pos8 = jnp.zeros((2, 8, CMP_BLOCK * NSA_DH), F32).at[:, 0, :].set(cmp_pos[l].reshape(2, -1))
    w2p = jnp.zeros((2, CMP_HID, LANES), F32)
    w2p = w2p.at[0, :, :NSA_DH].set(cmp_w2[l, 0]).at[1, :NSA_DH, :].set(cmp_w2[l, 1].T).astype(BF16)
    qk = nsa_qk_gain[l]
    z64 = jnp.zeros((NSA_DH,), F32)
    return dict(
        norm_mix=norm_mix[l][None, :], w_main=w_main, w_cmp=w_cmp, lbp=lbp,
        hg_gain=hg_out_gain[l][None, :],
        gsel=jnp.concatenate([qk[2], qk[2]])[None, :], gwin=jnp.concatenate([qk[3], qk[3]])[None, :],
        gcmp=jnp.concatenate([qk[1], z64])[None, :],
        qgain=jnp.broadcast_to((qk[0] * (NSA_DH ** -0.5 * LOG2E))[:, None], (NSA_DH, Q_BLOCK)),
        pos8=pos8, w1=cmp_w1[l].astype(BF16), w1c=w1c, b1=cmp_b1[l][:, None, :], w2p=w2p,
        wa=w_branch_a[l].astype(BF16), wb=w_branch_b[l].astype(BF16), wo=w_out[l].astype(BF16),
        norm_mlp=norm_mlp[l][None, :], wu=w_up[l].astype(BF16), wd=w_down[l].astype(BF16))


def kernel(x, rel_bias, hg_lb_logits, norm_mix, w_in, hg_out_gain, nsa_qk_gain, cmp_pos, cmp_w1, cmp_b1, cmp_w2, w_branch_a, w_branch_b, w_out, norm_mlp, w_up, w_down):
    B, S, D = x.shape
    T = B * S
    depth = w_in.shape[0]
    assert D == D_MODEL and S % SEL_CHUNK == 0 and S >= WINDOW and S // SEL_BLOCK <= SEL_ROWS
    lb_cum = jnp.cumsum(jax.nn.softmax(hg_lb_logits.astype(F32), axis=0), axis=0)
    lower_bounds = lb_cum - lb_cum[0:1]

    dsat, strip_len = _sel_table_geometry()
    tabc, tabw, tabs = _bias_tables(rel_bias, S, dsat, strip_len)
    ovl = _overlap_matrix(S)
    nc = S // CMP_STRIDE
    tm = min(1024, T)

    x2 = x.reshape(T, D)
    for l in range(depth):
        p = _layer_params(l, lower_bounds, norm_mix, w_in, hg_out_gain, nsa_qk_gain, cmp_pos, cmp_w1,
                          cmp_b1, cmp_w2, w_branch_a, w_branch_b, w_out, norm_mlp, w_up, w_down)
        proj, kc, vc = _rms_proj(x2, p["norm_mix"], p["w_main"], p["w_cmp"], tm, P_COLS // 7)
        proj3 = proj.reshape(B, S, P_COLS)
        kc2 = kc.reshape(B, nc, CMP_STRIDE * NSA_KVW)
        vc2 = vc.reshape(B, nc, CMP_STRIDE * NSA_KVW)
        o_a = _hgrn(proj3, p["lbp"], p["hg_gain"])
        prep = _nsa_prep(proj3, kc2, vc2, p["gsel"], p["gwin"], p["gcmp"], p["pos8"], p["w1"],
                         p["w1c"], p["b1"], p["w2p"])
        o_b = _nsa_attn(proj3, prep, tabc, tabw, tabs, ovl, p["qgain"], dsat)
        x2 = _merge(x2, o_a.reshape(T, HG_V), o_b.reshape(T, NSA_Q), proj, p["wa"], p["wb"], p["wo"],
                    min(512, T))
        x2 = _mlp(x2, p["norm_mlp"], p["wu"], p["wd"], tm, 1024)
    return x2.reshape(B, S, D)
```

```python
import functools
import math

import numpy as np
import jax
import jax.numpy as jnp
from jax import lax
from jax.experimental import pallas as pl
from jax.experimental.pallas import tpu as pltpu

F32 = jnp.float32
BF16 = jnp.bfloat16

D_MODEL = 1024
HG_HEADS = 4
HG_DK = 128
HG_DV = 128
HG_K = HG_HEADS * HG_DK
HG_V = HG_HEADS * HG_DV
NSA_HEADS = 8
NSA_KV = 2
NSA_DH = 64
NSA_HPG = NSA_HEADS // NSA_KV
NSA_Q = NSA_HEADS * NSA_DH
NSA_KVW = NSA_KV * NSA_DH
CMP_BLOCK = 32
CMP_STRIDE = 16
CMP_HID = 128
SEL_BLOCK = 64
SEL_TOPK = 8
WINDOW = 512
FORCE_BONUS = 1000.0
REL_BUCKETS = 32
REL_MAX_DIST = 1024
D_FF = 4 * D_MODEL
EPS = 1e-6
NEG_BIG = -1e30
LOG2E = math.log2(math.e)

LANES = 128
VMEM_LIMIT = 56 * 1024 * 1024

P_GAB = 0
P_HG = P_GAB + 2 * D_MODEL
P_KV = P_HG + 3 * HG_K
P_NQ = P_KV + 4 * NSA_KVW
P_GATE = P_NQ + NSA_Q
P_COLS = P_GATE + NSA_KV * LANES
P_TILE = P_COLS // 2

HG_CHUNK = 128
HG_TS = 512
Q_BLOCK = 128
SEL_CHUNK = 512
WIN_BLOCKS = WINDOW // Q_BLOCK + 1
SEL_SHIFT = int(math.log2(SEL_BLOCK))
SEL_ROWS = LANES - NSA_DH


def _cparams(sem):
    return pltpu.CompilerParams(dimension_semantics=sem, vmem_limit_bytes=VMEM_LIMIT)


def _rms_rows(x, gain):
    ms = jnp.mean(x * x, axis=-1, keepdims=True)
    return x * lax.rsqrt(ms + EPS) * gain


def _proj_kernel(x_ref, g_ref, w_ref, wx_ref, o_ref, z_ref, kc_ref, vc_ref, h_ref):
    @pl.when(pl.program_id(1) == 0)
    def _():
        h = _rms_rows(x_ref[...], g_ref[...]).astype(BF16)
        h_ref[...] = h
        c = jnp.dot(h, wx_ref[...], preferred_element_type=F32)
        z_ref[...] = c[:, :HG_K]
        kc_ref[...] = c[:, HG_K:HG_K + NSA_KVW].astype(BF16)
        vc_ref[...] = c[:, HG_K + NSA_KVW:].astype(BF16)

    o_ref[...] = jnp.dot(h_ref[...], w_ref[...], preferred_element_type=F32).astype(BF16)


def _rms_proj(x2, gain, w, wx, tm, tn):
    T, D = x2.shape
    N = w.shape[1]
    cmp_sds = jax.ShapeDtypeStruct((T, NSA_KVW), BF16)
    return pl.pallas_call(
        _proj_kernel,
        grid=(T // tm, N // tn),
        in_specs=[
            pl.BlockSpec((tm, D), lambda i, j: (i, 0)),
            pl.BlockSpec((1, D), lambda i, j: (0, 0)),
            pl.BlockSpec((D, tn), lambda i, j: (0, j)),
            pl.BlockSpec(wx.shape, lambda i, j: (0, 0)),
        ],
        out_specs=[
            pl.BlockSpec((tm, tn), lambda i, j: (i, j)),
            pl.BlockSpec((tm, HG_K), lambda i, j: (i, 0)),
            pl.BlockSpec((tm, NSA_KVW), lambda i, j: (i, 0)),
            pl.BlockSpec((tm, NSA_KVW), lambda i, j: (i, 0)),
        ],
        out_shape=[jax.ShapeDtypeStruct((T, N), BF16), jax.ShapeDtypeStruct((T, HG_K), F32),
                   cmp_sds, cmp_sds],
        scratch_shapes=[pltpu.VMEM((tm, D), BF16)],
        compiler_params=_cparams(("parallel", "arbitrary")),
        name="rms_proj",
    )(x2, gain, w, wx)


def _mlp_kernel(x_ref, g_ref, wu_ref, wd_ref, o_ref, h_ref):
    f = pl.program_id(1)

    @pl.when(f == 0)
    def _():
        x = x_ref[...]
        h_ref[...] = _rms_rows(x, g_ref[...]).astype(BF16)
        o_ref[...] = x

    u = jnp.dot(h_ref[...], wu_ref[...], preferred_element_type=F32)
    u = jnp.square(jnp.maximum(u, 0.0)).astype(BF16)
    o_ref[...] += jnp.dot(u, wd_ref[...], preferred_element_type=F32)


def _mlp(x2, gain, wu, wd, tm, tf):
    T, D = x2.shape
    FF = wu.shape[1]
    return pl.pallas_call(
        _mlp_kernel,
        grid=(T // tm, FF // tf),
        in_specs=[
            pl.BlockSpec((tm, D), lambda i, f: (i, 0)),
            pl.BlockSpec((1, D), lambda i, f: (0, 0)),
            pl.BlockSpec((D, tf), lambda i, f: (0, f)),
            pl.BlockSpec((tf, D), lambda i, f: (f, 0)),
        ],
        out_specs=pl.BlockSpec((tm, D), lambda i, f: (i, 0)),
        out_shape=jax.ShapeDtypeStruct((T, D), F32),
        scratch_shapes=[pltpu.VMEM((tm, D), BF16)],
        compiler_params=_cparams(("parallel", "arbitrary")),
        name="mlp",
    )(x2, gain, wu, wd)


def _merge_kernel(x_ref, oa_ref, ob_ref, ga_ref, gb_ref, wa_ref, wb_ref, wo_ref, o_ref):
    ya = jnp.dot(oa_ref[...], wa_ref[...], preferred_element_type=F32)
    yb = jnp.dot(ob_ref[...], wb_ref[...], preferred_element_type=F32)
    mixed = (jax.nn.sigmoid(ga_ref[...].astype(F32)) * ya
             + jax.nn.sigmoid(gb_ref[...].astype(F32)) * yb)
    o_ref[...] = x_ref[...] + jnp.dot(mixed.astype(BF16), wo_ref[...], preferred_element_type=F32)


def _merge(x2, oa, ob, proj, wa, wb, wo, tm):
    T, D = x2.shape
    ga_blk = P_GAB // D
    return pl.pallas_call(
        _merge_kernel,
        grid=(T // tm,),
        in_specs=[
            pl.BlockSpec((tm, D), lambda i: (i, 0)),
            pl.BlockSpec((tm, HG_V), lambda i: (i, 0)),
            pl.BlockSpec((tm, NSA_Q), lambda i: (i, 0)),
            pl.BlockSpec((tm, D), lambda i: (i, ga_blk)),
            pl.BlockSpec((tm, D), lambda i: (i, ga_blk + 1)),
            pl.BlockSpec((HG_V, D), lambda i: (0, 0)),
            pl.BlockSpec((NSA_Q, D), lambda i: (0, 0)),
            pl.BlockSpec((D, D), lambda i: (0, 0)),
        ],
        out_specs=pl.BlockSpec((tm, D), lambda i: (i, 0)),
        out_shape=jax.ShapeDtypeStruct((T, D), F32),
        compiler_params=_cparams(("parallel",)),
        name="merge",
    )(x2, oa, ob, proj, proj, wa, wb, wo)


def _hgrn_consts(C):
    L = int(math.log2(C))
    idx = np.arange(C)
    mats = [(idx[None, :] <= idx[:, None]),
            (idx[None, :] > idx[:, None])]
    bmask = []
    for l in range(L):
        m = 1 << l
        r = (idx & ~(2 * m - 1)) + m
        lo = np.minimum(idx, r)[:, None]
        hi = np.maximum(idx, r)[:, None]
        mats.append((idx[None, :] > lo) & (idx[None, :] <= hi))
        blk = idx >> (l + 1)
        bmask.append(blk[:, None] == blk[None, :])
    bmask.append(idx[:, None] == idx[None, :])
    seg = np.concatenate(mats, axis=0).astype(np.float32)
    seg = np.concatenate([seg, seg], axis=1)
    return seg, np.stack(bmask).astype(np.float32), L


def _hgrn_kernel(q_ref, f_ref, i_ref, g_ref, lbp_ref, gain_ref, seg_ref, bm_ref, o_ref, st_ref,
                 *, C, L, nchunk):
    @pl.when(pl.program_id(1) == 0)
    def _():
        st_ref[...] = jnp.zeros_like(st_ref)

    rowi = lax.broadcasted_iota(jnp.int32, (C, HG_DK), 0)
    nt = (((1,), (1,)), ((), ()))
    tn = (((0,), (0,)), ((), ()))

    def chunk(c, carry):
        r0 = pl.multiple_of(c * C, C)
        for h in range(HG_HEADS):
            cs = slice(h * HG_DK, (h + 1) * HG_DK)
            qb = q_ref[0, pl.ds(r0, C), cs]
            q = qb.astype(F32)
            z = f_ref[0, pl.ds(r0, C), cs]
            vb = i_ref[0, pl.ds(r0, C), cs]
            gg = g_ref[0, pl.ds(r0, C), cs].astype(F32)
            log_lb = lbp_ref[0:1, cs]
            log1m_lb = lbp_ref[1:2, cs]
            one_m_lb = lbp_ref[2:3, cs]
            y = log1m_lb + (jnp.minimum(z, 0.0) - jnp.log1p(jnp.exp(-jnp.abs(z))))
            logf = jnp.maximum(log_lb, y) + jnp.log1p(jnp.exp(-jnp.abs(log_lb - y)))
            k = one_m_lb * jax.nn.sigmoid(-z)
            g_hi = logf.astype(BF16)
            g_lo = (logf - g_hi.astype(F32)).astype(BF16)
            seg = jnp.dot(seg_ref[...], jnp.concatenate([g_hi, g_lo], axis=0),
                          preferred_element_type=F32)
            b = seg[0:C]
            brev = seg[C:2 * C]
            qd = (q * jnp.exp(b)).astype(BF16)
            kd = (k * jnp.exp(brev)).astype(BF16)
            st = st_ref[h]
            o = lax.dot_general(qd, st.astype(BF16), nt, preferred_element_type=F32)
            s = lax.dot_general(qb, k.astype(BF16), nt,
                                preferred_element_type=F32) * bm_ref[L]
            for l in range(L):
                e = jnp.exp(seg[(2 + l) * C:(3 + l) * C])
                second = (rowi & (1 << l)) != 0
                ql = jnp.where(second, q * e, 0.0).astype(BF16)
                kl = jnp.where(second, 0.0, k * e).astype(BF16)
                s = s + lax.dot_general(ql, kl, nt, preferred_element_type=F32) * bm_ref[l]
            o = o + jnp.dot(s.astype(BF16), vb, preferred_element_type=F32)
            st_ref[h] = (st * jnp.exp(b[C - 1:C, :])
                         + lax.dot_general(vb, kd, tn, preferred_element_type=F32))
            o = _rms_rows(o, gain_ref[...]) * (gg * jax.nn.sigmoid(gg))
            o_ref[0, pl.ds(r0, C), cs] = o.astype(BF16)
        return carry

    lax.fori_loop(0, nchunk, chunk, 0)


def _hgrn(proj3, z3, lbp, gain, C=HG_CHUNK, ts=HG_TS):
    B, S, _ = proj3.shape
    ts = min(ts, S)
    seg, bm, L = _hgrn_consts(C)
    blk = lambda k: pl.BlockSpec((1, ts, HG_K), lambda b, s: (b, s, P_HG // HG_K + k))
    return pl.pallas_call(
        functools.partial(_hgrn_kernel, C=C, L=L, nchunk=ts // C),
        grid=(B, S // ts),
        in_specs=[
            blk(0), pl.BlockSpec((1, ts, HG_K), lambda b, s: (b, s, 0)), blk(1), blk(2),
            pl.BlockSpec((3, HG_K), lambda b, s: (0, 0)),
            pl.BlockSpec((1, HG_DV), lambda b, s: (0, 0)),
            pl.BlockSpec(seg.shape, lambda b, s: (0, 0)),
            pl.BlockSpec(bm.shape, lambda b, s: (0, 0, 0)),
        ],
        out_specs=pl.BlockSpec((1, ts, HG_V), lambda b, s: (b, s, 0)),
        out_shape=jax.ShapeDtypeStruct((B, S, HG_V), BF16),
        scratch_shapes=[pltpu.VMEM((HG_HEADS, HG_DV, HG_DK), F32)],
        compiler_params=_cparams(("parallel", "arbitrary")),
        name="hgrn2",
    )(proj3, z3, proj3, proj3, lbp, gain, jnp.asarray(seg, BF16), jnp.asarray(bm))


def _half_rms(x, gain2, lane):
    sq = x * x
    s_lo = jnp.sum(jnp.where(lane < NSA_DH, sq, 0.0), axis=-1, keepdims=True)
    s_hi = jnp.sum(jnp.where(lane < NSA_DH, 0.0, sq), axis=-1, keepdims=True)
    ms = jnp.where(lane < NSA_DH, s_lo, s_hi) * (1.0 / NSA_DH)
    return x * lax.rsqrt(ms + EPS) * gain2


def _gelu_tanh(x):
    return 0.5 * x * (1.0 + jnp.tanh(math.sqrt(2.0 / math.pi) * (x + 0.044715 * (x * x * x))))


def _prep_kernel(kv_ref, kc_ref, vc_ref, gsel_ref, gwin_ref, gcmp_ref, pos_ref, w1_ref, w1c_ref,
                 b1_ref, w2_ref, ksa_ref, vst_ref, kwa_ref, vwt_ref, kca_ref, vct_ref, *, S, rows):
    lane = lax.broadcasted_iota(jnp.int32, (rows, LANES), 1)
    rowi = lax.broadcasted_iota(jnp.int32, (rows, LANES), 0)
    ones_row = (lax.broadcasted_iota(jnp.int32, (NSA_DH, rows), 0) == 0).astype(F32)
    low = lane < NSA_DH

    def body(i, carry):
        r0 = pl.multiple_of(i * rows, rows)
        blk_hot = jnp.where(lane - NSA_DH == ((r0 + rowi) >> SEL_SHIFT), 1.0, 0.0)
        for src, gain_ref, aug, k_dst, v_dst, vrows in (
                (0, gsel_ref, blk_hot, ksa_ref, vst_ref, rows),
                (2, gwin_ref, jnp.zeros((rows, LANES), F32), kwa_ref, vwt_ref, Q_BLOCK)):
            kk = kv_ref[0, pl.ds(r0, rows), src * LANES:(src + 1) * LANES].astype(F32)
            vv = kv_ref[0, pl.ds(r0, rows), (src + 1) * LANES:(src + 2) * LANES].astype(F32)
            kn = _half_rms(kk, gain_ref[...], lane)
            k_dst[0, 0, pl.ds(r0, rows), :] = jnp.where(low, kn, aug).astype(BF16)
            k_dst[0, 1, pl.ds(r0, rows), :] = jnp.where(low, pltpu.roll(kn, NSA_DH, 1), aug).astype(BF16)
            vt = vv.T
            for g in range(NSA_KV):
                vg = jnp.concatenate([vt[g * NSA_DH:(g + 1) * NSA_DH], ones_row], axis=0).astype(BF16)
                for u in range(rows // vrows):
                    v_dst[0, g, i * (rows // vrows) + u] = vg[:, u * vrows:(u + 1) * vrows]
        return carry

    lax.fori_loop(0, S // rows, body, 0)

    nc = S // CMP_STRIDE
    outs = []
    for j, src_ref in ((0, kc_ref), (1, vc_ref)):
        bias = jnp.dot(pos_ref[j].astype(BF16), w1_ref[j], preferred_element_type=F32)[0:1] + b1_ref[j]
        pj = jnp.dot(src_ref[0], w1c_ref[j], preferred_element_type=F32)
        per_g = []
        for g in range(NSA_KV):
            first = pj[:, (2 * g) * CMP_HID:(2 * g + 1) * CMP_HID]
            second = pj[:, (2 * g + 1) * CMP_HID:(2 * g + 2) * CMP_HID]
            hid = _gelu_tanh(first + pltpu.roll(second, nc - 1, 0) + bias).astype(BF16)
            if j == 0:
                per_g.append(jnp.dot(hid, w2_ref[0], preferred_element_type=F32))
            else:
                per_g.append(lax.dot_general(w2_ref[1], hid, (((1,), (1,)), ((), ())),
                                             preferred_element_type=F32))
        outs.append(per_g)
    for g in range(NSA_KV):
        kc = outs[0][g]
        ms = jnp.sum(kc * kc, axis=-1, keepdims=True) * (1.0 / NSA_DH)
        kca_ref[0, g] = (kc * lax.rsqrt(ms + EPS) * gcmp_ref[...]).astype(BF16)
        vct_ref[0, g] = outs[1][g].astype(BF16)


def _nsa_prep(proj3, kc2, vc2, gsel, gwin, gcmp, pos8, w1, w1c, b1, w2p):
    B, S, _ = proj3.shape
    nc = S // CMP_STRIDE
    rows = SEL_CHUNK
    full = lambda shp: pl.BlockSpec(shp, lambda b: (0,) * len(shp))
    whole = lambda shp: pl.BlockSpec((1,) + shp, lambda b: (b,) + (0,) * len(shp))
    shapes = [(NSA_KV, S, LANES), (NSA_KV, S // SEL_CHUNK, LANES, SEL_CHUNK),
              (NSA_KV, S, LANES), (NSA_KV, S // Q_BLOCK, LANES, Q_BLOCK),
              (NSA_KV, nc, LANES), (NSA_KV, LANES, nc)]
    return pl.pallas_call(
        functools.partial(_prep_kernel, S=S, rows=rows),
        grid=(B,),
        in_specs=[
            pl.BlockSpec((1, S, 4 * NSA_KVW), lambda b: (b, 0, P_KV // (4 * NSA_KVW))),
            pl.BlockSpec((1, nc, CMP_STRIDE * NSA_KVW), lambda b: (b, 0, 0)),
            pl.BlockSpec((1, nc, CMP_STRIDE * NSA_KVW), lambda b: (b, 0, 0)),
            full((1, LANES)), full((1, LANES)), full((1, LANES)),
            full(pos8.shape), full(w1.shape), full(w1c.shape), full(b1.shape), full(w2p.shape),
        ],
        out_specs=[whole(s) for s in shapes],
        out_shape=[jax.ShapeDtypeStruct((B,) + s, BF16) for s in shapes],
        compiler_params=_cparams(("parallel",)),
        name="nsa_prep",
    )(proj3, kc2, vc2, gsel, gwin, gcmp, pos8, w1, w1c, b1, w2p)


def _sel_table_geometry():
    last_start = (REL_BUCKETS // 2) * (REL_MAX_DIST / (REL_BUCKETS // 2)) ** (
        (REL_BUCKETS - REL_BUCKETS // 2 - 1) / (REL_BUCKETS - REL_BUCKETS // 2))
    dsat = int(math.ceil((last_start + 16 + SEL_CHUNK) / LANES)) * LANES
    return dsat, dsat + SEL_CHUNK


def _attn_kernel(q_ref, gate_ref, kca_ref, vct_ref, ksa_ref, vst_ref, kwa_ref, vwt_ref, tabc_ref,
                 tabw_ref, tabs_ref, ovl_ref, qg_ref, o_ref, m_ref, acc_ref, sa_ref, sb_ref, *, nsel, dsat):
    qi = pl.program_id(2)
    QB = Q_BLOCK
    R = NSA_HPG * QB
    DH = NSA_DH

    xt = q_ref[0].astype(F32).T
    cols = []
    for h in range(NSA_HPG):
        xh = xt[h * DH:(h + 1) * DH]
        ms = jnp.mean(xh * xh, axis=0, keepdims=True)
        cols.append(xh * lax.rsqrt(ms + EPS) * qg_ref[...])
    qt = jnp.concatenate(cols, axis=1).astype(BF16)
    qta = jnp.concatenate([qt, jnp.zeros_like(qt)], axis=0)

    s = jnp.dot(kca_ref[0, 0], qta, preferred_element_type=F32) + tabc_ref[0, 0]
    mx = jnp.max(s, axis=0, keepdims=True)
    e = jnp.exp2(s - mx)
    p = jnp.where(s > 0.5 * NEG_BIG, e / jnp.sum(e, axis=0, keepdims=True), 0.0)
    o_c = jnp.dot(vct_ref[0, 0], p.astype(BF16), preferred_element_type=F32)

    psum = p[:, 0:QB] + p[:, QB:2 * QB] + p[:, 2 * QB:3 * QB] + p[:, 3 * QB:4 * QB]
    p_hi = psum.astype(BF16)
    p_lo = (psum - p_hi.astype(F32)).astype(BF16)
    imp = (jnp.dot(ovl_ref[...], p_hi, preferred_element_type=F32)
           + jnp.dot(ovl_ref[...], p_lo, preferred_element_type=F32))
    t = qi * QB + lax.broadcasted_iota(jnp.int32, (SEL_ROWS, QB), 1)
    jrow = lax.broadcasted_iota(jnp.int32, (SEL_ROWS, QB), 0)
    jcur = t >> SEL_SHIFT
    forced = (jrow == 0) | (jrow == jcur) | (jrow == jcur - 1)
    work = jnp.where(jrow * SEL_BLOCK <= t, imp + jnp.where(forced, FORCE_BONUS, 0.0), -1.0)
    work = jnp.where(jrow < nsel, work, -5.0)
    jf = jrow.astype(F32)
    sel = jnp.zeros((SEL_ROWS, QB), F32)
    for _ in range(min(SEL_TOPK, nsel)):
        best = jnp.max(work, axis=0, keepdims=True)
        first = jnp.min(jnp.where(work == best, jf, float(SEL_ROWS)), axis=0, keepdims=True)
        hit = jf == first
        sel = jnp.where(hit, 1.0, sel)
        work = jnp.where(hit, -3.0, work)
    selneg = jnp.where(sel > 0.5, 0.0, NEG_BIG).astype(BF16)
    qts = jnp.concatenate([qt, jnp.concatenate([selneg] * NSA_HPG, axis=1)], axis=0)

    tiles, vts = [], []
    for r in range(WIN_BLOCKS):
        kb = qi - (WIN_BLOCKS - 1) + r
        kbc = jnp.maximum(kb, 0)
        kw = kwa_ref[0, 0, pl.ds(pl.multiple_of(kbc * QB, QB), QB), :]
        vts.append(vwt_ref[0, 0, kbc])
        sw = jnp.dot(kw, qta, preferred_element_type=F32) + tabw_ref[0, 0, r * QB:(r + 1) * QB, :]
        tiles.append(sw + jnp.where(kb >= 0, 0.0, NEG_BIG))
    mw = tiles[0]
    for r in range(1, WIN_BLOCKS):
        mw = jnp.maximum(mw, tiles[r])
    mw = jnp.max(mw, axis=0, keepdims=True)
    o_w = jnp.zeros((LANES, R), F32)
    for r in range(WIN_BLOCKS):
        o_w = o_w + jnp.dot(vts[r], jnp.exp2(tiles[r] - mw).astype(BF16), preferred_element_type=F32)

    m_ref[...] = jnp.full_like(m_ref, NEG_BIG)
    acc_ref[...] = jnp.zeros_like(acc_ref)

    def scores(c):
        k0 = pl.multiple_of(c * SEL_CHUNK, SEL_CHUNK)
        delta = qi * QB - c * SEL_CHUNK
        start = pl.multiple_of(dsat - jnp.minimum(delta, dsat), LANES)
        return (jnp.dot(ksa_ref[0, 0, pl.ds(k0, SEL_CHUNK), :], qts, preferred_element_type=F32)
                + tabs_ref[0, 0, pl.ds(start, SEL_CHUNK), :])

    def consume(s_ref, c):
        sc = s_ref[...]
        m_old = m_ref[...]
        m_new = jnp.maximum(m_old, jnp.max(sc, axis=0, keepdims=True))
        alpha = jnp.exp2(m_old - m_new)
        pe = jnp.exp2(sc - m_new).astype(BF16)
        acc_ref[...] = alpha * acc_ref[...] + jnp.dot(vst_ref[0, 0, c], pe, preferred_element_type=F32)
        m_ref[...] = m_new

    n_chunks = (qi * QB + QB - 1) // SEL_CHUNK + 1
    n_pairs = (n_chunks - 1) // 2
    sa_ref[...] = scores(0)

    def sel_pair(pr, carry):
        c = 2 * pr
        sb_ref[...] = scores(c + 1)
        consume(sa_ref, c)
        sa_ref[...] = scores(c + 2)
        consume(sb_ref, c + 1)
        return carry

    lax.fori_loop(0, n_pairs, sel_pair, 0)

    @pl.when(n_chunks % 2 == 0)
    def _():
        sb_ref[...] = scores(n_chunks - 1)
        consume(sa_ref, n_chunks - 2)
        consume(sb_ref, n_chunks - 1)

    @pl.when(n_chunks % 2 == 1)
    def _():
        consume(sa_ref, n_chunks - 1)

    acc = acc_ref[...]

    gt = jax.nn.sigmoid(gate_ref[0].astype(F32)).T
    o_s = acc[0:DH] / acc[DH:DH + 1]
    o_w = o_w[0:DH] / o_w[DH:DH + 1]
    heads = []
    for h in range(NSA_HPG):
        cs = slice(h * QB, (h + 1) * QB)
        heads.append(gt[3 * h:3 * h + 1] * o_c[0:DH, cs] + gt[3 * h + 1:3 * h + 2] * o_s[:, cs]
                     + gt[3 * h + 2:3 * h + 3] * o_w[:, cs])
    o_ref[0] = jnp.concatenate(heads, axis=0).T.astype(BF16)


def _nsa_attn(proj3, prep, tabc, tabw, tabs, ovl, qgain, dsat):
    B, S, _ = proj3.shape
    ksa, vst, kwa, vwt, kca, vct = prep
    nq = S // Q_BLOCK
    nc = S // CMP_STRIDE
    nsel = S // SEL_BLOCK
    R = NSA_HPG * Q_BLOCK
    gw = NSA_HPG * NSA_DH
    per_bg = lambda a: pl.BlockSpec((1, 1) + a.shape[2:], lambda b, g, i: (b, g) + (0,) * (a.ndim - 2))
    return pl.pallas_call(
        functools.partial(_attn_kernel, nsel=nsel, dsat=dsat),
        grid=(B, NSA_KV, nq),
        in_specs=[
            pl.BlockSpec((1, Q_BLOCK, gw), lambda b, g, i: (b, i, P_NQ // gw + g)),
            pl.BlockSpec((1, Q_BLOCK, LANES), lambda b, g, i: (b, i, P_GATE // LANES + g)),
            per_bg(kca), per_bg(vct), per_bg(ksa), per_bg(vst), per_bg(kwa), per_bg(vwt),
            pl.BlockSpec((1, 1, nc, R), lambda b, g, i: (g, i, 0, 0)),
            pl.BlockSpec((1, 1) + tabw.shape[2:], lambda b, g, i: (g, 0, 0, 0)),
            pl.BlockSpec((1, 1) + tabs.shape[2:], lambda b, g, i: (g, 0, 0, 0)),
            pl.BlockSpec(ovl.shape, lambda b, g, i: (0, 0)),
            pl.BlockSpec(qgain.shape, lambda b, g, i: (0, 0)),
        ],
        out_specs=pl.BlockSpec((1, Q_BLOCK, gw), lambda b, g, i: (b, i, g)),
        out_shape=jax.ShapeDtypeStruct((B, S, NSA_Q), BF16),
        scratch_shapes=[pltpu.VMEM((1, R), F32), pltpu.VMEM((LANES, R), F32),
                        pltpu.VMEM((SEL_CHUNK, R), F32), pltpu.VMEM((SEL_CHUNK, R), F32)],
        compiler_params=_cparams(("parallel", "parallel", "arbitrary")),
        name="nsa_attn",
    )(proj3, proj3, kca, vct, ksa, vst, kwa, vwt, tabc, tabw, tabs, ovl, qgain)


def _rel_bucket(d):
    max_exact = REL_BUCKETS // 2
    d = jnp.maximum(d, 0)
    df = jnp.maximum(d, 1).astype(F32)
    large = max_exact + (jnp.log(df / max_exact) / math.log(REL_MAX_DIST / max_exact)
                         * (REL_BUCKETS - max_exact)).astype(jnp.int32)
    return jnp.where(d < max_exact, d, jnp.minimum(large, REL_BUCKETS - 1))


def _bias_kernel(rb_ref, bkt_ref, o_ref, *, rows):
    g = pl.program_id(0)
    step = 32
    for r0 in range(0, rows, step):
        bkt = bkt_ref[0, r0:r0 + step, :]
        acc = [jnp.full(bkt.shape, NEG_BIG, F32) for _ in range(NSA_HPG)]
        for b in range(REL_BUCKETS):
            hit = bkt == b
            for h in range(NSA_HPG):
                acc[h] = jnp.where(hit, rb_ref[b, g * NSA_HPG + h] * LOG2E, acc[h])
        for h in range(NSA_HPG):
            o_ref[0, 0, r0:r0 + step, h * Q_BLOCK:(h + 1) * Q_BLOCK] = acc[h]


def _bias_table(rel_bias, bucket):
    nt, rows, cols = bucket.shape
    return pl.pallas_call(
        functools.partial(_bias_kernel, rows=rows),
        grid=(NSA_KV, nt),
        in_specs=[
            pl.BlockSpec(memory_space=pltpu.SMEM),
            pl.BlockSpec((1, rows, cols), lambda g, t: (t, 0, 0)),
        ],
        out_specs=pl.BlockSpec((1, 1, rows, NSA_HPG * cols), lambda g, t: (g, t, 0, 0)),
        out_shape=jax.ShapeDtypeStruct((NSA_KV, nt, rows, NSA_HPG * cols), F32),
        compiler_params=_cparams(("parallel", "parallel")),
        name="bias_table",
    )(rel_bias, bucket)


def _bias_tables(rel_bias, S, dsat, strip_len):
    nq = S // Q_BLOCK
    nc = S // CMP_STRIDE
    n_cmp = (S - CMP_BLOCK) // CMP_STRIDE + 1
    ar = lambda n: jnp.arange(n, dtype=jnp.int32)
    bucket = lambda d, valid: jnp.where(valid, _rel_bucket(d), -1)
    i = ar(Q_BLOCK)[None, None, :]
    c = ar(nc)[None, :, None]
    d_c = ar(nq)[:, None, None] * Q_BLOCK + i - (c * CMP_STRIDE + CMP_BLOCK - 1)
    tabc = _bias_table(rel_bias, bucket(d_c, (d_c >= 0) & (c < n_cmp)))
    d_w = i + WINDOW - ar(WIN_BLOCKS * Q_BLOCK)[None, :, None]
    tabw = _bias_table(rel_bias, bucket(d_w, (d_w >= 0) & (d_w < WINDOW)))
    d_s = i - ar(strip_len)[None, :, None] + dsat
    tabs = _bias_table(rel_bias, bucket(d_s, d_s >= 0))
    return tabc, tabw, tabs


def _overlap_matrix(S):
    nc = S // CMP_STRIDE
    nsel = S // SEL_BLOCK
    n_cmp = (S - CMP_BLOCK) // CMP_STRIDE + 1
    c = np.arange(nc)[None, :]
    sel_start = (np.arange(SEL_ROWS) * SEL_BLOCK)[:, None]
    ovl = ((c * CMP_STRIDE < sel_start + SEL_BLOCK) & (c * CMP_STRIDE + CMP_BLOCK - 1 >= sel_start)
           & (c < n_cmp) & (np.arange(SEL_ROWS)[:, None] < nsel))
    return jnp.asarray(ovl, BF16)


def _layer_params(l, lower_bounds, norm_mix, w_in, hg_out_gain, nsa_qk_gain, cmp_pos, cmp_w1, cmp_b1,
                  cmp_w2, w_branch_a, w_branch_b, w_out, norm_mlp, w_up, w_down):
    splits = (HG_K, HG_K, HG_V, HG_V, NSA_Q, NSA_KVW, NSA_KVW, NSA_KVW, NSA_KVW, NSA_KVW, NSA_KVW,
              3 * NSA_HEADS, D_MODEL, D_MODEL)
    offs = np.concatenate([[0], np.cumsum(splits)])
    col = lambda k: w_in[l][:, offs[k]:offs[k + 1]]
    gate_w = col(11)
    zpad = jnp.zeros((D_MODEL, LANES - 3 * NSA_HPG), F32)
    gate_cols = []
    for g in range(NSA_KV):
        gate_cols += [gate_w[:, g * 3 * NSA_HPG:(g + 1) * 3 * NSA_HPG], zpad]
    w_main = jnp.concatenate([col(12), col(13), col(0), col(2), col(3),
                              col(7), col(8), col(9), col(10), col(4)] + gate_cols, axis=1).astype(BF16)
    w_cmp = jnp.concatenate([col(1), col(5), col(6)], axis=1).astype(BF16)

    lb = lower_bounds[l]
    lbp = jnp.stack([jnp.log(lb), jnp.log1p(-lb), 1.0 - lb])

    w1 = cmp_w1[l].reshape(2, 2, CMP_STRIDE, NSA_DH, CMP_HID)
    w1c = jnp.zeros((2, CMP_STRIDE, NSA_KV, NSA_DH, NSA_KV, 2, CMP_HID), F32)
    for g in range(NSA_KV):
        w1c = w1c.at[:, :, g, :, g, :, :].set(w1.transpose(0, 2, 3, 1, 4))
    w1c = w1c.reshape(2, CMP_STRIDE * NSA_KVW, NSA_KV * 2 * CMP_HID).astype(BF16)
    pos8 = jnp.zeros((2, 8, CMP_BLOCK * NSA_DH), F32).at[:, 0, :].set(cmp_pos[l].reshape(2, -1))
    w2p = jnp.zeros((2, CMP_HID, LANES), F32)
    w2p = w2p.at[0, :, :NSA_DH].set(cmp_w2[l, 0]).at[1, :NSA_DH, :].set(cmp_w2[l, 1].T).astype(BF16)
    qk = nsa_qk_gain[l]
    z64 = jnp.zeros((NSA_DH,), F32)
    return dict(
        norm_mix=norm_mix[l][None, :], w_main=w_main, w_cmp=w_cmp, lbp=lbp,
        hg_gain=hg_out_gain[l][None, :],
        gsel=jnp.concatenate([qk[2], qk[2]])[None, :], gwin=jnp.concatenate([qk[3], qk[3]])[None, :],
        gcmp=jnp.concatenate([qk[1], z64])[None, :],
        qgain=jnp.broadcast_to((qk[0] * (NSA_DH ** -0.5 * LOG2E))[:, None], (NSA_DH, Q_BLOCK)),
        pos8=pos8, w1=cmp_w1[l].astype(BF16), w1c=w1c, b1=cmp_b1[l][:, None, :], w2p=w2p,
        wa=w_branch_a[l].astype(BF16), wb=w_branch_b[l].astype(BF16), wo=w_out[l].astype(BF16),
        norm_mlp=norm_mlp[l][None, :], wu=w_up[l].astype(BF16), wd=w_down[l].astype(BF16))


def kernel(x, rel_bias, hg_lb_logits, norm_mix, w_in, hg_out_gain, nsa_qk_gain, cmp_pos, cmp_w1, cmp_b1, cmp_w2, w_branch_a, w_branch_b, w_out, norm_mlp, w_up, w_down):
    B, S, D = x.shape
    T = B * S
    depth = w_in.shape[0]
    assert D == D_MODEL and S % SEL_CHUNK == 0 and S >= WINDOW and S // SEL_BLOCK <= SEL_ROWS
    lb_cum = jnp.cumsum(jax.nn.softmax(hg_lb_logits.astype(F32), axis=0), axis=0)
    lower_bounds = lb_cum - lb_cum[0:1]

    dsat, strip_len = _sel_table_geometry()
    tabc, tabw, tabs = _bias_tables(rel_bias, S, dsat, strip_len)
    ovl = _overlap_matrix(S)
    nc = S // CMP_STRIDE
    tm = min(1024, T)

    x2 = x.reshape(T, D)
    for l in range(depth):
        p = _layer_params(l, lower_bounds, norm_mix, w_in, hg_out_gain, nsa_qk_gain, cmp_pos, cmp_w1,
                          cmp_b1, cmp_w2, w_branch_a, w_branch_b, w_out, norm_mlp, w_up, w_down)
        proj, z, kc, vc = _rms_proj(x2, p["norm_mix"], p["w_main"], p["w_cmp"], tm, P_TILE)
        proj3 = proj.reshape(B, S, P_COLS)
        kc2 = kc.reshape(B, nc, CMP_STRIDE * NSA_KVW)
        vc2 = vc.reshape(B, nc, CMP_STRIDE * NSA_KVW)
        o_a = _hgrn(proj3, z.reshape(B, S, HG_K), p["lbp"], p["hg_gain"])
        prep = _nsa_prep(proj3, kc2, vc2, p["gsel"], p["gwin"], p["gcmp"], p["pos8"], p["w1"],
                         p["w1c"], p["b1"], p["w2p"])
        o_b = _nsa_attn(proj3, prep, tabc, tabw, tabs, ovl, p["qgain"], dsat)
        x2 = _merge(x2, o_a.reshape(T, HG_V), o_b.reshape(T, NSA_Q), proj, p["wa"], p["wb"], p["wo"],
                    min(512, T))
        x2 = _mlp(x2, p["norm_mlp"], p["wu"], p["wd"], tm, 1024)
    return x2.reshape(B, S, D)
```

```python
import functools
import math

import numpy as np
import jax
import jax.numpy as jnp
from jax import lax
from jax.experimental import pallas as pl
from jax.experimental.pallas import tpu as pltpu

F32 = jnp.float32
BF16 = jnp.bfloat16

D_MODEL = 1024
HG_HEADS = 4
HG_DK = 128
HG_DV = 128
HG_K = HG_HEADS * HG_DK
HG_V = HG_HEADS * HG_DV
NSA_HEADS = 8
NSA_KV = 2
NSA_DH = 64
NSA_HPG = NSA_HEADS // NSA_KV
NSA_Q = NSA_HEADS * NSA_DH
NSA_KVW = NSA_KV * NSA_DH
CMP_BLOCK = 32
CMP_STRIDE = 16
CMP_HID = 128
SEL_BLOCK = 64
SEL_TOPK = 8
WINDOW = 512
FORCE_BONUS = 1000.0
REL_BUCKETS = 32
REL_MAX_DIST = 1024
D_FF = 4 * D_MODEL
EPS = 1e-6
NEG_BIG = -1e30
LOG2E = math.log2(math.e)

LANES = 128
VMEM_LIMIT = 56 * 1024 * 1024

P_GAB = 0
P_HG = P_GAB + 2 * D_MODEL
P_KV = P_HG + 3 * HG_K
P_NQ = P_KV + 4 * NSA_KVW
P_GATE = P_NQ + NSA_Q
P_COLS = P_GATE + NSA_KV * LANES
P_TILE = P_COLS // 2

HG_CHUNK = 128
HG_TS = 512
Q_BLOCK = 256
WIN_KB = 128
SEL_CHUNK = 512
WIN_BLOCKS = (WINDOW + Q_BLOCK) // WIN_KB
SEL_SHIFT = int(math.log2(SEL_BLOCK))
SEL_ROWS = LANES - NSA_DH


def _cparams(sem):
    return pltpu.CompilerParams(dimension_semantics=sem, vmem_limit_bytes=VMEM_LIMIT)


def _rms_rows(x, gain):
    ms = jnp.mean(x * x, axis=-1, keepdims=True)
    return x * lax.rsqrt(ms + EPS) * gain


def _proj_kernel(x_ref, g_ref, w_ref, wx_ref, o_ref, z_ref, kc_ref, vc_ref, h_ref):
    @pl.when(pl.program_id(1) == 0)
    def _():
        h = _rms_rows(x_ref[...], g_ref[...]).astype(BF16)
        h_ref[...] = h
        c = jnp.dot(h, wx_ref[...], preferred_element_type=F32)
        z_ref[...] = c[:, :HG_K]
        kc_ref[...] = c[:, HG_K:HG_K + NSA_KVW].astype(BF16)
        vc_ref[...] = c[:, HG_K + NSA_KVW:].astype(BF16)

    o_ref[...] = jnp.dot(h_ref[...], w_ref[...], preferred_element_type=F32).astype(BF16)


def _rms_proj(x2, gain, w, wx, tm, tn):
    T, D = x2.shape
    N = w.shape[1]
    cmp_sds = jax.ShapeDtypeStruct((T, NSA_KVW), BF16)
    return pl.pallas_call(
        _proj_kernel,
        grid=(T // tm, N // tn),
        in_specs=[
            pl.BlockSpec((tm, D), lambda i, j: (i, 0)),
            pl.BlockSpec((1, D), lambda i, j: (0, 0)),
            pl.BlockSpec((D, tn), lambda i, j: (0, j)),
            pl.BlockSpec(wx.shape, lambda i, j: (0, 0)),
        ],
        out_specs=[
            pl.BlockSpec((tm, tn), lambda i, j: (i, j)),
            pl.BlockSpec((tm, HG_K), lambda i, j: (i, 0)),
            pl.BlockSpec((tm, NSA_KVW), lambda i, j: (i, 0)),
            pl.BlockSpec((tm, NSA_KVW), lambda i, j: (i, 0)),
        ],
        out_shape=[jax.ShapeDtypeStruct((T, N), BF16), jax.ShapeDtypeStruct((T, HG_K), F32),
                   cmp_sds, cmp_sds],
        scratch_shapes=[pltpu.VMEM((tm, D), BF16)],
        compiler_params=_cparams(("parallel", "arbitrary")),
        name="rms_proj",
    )(x2, gain, w, wx)


def _mlp_kernel(x_ref, g_ref, wu_ref, wd_ref, o_ref, h_ref):
    f = pl.program_id(1)

    @pl.when(f == 0)
    def _():
        x = x_ref[...]
        h_ref[...] = _rms_rows(x, g_ref[...]).astype(BF16)
        o_ref[...] = x

    u = jnp.dot(h_ref[...], wu_ref[...], preferred_element_type=F32)
    u = jnp.square(jnp.maximum(u, 0.0)).astype(BF16)
    o_ref[...] += jnp.dot(u, wd_ref[...], preferred_element_type=F32)


def _mlp(x2, gain, wu, wd, tm, tf):
    T, D = x2.shape
    FF = wu.shape[1]
    return pl.pallas_call(
        _mlp_kernel,
        grid=(T // tm, FF // tf),
        in_specs=[
            pl.BlockSpec((tm, D), lambda i, f: (i, 0)),
            pl.BlockSpec((1, D), lambda i, f: (0, 0)),
            pl.BlockSpec((D, tf), lambda i, f: (0, f)),
            pl.BlockSpec((tf, D), lambda i, f: (f, 0)),
        ],
        out_specs=pl.BlockSpec((tm, D), lambda i, f: (i, 0)),
        out_shape=jax.ShapeDtypeStruct((T, D), F32),
        scratch_shapes=[pltpu.VMEM((tm, D), BF16)],
        compiler_params=_cparams(("parallel", "arbitrary")),
        name="mlp",
    )(x2, gain, wu, wd)


def _merge_kernel(x_ref, oa_ref, ob_ref, ga_ref, gb_ref, wa_ref, wb_ref, wo_ref, o_ref):
    ya = jnp.dot(oa_ref[...], wa_ref[...], preferred_element_type=F32)
    yb = jnp.dot(ob_ref[...], wb_ref[...], preferred_element_type=F32)
    mixed = (jax.nn.sigmoid(ga_ref[...].astype(F32)) * ya
             + jax.nn.sigmoid(gb_ref[...].astype(F32)) * yb)
    o_ref[...] = x_ref[...] + jnp.dot(mixed.astype(BF16), wo_ref[...], preferred_element_type=F32)


def _merge(x2, oa, ob, proj, wa, wb, wo, tm):
    T, D = x2.shape
    ga_blk = P_GAB // D
    return pl.pallas_call(
        _merge_kernel,
        grid=(T // tm,),
        in_specs=[
            pl.BlockSpec((tm, D), lambda i: (i, 0)),
            pl.BlockSpec((tm, HG_V), lambda i: (i, 0)),
            pl.BlockSpec((tm, NSA_Q), lambda i: (i, 0)),
            pl.BlockSpec((tm, D), lambda i: (i, ga_blk)),
            pl.BlockSpec((tm, D), lambda i: (i, ga_blk + 1)),
            pl.BlockSpec((HG_V, D), lambda i: (0, 0)),
            pl.BlockSpec((NSA_Q, D), lambda i: (0, 0)),
            pl.BlockSpec((D, D), lambda i: (0, 0)),
        ],
        out_specs=pl.BlockSpec((tm, D), lambda i: (i, 0)),
        out_shape=jax.ShapeDtypeStruct((T, D), F32),
        compiler_params=_cparams(("parallel",)),
        name="merge",
    )(x2, oa, ob, proj, proj, wa, wb, wo)


def _hgrn_consts(C):
    L = int(math.log2(C))
    idx = np.arange(C)
    mats = [(idx[None, :] <= idx[:, None]),
            (idx[None, :] > idx[:, None])]
    bmask = []
    for l in range(L):
        m = 1 << l
        r = (idx & ~(2 * m - 1)) + m
        lo = np.minimum(idx, r)[:, None]
        hi = np.maximum(idx, r)[:, None]
        mats.append((idx[None, :] > lo) & (idx[None, :] <= hi))
        blk = idx >> (l + 1)
        bmask.append(blk[:, None] == blk[None, :])
    bmask.append(idx[:, None] == idx[None, :])
    seg = np.concatenate(mats, axis=0).astype(np.float32)
    seg = np.concatenate([seg, seg], axis=1)
    return seg, np.stack(bmask).astype(np.float32), L


def _hgrn_kernel(q_ref, f_ref, i_ref, g_ref, lbp_ref, gain_ref, seg_ref, bm_ref, o_ref, st_ref,
                 *, C, L, nchunk):
    @pl.when(pl.program_id(1) == 0)
    def _():
        st_ref[...] = jnp.zeros_like(st_ref)

    rowi = lax.broadcasted_iota(jnp.int32, (C, HG_DK), 0)
    nt = (((1,), (1,)), ((), ()))
    tn = (((0,), (0,)), ((), ()))

    def chunk(c, carry):
        r0 = pl.multiple_of(c * C, C)
        for h in range(HG_HEADS):
            cs = slice(h * HG_DK, (h + 1) * HG_DK)
            qb = q_ref[0, pl.ds(r0, C), cs]
            q = qb.astype(F32)
            z = f_ref[0, pl.ds(r0, C), cs]
            vb = i_ref[0, pl.ds(r0, C), cs]
            gg = g_ref[0, pl.ds(r0, C), cs].astype(F32)
            log_lb = lbp_ref[0:1, cs]
            log1m_lb = lbp_ref[1:2, cs]
            one_m_lb = lbp_ref[2:3, cs]
            y = log1m_lb + (jnp.minimum(z, 0.0) - jnp.log1p(jnp.exp(-jnp.abs(z))))
            logf = jnp.maximum(log_lb, y) + jnp.log1p(jnp.exp(-jnp.abs(log_lb - y)))
            k = one_m_lb * jax.nn.sigmoid(-z)
            g_hi = logf.astype(BF16)
            g_lo = (logf - g_hi.astype(F32)).astype(BF16)
            seg = jnp.dot(seg_ref[...], jnp.concatenate([g_hi, g_lo], axis=0),
                          preferred_element_type=F32)
            b = seg[0:C]
            brev = seg[C:2 * C]
            qd = (q * jnp.exp(b)).astype(BF16)
            kd = (k * jnp.exp(brev)).astype(BF16)
            st = st_ref[h]
            o = lax.dot_general(qd, st.astype(BF16), nt, preferred_element_type=F32)
            s = lax.dot_general(qb, k.astype(BF16), nt,
                                preferred_element_type=F32) * bm_ref[L]
            for l in range(L):
                e = jnp.exp(seg[(2 + l) * C:(3 + l) * C])
                second = (rowi & (1 << l)) != 0
                ql = jnp.where(second, q * e, 0.0).astype(BF16)
                kl = jnp.where(second, 0.0, k * e).astype(BF16)
                s = s + lax.dot_general(ql, kl, nt, preferred_element_type=F32) * bm_ref[l]
            o = o + jnp.dot(s.astype(BF16), vb, preferred_element_type=F32)
            st_ref[h] = (st * jnp.exp(b[C - 1:C, :])
                         + lax.dot_general(vb, kd, tn, preferred_element_type=F32))
            o = _rms_rows(o, gain_ref[...]) * (gg * jax.nn.sigmoid(gg))
            o_ref[0, pl.ds(r0, C), cs] = o.astype(BF16)
        return carry

    lax.fori_loop(0, nchunk, chunk, 0)


def _hgrn(proj3, z3, lbp, gain, C=HG_CHUNK, ts=HG_TS):
    B, S, _ = proj3.shape
    ts = min(ts, S)
    seg, bm, L = _hgrn_consts(C)
    blk = lambda k: pl.BlockSpec((1, ts, HG_K), lambda b, s: (b, s, P_HG // HG_K + k))
    return pl.pallas_call(
        functools.partial(_hgrn_kernel, C=C, L=L, nchunk=ts // C),
        grid=(B, S // ts),
        in_specs=[
            blk(0), pl.BlockSpec((1, ts, HG_K), lambda b, s: (b, s, 0)), blk(1), blk(2),
            pl.BlockSpec((3, HG_K), lambda b, s: (0, 0)),
            pl.BlockSpec((1, HG_DV), lambda b, s: (0, 0)),
            pl.BlockSpec(seg.shape, lambda b, s: (0, 0)),
            pl.BlockSpec(bm.shape, lambda b, s: (0, 0, 0)),
        ],
        out_specs=pl.BlockSpec((1, ts, HG_V), lambda b, s: (b, s, 0)),
        out_shape=jax.ShapeDtypeStruct((B, S, HG_V), BF16),
        scratch_shapes=[pltpu.VMEM((HG_HEADS, HG_DV, HG_DK), F32)],
        compiler_params=_cparams(("parallel", "arbitrary")),
        name="hgrn2",
    )(proj3, z3, proj3, proj3, lbp, gain, jnp.asarray(seg, BF16), jnp.asarray(bm))


def _half_rms(x, gain2, lane):
    sq = x * x
    s_lo = jnp.sum(jnp.where(lane < NSA_DH, sq, 0.0), axis=-1, keepdims=True)
    s_hi = jnp.sum(jnp.where(lane < NSA_DH, 0.0, sq), axis=-1, keepdims=True)
    ms = jnp.where(lane < NSA_DH, s_lo, s_hi) * (1.0 / NSA_DH)
    return x * lax.rsqrt(ms + EPS) * gain2


def _gelu_tanh(x):
    return 0.5 * x * (1.0 + jnp.tanh(math.sqrt(2.0 / math.pi) * (x + 0.044715 * (x * x * x))))


def _prep_kernel(kv_ref, kc_ref, vc_ref, gsel_ref, gwin_ref, gcmp_ref, pos_ref, w1_ref, w1c_ref,
                 b1_ref, w2_ref, ksa_ref, vst_ref, kwa_ref, vwt_ref, kca_ref, vct_ref, *, S, rows):
    lane = lax.broadcasted_iota(jnp.int32, (rows, LANES), 1)
    rowi = lax.broadcasted_iota(jnp.int32, (rows, LANES), 0)
    ones_row = (lax.broadcasted_iota(jnp.int32, (NSA_DH, rows), 0) == 0).astype(F32)
    low = lane < NSA_DH

    def body(i, carry):
        r0 = pl.multiple_of(i * rows, rows)
        blk_hot = jnp.where(lane - NSA_DH == ((r0 + rowi) >> SEL_SHIFT), 1.0, 0.0)
        for src, gain_ref, aug, k_dst, v_dst, vrows in (
                (0, gsel_ref, blk_hot, ksa_ref, vst_ref, rows),
                (2, gwin_ref, jnp.zeros((rows, LANES), F32), kwa_ref, vwt_ref, WIN_KB)):
            kk = kv_ref[0, pl.ds(r0, rows), src * LANES:(src + 1) * LANES].astype(F32)
            vv = kv_ref[0, pl.ds(r0, rows), (src + 1) * LANES:(src + 2) * LANES].astype(F32)
            kn = _half_rms(kk, gain_ref[...], lane)
            k_dst[0, 0, pl.ds(r0, rows), :] = jnp.where(low, kn, aug).astype(BF16)
            k_dst[0, 1, pl.ds(r0, rows), :] = jnp.where(low, pltpu.roll(kn, NSA_DH, 1), aug).astype(BF16)
            vt = vv.T
            for g in range(NSA_KV):
                vg = jnp.concatenate([vt[g * NSA_DH:(g + 1) * NSA_DH], ones_row], axis=0).astype(BF16)
                for u in range(rows // vrows):
                    v_dst[0, g, i * (rows // vrows) + u] = vg[:, u * vrows:(u + 1) * vrows]
        return carry

    lax.fori_loop(0, S // rows, body, 0)

    nc = S // CMP_STRIDE
    outs = []
    for j, src_ref in ((0, kc_ref), (1, vc_ref)):
        bias = jnp.dot(pos_ref[j].astype(BF16), w1_ref[j], preferred_element_type=F32)[0:1] + b1_ref[j]
        pj = jnp.dot(src_ref[0], w1c_ref[j], preferred_element_type=F32)
        per_g = []
        for g in range(NSA_KV):
            first = pj[:, (2 * g) * CMP_HID:(2 * g + 1) * CMP_HID]
            second = pj[:, (2 * g + 1) * CMP_HID:(2 * g + 2) * CMP_HID]
            hid = _gelu_tanh(first + pltpu.roll(second, nc - 1, 0) + bias).astype(BF16)
            if j == 0:
                per_g.append(jnp.dot(hid, w2_ref[0], preferred_element_type=F32))
            else:
                per_g.append(lax.dot_general(w2_ref[1], hid, (((1,), (1,)), ((), ())),
                                             preferred_element_type=F32))
        outs.append(per_g)
    for g in range(NSA_KV):
        kc = outs[0][g]
        ms = jnp.sum(kc * kc, axis=-1, keepdims=True) * (1.0 / NSA_DH)
        kca_ref[0, g] = (kc * lax.rsqrt(ms + EPS) * gcmp_ref[...]).astype(BF16)
        vct_ref[0, g] = outs[1][g].astype(BF16)


def _nsa_prep(proj3, kc2, vc2, gsel, gwin, gcmp, pos8, w1, w1c, b1, w2p):
    B, S, _ = proj3.shape
    nc = S // CMP_STRIDE
    rows = SEL_CHUNK
    full = lambda shp: pl.BlockSpec(shp, lambda b: (0,) * len(shp))
    whole = lambda shp: pl.BlockSpec((1,) + shp, lambda b: (b,) + (0,) * len(shp))
    shapes = [(NSA_KV, S, LANES), (NSA_KV, S // SEL_CHUNK, LANES, SEL_CHUNK),
              (NSA_KV, S, LANES), (NSA_KV, S // WIN_KB, LANES, WIN_KB),
              (NSA_KV, nc, LANES), (NSA_KV, LANES, nc)]
    return pl.pallas_call(
        functools.partial(_prep_kernel, S=S, rows=rows),
        grid=(B,),
        in_specs=[
            pl.BlockSpec((1, S, 4 * NSA_KVW), lambda b: (b, 0, P_KV // (4 * NSA_KVW))),
            pl.BlockSpec((1, nc, CMP_STRIDE * NSA_KVW), lambda b: (b, 0, 0)),
            pl.BlockSpec((1, nc, CMP_STRIDE * NSA_KVW), lambda b: (b, 0, 0)),
            full((1, LANES)), full((1, LANES)), full((1, LANES)),
            full(pos8.shape), full(w1.shape), full(w1c.shape), full(b1.shape), full(w2p.shape),
        ],
        out_specs=[whole(s) for s in shapes],
        out_shape=[jax.ShapeDtypeStruct((B,) + s, BF16) for s in shapes],
        compiler_params=_cparams(("parallel",)),
        name="nsa_prep",
    )(proj3, kc2, vc2, gsel, gwin, gcmp, pos8, w1, w1c, b1, w2p)


def _sel_table_geometry():
    last_start = (REL_BUCKETS // 2) * (REL_MAX_DIST / (REL_BUCKETS // 2)) ** (
        (REL_BUCKETS - REL_BUCKETS // 2 - 1) / (REL_BUCKETS - REL_BUCKETS // 2))
    dsat = int(math.ceil((last_start + 16 + SEL_CHUNK) / LANES)) * LANES
    return dsat, dsat + SEL_CHUNK


def _attn_kernel(q_ref, gate_ref, kca_ref, vct_ref, ksa_ref, vst_ref, kwa_ref, vwt_ref, tabc_ref,
                 tabw_ref, tabs_ref, ovl_ref, qg_ref, o_ref, m_ref, acc_ref, sa_ref, sb_ref, *, nsel, dsat):
    qi = pl.program_id(2)
    QB = Q_BLOCK
    R = NSA_HPG * QB
    DH = NSA_DH

    xt = q_ref[0].astype(F32).T
    cols = []
    for h in range(NSA_HPG):
        xh = xt[h * DH:(h + 1) * DH]
        ms = jnp.mean(xh * xh, axis=0, keepdims=True)
        cols.append(xh * lax.rsqrt(ms + EPS) * qg_ref[...])
    qt = jnp.concatenate(cols, axis=1).astype(BF16)
    qta = jnp.concatenate([qt, jnp.zeros_like(qt)], axis=0)

    s = jnp.dot(kca_ref[0, 0], qta, preferred_element_type=F32) + tabc_ref[0, 0]
    mx = jnp.max(s, axis=0, keepdims=True)
    e = jnp.exp2(s - mx)
    p = jnp.where(s > 0.5 * NEG_BIG, e / jnp.sum(e, axis=0, keepdims=True), 0.0)
    o_c = jnp.dot(vct_ref[0, 0], p.astype(BF16), preferred_element_type=F32)

    psum = p[:, 0:QB] + p[:, QB:2 * QB] + p[:, 2 * QB:3 * QB] + p[:, 3 * QB:4 * QB]
    p_hi = psum.astype(BF16)
    p_lo = (psum - p_hi.astype(F32)).astype(BF16)
    imp = (jnp.dot(ovl_ref[...], p_hi, preferred_element_type=F32)
           + jnp.dot(ovl_ref[...], p_lo, preferred_element_type=F32))
    t = qi * QB + lax.broadcasted_iota(jnp.int32, (SEL_ROWS, QB), 1)
    jrow = lax.broadcasted_iota(jnp.int32, (SEL_ROWS, QB), 0)
    jcur = t >> SEL_SHIFT
    forced = (jrow == 0) | (jrow == jcur) | (jrow == jcur - 1)
    work = jnp.where(jrow * SEL_BLOCK <= t, imp + jnp.where(forced, FORCE_BONUS, 0.0), -1.0)
    work = jnp.where(jrow < nsel, work, -5.0)
    jf = jrow.astype(F32)
    sel = jnp.zeros((SEL_ROWS, QB), F32)
    for _ in range(min(SEL_TOPK, nsel)):
        best = jnp.max(work, axis=0, keepdims=True)
        first = jnp.min(jnp.where(work == best, jf, float(SEL_ROWS)), axis=0, keepdims=True)
        hit = jf == first
        sel = jnp.where(hit, 1.0, sel)
        work = jnp.where(hit, -3.0, work)
    selneg = jnp.where(sel > 0.5, 0.0, NEG_BIG).astype(BF16)
    qts = jnp.concatenate([qt, jnp.concatenate([selneg] * NSA_HPG, axis=1)], axis=0)

    tiles, vts = [], []
    for r in range(WIN_BLOCKS):
        kb = qi * (QB // WIN_KB) - WINDOW // WIN_KB + r
        kbc = jnp.maximum(kb, 0)
        kw = kwa_ref[0, 0, pl.ds(pl.multiple_of(kbc * WIN_KB, WIN_KB), WIN_KB), :]
        vts.append(vwt_ref[0, 0, kbc])
        sw = jnp.dot(kw, qta, preferred_element_type=F32) + tabw_ref[0, 0, r * WIN_KB:(r + 1) * WIN_KB, :]
        tiles.append(sw + jnp.where(kb >= 0, 0.0, NEG_BIG))
    mw = tiles[0]
    for r in range(1, WIN_BLOCKS):
        mw = jnp.maximum(mw, tiles[r])
    mw = jnp.max(mw, axis=0, keepdims=True)
    o_w = jnp.zeros((LANES, R), F32)
    for r in range(WIN_BLOCKS):
        o_w = o_w + jnp.dot(vts[r], jnp.exp2(tiles[r] - mw).astype(BF16), preferred_element_type=F32)

    m_ref[...] = jnp.full_like(m_ref, NEG_BIG)
    acc_ref[...] = jnp.zeros_like(acc_ref)

    def scores(c):
        k0 = pl.multiple_of(c * SEL_CHUNK, SEL_CHUNK)
        delta = qi * QB - c * SEL_CHUNK
        start = pl.multiple_of(dsat - jnp.minimum(delta, dsat), LANES)
        return (jnp.dot(ksa_ref[0, 0, pl.ds(k0, SEL_CHUNK), :], qts, preferred_element_type=F32)
                + tabs_ref[0, 0, pl.ds(start, SEL_CHUNK), :])

    def consume(s_ref, c):
        sc = s_ref[...]
        m_old = m_ref[...]
        m_new = jnp.maximum(m_old, jnp.max(sc, axis=0, keepdims=True))
        alpha = jnp.exp2(m_old - m_new)
        pe = jnp.exp2(sc - m_new).astype(BF16)
        acc_ref[...] = alpha * acc_ref[...] + jnp.dot(vst_ref[0, 0, c], pe, preferred_element_type=F32)
        m_ref[...] = m_new

    n_chunks = (qi * QB + QB - 1) // SEL_CHUNK + 1
    n_pairs = (n_chunks - 1) // 2
    sa_ref[...] = scores(0)

    def sel_pair(pr, carry):
        c = 2 * pr
        sb_ref[...] = scores(c + 1)
        consume(sa_ref, c)
        sa_ref[...] = scores(c + 2)
        consume(sb_ref, c + 1)
        return carry

    lax.fori_loop(0, n_pairs, sel_pair, 0)

    @pl.when(n_chunks % 2 == 0)
    def _():
        sb_ref[...] = scores(n_chunks - 1)
        consume(sa_ref, n_chunks - 2)
        consume(sb_ref, n_chunks - 1)

    @pl.when(n_chunks % 2 == 1)
    def _():
        consume(sa_ref, n_chunks - 1)

    acc = acc_ref[...]

    gt = jax.nn.sigmoid(gate_ref[0].astype(F32)).T
    o_s = acc[0:DH] / acc[DH:DH + 1]
    o_w = o_w[0:DH] / o_w[DH:DH + 1]
    heads = []
    for h in range(NSA_HPG):
        cs = slice(h * QB, (h + 1) * QB)
        heads.append(gt[3 * h:3 * h + 1] * o_c[0:DH, cs] + gt[3 * h + 1:3 * h + 2] * o_s[:, cs]
                     + gt[3 * h + 2:3 * h + 3] * o_w[:, cs])
    o_ref[0] = jnp.concatenate(heads, axis=0).T.astype(BF16)


def _nsa_attn(proj3, prep, tabc, tabw, tabs, ovl, qgain, dsat):
    B, S, _ = proj3.shape
    ksa, vst, kwa, vwt, kca, vct = prep
    nq = S // Q_BLOCK
    nc = S // CMP_STRIDE
    nsel = S // SEL_BLOCK
    R = NSA_HPG * Q_BLOCK
    gw = NSA_HPG * NSA_DH
    per_bg = lambda a: pl.BlockSpec((1, 1) + a.shape[2:], lambda b, g, i: (b, g) + (0,) * (a.ndim - 2))
    return pl.pallas_call(
        functools.partial(_attn_kernel, nsel=nsel, dsat=dsat),
        grid=(B, NSA_KV, nq),
        in_specs=[
            pl.BlockSpec((1, Q_BLOCK, gw), lambda b, g, i: (b, i, P_NQ // gw + g)),
            pl.BlockSpec((1, Q_BLOCK, LANES), lambda b, g, i: (b, i, P_GATE // LANES + g)),
            per_bg(kca), per_bg(vct), per_bg(ksa), per_bg(vst), per_bg(kwa), per_bg(vwt),
            pl.BlockSpec((1, 1, nc, R), lambda b, g, i: (g, i, 0, 0)),
            pl.BlockSpec((1, 1) + tabw.shape[2:], lambda b, g, i: (g, 0, 0, 0)),
            pl.BlockSpec((1, 1) + tabs.shape[2:], lambda b, g, i: (g, 0, 0, 0)),
            pl.BlockSpec(ovl.shape, lambda b, g, i: (0, 0)),
            pl.BlockSpec(qgain.shape, lambda b, g, i: (0, 0)),
        ],
        out_specs=pl.BlockSpec((1, Q_BLOCK, gw), lambda b, g, i: (b, i, g)),
        out_shape=jax.ShapeDtypeStruct((B, S, NSA_Q), BF16),
        scratch_shapes=[pltpu.VMEM((1, R), F32), pltpu.VMEM((LANES, R), F32),
                        pltpu.VMEM((SEL_CHUNK, R), F32), pltpu.VMEM((SEL_CHUNK, R), F32)],
        compiler_params=_cparams(("parallel", "parallel", "arbitrary")),
        name="nsa_attn",
    )(proj3, proj3, kca, vct, ksa, vst, kwa, vwt, tabc, tabw, tabs, ovl, qgain)


def _rel_bucket(d):
    max_exact = REL_BUCKETS // 2
    d = jnp.maximum(d, 0)
    df = jnp.maximum(d, 1).astype(F32)
    large = max_exact + (jnp.log(df / max_exact) / math.log(REL_MAX_DIST / max_exact)
                         * (REL_BUCKETS - max_exact)).astype(jnp.int32)
    return jnp.where(d < max_exact, d, jnp.minimum(large, REL_BUCKETS - 1))


def _bias_kernel(rb_ref, bkt_ref, o_ref, *, rows):
    g = pl.program_id(0)
    step = 32
    for r0 in range(0, rows, step):
        bkt = bkt_ref[0, r0:r0 + step, :]
        acc = [jnp.full(bkt.shape, NEG_BIG, F32) for _ in range(NSA_HPG)]
        for b in range(REL_BUCKETS):
            hit = bkt == b
            for h in range(NSA_HPG):
                acc[h] = jnp.where(hit, rb_ref[b, g * NSA_HPG + h] * LOG2E, acc[h])
        for h in range(NSA_HPG):
            o_ref[0, 0, r0:r0 + step, h * Q_BLOCK:(h + 1) * Q_BLOCK] = acc[h]


def _bias_table(rel_bias, bucket):
    nt, rows, cols = bucket.shape
    return pl.pallas_call(
        functools.partial(_bias_kernel, rows=rows),
        grid=(NSA_KV, nt),
        in_specs=[
            pl.BlockSpec(memory_space=pltpu.SMEM),
            pl.BlockSpec((1, rows, cols), lambda g, t: (t, 0, 0)),
        ],
        out_specs=pl.BlockSpec((1, 1, rows, NSA_HPG * cols), lambda g, t: (g, t, 0, 0)),
        out_shape=jax.ShapeDtypeStruct((NSA_KV, nt, rows, NSA_HPG * cols), F32),
        compiler_params=_cparams(("parallel", "parallel")),
        name="bias_table",
    )(rel_bias, bucket)


def _bias_tables(rel_bias, S, dsat, strip_len):
    nq = S // Q_BLOCK
    nc = S // CMP_STRIDE
    n_cmp = (S - CMP_BLOCK) // CMP_STRIDE + 1
    ar = lambda n: jnp.arange(n, dtype=jnp.int32)
    bucket = lambda d, valid: jnp.where(valid, _rel_bucket(d), -1)
    i = ar(Q_BLOCK)[None, None, :]
    c = ar(nc)[None, :, None]
    d_c = ar(nq)[:, None, None] * Q_BLOCK + i - (c * CMP_STRIDE + CMP_BLOCK - 1)
    tabc = _bias_table(rel_bias, bucket(d_c, (d_c >= 0) & (c < n_cmp)))
    d_w = i + WINDOW - ar(WIN_BLOCKS * WIN_KB)[None, :, None]
    tabw = _bias_table(rel_bias, bucket(d_w, (d_w >= 0) & (d_w < WINDOW)))
    d_s = i - ar(strip_len)[None, :, None] + dsat
    tabs = _bias_table(rel_bias, bucket(d_s, d_s >= 0))
    return tabc, tabw, tabs


def _overlap_matrix(S):
    nc = S // CMP_STRIDE
    nsel = S // SEL_BLOCK
    n_cmp = (S - CMP_BLOCK) // CMP_STRIDE + 1
    c = np.arange(nc)[None, :]
    sel_start = (np.arange(SEL_ROWS) * SEL_BLOCK)[:, None]
    ovl = ((c * CMP_STRIDE < sel_start + SEL_BLOCK) & (c * CMP_STRIDE + CMP_BLOCK - 1 >= sel_start)
           & (c < n_cmp) & (np.arange(SEL_ROWS)[:, None] < nsel))
    return jnp.asarray(ovl, BF16)


def _layer_params(l, lower_bounds, norm_mix, w_in, hg_out_gain, nsa_qk_gain, cmp_pos, cmp_w1, cmp_b1,
                  cmp_w2, w_branch_a, w_branch_b, w_out, norm_mlp, w_up, w_down):
    splits = (HG_K, HG_K, HG_V, HG_V, NSA_Q, NSA_KVW, NSA_KVW, NSA_KVW, NSA_KVW, NSA_KVW, NSA_KVW,
              3 * NSA_HEADS, D_MODEL, D_MODEL)
    offs = np.concatenate([[0], np.cumsum(splits)])
    col = lambda k: w_in[l][:, offs[k]:offs[k + 1]]
    gate_w = col(11)
    zpad = jnp.zeros((D_MODEL, LANES - 3 * NSA_HPG), F32)
    gate_cols = []
    for g in range(NSA_KV):
        gate_cols += [gate_w[:, g * 3 * NSA_HPG:(g + 1) * 3 * NSA_HPG], zpad]
    w_main = jnp.concatenate([col(12), col(13), col(0), col(2), col(3),
                              col(7), col(8), col(9), col(10), col(4)] + gate_cols, axis=1).astype(BF16)
    w_cmp = jnp.concatenate([col(1), col(5), col(6)], axis=1).astype(BF16)

    lb = lower_bounds[l]
    lbp = jnp.stack([jnp.log(lb), jnp.log1p(-lb), 1.0 - lb])

    w1 = cmp_w1[l].reshape(2, 2, CMP_STRIDE, NSA_DH, CMP_HID)
    w1c = jnp.zeros((2, CMP_STRIDE, NSA_KV, NSA_DH, NSA_KV, 2, CMP_HID), F32)
    for g in range(NSA_KV):
        w1c = w1c.at[:, :, g, :, g, :, :].set(w1.transpose(0, 2, 3, 1, 4))
    w1c = w1c.reshape(2, CMP_STRIDE * NSA_KVW, NSA_KV * 2 * CMP_HID).astype(BF16)
    pos8 = jnp.zeros((2, 8, CMP_BLOCK * NSA_DH), F32).at[:, 0, :].set(cmp_pos[l].reshape(2, -1))
    w2p = jnp.zeros((2, CMP_HID, LANES), F32)
    w2p = w2p.at[0, :, :NSA_DH].set(cmp_w2[l, 0]).at[1, :NSA_DH, :].set(cmp_w2[l, 1].T).astype(BF16)
    qk = nsa_qk_gain[l]
    z64 = jnp.zeros((NSA_DH,), F32)
    return dict(
        norm_mix=norm_mix[l][None, :], w_main=w_main, w_cmp=w_cmp, lbp=lbp,
        hg_gain=hg_out_gain[l][None, :],
        gsel=jnp.concatenate([qk[2], qk[2]])[None, :], gwin=jnp.concatenate([qk[3], qk[3]])[None, :],
        gcmp=jnp.concatenate([qk[1], z64])[None, :],
        qgain=jnp.broadcast_to((qk[0] * (NSA_DH ** -0.5 * LOG2E))[:, None], (NSA_DH, Q_BLOCK)),
        pos8=pos8, w1=cmp_w1[l].astype(BF16), w1c=w1c, b1=cmp_b1[l][:, None, :], w2p=w2p,
        wa=w_branch_a[l].astype(BF16), wb=w_branch_b[l].astype(BF16), wo=w_out[l].astype(BF16),
        norm_mlp=norm_mlp[l][None, :], wu=w_up[l].astype(BF16), wd=w_down[l].astype(BF16))


def kernel(x, rel_bias, hg_lb_logits, norm_mix, w_in, hg_out_gain, nsa_qk_gain, cmp_pos, cmp_w1, cmp_b1, cmp_w2, w_branch_a, w_branch_b, w_out, norm_mlp, w_up, w_down):
    B, S, D = x.shape
    T = B * S
    depth = w_in.shape[0]
    assert D == D_MODEL and S % SEL_CHUNK == 0 and S >= WINDOW and S // SEL_BLOCK <= SEL_ROWS
    assert S % Q_BLOCK == 0 and Q_BLOCK % WIN_KB == 0 and WINDOW % WIN_KB == 0
    lb_cum = jnp.cumsum(jax.nn.softmax(hg_lb_logits.astype(F32), axis=0), axis=0)
    lower_bounds = lb_cum - lb_cum[0:1]

    dsat, strip_len = _sel_table_geometry()
    tabc, tabw, tabs = _bias_tables(rel_bias, S, dsat, strip_len)
    ovl = _overlap_matrix(S)
    nc = S // CMP_STRIDE
    tm = min(1024, T)

    x2 = x.reshape(T, D)
    for l in range(depth):
        p = _layer_params(l, lower_bounds, norm_mix, w_in, hg_out_gain, nsa_qk_gain, cmp_pos, cmp_w1,
                          cmp_b1, cmp_w2, w_branch_a, w_branch_b, w_out, norm_mlp, w_up, w_down)
        proj, z, kc, vc = _rms_proj(x2, p["norm_mix"], p["w_main"], p["w_cmp"], tm, P_TILE)
        proj3 = proj.reshape(B, S, P_COLS)
        kc2 = kc.reshape(B, nc, CMP_STRIDE * NSA_KVW)
        vc2 = vc.reshape(B, nc, CMP_STRIDE * NSA_KVW)
        o_a = _hgrn(proj3, z.reshape(B, S, HG_K), p["lbp"], p["hg_gain"])
        prep = _nsa_prep(proj3, kc2, vc2, p["gsel"], p["gwin"], p["gcmp"], p["pos8"], p["w1"],
                         p["w1c"], p["b1"], p["w2p"])
        o_b = _nsa_attn(proj3, prep, tabc, tabw, tabs, ovl, p["qgain"], dsat)
        x2 = _merge(x2, o_a.reshape(T, HG_V), o_b.reshape(T, NSA_Q), proj, p["wa"], p["wb"], p["wo"],
                    min(512, T))
        x2 = _mlp(x2, p["norm_mlp"], p["wu"], p["wd"], tm, 1024)
    return x2.reshape(B, S, D)
```

```python
import functools
import math

import numpy as np
import jax
import jax.numpy as jnp
from jax import lax
from jax.experimental import pallas as pl
from jax.experimental.pallas import tpu as pltpu

F32 = jnp.float32
BF16 = jnp.bfloat16

D_MODEL = 1024
HG_HEADS = 4
HG_DK = 128
HG_DV = 128
HG_K = HG_HEADS * HG_DK
HG_V = HG_HEADS * HG_DV
NSA_HEADS = 8
NSA_KV = 2
NSA_DH = 64
NSA_HPG = NSA_HEADS // NSA_KV
NSA_Q = NSA_HEADS * NSA_DH
NSA_KVW = NSA_KV * NSA_DH
CMP_BLOCK = 32
CMP_STRIDE = 16
CMP_HID = 128
SEL_BLOCK = 64
SEL_TOPK = 8
WINDOW = 512
FORCE_BONUS = 1000.0
REL_BUCKETS = 32
REL_MAX_DIST = 1024
D_FF = 4 * D_MODEL
EPS = 1e-6
NEG_BIG = -1e30
LOG2E = math.log2(math.e)

LANES = 128
VMEM_LIMIT = 56 * 1024 * 1024

P_GAB = 0
P_HG = P_GAB + 2 * D_MODEL
P_KV = P_HG + 3 * HG_K
P_NQ = P_KV + 4 * NSA_KVW
P_GATE = P_NQ + NSA_Q
P_COLS = P_GATE + NSA_KV * LANES
P_TILE = P_COLS // 2

HG_CHUNK = 128
HG_TS = 512
Q_BLOCK = 256
WIN_KB = 128
SEL_CHUNK = 512
WIN_BLOCKS = (WINDOW + Q_BLOCK) // WIN_KB
SEL_SHIFT = int(math.log2(SEL_BLOCK))
SEL_ROWS = LANES - NSA_DH


def _cparams(sem):
    return pltpu.CompilerParams(dimension_semantics=sem, vmem_limit_bytes=VMEM_LIMIT)


def _rms_rows(x, gain):
    ms = jnp.mean(x * x, axis=-1, keepdims=True)
    return x * lax.rsqrt(ms + EPS) * gain


def _proj_kernel(x_ref, g_ref, w_ref, wx_ref, o_ref, z_ref, kc_ref, vc_ref, h_ref):
    @pl.when(pl.program_id(1) == 0)
    def _():
        h = _rms_rows(x_ref[...], g_ref[...]).astype(BF16)
        h_ref[...] = h
        c = jnp.dot(h, wx_ref[...], preferred_element_type=F32)
        z_ref[...] = c[:, :HG_K]
        kc_ref[...] = c[:, HG_K:HG_K + NSA_KVW].astype(BF16)
        vc_ref[...] = c[:, HG_K + NSA_KVW:].astype(BF16)

    o_ref[...] = jnp.dot(h_ref[...], w_ref[...], preferred_element_type=F32).astype(BF16)


def _rms_proj(x2, gain, w, wx, tm, tn):
    T, D = x2.shape
    N = w.shape[1]
    cmp_sds = jax.ShapeDtypeStruct((T, NSA_KVW), BF16)
    return pl.pallas_call(
        _proj_kernel,
        grid=(T // tm, N // tn),
        in_specs=[
            pl.BlockSpec((tm, D), lambda i, j: (i, 0)),
            pl.BlockSpec((1, D), lambda i, j: (0, 0)),
            pl.BlockSpec((D, tn), lambda i, j: (0, j)),
            pl.BlockSpec(wx.shape, lambda i, j: (0, 0)),
        ],
        out_specs=[
            pl.BlockSpec((tm, tn), lambda i, j: (i, j)),
            pl.BlockSpec((tm, HG_K), lambda i, j: (i, 0)),
            pl.BlockSpec((tm, NSA_KVW), lambda i, j: (i, 0)),
            pl.BlockSpec((tm, NSA_KVW), lambda i, j: (i, 0)),
        ],
        out_shape=[jax.ShapeDtypeStruct((T, N), BF16), jax.ShapeDtypeStruct((T, HG_K), F32),
                   cmp_sds, cmp_sds],
        scratch_shapes=[pltpu.VMEM((tm, D), BF16)],
        compiler_params=_cparams(("parallel", "arbitrary")),
        name="rms_proj",
    )(x2, gain, w, wx)


def _mlp_kernel(x_ref, g_ref, wu_ref, wd_ref, o_ref, h_ref):
    f = pl.program_id(1)

    @pl.when(f == 0)
    def _():
        x = x_ref[...]
        h_ref[...] = _rms_rows(x, g_ref[...]).astype(BF16)
        o_ref[...] = x

    u = jnp.dot(h_ref[...], wu_ref[...], preferred_element_type=F32)
    u = jnp.square(jnp.maximum(u, 0.0)).astype(BF16)
    o_ref[...] += jnp.dot(u, wd_ref[...], preferred_element_type=F32)


def _mlp(x2, gain, wu, wd, tm, tf):
    T, D = x2.shape
    FF = wu.shape[1]
    return pl.pallas_call(
        _mlp_kernel,
        grid=(T // tm, FF // tf),
        in_specs=[
            pl.BlockSpec((tm, D), lambda i, f: (i, 0)),
            pl.BlockSpec((1, D), lambda i, f: (0, 0)),
            pl.BlockSpec((D, tf), lambda i, f: (0, f)),
            pl.BlockSpec((tf, D), lambda i, f: (f, 0)),
        ],
        out_specs=pl.BlockSpec((tm, D), lambda i, f: (i, 0)),
        out_shape=jax.ShapeDtypeStruct((T, D), F32),
        scratch_shapes=[pltpu.VMEM((tm, D), BF16)],
        compiler_params=_cparams(("parallel", "arbitrary")),
        name="mlp",
    )(x2, gain, wu, wd)


def _merge_kernel(x_ref, oa_ref, ob_ref, ga_ref, gb_ref, wa_ref, wb_ref, wo_ref, o_ref):
    ya = jnp.dot(oa_ref[...], wa_ref[...], preferred_element_type=F32)
    yb = jnp.dot(ob_ref[...], wb_ref[...], preferred_element_type=F32)
    mixed = (jax.nn.sigmoid(ga_ref[...].astype(F32)) * ya
             + jax.nn.sigmoid(gb_ref[...].astype(F32)) * yb)
    o_ref[...] = x_ref[...] + jnp.dot(mixed.astype(BF16), wo_ref[...], preferred_element_type=F32)


def _merge(x2, oa, ob, proj, wa, wb, wo, tm):
    T, D = x2.shape
    ga_blk = P_GAB // D
    return pl.pallas_call(
        _merge_kernel,
        grid=(T // tm,),
        in_specs=[
            pl.BlockSpec((tm, D), lambda i: (i, 0)),
            pl.BlockSpec((tm, HG_V), lambda i: (i, 0)),
            pl.BlockSpec((tm, NSA_Q), lambda i: (i, 0)),
            pl.BlockSpec((tm, D), lambda i: (i, ga_blk)),
            pl.BlockSpec((tm, D), lambda i: (i, ga_blk + 1)),
            pl.BlockSpec((HG_V, D), lambda i: (0, 0)),
            pl.BlockSpec((NSA_Q, D), lambda i: (0, 0)),
            pl.BlockSpec((D, D), lambda i: (0, 0)),
        ],
        out_specs=pl.BlockSpec((tm, D), lambda i: (i, 0)),
        out_shape=jax.ShapeDtypeStruct((T, D), F32),
        compiler_params=_cparams(("parallel",)),
        name="merge",
    )(x2, oa, ob, proj, proj, wa, wb, wo)


def _hgrn_consts(C):
    L = int(math.log2(C))
    idx = np.arange(C)
    mats = [(idx[None, :] <= idx[:, None]),
            (idx[None, :] > idx[:, None])]
    bmask = []
    for l in range(L):
        m = 1 << l
        r = (idx & ~(2 * m - 1)) + m
        lo = np.minimum(idx, r)[:, None]
        hi = np.maximum(idx, r)[:, None]
        mats.append((idx[None, :] > lo) & (idx[None, :] <= hi))
        blk = idx >> (l + 1)
        bmask.append(blk[:, None] == blk[None, :])
    bmask.append(idx[:, None] == idx[None, :])
    seg = np.concatenate(mats, axis=0).astype(np.float32)
    seg = np.concatenate([seg, seg], axis=1)
    return seg, np.stack(bmask).astype(np.float32), L


def _hgrn_kernel(q_ref, f_ref, i_ref, g_ref, lbp_ref, gain_ref, seg_ref, bm_ref, o_ref, st_ref,
                 k_ref, sg_ref, *, C, L, nchunk):
    @pl.when(pl.program_id(1) == 0)
    def _():
        st_ref[...] = jnp.zeros_like(st_ref)

    rowi = lax.broadcasted_iota(jnp.int32, (C, HG_DK), 0)
    nt = (((1,), (1,)), ((), ()))
    tn = (((0,), (0,)), ((), ()))

    def chunk(c, carry):
        r0 = pl.multiple_of(c * C, C)
        z = f_ref[0, pl.ds(r0, C), :]
        ez = jnp.exp(-jnp.abs(z))
        y = lbp_ref[1:2, :] + (jnp.minimum(z, 0.0) - jnp.log(1.0 + ez))
        log_lb = lbp_ref[0:1, :]
        logf = (jnp.maximum(log_lb, y) + jnp.log(1.0 + jnp.exp(-jnp.abs(log_lb - y)))) * LOG2E
        k_ref[...] = lbp_ref[2:3, :] * (jnp.where(z >= 0.0, ez, 1.0) / (1.0 + ez))
        g_hi = logf.astype(BF16)
        g_lo = (logf - g_hi.astype(F32)).astype(BF16)
        sg_ref[...] = jnp.dot(seg_ref[...], jnp.concatenate([g_hi, g_lo], axis=0),
                              preferred_element_type=F32)
        for h in range(HG_HEADS):
            cs = slice(h * HG_DK, (h + 1) * HG_DK)
            qb = q_ref[0, pl.ds(r0, C), cs]
            q = qb.astype(F32)
            vb = i_ref[0, pl.ds(r0, C), cs]
            gg = g_ref[0, pl.ds(r0, C), cs].astype(F32)
            k = k_ref[:, cs]
            b = sg_ref[0:C, cs]
            qd = (q * jnp.exp2(b)).astype(BF16)
            kd = (k * jnp.exp2(sg_ref[C:2 * C, cs])).astype(BF16)
            st = st_ref[h]
            o = lax.dot_general(qd, st.astype(BF16), nt, preferred_element_type=F32)
            s = lax.dot_general(qb, k.astype(BF16), nt,
                                preferred_element_type=F32) * bm_ref[L]
            for l in range(L):
                e = jnp.exp2(sg_ref[(2 + l) * C:(3 + l) * C, cs])
                second = (rowi & (1 << l)) != 0
                ql = jnp.where(second, q * e, 0.0).astype(BF16)
                kl = jnp.where(second, 0.0, k * e).astype(BF16)
                s = s + lax.dot_general(ql, kl, nt, preferred_element_type=F32) * bm_ref[l]
            o = o + jnp.dot(s.astype(BF16), vb, preferred_element_type=F32)
            st_ref[h] = (st * jnp.exp2(b[C - 1:C, :])
                         + lax.dot_general(vb, kd, tn, preferred_element_type=F32))
            o = _rms_rows(o, gain_ref[...]) * (gg * jax.nn.sigmoid(gg))
            o_ref[0, pl.ds(r0, C), cs] = o.astype(BF16)
        return carry

    lax.fori_loop(0, nchunk, chunk, 0)


def _hgrn(proj3, z3, lbp, gain, C=HG_CHUNK, ts=HG_TS):
    B, S, _ = proj3.shape
    ts = min(ts, S)
    seg, bm, L = _hgrn_consts(C)
    blk = lambda k: pl.BlockSpec((1, ts, HG_K), lambda b, s: (b, s, P_HG // HG_K + k))
    return pl.pallas_call(
        functools.partial(_hgrn_kernel, C=C, L=L, nchunk=ts // C),
        grid=(B, S // ts),
        in_specs=[
            blk(0), pl.BlockSpec((1, ts, HG_K), lambda b, s: (b, s, 0)), blk(1), blk(2),
            pl.BlockSpec((3, HG_K), lambda b, s: (0, 0)),
            pl.BlockSpec((1, HG_DV), lambda b, s: (0, 0)),
            pl.BlockSpec(seg.shape, lambda b, s: (0, 0)),
            pl.BlockSpec(bm.shape, lambda b, s: (0, 0, 0)),
        ],
        out_specs=pl.BlockSpec((1, ts, HG_V), lambda b, s: (b, s, 0)),
        out_shape=jax.ShapeDtypeStruct((B, S, HG_V), BF16),
        scratch_shapes=[pltpu.VMEM((HG_HEADS, HG_DV, HG_DK), F32), pltpu.VMEM((C, HG_K), F32),
                        pltpu.VMEM(((2 + L) * C, HG_K), F32)],
        compiler_params=_cparams(("parallel", "arbitrary")),
        name="hgrn2",
    )(proj3, z3, proj3, proj3, lbp, gain, jnp.asarray(seg, BF16), jnp.asarray(bm))


def _half_rms(x, gain2, lane):
    sq = x * x
    s_lo = jnp.sum(jnp.where(lane < NSA_DH, sq, 0.0), axis=-1, keepdims=True)
    s_hi = jnp.sum(jnp.where(lane < NSA_DH, 0.0, sq), axis=-1, keepdims=True)
    ms = jnp.where(lane < NSA_DH, s_lo, s_hi) * (1.0 / NSA_DH)
    return x * lax.rsqrt(ms + EPS) * gain2


def _gelu_tanh(x):
    return 0.5 * x * (1.0 + jnp.tanh(math.sqrt(2.0 / math.pi) * (x + 0.044715 * (x * x * x))))


def _prep_kernel(kv_ref, kc_ref, vc_ref, gsel_ref, gwin_ref, gcmp_ref, pos_ref, w1_ref, w1c_ref,
                 b1_ref, w2_ref, ksa_ref, vst_ref, kwa_ref, vwt_ref, kca_ref, vct_ref, *, S, rows):
    lane = lax.broadcasted_iota(jnp.int32, (rows, LANES), 1)
    rowi = lax.broadcasted_iota(jnp.int32, (rows, LANES), 0)
    ones_row = (lax.broadcasted_iota(jnp.int32, (NSA_DH, rows), 0) == 0).astype(F32)
    low = lane < NSA_DH

    def body(i, carry):
        r0 = pl.multiple_of(i * rows, rows)
        blk_hot = jnp.where(lane - NSA_DH == ((r0 + rowi) >> SEL_SHIFT), 1.0, 0.0)
        for src, gain_ref, aug, k_dst, v_dst, vrows in (
                (0, gsel_ref, blk_hot, ksa_ref, vst_ref, rows),
                (2, gwin_ref, jnp.zeros((rows, LANES), F32), kwa_ref, vwt_ref, WIN_KB)):
            kk = kv_ref[0, pl.ds(r0, rows), src * LANES:(src + 1) * LANES].astype(F32)
            vv = kv_ref[0, pl.ds(r0, rows), (src + 1) * LANES:(src + 2) * LANES].astype(F32)
            kn = _half_rms(kk, gain_ref[...], lane)
            k_dst[0, 0, pl.ds(r0, rows), :] = jnp.where(low, kn, aug).astype(BF16)
            k_dst[0, 1, pl.ds(r0, rows), :] = jnp.where(low, pltpu.roll(kn, NSA_DH, 1), aug).astype(BF16)
            vt = vv.T
            for g in range(NSA_KV):
                vg = jnp.concatenate([vt[g * NSA_DH:(g + 1) * NSA_DH], ones_row], axis=0).astype(BF16)
                for u in range(rows // vrows):
                    v_dst[0, g, i * (rows // vrows) + u] = vg[:, u * vrows:(u + 1) * vrows]
        return carry

    lax.fori_loop(0, S // rows, body, 0)

    nc = S // CMP_STRIDE
    outs = []
    for j, src_ref in ((0, kc_ref), (1, vc_ref)):
        bias = jnp.dot(pos_ref[j].astype(BF16), w1_ref[j], preferred_element_type=F32)[0:1] + b1_ref[j]
        pj = jnp.dot(src_ref[0], w1c_ref[j], preferred_element_type=F32)
        per_g = []
        for g in range(NSA_KV):
            first = pj[:, (2 * g) * CMP_HID:(2 * g + 1) * CMP_HID]
            second = pj[:, (2 * g + 1) * CMP_HID:(2 * g + 2) * CMP_HID]
            hid = _gelu_tanh(first + pltpu.roll(second, nc - 1, 0) + bias).astype(BF16)
            if j == 0:
                per_g.append(jnp.dot(hid, w2_ref[0], preferred_element_type=F32))
            else:
                per_g.append(lax.dot_general(w2_ref[1], hid, (((1,), (1,)), ((), ())),
                                             preferred_element_type=F32))
        outs.append(per_g)
    for g in range(NSA_KV):
        kc = outs[0][g]
        ms = jnp.sum(kc * kc, axis=-1, keepdims=True) * (1.0 / NSA_DH)
        kca_ref[0, g] = (kc * lax.rsqrt(ms + EPS) * gcmp_ref[...]).astype(BF16)
        vct_ref[0, g] = outs[1][g].astype(BF16)


def _nsa_prep(proj3, kc2, vc2, gsel, gwin, gcmp, pos8, w1, w1c, b1, w2p):
    B, S, _ = proj3.shape
    nc = S // CMP_STRIDE
    rows = SEL_CHUNK
    full = lambda shp: pl.BlockSpec(shp, lambda b: (0,) * len(shp))
    whole = lambda shp: pl.BlockSpec((1,) + shp, lambda b: (b,) + (0,) * len(shp))
    shapes = [(NSA_KV, S, LANES), (NSA_KV, S // SEL_CHUNK, LANES, SEL_CHUNK),
              (NSA_KV, S, LANES), (NSA_KV, S // WIN_KB, LANES, WIN_KB),
              (NSA_KV, nc, LANES), (NSA_KV, LANES, nc)]
    return pl.pallas_call(
        functools.partial(_prep_kernel, S=S, rows=rows),
        grid=(B,),
        in_specs=[
            pl.BlockSpec((1, S, 4 * NSA_KVW), lambda b: (b, 0, P_KV // (4 * NSA_KVW))),
            pl.BlockSpec((1, nc, CMP_STRIDE * NSA_KVW), lambda b: (b, 0, 0)),
            pl.BlockSpec((1, nc, CMP_STRIDE * NSA_KVW), lambda b: (b, 0, 0)),
            full((1, LANES)), full((1, LANES)), full((1, LANES)),
            full(pos8.shape), full(w1.shape), full(w1c.shape), full(b1.shape), full(w2p.shape),
        ],
        out_specs=[whole(s) for s in shapes],
        out_shape=[jax.ShapeDtypeStruct((B,) + s, BF16) for s in shapes],
        compiler_params=_cparams(("parallel",)),
        name="nsa_prep",
    )(proj3, kc2, vc2, gsel, gwin, gcmp, pos8, w1, w1c, b1, w2p)


def _sel_table_geometry():
    last_start = (REL_BUCKETS // 2) * (REL_MAX_DIST / (REL_BUCKETS // 2)) ** (
        (REL_BUCKETS - REL_BUCKETS // 2 - 1) / (REL_BUCKETS - REL_BUCKETS // 2))
    dsat = int(math.ceil((last_start + 16 + SEL_CHUNK) / LANES)) * LANES
    return dsat, dsat + SEL_CHUNK


def _attn_kernel(q_ref, gate_ref, kca_ref, vct_ref, ksa_ref, vst_ref, kwa_ref, vwt_ref, tabc_ref,
                 tabw_ref, tabs_ref, ovl_ref, qg_ref, o_ref, m_ref, acc_ref, sa_ref, sb_ref, *, nsel, dsat):
    qi = pl.program_id(2)
    QB = Q_BLOCK
    R = NSA_HPG * QB
    DH = NSA_DH

    xt = q_ref[0].astype(F32).T
    cols = []
    for h in range(NSA_HPG):
        xh = xt[h * DH:(h + 1) * DH]
        ms = jnp.mean(xh * xh, axis=0, keepdims=True)
        cols.append(xh * lax.rsqrt(ms + EPS) * qg_ref[...])
    qt = jnp.concatenate(cols, axis=1).astype(BF16)
    qta = jnp.concatenate([qt, jnp.zeros_like(qt)], axis=0)

    s = jnp.dot(kca_ref[0, 0], qta, preferred_element_type=F32) + tabc_ref[0, 0]
    mx = jnp.max(s, axis=0, keepdims=True)
    e = jnp.exp2(s - mx)
    p = jnp.where(s > 0.5 * NEG_BIG, e / jnp.sum(e, axis=0, keepdims=True), 0.0)
    o_c = jnp.dot(vct_ref[0, 0], p.astype(BF16), preferred_element_type=F32)

    psum = p[:, 0:QB] + p[:, QB:2 * QB] + p[:, 2 * QB:3 * QB] + p[:, 3 * QB:4 * QB]
    p_hi = psum.astype(BF16)
    p_lo = (psum - p_hi.astype(F32)).astype(BF16)
    imp = (jnp.dot(ovl_ref[...], p_hi, preferred_element_type=F32)
           + jnp.dot(ovl_ref[...], p_lo, preferred_element_type=F32))
    t = qi * QB + lax.broadcasted_iota(jnp.int32, (SEL_ROWS, QB), 1)
    jrow = lax.broadcasted_iota(jnp.int32, (SEL_ROWS, QB), 0)
    jcur = t >> SEL_SHIFT
    forced = (jrow == 0) | (jrow == jcur) | (jrow == jcur - 1)
    work = jnp.where(jrow * SEL_BLOCK <= t, imp + jnp.where(forced, FORCE_BONUS, 0.0), -1.0)
    work = jnp.where(jrow < nsel, work, -5.0)
    jf = jrow.astype(F32)
    sel = jnp.zeros((SEL_ROWS, QB), F32)
    for _ in range(min(SEL_TOPK, nsel)):
        best = jnp.max(work, axis=0, keepdims=True)
        first = jnp.min(jnp.where(work == best, jf, float(SEL_ROWS)), axis=0, keepdims=True)
        hit = jf == first
        sel = jnp.where(hit, 1.0, sel)
        work = jnp.where(hit, -3.0, work)
    selneg = jnp.where(sel > 0.5, 0.0, NEG_BIG).astype(BF16)
    qts = jnp.concatenate([qt, jnp.concatenate([selneg] * NSA_HPG, axis=1)], axis=0)

    tiles, vts = [], []
    for r in range(WIN_BLOCKS):
        kb = qi * (QB // WIN_KB) - WINDOW // WIN_KB + r
        kbc = jnp.maximum(kb, 0)
        kw = kwa_ref[0, 0, pl.ds(pl.multiple_of(kbc * WIN_KB, WIN_KB), WIN_KB), :]
        vts.append(vwt_ref[0, 0, kbc])
        sw = jnp.dot(kw, qta, preferred_element_type=F32) + tabw_ref[0, 0, r * WIN_KB:(r + 1) * WIN_KB, :]
        tiles.append(sw + jnp.where(kb >= 0, 0.0, NEG_BIG))
    mw = tiles[0]
    for r in range(1, WIN_BLOCKS):
        mw = jnp.maximum(mw, tiles[r])
    mw = jnp.max(mw, axis=0, keepdims=True)
    o_w = jnp.zeros((LANES, R), F32)
    for r in range(WIN_BLOCKS):
        o_w = o_w + jnp.dot(vts[r], jnp.exp2(tiles[r] - mw).astype(BF16), preferred_element_type=F32)

    m_ref[...] = jnp.full_like(m_ref, NEG_BIG)
    acc_ref[...] = jnp.zeros_like(acc_ref)

    def scores(c):
        k0 = pl.multiple_of(c * SEL_CHUNK, SEL_CHUNK)
        delta = qi * QB - c * SEL_CHUNK
        start = pl.multiple_of(dsat - jnp.minimum(delta, dsat), LANES)
        return (jnp.dot(ksa_ref[0, 0, pl.ds(k0, SEL_CHUNK), :], qts, preferred_element_type=F32)
                + tabs_ref[0, 0, pl.ds(start, SEL_CHUNK), :])

    def consume(s_ref, c):
        sc = s_ref[...]
        m_old = m_ref[...]
        m_new = jnp.maximum(m_old, jnp.max(sc, axis=0, keepdims=True))
        alpha = jnp.exp2(m_old - m_new)
        pe = jnp.exp2(sc - m_new).astype(BF16)
        acc_ref[...] = alpha * acc_ref[...] + jnp.dot(vst_ref[0, 0, c], pe, preferred_element_type=F32)
        m_ref[...] = m_new

    n_chunks = (qi * QB + QB - 1) // SEL_CHUNK + 1
    n_pairs = (n_chunks - 1) // 2
    sa_ref[...] = scores(0)

    def sel_pair(pr, carry):
        c = 2 * pr
        sb_ref[...] = scores(c + 1)
        consume(sa_ref, c)
        sa_ref[...] = scores(c + 2)
        consume(sb_ref, c + 1)
        return carry

    lax.fori_loop(0, n_pairs, sel_pair, 0)

    @pl.when(n_chunks % 2 == 0)
    def _():
        sb_ref[...] = scores(n_chunks - 1)
        consume(sa_ref, n_chunks - 2)
        consume(sb_ref, n_chunks - 1)

    @pl.when(n_chunks % 2 == 1)
    def _():
        consume(sa_ref, n_chunks - 1)

    acc = acc_ref[...]

    gt = jax.nn.sigmoid(gate_ref[0].astype(F32)).T
    o_s = acc[0:DH] / acc[DH:DH + 1]
    o_w = o_w[0:DH] / o_w[DH:DH + 1]
    heads = []
    for h in range(NSA_HPG):
        cs = slice(h * QB, (h + 1) * QB)
        heads.append(gt[3 * h:3 * h + 1] * o_c[0:DH, cs] + gt[3 * h + 1:3 * h + 2] * o_s[:, cs]
                     + gt[3 * h + 2:3 * h + 3] * o_w[:, cs])
    o_ref[0] = jnp.concatenate(heads, axis=0).T.astype(BF16)


def _nsa_attn(proj3, prep, tabc, tabw, tabs, ovl, qgain, dsat):
    B, S, _ = proj3.shape
    ksa, vst, kwa, vwt, kca, vct = prep
    nq = S // Q_BLOCK
    nc = S // CMP_STRIDE
    nsel = S // SEL_BLOCK
    R = NSA_HPG * Q_BLOCK
    gw = NSA_HPG * NSA_DH
    per_bg = lambda a: pl.BlockSpec((1, 1) + a.shape[2:], lambda b, g, i: (b, g) + (0,) * (a.ndim - 2))
    return pl.pallas_call(
        functools.partial(_attn_kernel, nsel=nsel, dsat=dsat),
        grid=(B, NSA_KV, nq),
        in_specs=[
            pl.BlockSpec((1, Q_BLOCK, gw), lambda b, g, i: (b, i, P_NQ // gw + g)),
            pl.BlockSpec((1, Q_BLOCK, LANES), lambda b, g, i: (b, i, P_GATE // LANES + g)),
            per_bg(kca), per_bg(vct), per_bg(ksa), per_bg(vst), per_bg(kwa), per_bg(vwt),
            pl.BlockSpec((1, 1, nc, R), lambda b, g, i: (g, i, 0, 0)),
            pl.BlockSpec((1, 1) + tabw.shape[2:], lambda b, g, i: (g, 0, 0, 0)),
            pl.BlockSpec((1, 1) + tabs.shape[2:], lambda b, g, i: (g, 0, 0, 0)),
            pl.BlockSpec(ovl.shape, lambda b, g, i: (0, 0)),
            pl.BlockSpec(qgain.shape, lambda b, g, i: (0, 0)),
        ],
        out_specs=pl.BlockSpec((1, Q_BLOCK, gw), lambda b, g, i: (b, i, g)),
        out_shape=jax.ShapeDtypeStruct((B, S, NSA_Q), BF16),
        scratch_shapes=[pltpu.VMEM((1, R), F32), pltpu.VMEM((LANES, R), F32),
                        pltpu.VMEM((SEL_CHUNK, R), F32), pltpu.VMEM((SEL_CHUNK, R), F32)],
        compiler_params=_cparams(("parallel", "parallel", "arbitrary")),
        name="nsa_attn",
    )(proj3, proj3, kca, vct, ksa, vst, kwa, vwt, tabc, tabw, tabs, ovl, qgain)


def _rel_bucket(d):
    max_exact = REL_BUCKETS // 2
    d = jnp.maximum(d, 0)
    df = jnp.maximum(d, 1).astype(F32)
    large = max_exact + (jnp.log(df / max_exact) / math.log(REL_MAX_DIST / max_exact)
                         * (REL_BUCKETS - max_exact)).astype(jnp.int32)
    return jnp.where(d < max_exact, d, jnp.minimum(large, REL_BUCKETS - 1))


def _bias_kernel(rb_ref, bkt_ref, o_ref, *, rows):
    g = pl.program_id(0)
    step = 32
    for r0 in range(0, rows, step):
        bkt = bkt_ref[0, r0:r0 + step, :]
        acc = [jnp.full(bkt.shape, NEG_BIG, F32) for _ in range(NSA_HPG)]
        for b in range(REL_BUCKETS):
            hit = bkt == b
            for h in range(NSA_HPG):
                acc[h] = jnp.where(hit, rb_ref[b, g * NSA_HPG + h] * LOG2E, acc[h])
        for h in range(NSA_HPG):
            o_ref[0, 0, r0:r0 + step, h * Q_BLOCK:(h + 1) * Q_BLOCK] = acc[h]


def _bias_table(rel_bias, bucket):
    nt, rows, cols = bucket.shape
    return pl.pallas_call(
        functools.partial(_bias_kernel, rows=rows),
        grid=(NSA_KV, nt),
        in_specs=[
            pl.BlockSpec(memory_space=pltpu.SMEM),
            pl.BlockSpec((1, rows, cols), lambda g, t: (t, 0, 0)),
        ],
        out_specs=pl.BlockSpec((1, 1, rows, NSA_HPG * cols), lambda g, t: (g, t, 0, 0)),
        out_shape=jax.ShapeDtypeStruct((NSA_KV, nt, rows, NSA_HPG * cols), F32),
        compiler_params=_cparams(("parallel", "parallel")),
        name="bias_table",
    )(rel_bias, bucket)


def _bias_tables(rel_bias, S, dsat, strip_len):
    nq = S // Q_BLOCK
    nc = S // CMP_STRIDE
    n_cmp = (S - CMP_BLOCK) // CMP_STRIDE + 1
    ar = lambda n: jnp.arange(n, dtype=jnp.int32)
    bucket = lambda d, valid: jnp.where(valid, _rel_bucket(d), -1)
    i = ar(Q_BLOCK)[None, None, :]
    c = ar(nc)[None, :, None]
    d_c = ar(nq)[:, None, None] * Q_BLOCK + i - (c * CMP_STRIDE + CMP_BLOCK - 1)
    tabc = _bias_table(rel_bias, bucket(d_c, (d_c >= 0) & (c < n_cmp)))
    d_w = i + WINDOW - ar(WIN_BLOCKS * WIN_KB)[None, :, None]
    tabw = _bias_table(rel_bias, bucket(d_w, (d_w >= 0) & (d_w < WINDOW)))
    d_s = i - ar(strip_len)[None, :, None] + dsat
    tabs = _bias_table(rel_bias, bucket(d_s, d_s >= 0))
    return tabc, tabw, tabs


def _overlap_matrix(S):
    nc = S // CMP_STRIDE
    nsel = S // SEL_BLOCK
    n_cmp = (S - CMP_BLOCK) // CMP_STRIDE + 1
    c = np.arange(nc)[None, :]
    sel_start = (np.arange(SEL_ROWS) * SEL_BLOCK)[:, None]
    ovl = ((c * CMP_STRIDE < sel_start + SEL_BLOCK) & (c * CMP_STRIDE + CMP_BLOCK - 1 >= sel_start)
           & (c < n_cmp) & (np.arange(SEL_ROWS)[:, None] < nsel))
    return jnp.asarray(ovl, BF16)


def _layer_params(l, lower_bounds, norm_mix, w_in, hg_out_gain, nsa_qk_gain, cmp_pos, cmp_w1, cmp_b1,
                  cmp_w2, w_branch_a, w_branch_b, w_out, norm_mlp, w_up, w_down):
    splits = (HG_K, HG_K, HG_V, HG_V, NSA_Q, NSA_KVW, NSA_KVW, NSA_KVW, NSA_KVW, NSA_KVW, NSA_KVW,
              3 * NSA_HEADS, D_MODEL, D_MODEL)
    offs = np.concatenate([[0], np.cumsum(splits)])
    col = lambda k: w_in[l][:, offs[k]:offs[k + 1]]
    gate_w = col(11)
    zpad = jnp.zeros((D_MODEL, LANES - 3 * NSA_HPG), F32)
    gate_cols = []
    for g in range(NSA_KV):
        gate_cols += [gate_w[:, g * 3 * NSA_HPG:(g + 1) * 3 * NSA_HPG], zpad]
    w_main = jnp.concatenate([col(12), col(13), col(0), col(2), col(3),
                              col(7), col(8), col(9), col(10), col(4)] + gate_cols, axis=1).astype(BF16)
    w_cmp = jnp.concatenate([col(1), col(5), col(6)], axis=1).astype(BF16)

    lb = lower_bounds[l]
    lbp = jnp.stack([jnp.log(lb), jnp.log1p(-lb), 1.0 - lb])

    w1 = cmp_w1[l].reshape(2, 2, CMP_STRIDE, NSA_DH, CMP_HID)
    w1c = jnp.zeros((2, CMP_STRIDE, NSA_KV, NSA_DH, NSA_KV, 2, CMP_HID), F32)
    for g in range(NSA_KV):
        w1c = w1c.at[:, :, g, :, g, :, :].set(w1.transpose(0, 2, 3, 1, 4))
    w1c = w1c.reshape(2, CMP_STRIDE * NSA_KVW, NSA_KV * 2 * CMP_HID).astype(BF16)
    pos8 = jnp.zeros((2, 8, CMP_BLOCK * NSA_DH), F32).at[:, 0, :].set(cmp_pos[l].reshape(2, -1))
    w2p = jnp.zeros((2, CMP_HID, LANES), F32)
    w2p = w2p.at[0, :, :NSA_DH].set(cmp_w2[l, 0]).at[1, :NSA_DH, :].set(cmp_w2[l, 1].T).astype(BF16)
    qk = nsa_qk_gain[l]
    z64 = jnp.zeros((NSA_DH,), F32)
    return dict(
        norm_mix=norm_mix[l][None, :], w_main=w_main, w_cmp=w_cmp, lbp=lbp,
        hg_gain=hg_out_gain[l][None, :],
        gsel=jnp.concatenate([qk[2], qk[2]])[None, :], gwin=jnp.concatenate([qk[3], qk[3]])[None, :],
        gcmp=jnp.concatenate([qk[1], z64])[None, :],
        qgain=jnp.broadcast_to((qk[0] * (NSA_DH ** -0.5 * LOG2E))[:, None], (NSA_DH, Q_BLOCK)),
        pos8=pos8, w1=cmp_w1[l].astype(BF16), w1c=w1c, b1=cmp_b1[l][:, None, :], w2p=w2p,
        wa=w_branch_a[l].astype(BF16), wb=w_branch_b[l].astype(BF16), wo=w_out[l].astype(BF16),
        norm_mlp=norm_mlp[l][None, :], wu=w_up[l].astype(BF16), wd=w_down[l].astype(BF16))


def kernel(x, rel_bias, hg_lb_logits, norm_mix, w_in, hg_out_gain, nsa_qk_gain, cmp_pos, cmp_w1, cmp_b1, cmp_w2, w_branch_a, w_branch_b, w_out, norm_mlp, w_up, w_down):
    B, S, D = x.shape
    T = B * S
    depth = w_in.shape[0]
    assert D == D_MODEL and S % SEL_CHUNK == 0 and S >= WINDOW and S // SEL_BLOCK <= SEL_ROWS
    assert S % Q_BLOCK == 0 and Q_BLOCK % WIN_KB == 0 and WINDOW % WIN_KB == 0
    lb_cum = jnp.cumsum(jax.nn.softmax(hg_lb_logits.astype(F32), axis=0), axis=0)
    lower_bounds = lb_cum - lb_cum[0:1]

    dsat, strip_len = _sel_table_geometry()
    tabc, tabw, tabs = _bias_tables(rel_bias, S, dsat, strip_len)
    ovl = _overlap_matrix(S)
    nc = S // CMP_STRIDE
    tm = min(1024, T)

    x2 = x.reshape(T, D)
    for l in range(depth):
        p = _layer_params(l, lower_bounds, norm_mix, w_in, hg_out_gain, nsa_qk_gain, cmp_pos, cmp_w1,
                          cmp_b1, cmp_w2, w_branch_a, w_branch_b, w_out, norm_mlp, w_up, w_down)
        proj, z, kc, vc = _rms_proj(x2, p["norm_mix"], p["w_main"], p["w_cmp"], tm, P_TILE)
        proj3 = proj.reshape(B, S, P_COLS)
        kc2 = kc.reshape(B, nc, CMP_STRIDE * NSA_KVW)
        vc2 = vc.reshape(B, nc, CMP_STRIDE * NSA_KVW)
        o_a = _hgrn(proj3, z.reshape(B, S, HG_K), p["lbp"], p["hg_gain"])
        prep = _nsa_prep(proj3, kc2, vc2, p["gsel"], p["gwin"], p["gcmp"], p["pos8"], p["w1"],
                         p["w1c"], p["b1"], p["w2p"])
        o_b = _nsa_attn(proj3, prep, tabc, tabw, tabs, ovl, p["qgain"], dsat)
        x2 = _merge(x2, o_a.reshape(T, HG_V), o_b.reshape(T, NSA_Q), proj, p["wa"], p["wb"], p["wo"],
                    min(512, T))
        x2 = _mlp(x2, p["norm_mlp"], p["wu"], p["wd"], tm, 1024)
    return x2.reshape(B, S, D)
```

```python
import functools
import math

import numpy as np
import jax
import jax.numpy as jnp
from jax import lax
from jax.experimental import pallas as pl
from jax.experimental.pallas import tpu as pltpu

F32 = jnp.float32
BF16 = jnp.bfloat16

D_MODEL = 1024
HG_HEADS = 4
HG_DK = 128
HG_DV = 128
HG_K = HG_HEADS * HG_DK
HG_V = HG_HEADS * HG_DV
NSA_HEADS = 8
NSA_KV = 2
NSA_DH = 64
NSA_HPG = NSA_HEADS // NSA_KV
NSA_Q = NSA_HEADS * NSA_DH
NSA_KVW = NSA_KV * NSA_DH
CMP_BLOCK = 32
CMP_STRIDE = 16
CMP_HID = 128
SEL_BLOCK = 64
SEL_TOPK = 8
WINDOW = 512
FORCE_BONUS = 1000.0
REL_BUCKETS = 32
REL_MAX_DIST = 1024
D_FF = 4 * D_MODEL
EPS = 1e-6
NEG_BIG = -1e30
LOG2E = math.log2(math.e)

LANES = 128
VMEM_LIMIT = 56 * 1024 * 1024

P_GAB = 0
P_HG = P_GAB + 2 * D_MODEL
P_KV = P_HG + 3 * HG_K
P_NQ = P_KV + 4 * NSA_KVW
P_GATE = P_NQ + NSA_Q
P_COLS = P_GATE + NSA_KV * LANES
P_TILE = P_COLS // 2

HG_CHUNK = 128
HG_TS = 512
Q_BLOCK = 256
WIN_KB = 128
SEL_CHUNK = 512
WIN_BLOCKS = (WINDOW + Q_BLOCK) // WIN_KB
SEL_SHIFT = int(math.log2(SEL_BLOCK))
SEL_ROWS = LANES - NSA_DH


def _cparams(sem):
    return pltpu.CompilerParams(dimension_semantics=sem, vmem_limit_bytes=VMEM_LIMIT)


def _rms_rows(x, gain):
    ms = jnp.mean(x * x, axis=-1, keepdims=True)
    return x * lax.rsqrt(ms + EPS) * gain


def _proj_kernel(x_ref, g_ref, w_ref, wx_ref, o_ref, z_ref, kc_ref, vc_ref, h_ref):
    @pl.when(pl.program_id(1) == 0)
    def _():
        h = _rms_rows(x_ref[...], g_ref[...]).astype(BF16)
        h_ref[...] = h
        c = jnp.dot(h, wx_ref[...], preferred_element_type=F32)
        z_ref[...] = c[:, :HG_K]
        kc_ref[...] = c[:, HG_K:HG_K + NSA_KVW].astype(BF16)
        vc_ref[...] = c[:, HG_K + NSA_KVW:].astype(BF16)

    o_ref[...] = jnp.dot(h_ref[...], w_ref[...], preferred_element_type=F32).astype(BF16)


def _rms_proj(x2, gain, w, wx, tm, tn):
    T, D = x2.shape
    N = w.shape[1]
    cmp_sds = jax.ShapeDtypeStruct((T, NSA_KVW), BF16)
    return pl.pallas_call(
        _proj_kernel,
        grid=(T // tm, N // tn),
        in_specs=[
            pl.BlockSpec((tm, D), lambda i, j: (i, 0)),
            pl.BlockSpec((1, D), lambda i, j: (0, 0)),
            pl.BlockSpec((D, tn), lambda i, j: (0, j)),
            pl.BlockSpec(wx.shape, lambda i, j: (0, 0)),
        ],
        out_specs=[
            pl.BlockSpec((tm, tn), lambda i, j: (i, j)),
            pl.BlockSpec((tm, HG_K), lambda i, j: (i, 0)),
            pl.BlockSpec((tm, NSA_KVW), lambda i, j: (i, 0)),
            pl.BlockSpec((tm, NSA_KVW), lambda i, j: (i, 0)),
        ],
        out_shape=[jax.ShapeDtypeStruct((T, N), BF16), jax.ShapeDtypeStruct((T, HG_K), F32),
                   cmp_sds, cmp_sds],
        scratch_shapes=[pltpu.VMEM((tm, D), BF16)],
        compiler_params=_cparams(("parallel", "arbitrary")),
        name="rms_proj",
    )(x2, gain, w, wx)


def _mlp_kernel(x_ref, g_ref, wu_ref, wd_ref, o_ref, h_ref):
    f = pl.program_id(1)

    @pl.when(f == 0)
    def _():
        x = x_ref[...]
        h_ref[...] = _rms_rows(x, g_ref[...]).astype(BF16)
        o_ref[...] = x

    u = jnp.dot(h_ref[...], wu_ref[...], preferred_element_type=F32)
    u = jnp.square(jnp.maximum(u, 0.0)).astype(BF16)
    o_ref[...] += jnp.dot(u, wd_ref[...], preferred_element_type=F32)


def _mlp(x2, gain, wu, wd, tm, tf):
    T, D = x2.shape
    FF = wu.shape[1]
    return pl.pallas_call(
        _mlp_kernel,
        grid=(T // tm, FF // tf),
        in_specs=[
            pl.BlockSpec((tm, D), lambda i, f: (i, 0)),
            pl.BlockSpec((1, D), lambda i, f: (0, 0)),
            pl.BlockSpec((D, tf), lambda i, f: (0, f)),
            pl.BlockSpec((tf, D), lambda i, f: (f, 0)),
        ],
        out_specs=pl.BlockSpec((tm, D), lambda i, f: (i, 0)),
        out_shape=jax.ShapeDtypeStruct((T, D), F32),
        scratch_shapes=[pltpu.VMEM((tm, D), BF16)],
        compiler_params=_cparams(("parallel", "arbitrary")),
        name="mlp",
    )(x2, gain, wu, wd)


def _merge_kernel(x_ref, oa_ref, ob_ref, ga_ref, gb_ref, wa_ref, wb_ref, wo_ref, o_ref):
    ya = jnp.dot(oa_ref[...], wa_ref[...], preferred_element_type=F32)
    yb = jnp.dot(ob_ref[...], wb_ref[...], preferred_element_type=F32)
    mixed = (jax.nn.sigmoid(ga_ref[...].astype(F32)) * ya
             + jax.nn.sigmoid(gb_ref[...].astype(F32)) * yb)
    o_ref[...] = x_ref[...] + jnp.dot(mixed.astype(BF16), wo_ref[...], preferred_element_type=F32)


def _merge(x2, oa, ob, proj, wa, wb, wo, tm):
    T, D = x2.shape
    ga_blk = P_GAB // D
    return pl.pallas_call(
        _merge_kernel,
        grid=(T // tm,),
        in_specs=[
            pl.BlockSpec((tm, D), lambda i: (i, 0)),
            pl.BlockSpec((tm, HG_V), lambda i: (i, 0)),
            pl.BlockSpec((tm, NSA_Q), lambda i: (i, 0)),
            pl.BlockSpec((tm, D), lambda i: (i, ga_blk)),
            pl.BlockSpec((tm, D), lambda i: (i, ga_blk + 1)),
            pl.BlockSpec((HG_V, D), lambda i: (0, 0)),
            pl.BlockSpec((NSA_Q, D), lambda i: (0, 0)),
            pl.BlockSpec((D, D), lambda i: (0, 0)),
        ],
        out_specs=pl.BlockSpec((tm, D), lambda i: (i, 0)),
        out_shape=jax.ShapeDtypeStruct((T, D), F32),
        compiler_params=_cparams(("parallel",)),
        name="merge",
    )(x2, oa, ob, proj, proj, wa, wb, wo)


def _hgrn_consts(C):
    L = int(math.log2(C))
    idx = np.arange(C)
    mats = [(idx[None, :] <= idx[:, None]),
            (idx[None, :] > idx[:, None])]
    bmask = []
    for l in range(L):
        m = 1 << l
        r = (idx & ~(2 * m - 1)) + m
        lo = np.minimum(idx, r)[:, None]
        hi = np.maximum(idx, r)[:, None]
        mats.append((idx[None, :] > lo) & (idx[None, :] <= hi))
        blk = idx >> (l + 1)
        bmask.append(blk[:, None] == blk[None, :])
    bmask.append(idx[:, None] == idx[None, :])
    seg = np.concatenate(mats, axis=0).astype(np.float32)
    seg = np.concatenate([seg, seg], axis=1)
    return seg, np.stack(bmask).astype(np.float32), L


def _hgrn_kernel(q_ref, f_ref, i_ref, g_ref, lbp_ref, gain_ref, seg_ref, bm_ref, o_ref, st_ref,
                 k_ref, sg_ref, *, C, L, nchunk):
    @pl.when(pl.program_id(1) == 0)
    def _():
        st_ref[...] = jnp.zeros_like(st_ref)

    rowi = lax.broadcasted_iota(jnp.int32, (C, HG_DK), 0)
    nt = (((1,), (1,)), ((), ()))
    tn = (((0,), (0,)), ((), ()))

    def chunk(c, carry):
        r0 = pl.multiple_of(c * C, C)
        z = f_ref[0, pl.ds(r0, C), :]
        ez = jnp.exp(-jnp.abs(z))
        y = lbp_ref[1:2, :] + (jnp.minimum(z, 0.0) - jnp.log(1.0 + ez))
        log_lb = lbp_ref[0:1, :]
        logf = (jnp.maximum(log_lb, y) + jnp.log(1.0 + jnp.exp(-jnp.abs(log_lb - y)))) * LOG2E
        k_ref[...] = lbp_ref[2:3, :] * (jnp.where(z >= 0.0, ez, 1.0) / (1.0 + ez))
        g_hi = logf.astype(BF16)
        g_lo = (logf - g_hi.astype(F32)).astype(BF16)
        sg_ref[...] = jnp.dot(seg_ref[...], jnp.concatenate([g_hi, g_lo], axis=0),
                              preferred_element_type=F32)
        for h in range(HG_HEADS):
            cs = slice(h * HG_DK, (h + 1) * HG_DK)
            qb = q_ref[0, pl.ds(r0, C), cs]
            q = qb.astype(F32)
            vb = i_ref[0, pl.ds(r0, C), cs]
            gg = g_ref[0, pl.ds(r0, C), cs].astype(F32)
            k = k_ref[:, cs]
            b = sg_ref[0:C, cs]
            qd = (q * jnp.exp2(b)).astype(BF16)
            kd = (k * jnp.exp2(sg_ref[C:2 * C, cs])).astype(BF16)
            st = st_ref[h]
            o = lax.dot_general(qd, st.astype(BF16), nt, preferred_element_type=F32)
            s = lax.dot_general(qb, k.astype(BF16), nt,
                                preferred_element_type=F32) * bm_ref[L]
            for l in range(L):
                e = jnp.exp2(sg_ref[(2 + l) * C:(3 + l) * C, cs])
                second = (rowi & (1 << l)) != 0
                ql = jnp.where(second, q * e, 0.0).astype(BF16)
                kl = jnp.where(second, 0.0, k * e).astype(BF16)
                s = s + lax.dot_general(ql, kl, nt, preferred_element_type=F32) * bm_ref[l]
            o = o + jnp.dot(s.astype(BF16), vb, preferred_element_type=F32)
            st_ref[h] = (st * jnp.exp2(b[C - 1:C, :])
                         + lax.dot_general(vb, kd, tn, preferred_element_type=F32))
            o = _rms_rows(o, gain_ref[...]) * (gg * jax.nn.sigmoid(gg))
            o_ref[0, pl.ds(r0, C), cs] = o.astype(BF16)
        return carry

    lax.fori_loop(0, nchunk, chunk, 0)


def _hgrn(proj3, z3, lbp, gain, C=HG_CHUNK, ts=HG_TS):
    B, S, _ = proj3.shape
    ts = min(ts, S)
    seg, bm, L = _hgrn_consts(C)
    blk = lambda k: pl.BlockSpec((1, ts, HG_K), lambda b, s: (b, s, P_HG // HG_K + k))
    return pl.pallas_call(
        functools.partial(_hgrn_kernel, C=C, L=L, nchunk=ts // C),
        grid=(B, S // ts),
        in_specs=[
            blk(0), pl.BlockSpec((1, ts, HG_K), lambda b, s: (b, s, 0)), blk(1), blk(2),
            pl.BlockSpec((3, HG_K), lambda b, s: (0, 0)),
            pl.BlockSpec((1, HG_DV), lambda b, s: (0, 0)),
            pl.BlockSpec(seg.shape, lambda b, s: (0, 0)),
            pl.BlockSpec(bm.shape, lambda b, s: (0, 0, 0)),
        ],
        out_specs=pl.BlockSpec((1, ts, HG_V), lambda b, s: (b, s, 0)),
        out_shape=jax.ShapeDtypeStruct((B, S, HG_V), BF16),
        scratch_shapes=[pltpu.VMEM((HG_HEADS, HG_DV, HG_DK), F32), pltpu.VMEM((C, HG_K), F32),
                        pltpu.VMEM(((2 + L) * C, HG_K), F32)],
        compiler_params=_cparams(("parallel", "arbitrary")),
        name="hgrn2",
    )(proj3, z3, proj3, proj3, lbp, gain, jnp.asarray(seg, BF16), jnp.asarray(bm))


def _half_rms(x, gain2, lane):
    sq = x * x
    s_lo = jnp.sum(jnp.where(lane < NSA_DH, sq, 0.0), axis=-1, keepdims=True)
    s_hi = jnp.sum(jnp.where(lane < NSA_DH, 0.0, sq), axis=-1, keepdims=True)
    ms = jnp.where(lane < NSA_DH, s_lo, s_hi) * (1.0 / NSA_DH)
    return x * lax.rsqrt(ms + EPS) * gain2


def _gelu_tanh(x):
    return 0.5 * x * (1.0 + jnp.tanh(math.sqrt(2.0 / math.pi) * (x + 0.044715 * (x * x * x))))


def _prep_kernel(kv_ref, kc_ref, vc_ref, gsel_ref, gwin_ref, gcmp_ref, pos_ref, w1_ref, w1c_ref,
                 b1_ref, w2_ref, ksa_ref, vst_ref, kwa_ref, vwt_ref, kca_ref, vct_ref, *, S, rows):
    lane = lax.broadcasted_iota(jnp.int32, (rows, LANES), 1)
    rowi = lax.broadcasted_iota(jnp.int32, (rows, LANES), 0)
    ones_row = (lax.broadcasted_iota(jnp.int32, (NSA_DH, rows), 0) == 0).astype(F32)
    low = lane < NSA_DH

    def body(i, carry):
        r0 = pl.multiple_of(i * rows, rows)
        blk_hot = jnp.where(lane - NSA_DH == ((r0 + rowi) >> SEL_SHIFT), 1.0, 0.0)
        for src, gain_ref, aug, k_dst, v_dst, vrows in (
                (0, gsel_ref, blk_hot, ksa_ref, vst_ref, rows),
                (2, gwin_ref, jnp.zeros((rows, LANES), F32), kwa_ref, vwt_ref, WIN_KB)):
            kk = kv_ref[0, pl.ds(r0, rows), src * LANES:(src + 1) * LANES].astype(F32)
            vv = kv_ref[0, pl.ds(r0, rows), (src + 1) * LANES:(src + 2) * LANES].astype(F32)
            kn = _half_rms(kk, gain_ref[...], lane)
            k_dst[0, 0, pl.ds(r0, rows), :] = jnp.where(low, kn, aug).astype(BF16)
            k_dst[0, 1, pl.ds(r0, rows), :] = jnp.where(low, pltpu.roll(kn, NSA_DH, 1), aug).astype(BF16)
            vt = vv.T
            for g in range(NSA_KV):
                vg = jnp.concatenate([vt[g * NSA_DH:(g + 1) * NSA_DH], ones_row], axis=0).astype(BF16)
                for u in range(rows // vrows):
                    v_dst[0, g, i * (rows // vrows) + u] = vg[:, u * vrows:(u + 1) * vrows]
        return carry

    lax.fori_loop(0, S // rows, body, 0)

    nc = S // CMP_STRIDE
    outs = []
    for j, src_ref in ((0, kc_ref), (1, vc_ref)):
        bias = jnp.dot(pos_ref[j].astype(BF16), w1_ref[j], preferred_element_type=F32)[0:1] + b1_ref[j]
        pj = jnp.dot(src_ref[0], w1c_ref[j], preferred_element_type=F32)
        per_g = []
        for g in range(NSA_KV):
            first = pj[:, (2 * g) * CMP_HID:(2 * g + 1) * CMP_HID]
            second = pj[:, (2 * g + 1) * CMP_HID:(2 * g + 2) * CMP_HID]
            hid = _gelu_tanh(first + pltpu.roll(second, nc - 1, 0) + bias).astype(BF16)
            if j == 0:
                per_g.append(jnp.dot(hid, w2_ref[0], preferred_element_type=F32))
            else:
                per_g.append(lax.dot_general(w2_ref[1], hid, (((1,), (1,)), ((), ())),
                                             preferred_element_type=F32))
        outs.append(per_g)
    for g in range(NSA_KV):
        kc = outs[0][g]
        ms = jnp.sum(kc * kc, axis=-1, keepdims=True) * (1.0 / NSA_DH)
        kca_ref[0, g] = (kc * lax.rsqrt(ms + EPS) * gcmp_ref[...]).astype(BF16)
        vct_ref[0, g] = outs[1][g].astype(BF16)


def _nsa_prep(proj3, kc2, vc2, gsel, gwin, gcmp, pos8, w1, w1c, b1, w2p):
    B, S, _ = proj3.shape
    nc = S // CMP_STRIDE
    rows = SEL_CHUNK
    full = lambda shp: pl.BlockSpec(shp, lambda b: (0,) * len(shp))
    whole = lambda shp: pl.BlockSpec((1,) + shp, lambda b: (b,) + (0,) * len(shp))
    shapes = [(NSA_KV, S, LANES), (NSA_KV, S // SEL_CHUNK, LANES, SEL_CHUNK),
              (NSA_KV, S, LANES), (NSA_KV, S // WIN_KB, LANES, WIN_KB),
              (NSA_KV, nc, LANES), (NSA_KV, LANES, nc)]
    return pl.pallas_call(
        functools.partial(_prep_kernel, S=S, rows=rows),
        grid=(B,),
        in_specs=[
            pl.BlockSpec((1, S, 4 * NSA_KVW), lambda b: (b, 0, P_KV // (4 * NSA_KVW))),
            pl.BlockSpec((1, nc, CMP_STRIDE * NSA_KVW), lambda b: (b, 0, 0)),
            pl.BlockSpec((1, nc, CMP_STRIDE * NSA_KVW), lambda b: (b, 0, 0)),
            full((1, LANES)), full((1, LANES)), full((1, LANES)),
            full(pos8.shape), full(w1.shape), full(w1c.shape), full(b1.shape), full(w2p.shape),
        ],
        out_specs=[whole(s) for s in shapes],
        out_shape=[jax.ShapeDtypeStruct((B,) + s, BF16) for s in shapes],
        compiler_params=_cparams(("parallel",)),
        name="nsa_prep",
    )(proj3, kc2, vc2, gsel, gwin, gcmp, pos8, w1, w1c, b1, w2p)


def _sel_table_geometry():
    last_start = (REL_BUCKETS // 2) * (REL_MAX_DIST / (REL_BUCKETS // 2)) ** (
        (REL_BUCKETS - REL_BUCKETS // 2 - 1) / (REL_BUCKETS - REL_BUCKETS // 2))
    dsat = int(math.ceil((last_start + 16 + SEL_CHUNK) / LANES)) * LANES
    return dsat, dsat + SEL_CHUNK


def _attn_kernel(q_ref, gate_ref, kca_ref, vct_ref, ksa_ref, vst_ref, kwa_ref, vwt_ref, tabc_ref,
                 tabw_ref, tabs_ref, ovl_ref, qg_ref, o_ref, m_ref, acc_ref, sa_ref, sb_ref, sw_ref,
                 *, nsel, dsat):
    qi = pl.program_id(2)
    QB = Q_BLOCK
    R = NSA_HPG * QB
    DH = NSA_DH

    xt = q_ref[0].astype(F32).T
    cols = []
    for h in range(NSA_HPG):
        xh = xt[h * DH:(h + 1) * DH]
        ms = jnp.mean(xh * xh, axis=0, keepdims=True)
        cols.append(xh * lax.rsqrt(ms + EPS) * qg_ref[...])
    qt = jnp.concatenate(cols, axis=1).astype(BF16)
    qta = jnp.concatenate([qt, jnp.zeros_like(qt)], axis=0)

    vts = []
    for r in range(WIN_BLOCKS):
        kb = qi * (QB // WIN_KB) - WINDOW // WIN_KB + r
        kbc = jnp.maximum(kb, 0)
        kw = kwa_ref[0, 0, pl.ds(pl.multiple_of(kbc * WIN_KB, WIN_KB), WIN_KB), :]
        vts.append(vwt_ref[0, 0, kbc])
        sw_ref[r * WIN_KB:(r + 1) * WIN_KB, :] = (
            jnp.dot(kw, qta, preferred_element_type=F32)
            + tabw_ref[0, 0, r * WIN_KB:(r + 1) * WIN_KB, :] + jnp.where(kb >= 0, 0.0, NEG_BIG))

    def scores(c, q_aug):
        k0 = pl.multiple_of(c * SEL_CHUNK, SEL_CHUNK)
        delta = qi * QB - c * SEL_CHUNK
        start = pl.multiple_of(dsat - jnp.minimum(delta, dsat), LANES)
        return (jnp.dot(ksa_ref[0, 0, pl.ds(k0, SEL_CHUNK), :], q_aug, preferred_element_type=F32)
                + tabs_ref[0, 0, pl.ds(start, SEL_CHUNK), :])

    sa_ref[...] = scores(0, qta)

    s = jnp.dot(kca_ref[0, 0], qta, preferred_element_type=F32) + tabc_ref[0, 0]
    mx = jnp.max(s, axis=0, keepdims=True)
    e = jnp.exp2(s - mx)
    inv = jnp.where(mx > 0.5 * NEG_BIG, 1.0 / jnp.sum(e, axis=0, keepdims=True), 0.0)
    p = e * inv
    o_c = jnp.dot(vct_ref[0, 0], p.astype(BF16), preferred_element_type=F32)

    psum = p[:, 0:QB] + p[:, QB:2 * QB] + p[:, 2 * QB:3 * QB] + p[:, 3 * QB:4 * QB]
    p_hi = psum.astype(BF16)
    p_lo = (psum - p_hi.astype(F32)).astype(BF16)
    imp = (jnp.dot(ovl_ref[...], p_hi, preferred_element_type=F32)
           + jnp.dot(ovl_ref[...], p_lo, preferred_element_type=F32))
    t = qi * QB + lax.broadcasted_iota(jnp.int32, (SEL_ROWS, QB), 1)
    jrow = lax.broadcasted_iota(jnp.int32, (SEL_ROWS, QB), 0)
    jcur = t >> SEL_SHIFT
    forced = (jrow == 0) | (jrow == jcur) | (jrow == jcur - 1)
    work = jnp.where(jrow * SEL_BLOCK <= t, imp + jnp.where(forced, FORCE_BONUS, 0.0), -1.0)
    work = jnp.where(jrow < nsel, work, -5.0)
    jf = jrow.astype(F32)
    sel = jnp.zeros((SEL_ROWS, QB), F32)
    for _ in range(min(SEL_TOPK, nsel)):
        best = jnp.max(work, axis=0, keepdims=True)
        first = jnp.min(jnp.where(work == best, jf, float(SEL_ROWS)), axis=0, keepdims=True)
        hit = jf == first
        sel = jnp.where(hit, 1.0, sel)
        work = jnp.where(hit, -3.0, work)
    selneg = jnp.concatenate([jnp.where(sel > 0.5, 0.0, NEG_BIG)] * NSA_HPG, axis=1)
    qts = jnp.concatenate([qt, selneg.astype(BF16)], axis=0)
    for j in range(SEL_CHUNK // SEL_BLOCK):
        sa_ref[j * SEL_BLOCK:(j + 1) * SEL_BLOCK, :] += selneg[j:j + 1, :]

    sw = sw_ref[...]
    ew = jnp.exp2(sw - jnp.max(sw, axis=0, keepdims=True)).astype(BF16)
    o_w = jnp.dot(jnp.concatenate(vts, axis=1), ew, preferred_element_type=F32)

    m_ref[...] = jnp.full_like(m_ref, NEG_BIG)
    acc_ref[...] = jnp.zeros_like(acc_ref)

    def consume(s_ref, c):
        sc = s_ref[...]
        m_old = m_ref[...]
        m_new = jnp.maximum(m_old, jnp.max(sc, axis=0, keepdims=True))
        alpha = jnp.exp2(m_old - m_new)
        pe = jnp.exp2(sc - m_new).astype(BF16)
        acc_ref[...] = alpha * acc_ref[...] + jnp.dot(vst_ref[0, 0, c], pe, preferred_element_type=F32)
        m_ref[...] = m_new

    n_chunks = (qi * QB + QB - 1) // SEL_CHUNK + 1
    n_pairs = (n_chunks - 1) // 2

    def sel_pair(pr, carry):
        c = 2 * pr
        sb_ref[...] = scores(c + 1, qts)
        consume(sa_ref, c)
        sa_ref[...] = scores(c + 2, qts)
        consume(sb_ref, c + 1)
        return carry

    lax.fori_loop(0, n_pairs, sel_pair, 0)

    @pl.when(n_chunks % 2 == 0)
    def _():
        sb_ref[...] = scores(n_chunks - 1, qts)
        consume(sa_ref, n_chunks - 2)
        consume(sb_ref, n_chunks - 1)

    @pl.when(n_chunks % 2 == 1)
    def _():
        consume(sa_ref, n_chunks - 1)

    acc = acc_ref[...]

    gt = jax.nn.sigmoid(gate_ref[0].astype(F32)).T
    o_s = acc[0:DH] / acc[DH:DH + 1]
    o_w = o_w[0:DH] / o_w[DH:DH + 1]
    heads = []
    for h in range(NSA_HPG):
        cs = slice(h * QB, (h + 1) * QB)
        heads.append(gt[3 * h:3 * h + 1] * o_c[0:DH, cs] + gt[3 * h + 1:3 * h + 2] * o_s[:, cs]
                     + gt[3 * h + 2:3 * h + 3] * o_w[:, cs])
    o_ref[0] = jnp.concatenate(heads, axis=0).T.astype(BF16)


def _nsa_attn(proj3, prep, tabc, tabw, tabs, ovl, qgain, dsat):
    B, S, _ = proj3.shape
    ksa, vst, kwa, vwt, kca, vct = prep
    nq = S // Q_BLOCK
    nc = S // CMP_STRIDE
    nsel = S // SEL_BLOCK
    R = NSA_HPG * Q_BLOCK
    gw = NSA_HPG * NSA_DH
    per_bg = lambda a: pl.BlockSpec((1, 1) + a.shape[2:], lambda b, g, i: (b, g) + (0,) * (a.ndim - 2))
    return pl.pallas_call(
        functools.partial(_attn_kernel, nsel=nsel, dsat=dsat),
        grid=(B, NSA_KV, nq),
        in_specs=[
            pl.BlockSpec((1, Q_BLOCK, gw), lambda b, g, i: (b, i, P_NQ // gw + g)),
            pl.BlockSpec((1, Q_BLOCK, LANES), lambda b, g, i: (b, i, P_GATE // LANES + g)),
            per_bg(kca), per_bg(vct), per_bg(ksa), per_bg(vst), per_bg(kwa), per_bg(vwt),
            pl.BlockSpec((1, 1, nc, R), lambda b, g, i: (g, i, 0, 0)),
            pl.BlockSpec((1, 1) + tabw.shape[2:], lambda b, g, i: (g, 0, 0, 0)),
            pl.BlockSpec((1, 1) + tabs.shape[2:], lambda b, g, i: (g, 0, 0, 0)),
            pl.BlockSpec(ovl.shape, lambda b, g, i: (0, 0)),
            pl.BlockSpec(qgain.shape, lambda b, g, i: (0, 0)),
        ],
        out_specs=pl.BlockSpec((1, Q_BLOCK, gw), lambda b, g, i: (b, i, g)),
        out_shape=jax.ShapeDtypeStruct((B, S, NSA_Q), BF16),
        scratch_shapes=[pltpu.VMEM((1, R), F32), pltpu.VMEM((LANES, R), F32),
                        pltpu.VMEM((SEL_CHUNK, R), F32), pltpu.VMEM((SEL_CHUNK, R), F32),
                        pltpu.VMEM((WIN_BLOCKS * WIN_KB, R), F32)],
        compiler_params=_cparams(("parallel", "parallel", "arbitrary")),
        name="nsa_attn",
    )(proj3, proj3, kca, vct, ksa, vst, kwa, vwt, tabc, tabw, tabs, ovl, qgain)


def _rel_bucket(d):
    max_exact = REL_BUCKETS // 2
    d = jnp.maximum(d, 0)
    df = jnp.maximum(d, 1).astype(F32)
    large = max_exact + (jnp.log(df / max_exact) / math.log(REL_MAX_DIST / max_exact)
                         * (REL_BUCKETS - max_exact)).astype(jnp.int32)
    return jnp.where(d < max_exact, d, jnp.minimum(large, REL_BUCKETS - 1))


def _bias_kernel(rb_ref, bkt_ref, o_ref, *, rows):
    g = pl.program_id(0)
    step = 32
    for r0 in range(0, rows, step):
        bkt = bkt_ref[0, r0:r0 + step, :]
        acc = [jnp.full(bkt.shape, NEG_BIG, F32) for _ in range(NSA_HPG)]
        for b in range(REL_BUCKETS):
            hit = bkt == b
            for h in range(NSA_HPG):
                acc[h] = jnp.where(hit, rb_ref[b, g * NSA_HPG + h] * LOG2E, acc[h])
        for h in range(NSA_HPG):
            o_ref[0, 0, r0:r0 + step, h * Q_BLOCK:(h + 1) * Q_BLOCK] = acc[h]


def _bias_table(rel_bias, bucket):
    nt, rows, cols = bucket.shape
    return pl.pallas_call(
        functools.partial(_bias_kernel, rows=rows),
        grid=(NSA_KV, nt),
        in_specs=[
            pl.BlockSpec(memory_space=pltpu.SMEM),
            pl.BlockSpec((1, rows, cols), lambda g, t: (t, 0, 0)),
        ],
        out_specs=pl.BlockSpec((1, 1, rows, NSA_HPG * cols), lambda g, t: (g, t, 0, 0)),
        out_shape=jax.ShapeDtypeStruct((NSA_KV, nt, rows, NSA_HPG * cols), F32),
        compiler_params=_cparams(("parallel", "parallel")),
        name="bias_table",
    )(rel_bias, bucket)


def _bias_tables(rel_bias, S, dsat, strip_len):
    nq = S // Q_BLOCK
    nc = S // CMP_STRIDE
    n_cmp = (S - CMP_BLOCK) // CMP_STRIDE + 1
    ar = lambda n: jnp.arange(n, dtype=jnp.int32)
    bucket = lambda d, valid: jnp.where(valid, _rel_bucket(d), -1)
    i = ar(Q_BLOCK)[None, None, :]
    c = ar(nc)[None, :, None]
    d_c = ar(nq)[:, None, None] * Q_BLOCK + i - (c * CMP_STRIDE + CMP_BLOCK - 1)
    tabc = _bias_table(rel_bias, bucket(d_c, (d_c >= 0) & (c < n_cmp)))
    d_w = i + WINDOW - ar(WIN_BLOCKS * WIN_KB)[None, :, None]
    tabw = _bias_table(rel_bias, bucket(d_w, (d_w >= 0) & (d_w < WINDOW)))
    d_s = i - ar(strip_len)[None, :, None] + dsat
    tabs = _bias_table(rel_bias, bucket(d_s, d_s >= 0))
    return tabc, tabw, tabs


def _overlap_matrix(S):
    nc = S // CMP_STRIDE
    nsel = S // SEL_BLOCK
    n_cmp = (S - CMP_BLOCK) // CMP_STRIDE + 1
    c = np.arange(nc)[None, :]
    sel_start = (np.arange(SEL_ROWS) * SEL_BLOCK)[:, None]
    ovl = ((c * CMP_STRIDE < sel_start + SEL_BLOCK) & (c * CMP_STRIDE + CMP_BLOCK - 1 >= sel_start)
           & (c < n_cmp) & (np.arange(SEL_ROWS)[:, None] < nsel))
    return jnp.asarray(ovl, BF16)


def _layer_params(l, lower_bounds, norm_mix, w_in, hg_out_gain, nsa_qk_gain, cmp_pos, cmp_w1, cmp_b1,
                  cmp_w2, w_branch_a, w_branch_b, w_out, norm_mlp, w_up, w_down):
    splits = (HG_K, HG_K, HG_V, HG_V, NSA_Q, NSA_KVW, NSA_KVW, NSA_KVW, NSA_KVW, NSA_KVW, NSA_KVW,
              3 * NSA_HEADS, D_MODEL, D_MODEL)
    offs = np.concatenate([[0], np.cumsum(splits)])
    col = lambda k: w_in[l][:, offs[k]:offs[k + 1]]
    gate_w = col(11)
    zpad = jnp.zeros((D_MODEL, LANES - 3 * NSA_HPG), F32)
    gate_cols = []
    for g in range(NSA_KV):
        gate_cols += [gate_w[:, g * 3 * NSA_HPG:(g + 1) * 3 * NSA_HPG], zpad]
    w_main = jnp.concatenate([col(12), col(13), col(0), col(2), col(3),
                              col(7), col(8), col(9), col(10), col(4)] + gate_cols, axis=1).astype(BF16)
    w_cmp = jnp.concatenate([col(1), col(5), col(6)], axis=1).astype(BF16)

    lb = lower_bounds[l]
    lbp = jnp.stack([jnp.log(lb), jnp.log1p(-lb), 1.0 - lb])

    w1 = cmp_w1[l].reshape(2, 2, CMP_STRIDE, NSA_DH, CMP_HID)
    w1c = jnp.zeros((2, CMP_STRIDE, NSA_KV, NSA_DH, NSA_KV, 2, CMP_HID), F32)
    for g in range(NSA_KV):
        w1c = w1c.at[:, :, g, :, g, :, :].set(w1.transpose(0, 2, 3, 1, 4))
    w1c = w1c.reshape(2, CMP_STRIDE * NSA_KVW, NSA_KV * 2 * CMP_HID).astype(BF16)
    pos8 = jnp.zeros((2, 8, CMP_BLOCK * NSA_DH), F32).at[:, 0, :].set(cmp_pos[l].reshape(2, -1))
    w2p = jnp.zeros((2, CMP_HID, LANES), F32)
    w2p = w2p.at[0, :, :NSA_DH].set(cmp_w2[l, 0]).at[1, :NSA_DH, :].set(cmp_w2[l, 1].T).astype(BF16)
    qk = nsa_qk_gain[l]
    z64 = jnp.zeros((NSA_DH,), F32)
    return dict(
        norm_mix=norm_mix[l][None, :], w_main=w_main, w_cmp=w_cmp, lbp=lbp,
        hg_gain=hg_out_gain[l][None, :],
        gsel=jnp.concatenate([qk[2], qk[2]])[None, :], gwin=jnp.concatenate([qk[3], qk[3]])[None, :],
        gcmp=jnp.concatenate([qk[1], z64])[None, :],
        qgain=jnp.broadcast_to((qk[0] * (NSA_DH ** -0.5 * LOG2E))[:, None], (NSA_DH, Q_BLOCK)),
        pos8=pos8, w1=cmp_w1[l].astype(BF16), w1c=w1c, b1=cmp_b1[l][:, None, :], w2p=w2p,
        wa=w_branch_a[l].astype(BF16), wb=w_branch_b[l].astype(BF16), wo=w_out[l].astype(BF16),
        norm_mlp=norm_mlp[l][None, :], wu=w_up[l].astype(BF16), wd=w_down[l].astype(BF16))


def kernel(x, rel_bias, hg_lb_logits, norm_mix, w_in, hg_out_gain, nsa_qk_gain, cmp_pos, cmp_w1, cmp_b1, cmp_w2, w_branch_a, w_branch_b, w_out, norm_mlp, w_up, w_down):
    B, S, D = x.shape
    T = B * S
    depth = w_in.shape[0]
    assert D == D_MODEL and S % SEL_CHUNK == 0 and S >= WINDOW and S // SEL_BLOCK <= SEL_ROWS
    assert S % Q_BLOCK == 0 and Q_BLOCK % WIN_KB == 0 and WINDOW % WIN_KB == 0
    lb_cum = jnp.cumsum(jax.nn.softmax(hg_lb_logits.astype(F32), axis=0), axis=0)
    lower_bounds = lb_cum - lb_cum[0:1]

    dsat, strip_len = _sel_table_geometry()
    tabc, tabw, tabs = _bias_tables(rel_bias, S, dsat, strip_len)
    ovl = _overlap_matrix(S)
    nc = S // CMP_STRIDE
    tm = min(1024, T)

    x2 = x.reshape(T, D)
    for l in range(depth):
        p = _layer_params(l, lower_bounds, norm_mix, w_in, hg_out_gain, nsa_qk_gain, cmp_pos, cmp_w1,
                          cmp_b1, cmp_w2, w_branch_a, w_branch_b, w_out, norm_mlp, w_up, w_down)
        proj, z, kc, vc = _rms_proj(x2, p["norm_mix"], p["w_main"], p["w_cmp"], tm, P_TILE)
        proj3 = proj.reshape(B, S, P_COLS)
        kc2 = kc.reshape(B, nc, CMP_STRIDE * NSA_KVW)
        vc2 = vc.reshape(B, nc, CMP_STRIDE * NSA_KVW)
        o_a = _hgrn(proj3, z.reshape(B, S, HG_K), p["lbp"], p["hg_gain"])
        prep = _nsa_prep(proj3, kc2, vc2, p["gsel"], p["gwin"], p["gcmp"], p["pos8"], p["w1"],
                         p["w1c"], p["b1"], p["w2p"])
        o_b = _nsa_attn(proj3, prep, tabc, tabw, tabs, ovl, p["qgain"], dsat)
        x2 = _merge(x2, o_a.reshape(T, HG_V), o_b.reshape(T, NSA_Q), proj, p["wa"], p["wb"], p["wo"],
                    min(512, T))
        x2 = _mlp(x2, p["norm_mlp"], p["wu"], p["wd"], tm, 1024)
    return x2.reshape(B, S, D)
```

```python
import functools
import math

import numpy as np
import jax
import jax.numpy as jnp
from jax import lax
from jax.experimental import pallas as pl
from jax.experimental.pallas import tpu as pltpu

F32 = jnp.float32
BF16 = jnp.bfloat16

D_MODEL = 1024
HG_HEADS = 4
HG_DK = 128
HG_DV = 128
HG_K = HG_HEADS * HG_DK
HG_V = HG_HEADS * HG_DV
NSA_HEADS = 8
NSA_KV = 2
NSA_DH = 64
NSA_HPG = NSA_HEADS // NSA_KV
NSA_Q = NSA_HEADS * NSA_DH
NSA_KVW = NSA_KV * NSA_DH
CMP_BLOCK = 32
CMP_STRIDE = 16
CMP_HID = 128
SEL_BLOCK = 64
SEL_TOPK = 8
WINDOW = 512
FORCE_BONUS = 1000.0
REL_BUCKETS = 32
REL_MAX_DIST = 1024
D_FF = 4 * D_MODEL
EPS = 1e-6
NEG_BIG = -1e30
LOG2E = math.log2(math.e)

LANES = 128
VMEM_LIMIT = 56 * 1024 * 1024

P_GAB = 0
P_HG = P_GAB + 2 * D_MODEL
P_KV = P_HG + 3 * HG_K
P_NQ = P_KV + 4 * NSA_KVW
P_GATE = P_NQ + NSA_Q
P_COLS = P_GATE + NSA_KV * LANES
P_TILE = P_COLS // 2

HG_CHUNK = 128
HG_TS = 1024
Q_BLOCK = 256
WIN_KB = 128
SEL_CHUNK = 512
WIN_BLOCKS = (WINDOW + Q_BLOCK) // WIN_KB
SEL_SHIFT = int(math.log2(SEL_BLOCK))
SEL_ROWS = LANES - NSA_DH


def _cparams(sem):
    return pltpu.CompilerParams(dimension_semantics=sem, vmem_limit_bytes=VMEM_LIMIT)


def _rms_rows(x, gain):
    ms = jnp.mean(x * x, axis=-1, keepdims=True)
    return x * lax.rsqrt(ms + EPS) * gain


def _proj_kernel(x_ref, g_ref, w_ref, wx_ref, o_ref, z_ref, kc_ref, vc_ref, h_ref):
    @pl.when(pl.program_id(1) == 0)
    def _():
        h = _rms_rows(x_ref[...], g_ref[...]).astype(BF16)
        h_ref[...] = h
        c = jnp.dot(h, wx_ref[...], preferred_element_type=F32)
        z_ref[...] = c[:, :HG_K]
        kc_ref[...] = c[:, HG_K:HG_K + NSA_KVW].astype(BF16)
        vc_ref[...] = c[:, HG_K + NSA_KVW:].astype(BF16)

    o_ref[...] = jnp.dot(h_ref[...], w_ref[...], preferred_element_type=F32).astype(BF16)


def _rms_proj(x2, gain, w, wx, tm, tn):
    T, D = x2.shape
    N = w.shape[1]
    cmp_sds = jax.ShapeDtypeStruct((T, NSA_KVW), BF16)
    return pl.pallas_call(
        _proj_kernel,
        grid=(T // tm, N // tn),
        in_specs=[
            pl.BlockSpec((tm, D), lambda i, j: (i, 0)),
            pl.BlockSpec((1, D), lambda i, j: (0, 0)),
            pl.BlockSpec((D, tn), lambda i, j: (0, j)),
            pl.BlockSpec(wx.shape, lambda i, j: (0, 0)),
        ],
        out_specs=[
            pl.BlockSpec((tm, tn), lambda i, j: (i, j)),
            pl.BlockSpec((tm, HG_K), lambda i, j: (i, 0)),
            pl.BlockSpec((tm, NSA_KVW), lambda i, j: (i, 0)),
            pl.BlockSpec((tm, NSA_KVW), lambda i, j: (i, 0)),
        ],
        out_shape=[jax.ShapeDtypeStruct((T, N), BF16), jax.ShapeDtypeStruct((T, HG_K), F32),
                   cmp_sds, cmp_sds],
        scratch_shapes=[pltpu.VMEM((tm, D), BF16)],
        compiler_params=_cparams(("parallel", "arbitrary")),
        name="rms_proj",
    )(x2, gain, w, wx)


def _mlp_kernel(x_ref, g_ref, wu_ref, wd_ref, o_ref, h_ref):
    f = pl.program_id(1)

    @pl.when(f == 0)
    def _():
        x = x_ref[...]
        h_ref[...] = _rms_rows(x, g_ref[...]).astype(BF16)
        o_ref[...] = x

    u = jnp.dot(h_ref[...], wu_ref[...], preferred_element_type=F32)
    u = jnp.square(jnp.maximum(u, 0.0)).astype(BF16)
    o_ref[...] += jnp.dot(u, wd_ref[...], preferred_element_type=F32)


def _mlp(x2, gain, wu, wd, tm, tf):
    T, D = x2.shape
    FF = wu.shape[1]
    return pl.pallas_call(
        _mlp_kernel,
        grid=(T // tm, FF // tf),
        in_specs=[
            pl.BlockSpec((tm, D), lambda i, f: (i, 0)),
            pl.BlockSpec((1, D), lambda i, f: (0, 0)),
            pl.BlockSpec((D, tf), lambda i, f: (0, f)),
            pl.BlockSpec((tf, D), lambda i, f: (f, 0)),
        ],
        out_specs=pl.BlockSpec((tm, D), lambda i, f: (i, 0)),
        out_shape=jax.ShapeDtypeStruct((T, D), F32),
        scratch_shapes=[pltpu.VMEM((tm, D), BF16)],
        compiler_params=_cparams(("parallel", "arbitrary")),
        name="mlp",
    )(x2, gain, wu, wd)


def _merge_kernel(x_ref, oa_ref, ob_ref, ga_ref, gb_ref, wa_ref, wb_ref, wo_ref, o_ref):
    ya = jnp.dot(oa_ref[...], wa_ref[...], preferred_element_type=F32)
    yb = jnp.dot(ob_ref[...], wb_ref[...], preferred_element_type=F32)
    mixed = (jax.nn.sigmoid(ga_ref[...].astype(F32)) * ya
             + jax.nn.sigmoid(gb_ref[...].astype(F32)) * yb)
    o_ref[...] = x_ref[...] + jnp.dot(mixed.astype(BF16), wo_ref[...], preferred_element_type=F32)


def _merge(x2, oa, ob, proj, wa, wb, wo, tm):
    T, D = x2.shape
    ga_blk = P_GAB // D
    return pl.pallas_call(
        _merge_kernel,
        grid=(T // tm,),
        in_specs=[
            pl.BlockSpec((tm, D), lambda i: (i, 0)),
            pl.BlockSpec((tm, HG_V), lambda i: (i, 0)),
            pl.BlockSpec((tm, NSA_Q), lambda i: (i, 0)),
            pl.BlockSpec((tm, D), lambda i: (i, ga_blk)),
            pl.BlockSpec((tm, D), lambda i: (i, ga_blk + 1)),
            pl.BlockSpec((HG_V, D), lambda i: (0, 0)),
            pl.BlockSpec((NSA_Q, D), lambda i: (0, 0)),
            pl.BlockSpec((D, D), lambda i: (0, 0)),
        ],
        out_specs=pl.BlockSpec((tm, D), lambda i: (i, 0)),
        out_shape=jax.ShapeDtypeStruct((T, D), F32),
        compiler_params=_cparams(("parallel",)),
        name="merge",
    )(x2, oa, ob, proj, proj, wa, wb, wo)


def _hgrn_consts(C):
    L = int(math.log2(C))
    idx = np.arange(C)
    mats = [(idx[None, :] <= idx[:, None]),
            (idx[None, :] > idx[:, None])]
    bmask = []
    for l in range(L):
        m = 1 << l
        r = (idx & ~(2 * m - 1)) + m
        lo = np.minimum(idx, r)[:, None]
        hi = np.maximum(idx, r)[:, None]
        mats.append((idx[None, :] > lo) & (idx[None, :] <= hi))
        blk = idx >> (l + 1)
        bmask.append(blk[:, None] == blk[None, :])
    bmask.append(idx[:, None] == idx[None, :])
    seg = np.concatenate(mats, axis=0).astype(np.float32)
    seg = np.concatenate([seg, seg], axis=1)
    return seg, np.stack(bmask).astype(np.float32), L


def _hgrn_kernel(q_ref, f_ref, i_ref, g_ref, lbp_ref, gain_ref, seg_ref, bm_ref, o_ref, st_ref,
                 k_ref, sg_ref, *, C, L, nchunk):
    @pl.when(pl.program_id(1) == 0)
    def _():
        st_ref[...] = jnp.zeros_like(st_ref)

    rowi = lax.broadcasted_iota(jnp.int32, (C, HG_DK), 0)
    nt = (((1,), (1,)), ((), ()))
    tn = (((0,), (0,)), ((), ()))

    H = range(HG_HEADS)
    css = [slice(h * HG_DK, (h + 1) * HG_DK) for h in H]

    def gates(c, k_ref, sg_ref):
        r0 = pl.multiple_of(c * C, C)
        z = f_ref[0, pl.ds(r0, C), :]
        ez = jnp.exp(-jnp.abs(z))
        y = lbp_ref[1:2, :] + (jnp.minimum(z, 0.0) - jnp.log(1.0 + ez))
        log_lb = lbp_ref[0:1, :]
        logf = (jnp.maximum(log_lb, y) + jnp.log(1.0 + jnp.exp(-jnp.abs(log_lb - y)))) * LOG2E
        k_ref[...] = lbp_ref[2:3, :] * (jnp.where(z >= 0.0, ez, 1.0) / (1.0 + ez))
        g_hi = logf.astype(BF16)
        g_lo = (logf - g_hi.astype(F32)).astype(BF16)
        sg_ref[...] = jnp.dot(seg_ref[...], jnp.concatenate([g_hi, g_lo], axis=0),
                              preferred_element_type=F32)

    def mix(c, k_ref, sg_ref):
        r0 = pl.multiple_of(c * C, C)
        qbs = [q_ref[0, pl.ds(r0, C), cs] for cs in css]
        qs = [qb.astype(F32) for qb in qbs]
        ks = [k_ref[:, cs] for cs in css]
        ss = [lax.dot_general(qbs[h], ks[h].astype(BF16), nt, preferred_element_type=F32) * bm_ref[L]
              for h in H]
        for l in range(L):
            second = (rowi & (1 << l)) != 0
            for h in H:
                e = jnp.exp2(sg_ref[(2 + l) * C:(3 + l) * C, css[h]])
                ql = jnp.where(second, qs[h] * e, 0.0).astype(BF16)
                kl = jnp.where(second, 0.0, ks[h] * e).astype(BF16)
                ss[h] = ss[h] + lax.dot_general(ql, kl, nt, preferred_element_type=F32) * bm_ref[l]
        vbs = [i_ref[0, pl.ds(r0, C), cs] for cs in css]
        bs = [sg_ref[0:C, cs] for cs in css]
        sts = [st_ref[h] for h in H]
        outs = []
        for h in H:
            qd = (qs[h] * jnp.exp2(bs[h])).astype(BF16)
            o = lax.dot_general(qd, sts[h].astype(BF16), nt, preferred_element_type=F32)
            outs.append(o + jnp.dot(ss[h].astype(BF16), vbs[h], preferred_element_type=F32))
        for h in H:
            kd = (ks[h] * jnp.exp2(sg_ref[C:2 * C, css[h]])).astype(BF16)
            st_ref[h] = (sts[h] * jnp.exp2(bs[h][C - 1:C, :])
                         + lax.dot_general(vbs[h], kd, tn, preferred_element_type=F32))
        for h in H:
            gg = g_ref[0, pl.ds(r0, C), css[h]].astype(F32)
            o = _rms_rows(outs[h], gain_ref[...]) * (gg * jax.nn.sigmoid(gg))
            o_ref[0, pl.ds(r0, C), css[h]] = o.astype(BF16)

    def chunk(c, carry):
        gates(c, k_ref, sg_ref)
        mix(c, k_ref, sg_ref)
        return carry

    lax.fori_loop(0, nchunk, chunk, 0)


def _hgrn(proj3, z3, lbp, gain, C=HG_CHUNK, ts=HG_TS):
    B, S, _ = proj3.shape
    ts = min(ts, S)
    seg, bm, L = _hgrn_consts(C)
    blk = lambda k: pl.BlockSpec((1, ts, HG_K), lambda b, s: (b, s, P_HG // HG_K + k))
    return pl.pallas_call(
        functools.partial(_hgrn_kernel, C=C, L=L, nchunk=ts // C),
        grid=(B, S // ts),
        in_specs=[
            blk(0), pl.BlockSpec((1, ts, HG_K), lambda b, s: (b, s, 0)), blk(1), blk(2),
            pl.BlockSpec((3, HG_K), lambda b, s: (0, 0)),
            pl.BlockSpec((1, HG_DV), lambda b, s: (0, 0)),
            pl.BlockSpec(seg.shape, lambda b, s: (0, 0)),
            pl.BlockSpec(bm.shape, lambda b, s: (0, 0, 0)),
        ],
        out_specs=pl.BlockSpec((1, ts, HG_V), lambda b, s: (b, s, 0)),
        out_shape=jax.ShapeDtypeStruct((B, S, HG_V), BF16),
        scratch_shapes=[pltpu.VMEM((HG_HEADS, HG_DV, HG_DK), F32), pltpu.VMEM((C, HG_K), F32),
                        pltpu.VMEM(((2 + L) * C, HG_K), F32)],
        compiler_params=_cparams(("parallel", "arbitrary")),
        name="hgrn2",
    )(proj3, z3, proj3, proj3, lbp, gain, jnp.asarray(seg, BF16), jnp.asarray(bm))


def _half_rms(x, gain2, lane):
    sq = x * x
    s_lo = jnp.sum(jnp.where(lane < NSA_DH, sq, 0.0), axis=-1, keepdims=True)
    s_hi = jnp.sum(jnp.where(lane < NSA_DH, 0.0, sq), axis=-1, keepdims=True)
    ms = jnp.where(lane < NSA_DH, s_lo, s_hi) * (1.0 / NSA_DH)
    return x * lax.rsqrt(ms + EPS) * gain2


def _gelu_tanh(x):
    return 0.5 * x * (1.0 + jnp.tanh(math.sqrt(2.0 / math.pi) * (x + 0.044715 * (x * x * x))))


def _prep_kernel(kv_ref, kc_ref, vc_ref, gsel_ref, gwin_ref, gcmp_ref, pos_ref, w1_ref, w1c_ref,
                 b1_ref, w2_ref, ksa_ref, vst_ref, kwa_ref, vwt_ref, kca_ref, vct_ref, *, S, rows):
    lane = lax.broadcasted_iota(jnp.int32, (rows, LANES), 1)
    rowi = lax.broadcasted_iota(jnp.int32, (rows, LANES), 0)
    ones_row = (lax.broadcasted_iota(jnp.int32, (NSA_DH, rows), 0) == 0).astype(F32)
    low = lane < NSA_DH

    def body(i, carry):
        r0 = pl.multiple_of(i * rows, rows)
        blk_hot = jnp.where(lane - NSA_DH == ((r0 + rowi) >> SEL_SHIFT), 1.0, 0.0)
        for src, gain_ref, aug, k_dst, v_dst, vrows in (
                (0, gsel_ref, blk_hot, ksa_ref, vst_ref, rows),
                (2, gwin_ref, jnp.zeros((rows, LANES), F32), kwa_ref, vwt_ref, WIN_KB)):
            kk = kv_ref[0, pl.ds(r0, rows), src * LANES:(src + 1) * LANES].astype(F32)
            vv = kv_ref[0, pl.ds(r0, rows), (src + 1) * LANES:(src + 2) * LANES].astype(F32)
            kn = _half_rms(kk, gain_ref[...], lane)
            k_dst[0, 0, pl.ds(r0, rows), :] = jnp.where(low, kn, aug).astype(BF16)
            k_dst[0, 1, pl.ds(r0, rows), :] = jnp.where(low, pltpu.roll(kn, NSA_DH, 1), aug).astype(BF16)
            vt = vv.T
            for g in range(NSA_KV):
                vg = jnp.concatenate([vt[g * NSA_DH:(g + 1) * NSA_DH], ones_row], axis=0).astype(BF16)
                for u in range(rows // vrows):
                    v_dst[0, g, i * (rows // vrows) + u] = vg[:, u * vrows:(u + 1) * vrows]
        return carry

    lax.fori_loop(0, S // rows, body, 0)

    nc = S // CMP_STRIDE
    outs = []
    for j, src_ref in ((0, kc_ref), (1, vc_ref)):
        bias = jnp.dot(pos_ref[j].astype(BF16), w1_ref[j], preferred_element_type=F32)[0:1] + b1_ref[j]
        pj = jnp.dot(src_ref[0], w1c_ref[j], preferred_element_type=F32)
        per_g = []
        for g in range(NSA_KV):
            first = pj[:, (2 * g) * CMP_HID:(2 * g + 1) * CMP_HID]
            second = pj[:, (2 * g + 1) * CMP_HID:(2 * g + 2) * CMP_HID]
            hid = _gelu_tanh(first + pltpu.roll(second, nc - 1, 0) + bias).astype(BF16)
            if j == 0:
                per_g.append(jnp.dot(hid, w2_ref[0], preferred_element_type=F32))
            else:
                per_g.append(lax.dot_general(w2_ref[1], hid, (((1,), (1,)), ((), ())),
                                             preferred_element_type=F32))
        outs.append(per_g)
    for g in range(NSA_KV):
        kc = outs[0][g]
        ms = jnp.sum(kc * kc, axis=-1, keepdims=True) * (1.0 / NSA_DH)
        kca_ref[0, g] = (kc * lax.rsqrt(ms + EPS) * gcmp_ref[...]).astype(BF16)
        vct_ref[0, g] = outs[1][g].astype(BF16)


def _nsa_prep(proj3, kc2, vc2, gsel, gwin, gcmp, pos8, w1, w1c, b1, w2p):
    B, S, _ = proj3.shape
    nc = S // CMP_STRIDE
    rows = SEL_CHUNK
    full = lambda shp: pl.BlockSpec(shp, lambda b: (0,) * len(shp))
    whole = lambda shp: pl.BlockSpec((1,) + shp, lambda b: (b,) + (0,) * len(shp))
    shapes = [(NSA_KV, S, LANES), (NSA_KV, S // SEL_CHUNK, LANES, SEL_CHUNK),
              (NSA_KV, S, LANES), (NSA_KV, S // WIN_KB, LANES, WIN_KB),
              (NSA_KV, nc, LANES), (NSA_KV, LANES, nc)]
    return pl.pallas_call(
        functools.partial(_prep_kernel, S=S, rows=rows),
        grid=(B,),
        in_specs=[
            pl.BlockSpec((1, S, 4 * NSA_KVW), lambda b: (b, 0, P_KV // (4 * NSA_KVW))),
            pl.BlockSpec((1, nc, CMP_STRIDE * NSA_KVW), lambda b: (b, 0, 0)),
            pl.BlockSpec((1, nc, CMP_STRIDE * NSA_KVW), lambda b: (b, 0, 0)),
            full((1, LANES)), full((1, LANES)), full((1, LANES)),
            full(pos8.shape), full(w1.shape), full(w1c.shape), full(b1.shape), full(w2p.shape),
        ],
        out_specs=[whole(s) for s in shapes],
        out_shape=[jax.ShapeDtypeStruct((B,) + s, BF16) for s in shapes],
        compiler_params=_cparams(("parallel",)),
        name="nsa_prep",
    )(proj3, kc2, vc2, gsel, gwin, gcmp, pos8, w1, w1c, b1, w2p)


def _sel_table_geometry():
    last_start = (REL_BUCKETS // 2) * (REL_MAX_DIST / (REL_BUCKETS // 2)) ** (
        (REL_BUCKETS - REL_BUCKETS // 2 - 1) / (REL_BUCKETS - REL_BUCKETS // 2))
    dsat = int(math.ceil((last_start + 16 + SEL_CHUNK) / LANES)) * LANES
    return dsat, dsat + SEL_CHUNK


def _attn_kernel(q_ref, gate_ref, kca_ref, vct_ref, ksa_ref, vst_ref, kwa_ref, vwt_ref, tabc_ref,
                 tabw_ref, tabs_ref, ovl_ref, qg_ref, o_ref, m_ref, acc_ref, sa_ref, sb_ref, sw_ref,
                 *, nsel, dsat):
    qi = pl.program_id(2)
    QB = Q_BLOCK
    R = NSA_HPG * QB
    DH = NSA_DH

    xt = q_ref[0].astype(F32).T
    cols = []
    for h in range(NSA_HPG):
        xh = xt[h * DH:(h + 1) * DH]
        ms = jnp.mean(xh * xh, axis=0, keepdims=True)
        cols.append(xh * lax.rsqrt(ms + EPS) * qg_ref[...])
    qt = jnp.concatenate(cols, axis=1).astype(BF16)
    qta = jnp.concatenate([qt, jnp.zeros_like(qt)], axis=0)

    vts = []
    for r in range(WIN_BLOCKS):
        kb = qi * (QB // WIN_KB) - WINDOW // WIN_KB + r
        kbc = jnp.maximum(kb, 0)
        kw = kwa_ref[0, 0, pl.ds(pl.multiple_of(kbc * WIN_KB, WIN_KB), WIN_KB), :]
        vts.append(vwt_ref[0, 0, kbc])
        sw_ref[r * WIN_KB:(r + 1) * WIN_KB, :] = (
            jnp.dot(kw, qta, preferred_element_type=F32)
            + tabw_ref[0, 0, r * WIN_KB:(r + 1) * WIN_KB, :] + jnp.where(kb >= 0, 0.0, NEG_BIG))

    def scores(c, q_aug):
        k0 = pl.multiple_of(c * SEL_CHUNK, SEL_CHUNK)
        delta = qi * QB - c * SEL_CHUNK
        start = pl.multiple_of(dsat - jnp.minimum(delta, dsat), LANES)
        return (jnp.dot(ksa_ref[0, 0, pl.ds(k0, SEL_CHUNK), :], q_aug, preferred_element_type=F32)
                + tabs_ref[0, 0, pl.ds(start, SEL_CHUNK), :])

    sa_ref[...] = scores(0, qta)

    s = jnp.dot(kca_ref[0, 0], qta, preferred_element_type=F32) + tabc_ref[0, 0]
    mx = jnp.max(s, axis=0, keepdims=True)
    e = jnp.exp2(s - mx)
    inv = jnp.where(mx > 0.5 * NEG_BIG, 1.0 / jnp.sum(e, axis=0, keepdims=True), 0.0)
    p = e * inv
    o_c = jnp.dot(vct_ref[0, 0], p.astype(BF16), preferred_element_type=F32)

    psum = p[:, 0:QB] + p[:, QB:2 * QB] + p[:, 2 * QB:3 * QB] + p[:, 3 * QB:4 * QB]
    p_hi = psum.astype(BF16)
    p_lo = (psum - p_hi.astype(F32)).astype(BF16)
    imp = (jnp.dot(ovl_ref[...], p_hi, preferred_element_type=F32)
           + jnp.dot(ovl_ref[...], p_lo, preferred_element_type=F32))
    t = qi * QB + lax.broadcasted_iota(jnp.int32, (SEL_ROWS, QB), 1)
    jrow = lax.broadcasted_iota(jnp.int32, (SEL_ROWS, QB), 0)
    jcur = t >> SEL_SHIFT
    forced = (jrow == 0) | (jrow == jcur) | (jrow == jcur - 1)
    work = jnp.where(jrow * SEL_BLOCK <= t, imp + jnp.where(forced, FORCE_BONUS, 0.0), -1.0)
    work = jnp.where(jrow < nsel, work, -5.0)
    jf = jrow.astype(F32)
    sel = jnp.zeros((SEL_ROWS, QB), F32)
    for _ in range(min(SEL_TOPK, nsel)):
        best = jnp.max(work, axis=0, keepdims=True)
        first = jnp.min(jnp.where(work == best, jf, float(SEL_ROWS)), axis=0, keepdims=True)
        hit = jf == first
        sel = jnp.where(hit, 1.0, sel)
        work = jnp.where(hit, -3.0, work)
    selneg = jnp.concatenate([jnp.where(sel > 0.5, 0.0, NEG_BIG)] * NSA_HPG, axis=1)
    qts = jnp.concatenate([qt, selneg.astype(BF16)], axis=0)
    for j in range(SEL_CHUNK // SEL_BLOCK):
        sa_ref[j * SEL_BLOCK:(j + 1) * SEL_BLOCK, :] += selneg[j:j + 1, :]

    sw = sw_ref[...]
    ew = jnp.exp2(sw - jnp.max(sw, axis=0, keepdims=True)).astype(BF16)
    o_w = jnp.dot(jnp.concatenate(vts, axis=1), ew, preferred_element_type=F32)

    m_ref[...] = jnp.full_like(m_ref, NEG_BIG)
    acc_ref[...] = jnp.zeros_like(acc_ref)

    def consume(s_ref, c):
        sc = s_ref[...]
        m_old = m_ref[...]
        m_new = jnp.maximum(m_old, jnp.max(sc, axis=0, keepdims=True))
        alpha = jnp.exp2(m_old - m_new)
        pe = jnp.exp2(sc - m_new).astype(BF16)
        acc_ref[...] = alpha * acc_ref[...] + jnp.dot(vst_ref[0, 0, c], pe, preferred_element_type=F32)
        m_ref[...] = m_new

    n_chunks = (qi * QB + QB - 1) // SEL_CHUNK + 1
    n_pairs = (n_chunks - 1) // 2

    def sel_pair(pr, carry):
        c = 2 * pr
        sb_ref[...] = scores(c + 1, qts)
        consume(sa_ref, c)
        sa_ref[...] = scores(c + 2, qts)
        consume(sb_ref, c + 1)
        return carry

    lax.fori_loop(0, n_pairs, sel_pair, 0)

    @pl.when(n_chunks % 2 == 0)
    def _():
        sb_ref[...] = scores(n_chunks - 1, qts)
        consume(sa_ref, n_chunks - 2)
        consume(sb_ref, n_chunks - 1)

    @pl.when(n_chunks % 2 == 1)
    def _():
        consume(sa_ref, n_chunks - 1)

    acc = acc_ref[...]

    gt = jax.nn.sigmoid(gate_ref[0].astype(F32)).T
    o_s = acc[0:DH] / acc[DH:DH + 1]
    o_w = o_w[0:DH] / o_w[DH:DH + 1]
    heads = []
    for h in range(NSA_HPG):
        cs = slice(h * QB, (h + 1) * QB)
        heads.append(gt[3 * h:3 * h + 1] * o_c[0:DH, cs] + gt[3 * h + 1:3 * h + 2] * o_s[:, cs]
                     + gt[3 * h + 2:3 * h + 3] * o_w[:, cs])
    o_ref[0] = jnp.concatenate(heads, axis=0).T.astype(BF16)


def _nsa_attn(proj3, prep, tabc, tabw, tabs, ovl, qgain, dsat):
    B, S, _ = proj3.shape
    ksa, vst, kwa, vwt, kca, vct = prep
    nq = S // Q_BLOCK
    nc = S // CMP_STRIDE
    nsel = S // SEL_BLOCK
    R = NSA_HPG * Q_BLOCK
    gw = NSA_HPG * NSA_DH
    per_bg = lambda a: pl.BlockSpec((1, 1) + a.shape[2:], lambda b, g, i: (b, g) + (0,) * (a.ndim - 2))
    return pl.pallas_call(
        functools.partial(_attn_kernel, nsel=nsel, dsat=dsat),
        grid=(B, NSA_KV, nq),
        in_specs=[
            pl.BlockSpec((1, Q_BLOCK, gw), lambda b, g, i: (b, i, P_NQ // gw + g)),
            pl.BlockSpec((1, Q_BLOCK, LANES), lambda b, g, i: (b, i, P_GATE // LANES + g)),
            per_bg(kca), per_bg(vct), per_bg(ksa), per_bg(vst), per_bg(kwa), per_bg(vwt),
            pl.BlockSpec((1, 1, nc, R), lambda b, g, i: (g, i, 0, 0)),
            pl.BlockSpec((1, 1) + tabw.shape[2:], lambda b, g, i: (g, 0, 0, 0)),
            pl.BlockSpec((1, 1) + tabs.shape[2:], lambda b, g, i: (g, 0, 0, 0)),
            pl.BlockSpec(ovl.shape, lambda b, g, i: (0, 0)),
            pl.BlockSpec(qgain.shape, lambda b, g, i: (0, 0)),
        ],
        out_specs=pl.BlockSpec((1, Q_BLOCK, gw), lambda b, g, i: (b, i, g)),
        out_shape=jax.ShapeDtypeStruct((B, S, NSA_Q), BF16),
        scratch_shapes=[pltpu.VMEM((1, R), F32), pltpu.VMEM((LANES, R), F32),
                        pltpu.VMEM((SEL_CHUNK, R), F32), pltpu.VMEM((SEL_CHUNK, R), F32),
                        pltpu.VMEM((WIN_BLOCKS * WIN_KB, R), F32)],
        compiler_params=_cparams(("parallel", "parallel", "arbitrary")),
        name="nsa_attn",
    )(proj3, proj3, kca, vct, ksa, vst, kwa, vwt, tabc, tabw, tabs, ovl, qgain)


def _rel_bucket(d):
    max_exact = REL_BUCKETS // 2
    d = jnp.maximum(d, 0)
    df = jnp.maximum(d, 1).astype(F32)
    large = max_exact + (jnp.log(df / max_exact) / math.log(REL_MAX_DIST / max_exact)
                         * (REL_BUCKETS - max_exact)).astype(jnp.int32)
    return jnp.where(d < max_exact, d, jnp.minimum(large, REL_BUCKETS - 1))


def _bias_kernel(rb_ref, bkt_ref, o_ref, *, rows):
    g = pl.program_id(0)
    step = 32
    for r0 in range(0, rows, step):
        bkt = bkt_ref[0, r0:r0 + step, :]
        acc = [jnp.full(bkt.shape, NEG_BIG, F32) for _ in range(NSA_HPG)]
        for b in range(REL_BUCKETS):
            hit = bkt == b
            for h in range(NSA_HPG):
                acc[h] = jnp.where(hit, rb_ref[b, g * NSA_HPG + h] * LOG2E, acc[h])
        for h in range(NSA_HPG):
            o_ref[0, 0, r0:r0 + step, h * Q_BLOCK:(h + 1) * Q_BLOCK] = acc[h]


def _bias_table(rel_bias, bucket):
    nt, rows, cols = bucket.shape
    return pl.pallas_call(
        functools.partial(_bias_kernel, rows=rows),
        grid=(NSA_KV, nt),
        in_specs=[
            pl.BlockSpec(memory_space=pltpu.SMEM),
            pl.BlockSpec((1, rows, cols), lambda g, t: (t, 0, 0)),
        ],
        out_specs=pl.BlockSpec((1, 1, rows, NSA_HPG * cols), lambda g, t: (g, t, 0, 0)),
        out_shape=jax.ShapeDtypeStruct((NSA_KV, nt, rows, NSA_HPG * cols), F32),
        compiler_params=_cparams(("parallel", "parallel")),
        name="bias_table",
    )(rel_bias, bucket)


def _bias_tables(rel_bias, S, dsat, strip_len):
    nq = S // Q_BLOCK
    nc = S // CMP_STRIDE
    n_cmp = (S - CMP_BLOCK) // CMP_STRIDE + 1
    ar = lambda n: jnp.arange(n, dtype=jnp.int32)
    bucket = lambda d, valid: jnp.where(valid, _rel_bucket(d), -1)
    i = ar(Q_BLOCK)[None, None, :]
    c = ar(nc)[None, :, None]
    d_c = ar(nq)[:, None, None] * Q_BLOCK + i - (c * CMP_STRIDE + CMP_BLOCK - 1)
    tabc = _bias_table(rel_bias, bucket(d_c, (d_c >= 0) & (c < n_cmp)))
    d_w = i + WINDOW - ar(WIN_BLOCKS * WIN_KB)[None, :, None]
    tabw = _bias_table(rel_bias, bucket(d_w, (d_w >= 0) & (d_w < WINDOW)))
    d_s = i - ar(strip_len)[None, :, None] + dsat
    tabs = _bias_table(rel_bias, bucket(d_s, d_s >= 0))
    return tabc, tabw, tabs


def _overlap_matrix(S):
    nc = S // CMP_STRIDE
    nsel = S // SEL_BLOCK
    n_cmp = (S - CMP_BLOCK) // CMP_STRIDE + 1
    c = np.arange(nc)[None, :]
    sel_start = (np.arange(SEL_ROWS) * SEL_BLOCK)[:, None]
    ovl = ((c * CMP_STRIDE < sel_start + SEL_BLOCK) & (c * CMP_STRIDE + CMP_BLOCK - 1 >= sel_start)
           & (c < n_cmp) & (np.arange(SEL_ROWS)[:, None] < nsel))
    return jnp.asarray(ovl, BF16)


def _layer_params(l, lower_bounds, norm_mix, w_in, hg_out_gain, nsa_qk_gain, cmp_pos, cmp_w1, cmp_b1,
                  cmp_w2, w_branch_a, w_branch_b, w_out, norm_mlp, w_up, w_down):
    splits = (HG_K, HG_K, HG_V, HG_V, NSA_Q, NSA_KVW, NSA_KVW, NSA_KVW, NSA_KVW, NSA_KVW, NSA_KVW,
              3 * NSA_HEADS, D_MODEL, D_MODEL)
    offs = np.concatenate([[0], np.cumsum(splits)])
    col = lambda k: w_in[l][:, offs[k]:offs[k + 1]]
    gate_w = col(11)
    zpad = jnp.zeros((D_MODEL, LANES - 3 * NSA_HPG), F32)
    gate_cols = []
    for g in range(NSA_KV):
        gate_cols += [gate_w[:, g * 3 * NSA_HPG:(g + 1) * 3 * NSA_HPG], zpad]
    w_main = jnp.concatenate([col(12), col(13), col(0), col(2), col(3),
                              col(7), col(8), col(9), col(10), col(4)] + gate_cols, axis=1).astype(BF16)
    w_cmp = jnp.concatenate([col(1), col(5), col(6)], axis=1).astype(BF16)

    lb = lower_bounds[l]
    lbp = jnp.stack([jnp.log(lb), jnp.log1p(-lb), 1.0 - lb])

    w1 = cmp_w1[l].reshape(2, 2, CMP_STRIDE, NSA_DH, CMP_HID)
    w1c = jnp.zeros((2, CMP_STRIDE, NSA_KV, NSA_DH, NSA_KV, 2, CMP_HID), F32)
    for g in range(NSA_KV):
        w1c = w1c.at[:, :, g, :, g, :, :].set(w1.transpose(0, 2, 3, 1, 4))
    w1c = w1c.reshape(2, CMP_STRIDE * NSA_KVW, NSA_KV * 2 * CMP_HID).astype(BF16)
    pos8 = jnp.zeros((2, 8, CMP_BLOCK * NSA_DH), F32).at[:, 0, :].set(cmp_pos[l].reshape(2, -1))
    w2p = jnp.zeros((2, CMP_HID, LANES), F32)
    w2p = w2p.at[0, :, :NSA_DH].set(cmp_w2[l, 0]).at[1, :NSA_DH, :].set(cmp_w2[l, 1].T).astype(BF16)
    qk = nsa_qk_gain[l]
    z64 = jnp.zeros((NSA_DH,), F32)
    return dict(
        norm_mix=norm_mix[l][None, :], w_main=w_main, w_cmp=w_cmp, lbp=lbp,
        hg_gain=hg_out_gain[l][None, :],
        gsel=jnp.concatenate([qk[2], qk[2]])[None, :], gwin=jnp.concatenate([qk[3], qk[3]])[None, :],
        gcmp=jnp.concatenate([qk[1], z64])[None, :],
        qgain=jnp.broadcast_to((qk[0] * (NSA_DH ** -0.5 * LOG2E))[:, None], (NSA_DH, Q_BLOCK)),
        pos8=pos8, w1=cmp_w1[l].astype(BF16), w1c=w1c, b1=cmp_b1[l][:, None, :], w2p=w2p,
        wa=w_branch_a[l].astype(BF16), wb=w_branch_b[l].astype(BF16), wo=w_out[l].astype(BF16),
        norm_mlp=norm_mlp[l][None, :], wu=w_up[l].astype(BF16), wd=w_down[l].astype(BF16))


def kernel(x, rel_bias, hg_lb_logits, norm_mix, w_in, hg_out_gain, nsa_qk_gain, cmp_pos, cmp_w1, cmp_b1, cmp_w2, w_branch_a, w_branch_b, w_out, norm_mlp, w_up, w_down):
    B, S, D = x.shape
    T = B * S
    depth = w_in.shape[0]
    assert D == D_MODEL and S % SEL_CHUNK == 0 and S >= WINDOW and S // SEL_BLOCK <= SEL_ROWS
    assert S % Q_BLOCK == 0 and Q_BLOCK % WIN_KB == 0 and WINDOW % WIN_KB == 0
    lb_cum = jnp.cumsum(jax.nn.softmax(hg_lb_logits.astype(F32), axis=0), axis=0)
    lower_bounds = lb_cum - lb_cum[0:1]

    dsat, strip_len = _sel_table_geometry()
    tabc, tabw, tabs = _bias_tables(rel_bias, S, dsat, strip_len)
    ovl = _overlap_matrix(S)
    nc = S // CMP_STRIDE
    tm = min(1024, T)

    x2 = x.reshape(T, D)
    for l in range(depth):
        p = _layer_params(l, lower_bounds, norm_mix, w_in, hg_out_gain, nsa_qk_gain, cmp_pos, cmp_w1,
                          cmp_b1, cmp_w2, w_branch_a, w_branch_b, w_out, norm_mlp, w_up, w_down)
        proj, z, kc, vc = _rms_proj(x2, p["norm_mix"], p["w_main"], p["w_cmp"], tm, P_TILE)
        proj3 = proj.reshape(B, S, P_COLS)
        kc2 = kc.reshape(B, nc, CMP_STRIDE * NSA_KVW)
        vc2 = vc.reshape(B, nc, CMP_STRIDE * NSA_KVW)
        o_a = _hgrn(proj3, z.reshape(B, S, HG_K), p["lbp"], p["hg_gain"])
        prep = _nsa_prep(proj3, kc2, vc2, p["gsel"], p["gwin"], p["gcmp"], p["pos8"], p["w1"],
                         p["w1c"], p["b1"], p["w2p"])
        o_b = _nsa_attn(proj3, prep, tabc, tabw, tabs, ovl, p["qgain"], dsat)
        x2 = _merge(x2, o_a.reshape(T, HG_V), o_b.reshape(T, NSA_Q), proj, p["wa"], p["wb"], p["wo"],
                    min(512, T))
        x2 = _mlp(x2, p["norm_mlp"], p["wu"], p["wd"], tm, 1024)
    return x2.reshape(B, S, D)
```

```python
import functools
import math

import numpy as np
import jax
import jax.numpy as jnp
from jax import lax
from jax.experimental import pallas as pl
from jax.experimental.pallas import tpu as pltpu

F32 = jnp.float32
BF16 = jnp.bfloat16

D_MODEL = 1024
HG_HEADS = 4
HG_DK = 128
HG_DV = 128
HG_K = HG_HEADS * HG_DK
HG_V = HG_HEADS * HG_DV
NSA_HEADS = 8
NSA_KV = 2
NSA_DH = 64
NSA_HPG = NSA_HEADS // NSA_KV
NSA_Q = NSA_HEADS * NSA_DH
NSA_KVW = NSA_KV * NSA_DH
CMP_BLOCK = 32
CMP_STRIDE = 16
CMP_HID = 128
SEL_BLOCK = 64
SEL_TOPK = 8
WINDOW = 512
FORCE_BONUS = 1000.0
REL_BUCKETS = 32
REL_MAX_DIST = 1024
D_FF = 4 * D_MODEL
EPS = 1e-6
NEG_BIG = -1e30
LOG2E = math.log2(math.e)

LANES = 128
VMEM_LIMIT = 56 * 1024 * 1024

P_GAB = 0
P_HG = P_GAB + 2 * D_MODEL
P_KV = P_HG + 3 * HG_K
P_NQ = P_KV + 4 * NSA_KVW
P_GATE = P_NQ + NSA_Q
P_COLS = P_GATE + NSA_KV * LANES
P_TILE = P_COLS // 2

HG_CHUNK = 128
HG_TS = 1024
Q_BLOCK = 256
WIN_KB = 128
SEL_CHUNK = 512
WIN_BLOCKS = (WINDOW + Q_BLOCK) // WIN_KB
SEL_SHIFT = int(math.log2(SEL_BLOCK))
SEL_ROWS = LANES - NSA_DH


def _cparams(sem):
    return pltpu.CompilerParams(dimension_semantics=sem, vmem_limit_bytes=VMEM_LIMIT)


def _rms_rows(x, gain):
    ms = jnp.mean(x * x, axis=-1, keepdims=True)
    return x * lax.rsqrt(ms + EPS) * gain


def _proj_kernel(x_ref, g_ref, w_ref, wx_ref, o_ref, z_ref, kc_ref, vc_ref, h_ref):
    @pl.when(pl.program_id(1) == 0)
    def _():
        h = _rms_rows(x_ref[...], g_ref[...]).astype(BF16)
        h_ref[...] = h
        c = jnp.dot(h, wx_ref[...], preferred_element_type=F32)
        z_ref[...] = c[:, :HG_K]
        kc_ref[...] = c[:, HG_K:HG_K + NSA_KVW].astype(BF16)
        vc_ref[...] = c[:, HG_K + NSA_KVW:].astype(BF16)

    o_ref[...] = jnp.dot(h_ref[...], w_ref[...], preferred_element_type=F32).astype(BF16)


def _rms_proj(x2, gain, w, wx, tm, tn):
    T, D = x2.shape
    N = w.shape[1]
    cmp_sds = jax.ShapeDtypeStruct((T, NSA_KVW), BF16)
    return pl.pallas_call(
        _proj_kernel,
        grid=(T // tm, N // tn),
        in_specs=[
            pl.BlockSpec((tm, D), lambda i, j: (i, 0)),
            pl.BlockSpec((1, D), lambda i, j: (0, 0)),
            pl.BlockSpec((D, tn), lambda i, j: (0, j)),
            pl.BlockSpec(wx.shape, lambda i, j: (0, 0)),
        ],
        out_specs=[
            pl.BlockSpec((tm, tn), lambda i, j: (i, j)),
            pl.BlockSpec((tm, HG_K), lambda i, j: (i, 0)),
            pl.BlockSpec((tm, NSA_KVW), lambda i, j: (i, 0)),
            pl.BlockSpec((tm, NSA_KVW), lambda i, j: (i, 0)),
        ],
        out_shape=[jax.ShapeDtypeStruct((T, N), BF16), jax.ShapeDtypeStruct((T, HG_K), F32),
                   cmp_sds, cmp_sds],
        scratch_shapes=[pltpu.VMEM((tm, D), BF16)],
        compiler_params=_cparams(("parallel", "arbitrary")),
        name="rms_proj",
    )(x2, gain, w, wx)


def _mlp_kernel(x_ref, g_ref, wu_ref, wd_ref, o_ref, h_ref):
    f = pl.program_id(1)

    @pl.when(f == 0)
    def _():
        x = x_ref[...]
        h_ref[...] = _rms_rows(x, g_ref[...]).astype(BF16)
        o_ref[...] = x

    u = jnp.dot(h_ref[...], wu_ref[...], preferred_element_type=F32)
    u = jnp.square(jnp.maximum(u, 0.0)).astype(BF16)
    o_ref[...] += jnp.dot(u, wd_ref[...], preferred_element_type=F32)


def _mlp(x2, gain, wu, wd, tm, tf):
    T, D = x2.shape
    FF = wu.shape[1]
    return pl.pallas_call(
        _mlp_kernel,
        grid=(T // tm, FF // tf),
        in_specs=[
            pl.BlockSpec((tm, D), lambda i, f: (i, 0)),
            pl.BlockSpec((1, D), lambda i, f: (0, 0)),
            pl.BlockSpec((D, tf), lambda i, f: (0, f)),
            pl.BlockSpec((tf, D), lambda i, f: (f, 0)),
        ],
        out_specs=pl.BlockSpec((tm, D), lambda i, f: (i, 0)),
        out_shape=jax.ShapeDtypeStruct((T, D), F32),
        scratch_shapes=[pltpu.VMEM((tm, D), BF16)],
        compiler_params=_cparams(("parallel", "arbitrary")),
        name="mlp",
    )(x2, gain, wu, wd)


def _merge_kernel(x_ref, oa_ref, ob_ref, ga_ref, gb_ref, wa_ref, wb_ref, wo_ref, o_ref):
    ya = jnp.dot(oa_ref[...], wa_ref[...], preferred_element_type=F32)
    yb = jnp.dot(ob_ref[...], wb_ref[...], preferred_element_type=F32)
    mixed = (jax.nn.sigmoid(ga_ref[...].astype(F32)) * ya
             + jax.nn.sigmoid(gb_ref[...].astype(F32)) * yb)
    o_ref[...] = x_ref[...] + jnp.dot(mixed.astype(BF16), wo_ref[...], preferred_element_type=F32)


def _merge(x2, oa, ob, proj, wa, wb, wo, tm):
    T, D = x2.shape
    ga_blk = P_GAB // D
    return pl.pallas_call(
        _merge_kernel,
        grid=(T // tm,),
        in_specs=[
            pl.BlockSpec((tm, D), lambda i: (i, 0)),
            pl.BlockSpec((tm, HG_V), lambda i: (i, 0)),
            pl.BlockSpec((tm, NSA_Q), lambda i: (i, 0)),
            pl.BlockSpec((tm, D), lambda i: (i, ga_blk)),
            pl.BlockSpec((tm, D), lambda i: (i, ga_blk + 1)),
            pl.BlockSpec((HG_V, D), lambda i: (0, 0)),
            pl.BlockSpec((NSA_Q, D), lambda i: (0, 0)),
            pl.BlockSpec((D, D), lambda i: (0, 0)),
        ],
        out_specs=pl.BlockSpec((tm, D), lambda i: (i, 0)),
        out_shape=jax.ShapeDtypeStruct((T, D), F32),
        compiler_params=_cparams(("parallel",)),
        name="merge",
    )(x2, oa, ob, proj, proj, wa, wb, wo)


def _hgrn_consts(C):
    L = int(math.log2(C))
    idx = np.arange(C)
    mats = [(idx[None, :] <= idx[:, None]),
            (idx[None, :] > idx[:, None])]
    bmask = []
    for l in range(L):
        m = 1 << l
        r = (idx & ~(2 * m - 1)) + m
        lo = np.minimum(idx, r)[:, None]
        hi = np.maximum(idx, r)[:, None]
        mats.append((idx[None, :] > lo) & (idx[None, :] <= hi))
        blk = idx >> (l + 1)
        bmask.append(blk[:, None] == blk[None, :])
    bmask.append(idx[:, None] == idx[None, :])
    seg = np.concatenate(mats, axis=0).astype(np.float32)
    seg = np.concatenate([seg, seg], axis=1)
    return seg, np.stack(bmask).astype(np.float32), L


def _hgrn_kernel(q_ref, f_ref, i_ref, g_ref, lbp_ref, gain_ref, seg_ref, bm_ref, o_ref, st_ref,
                 k_ref, sg_ref, *, C, L, nchunk):
    @pl.when(pl.program_id(1) == 0)
    def _():
        st_ref[...] = jnp.zeros_like(st_ref)

    rowi = lax.broadcasted_iota(jnp.int32, (C, HG_DK), 0)
    nt = (((1,), (1,)), ((), ()))
    tn = (((0,), (0,)), ((), ()))

    H = range(HG_HEADS)
    css = [slice(h * HG_DK, (h + 1) * HG_DK) for h in H]

    def gates(c, k_ref, sg_ref):
        r0 = pl.multiple_of(c * C, C)
        z = f_ref[0, pl.ds(r0, C), :]
        ez = jnp.exp(-jnp.abs(z))
        y = lbp_ref[1:2, :] + (jnp.minimum(z, 0.0) - jnp.log(1.0 + ez))
        log_lb = lbp_ref[0:1, :]
        logf = (jnp.maximum(log_lb, y) + jnp.log(1.0 + jnp.exp(-jnp.abs(log_lb - y)))) * LOG2E
        k_ref[...] = lbp_ref[2:3, :] * (jnp.where(z >= 0.0, ez, 1.0) / (1.0 + ez))
        g_hi = logf.astype(BF16)
        g_lo = (logf - g_hi.astype(F32)).astype(BF16)
        sg_ref[...] = jnp.dot(seg_ref[...], jnp.concatenate([g_hi, g_lo], axis=0),
                              preferred_element_type=F32)

    def mix(c, k_ref, sg_ref):
        r0 = pl.multiple_of(c * C, C)
        qbs = [q_ref[0, pl.ds(r0, C), cs] for cs in css]
        qs = [qb.astype(F32) for qb in qbs]
        ks = [k_ref[:, cs] for cs in css]
        ss = [lax.dot_general(qbs[h], ks[h].astype(BF16), nt, preferred_element_type=F32) * bm_ref[L]
              for h in H]
        for l in range(L):
            second = (rowi & (1 << l)) != 0
            for h in H:
                e = jnp.exp2(sg_ref[(2 + l) * C:(3 + l) * C, css[h]])
                ql = jnp.where(second, qs[h] * e, 0.0).astype(BF16)
                kl = jnp.where(second, 0.0, ks[h] * e).astype(BF16)
                ss[h] = ss[h] + lax.dot_general(ql, kl, nt, preferred_element_type=F32) * bm_ref[l]
        vbs = [i_ref[0, pl.ds(r0, C), cs] for cs in css]
        bs = [sg_ref[0:C, cs] for cs in css]
        sts = [st_ref[h] for h in H]
        outs = []
        for h in H:
            qd = (qs[h] * jnp.exp2(bs[h])).astype(BF16)
            o = lax.dot_general(qd, sts[h].astype(BF16), nt, preferred_element_type=F32)
            outs.append(o + jnp.dot(ss[h].astype(BF16), vbs[h], preferred_element_type=F32))
        for h in H:
            kd = (ks[h] * jnp.exp2(sg_ref[C:2 * C, css[h]])).astype(BF16)
            st_ref[h] = (sts[h] * jnp.exp2(bs[h][C - 1:C, :])
                         + lax.dot_general(vbs[h], kd, tn, preferred_element_type=F32))
        for h in H:
            gg = g_ref[0, pl.ds(r0, C), css[h]].astype(F32)
            o = _rms_rows(outs[h], gain_ref[...]) * (gg * jax.nn.sigmoid(gg))
            o_ref[0, pl.ds(r0, C), css[h]] = o.astype(BF16)

    def chunk(c, carry):
        gates(c, k_ref, sg_ref)
        mix(c, k_ref, sg_ref)
        return carry

    lax.fori_loop(0, nchunk, chunk, 0)


def _hgrn(proj3, z3, lbp, gain, C=HG_CHUNK, ts=HG_TS):
    B, S, _ = proj3.shape
    ts = min(ts, S)
    seg, bm, L = _hgrn_consts(C)
    blk = lambda k: pl.BlockSpec((1, ts, HG_K), lambda b, s: (b, s, P_HG // HG_K + k))
    return pl.pallas_call(
        functools.partial(_hgrn_kernel, C=C, L=L, nchunk=ts // C),
        grid=(B, S // ts),
        in_specs=[
            blk(0), pl.BlockSpec((1, ts, HG_K), lambda b, s: (b, s, 0)), blk(1), blk(2),
            pl.BlockSpec((3, HG_K), lambda b, s: (0, 0)),
            pl.BlockSpec((1, HG_DV), lambda b, s: (0, 0)),
            pl.BlockSpec(seg.shape, lambda b, s: (0, 0)),
            pl.BlockSpec(bm.shape, lambda b, s: (0, 0, 0)),
        ],
        out_specs=pl.BlockSpec((1, ts, HG_V), lambda b, s: (b, s, 0)),
        out_shape=jax.ShapeDtypeStruct((B, S, HG_V), BF16),
        scratch_shapes=[pltpu.VMEM((HG_HEADS, HG_DV, HG_DK), F32), pltpu.VMEM((C, HG_K), F32),
                        pltpu.VMEM(((2 + L) * C, HG_K), F32)],
        compiler_params=_cparams(("parallel", "arbitrary")),
        name="hgrn2",
    )(proj3, z3, proj3, proj3, lbp, gain, jnp.asarray(seg, BF16), jnp.asarray(bm))


def _half_rms(x, gain2, lane):
    sq = x * x
    s_lo = jnp.sum(jnp.where(lane < NSA_DH, sq, 0.0), axis=-1, keepdims=True)
    s_hi = jnp.sum(jnp.where(lane < NSA_DH, 0.0, sq), axis=-1, keepdims=True)
    ms = jnp.where(lane < NSA_DH, s_lo, s_hi) * (1.0 / NSA_DH)
    return x * lax.rsqrt(ms + EPS) * gain2


def _gelu_tanh(x):
    return 0.5 * x * (1.0 + jnp.tanh(math.sqrt(2.0 / math.pi) * (x + 0.044715 * (x * x * x))))


def _prep_kernel(kv_ref, kc_ref, vc_ref, gsel_ref, gwin_ref, gcmp_ref, pos_ref, w1_ref, w1c_ref,
                 b1_ref, w2_ref, ksa_ref, vst_ref, kwa_ref, vwt_ref, kca_ref, vct_ref, *, S, rows):
    lane = lax.broadcasted_iota(jnp.int32, (rows, LANES), 1)
    rowi = lax.broadcasted_iota(jnp.int32, (rows, LANES), 0)
    ones_row = (lax.broadcasted_iota(jnp.int32, (NSA_DH, rows), 0) == 0).astype(F32)
    low = lane < NSA_DH

    def body(i, carry):
        r0 = pl.multiple_of(i * rows, rows)
        blk_hot = jnp.where(lane - NSA_DH == ((r0 + rowi) >> SEL_SHIFT), 1.0, 0.0)
        for src, gain_ref, aug, k_dst, v_dst, vrows in (
                (0, gsel_ref, blk_hot, ksa_ref, vst_ref, rows),
                (2, gwin_ref, jnp.zeros((rows, LANES), F32), kwa_ref, vwt_ref, WIN_KB)):
            kk = kv_ref[0, pl.ds(r0, rows), src * LANES:(src + 1) * LANES].astype(F32)
            vv = kv_ref[0, pl.ds(r0, rows), (src + 1) * LANES:(src + 2) * LANES].astype(F32)
            kn = _half_rms(kk, gain_ref[...], lane)
            k_dst[0, 0, pl.ds(r0, rows), :] = jnp.where(low, kn, aug).astype(BF16)
            k_dst[0, 1, pl.ds(r0, rows), :] = jnp.where(low, pltpu.roll(kn, NSA_DH, 1), aug).astype(BF16)
            vt = vv.T
            for g in range(NSA_KV):
                vg = jnp.concatenate([vt[g * NSA_DH:(g + 1) * NSA_DH], ones_row], axis=0).astype(BF16)
                for u in range(rows // vrows):
                    v_dst[0, g, i * (rows // vrows) + u] = vg[:, u * vrows:(u + 1) * vrows]
        return carry

    lax.fori_loop(0, S // rows, body, 0)

    nc = S // CMP_STRIDE
    outs = []
    for j, src_ref in ((0, kc_ref), (1, vc_ref)):
        bias = jnp.dot(pos_ref[j].astype(BF16), w1_ref[j], preferred_element_type=F32)[0:1] + b1_ref[j]
        pj = jnp.dot(src_ref[0], w1c_ref[j], preferred_element_type=F32)
        per_g = []
        for g in range(NSA_KV):
            first = pj[:, (2 * g) * CMP_HID:(2 * g + 1) * CMP_HID]
            second = pj[:, (2 * g + 1) * CMP_HID:(2 * g + 2) * CMP_HID]
            hid = _gelu_tanh(first + pltpu.roll(second, nc - 1, 0) + bias).astype(BF16)
            if j == 0:
                per_g.append(jnp.dot(hid, w2_ref[0], preferred_element_type=F32))
            else:
                per_g.append(lax.dot_general(w2_ref[1], hid, (((1,), (1,)), ((), ())),
                                             preferred_element_type=F32))
        outs.append(per_g)
    for g in range(NSA_KV):
        kc = outs[0][g]
        ms = jnp.sum(kc * kc, axis=-1, keepdims=True) * (1.0 / NSA_DH)
        kca_ref[0, g] = (kc * lax.rsqrt(ms + EPS) * gcmp_ref[...]).astype(BF16)
        vct_ref[0, g] = outs[1][g].astype(BF16)


def _nsa_prep(proj3, kc2, vc2, gsel, gwin, gcmp, pos8, w1, w1c, b1, w2p):
    B, S, _ = proj3.shape
    nc = S // CMP_STRIDE
    rows = SEL_CHUNK
    full = lambda shp: pl.BlockSpec(shp, lambda b: (0,) * len(shp))
    whole = lambda shp: pl.BlockSpec((1,) + shp, lambda b: (b,) + (0,) * len(shp))
    shapes = [(NSA_KV, S, LANES), (NSA_KV, S // SEL_CHUNK, LANES, SEL_CHUNK),
              (NSA_KV, S, LANES), (NSA_KV, S // WIN_KB, LANES, WIN_KB),
              (NSA_KV, nc, LANES), (NSA_KV, LANES, nc)]
    return pl.pallas_call(
        functools.partial(_prep_kernel, S=S, rows=rows),
        grid=(B,),
        in_specs=[
            pl.BlockSpec((1, S, 4 * NSA_KVW), lambda b: (b, 0, P_KV // (4 * NSA_KVW))),
            pl.BlockSpec((1, nc, CMP_STRIDE * NSA_KVW), lambda b: (b, 0, 0)),
            pl.BlockSpec((1, nc, CMP_STRIDE * NSA_KVW), lambda b: (b, 0, 0)),
            full((1, LANES)), full((1, LANES)), full((1, LANES)),
            full(pos8.shape), full(w1.shape), full(w1c.shape), full(b1.shape), full(w2p.shape),
        ],
        out_specs=[whole(s) for s in shapes],
        out_shape=[jax.ShapeDtypeStruct((B,) + s, BF16) for s in shapes],
        compiler_params=_cparams(("parallel",)),
        name="nsa_prep",
    )(proj3, kc2, vc2, gsel, gwin, gcmp, pos8, w1, w1c, b1, w2p)


def _sel_table_geometry():
    last_start = (REL_BUCKETS // 2) * (REL_MAX_DIST / (REL_BUCKETS // 2)) ** (
        (REL_BUCKETS - REL_BUCKETS // 2 - 1) / (REL_BUCKETS - REL_BUCKETS // 2))
    dsat = int(math.ceil((last_start + 16 + SEL_CHUNK) / LANES)) * LANES
    return dsat, dsat + SEL_CHUNK


def _attn_kernel(q_ref, gate_ref, kca_ref, vct_ref, ksa_ref, vst_ref, kwa_ref, vwt_ref, tabc_ref,
                 tabw_ref, tabs_ref, ovl_ref, qg_ref, o_ref, m_ref, acc_ref, sa_ref, sb_ref, sw_ref,
                 *, nsel, dsat):
    qi = pl.program_id(2)
    QB = Q_BLOCK
    R = NSA_HPG * QB
    DH = NSA_DH

    xt = q_ref[0].astype(F32).T
    cols = []
    for h in range(NSA_HPG):
        xh = xt[h * DH:(h + 1) * DH]
        ms = jnp.mean(xh * xh, axis=0, keepdims=True)
        cols.append(xh * lax.rsqrt(ms + EPS) * qg_ref[...])
    qt = jnp.concatenate(cols, axis=1).astype(BF16)
    qta = jnp.concatenate([qt, jnp.zeros_like(qt)], axis=0)

    vts, slabs = [], []
    for r in range(WIN_BLOCKS):
        kb = qi * (QB // WIN_KB) - WINDOW // WIN_KB + r
        kbc = jnp.maximum(kb, 0)
        kw = kwa_ref[0, 0, pl.ds(pl.multiple_of(kbc * WIN_KB, WIN_KB), WIN_KB), :]
        vts.append(vwt_ref[0, 0, kbc])
        slabs.append(pl.multiple_of(jnp.where(kb >= 0, r, WIN_BLOCKS) * WIN_KB, WIN_KB))
        sw_ref[r * WIN_KB:(r + 1) * WIN_KB, :] = jnp.dot(kw, qta, preferred_element_type=F32)

    def scores(c, q_aug):
        k0 = pl.multiple_of(c * SEL_CHUNK, SEL_CHUNK)
        delta = qi * QB - c * SEL_CHUNK
        start = pl.multiple_of(dsat - jnp.minimum(delta, dsat), LANES)
        return (jnp.dot(ksa_ref[0, 0, pl.ds(k0, SEL_CHUNK), :], q_aug, preferred_element_type=F32)
                + tabs_ref[0, 0, pl.ds(start, SEL_CHUNK), :])

    s = jnp.dot(kca_ref[0, 0], qta, preferred_element_type=F32) + tabc_ref[0, 0]
    mx = jnp.max(s, axis=0, keepdims=True)
    e = jnp.exp2(s - mx)
    inv = jnp.where(mx > 0.5 * NEG_BIG, 1.0 / jnp.sum(e, axis=0, keepdims=True), 0.0)
    p = e * inv
    o_c = jnp.dot(vct_ref[0, 0], p.astype(BF16), preferred_element_type=F32)

    psum = p[:, 0:QB] + p[:, QB:2 * QB] + p[:, 2 * QB:3 * QB] + p[:, 3 * QB:4 * QB]
    p_hi = psum.astype(BF16)
    p_lo = (psum - p_hi.astype(F32)).astype(BF16)
    imp = (jnp.dot(ovl_ref[...], p_hi, preferred_element_type=F32)
           + jnp.dot(ovl_ref[...], p_lo, preferred_element_type=F32))
    t = qi * QB + lax.broadcasted_iota(jnp.int32, (SEL_ROWS, QB), 1)
    jrow = lax.broadcasted_iota(jnp.int32, (SEL_ROWS, QB), 0)
    jcur = t >> SEL_SHIFT
    n_top = min(SEL_TOPK, nsel)
    forced = (jrow == 0) | (jrow == jcur) | (jrow == jcur - 1)
    work = jnp.where(forced, -1.0, jnp.where(jrow * SEL_BLOCK <= t, imp, -1.0))
    work = jnp.where(jrow < nsel, work, -5.0)
    jf = jrow.astype(F32)
    sel = jnp.where(forced, 1.0, 0.0)
    for _ in range(n_top - 3):
        best = jnp.max(work, axis=0, keepdims=True)
        first = jnp.min(jnp.where(work == best, jf, float(SEL_ROWS)), axis=0, keepdims=True)
        hit = jf == first
        sel = jnp.where(hit, 1.0, sel)
        work = jnp.where(hit, -3.0, work)
    sel = jnp.where(jcur < n_top, jnp.where(jrow < n_top, 1.0, 0.0), sel)
    selneg = jnp.concatenate([jnp.where(sel > 0.5, 0.0, NEG_BIG)] * NSA_HPG, axis=1)
    qts = jnp.concatenate([qt, selneg.astype(BF16)], axis=0)
    sa_ref[...] = scores(0, qts)

    sw = jnp.concatenate([sw_ref[r * WIN_KB:(r + 1) * WIN_KB, :] + tabw_ref[0, 0, pl.ds(slabs[r], WIN_KB), :]
                          for r in range(WIN_BLOCKS)], axis=0)
    ew = jnp.exp2(sw - jnp.max(sw, axis=0, keepdims=True)).astype(BF16)
    o_w = jnp.dot(jnp.concatenate(vts, axis=1), ew, preferred_element_type=F32)

    m_ref[...] = jnp.full_like(m_ref, NEG_BIG)
    acc_ref[...] = jnp.zeros_like(acc_ref)

    def consume(s_ref, c):
        sc = s_ref[...]
        m_old = m_ref[...]
        m_new = jnp.maximum(m_old, jnp.max(sc, axis=0, keepdims=True))
        alpha = jnp.exp2(m_old - m_new)
        pe = jnp.exp2(sc - m_new).astype(BF16)
        acc_ref[...] = alpha * acc_ref[...] + jnp.dot(vst_ref[0, 0, c], pe, preferred_element_type=F32)
        m_ref[...] = m_new

    n_chunks = (qi * QB + QB - 1) // SEL_CHUNK + 1
    n_pairs = (n_chunks - 1) // 2

    def sel_pair(pr, carry):
        c = 2 * pr
        sb_ref[...] = scores(c + 1, qts)
        consume(sa_ref, c)
        sa_ref[...] = scores(c + 2, qts)
        consume(sb_ref, c + 1)
        return carry

    lax.fori_loop(0, n_pairs, sel_pair, 0)

    @pl.when(n_chunks % 2 == 0)
    def _():
        sb_ref[...] = scores(n_chunks - 1, qts)
        consume(sa_ref, n_chunks - 2)
        consume(sb_ref, n_chunks - 1)

    @pl.when(n_chunks % 2 == 1)
    def _():
        consume(sa_ref, n_chunks - 1)

    acc = acc_ref[...]

    gt = jax.nn.sigmoid(gate_ref[0].astype(F32)).T
    o_s = acc[0:DH] / acc[DH:DH + 1]
    o_w = o_w[0:DH] / o_w[DH:DH + 1]
    heads = []
    for h in range(NSA_HPG):
        cs = slice(h * QB, (h + 1) * QB)
        heads.append(gt[3 * h:3 * h + 1] * o_c[0:DH, cs] + gt[3 * h + 1:3 * h + 2] * o_s[:, cs]
                     + gt[3 * h + 2:3 * h + 3] * o_w[:, cs])
    o_ref[0] = jnp.concatenate(heads, axis=0).T.astype(BF16)


def _nsa_attn(proj3, prep, tabc, tabw, tabs, ovl, qgain, dsat):
    B, S, _ = proj3.shape
    ksa, vst, kwa, vwt, kca, vct = prep
    nq = S // Q_BLOCK
    nc = S // CMP_STRIDE
    nsel = S // SEL_BLOCK
    R = NSA_HPG * Q_BLOCK
    gw = NSA_HPG * NSA_DH
    per_bg = lambda a: pl.BlockSpec((1, 1) + a.shape[2:], lambda b, g, i: (b, g) + (0,) * (a.ndim - 2))
    return pl.pallas_call(
        functools.partial(_attn_kernel, nsel=nsel, dsat=dsat),
        grid=(B, NSA_KV, nq),
        in_specs=[
            pl.BlockSpec((1, Q_BLOCK, gw), lambda b, g, i: (b, i, P_NQ // gw + g)),
            pl.BlockSpec((1, Q_BLOCK, LANES), lambda b, g, i: (b, i, P_GATE // LANES + g)),
            per_bg(kca), per_bg(vct), per_bg(ksa), per_bg(vst), per_bg(kwa), per_bg(vwt),
            pl.BlockSpec((1, 1, nc, R), lambda b, g, i: (g, i, 0, 0)),
            pl.BlockSpec((1, 1) + tabw.shape[2:], lambda b, g, i: (g, 0, 0, 0)),
            pl.BlockSpec((1, 1) + tabs.shape[2:], lambda b, g, i: (g, 0, 0, 0)),
            pl.BlockSpec(ovl.shape, lambda b, g, i: (0, 0)),
            pl.BlockSpec(qgain.shape, lambda b, g, i: (0, 0)),
        ],
        out_specs=pl.BlockSpec((1, Q_BLOCK, gw), lambda b, g, i: (b, i, g)),
        out_shape=jax.ShapeDtypeStruct((B, S, NSA_Q), BF16),
        scratch_shapes=[pltpu.VMEM((1, R), F32), pltpu.VMEM((LANES, R), F32),
                        pltpu.VMEM((SEL_CHUNK, R), F32), pltpu.VMEM((SEL_CHUNK, R), F32),
                        pltpu.VMEM((WIN_BLOCKS * WIN_KB, R), F32)],
        compiler_params=_cparams(("parallel", "parallel", "arbitrary")),
        name="nsa_attn",
    )(proj3, proj3, kca, vct, ksa, vst, kwa, vwt, tabc, tabw, tabs, ovl, qgain)


def _rel_bucket(d):
    max_exact = REL_BUCKETS // 2
    d = jnp.maximum(d, 0)
    df = jnp.maximum(d, 1).astype(F32)
    large = max_exact + (jnp.log(df / max_exact) / math.log(REL_MAX_DIST / max_exact)
                         * (REL_BUCKETS - max_exact)).astype(jnp.int32)
    return jnp.where(d < max_exact, d, jnp.minimum(large, REL_BUCKETS - 1))


def _bias_kernel(rb_ref, bkt_ref, o_ref, *, rows):
    g = pl.program_id(0)
    step = 32
    for r0 in range(0, rows, step):
        bkt = bkt_ref[0, r0:r0 + step, :]
        acc = [jnp.full(bkt.shape, NEG_BIG, F32) for _ in range(NSA_HPG)]
        for b in range(REL_BUCKETS):
            hit = bkt == b
            for h in range(NSA_HPG):
                acc[h] = jnp.where(hit, rb_ref[b, g * NSA_HPG + h] * LOG2E, acc[h])
        for h in range(NSA_HPG):
            o_ref[0, 0, r0:r0 + step, h * Q_BLOCK:(h + 1) * Q_BLOCK] = acc[h]


def _bias_table(rel_bias, bucket):
    nt, rows, cols = bucket.shape
    return pl.pallas_call(
        functools.partial(_bias_kernel, rows=rows),
        grid=(NSA_KV, nt),
        in_specs=[
            pl.BlockSpec(memory_space=pltpu.SMEM),
            pl.BlockSpec((1, rows, cols), lambda g, t: (t, 0, 0)),
        ],
        out_specs=pl.BlockSpec((1, 1, rows, NSA_HPG * cols), lambda g, t: (g, t, 0, 0)),
        out_shape=jax.ShapeDtypeStruct((NSA_KV, nt, rows, NSA_HPG * cols), F32),
        compiler_params=_cparams(("parallel", "parallel")),
        name="bias_table",
    )(rel_bias, bucket)


def _bias_tables(rel_bias, S, dsat, strip_len):
    nq = S // Q_BLOCK
    nc = S // CMP_STRIDE
    n_cmp = (S - CMP_BLOCK) // CMP_STRIDE + 1
    ar = lambda n: jnp.arange(n, dtype=jnp.int32)
    bucket = lambda d, valid: jnp.where(valid, _rel_bucket(d), -1)
    i = ar(Q_BLOCK)[None, None, :]
    c = ar(nc)[None, :, None]
    d_c = ar(nq)[:, None, None] * Q_BLOCK + i - (c * CMP_STRIDE + CMP_BLOCK - 1)
    tabc = _bias_table(rel_bias, bucket(d_c, (d_c >= 0) & (c < n_cmp)))
    jw = ar((WIN_BLOCKS + 1) * WIN_KB)[None, :, None]
    d_w = i + WINDOW - jw
    tabw = _bias_table(rel_bias, bucket(d_w, (d_w >= 0) & (d_w < WINDOW) & (jw < WIN_BLOCKS * WIN_KB)))
    d_s = i - ar(strip_len)[None, :, None] + dsat
    tabs = _bias_table(rel_bias, bucket(d_s, d_s >= 0))
    return tabc, tabw, tabs


def _overlap_matrix(S):
    nc = S // CMP_STRIDE
    nsel = S // SEL_BLOCK
    n_cmp = (S - CMP_BLOCK) // CMP_STRIDE + 1
    c = np.arange(nc)[None, :]
    sel_start = (np.arange(SEL_ROWS) * SEL_BLOCK)[:, None]
    ovl = ((c * CMP_STRIDE < sel_start + SEL_BLOCK) & (c * CMP_STRIDE + CMP_BLOCK - 1 >= sel_start)
           & (c < n_cmp) & (np.arange(SEL_ROWS)[:, None] < nsel))
    return jnp.asarray(ovl, BF16)


def _layer_params(l, lower_bounds, norm_mix, w_in, hg_out_gain, nsa_qk_gain, cmp_pos, cmp_w1, cmp_b1,
                  cmp_w2, w_branch_a, w_branch_b, w_out, norm_mlp, w_up, w_down):
    splits = (HG_K, HG_K, HG_V, HG_V, NSA_Q, NSA_KVW, NSA_KVW, NSA_KVW, NSA_KVW, NSA_KVW, NSA_KVW,
              3 * NSA_HEADS, D_MODEL, D_MODEL)
    offs = np.concatenate([[0], np.cumsum(splits)])
    col = lambda k: w_in[l][:, offs[k]:offs[k + 1]]
    gate_w = col(11)
    zpad = jnp.zeros((D_MODEL, LANES - 3 * NSA_HPG), F32)
    gate_cols = []
    for g in range(NSA_KV):
        gate_cols += [gate_w[:, g * 3 * NSA_HPG:(g + 1) * 3 * NSA_HPG], zpad]
    w_main = jnp.concatenate([col(12), col(13), col(0), col(2), col(3),
                              col(7), col(8), col(9), col(10), col(4)] + gate_cols, axis=1).astype(BF16)
    w_cmp = jnp.concatenate([col(1), col(5), col(6)], axis=1).astype(BF16)

    lb = lower_bounds[l]
    lbp = jnp.stack([jnp.log(lb), jnp.log1p(-lb), 1.0 - lb])

    w1 = cmp_w1[l].reshape(2, 2, CMP_STRIDE, NSA_DH, CMP_HID)
    w1c = jnp.zeros((2, CMP_STRIDE, NSA_KV, NSA_DH, NSA_KV, 2, CMP_HID), F32)
    for g in range(NSA_KV):
        w1c = w1c.at[:, :, g, :, g, :, :].set(w1.transpose(0, 2, 3, 1, 4))
    w1c = w1c.reshape(2, CMP_STRIDE * NSA_KVW, NSA_KV * 2 * CMP_HID).astype(BF16)
    pos8 = jnp.zeros((2, 8, CMP_BLOCK * NSA_DH), F32).at[:, 0, :].set(cmp_pos[l].reshape(2, -1))
    w2p = jnp.zeros((2, CMP_HID, LANES), F32)
    w2p = w2p.at[0, :, :NSA_DH].set(cmp_w2[l, 0]).at[1, :NSA_DH, :].set(cmp_w2[l, 1].T).astype(BF16)
    qk = nsa_qk_gain[l]
    z64 = jnp.zeros((NSA_DH,), F32)
    return dict(
        norm_mix=norm_mix[l][None, :], w_main=w_main, w_cmp=w_cmp, lbp=lbp,
        hg_gain=hg_out_gain[l][None, :],
        gsel=jnp.concatenate([qk[2], qk[2]])[None, :], gwin=jnp.concatenate([qk[3], qk[3]])[None, :],
        gcmp=jnp.concatenate([qk[1], z64])[None, :],
        qgain=jnp.broadcast_to((qk[0] * (NSA_DH ** -0.5 * LOG2E))[:, None], (NSA_DH, Q_BLOCK)),
        pos8=pos8, w1=cmp_w1[l].astype(BF16), w1c=w1c, b1=cmp_b1[l][:, None, :], w2p=w2p,
        wa=w_branch_a[l].astype(BF16), wb=w_branch_b[l].astype(BF16), wo=w_out[l].astype(BF16),
        norm_mlp=norm_mlp[l][None, :], wu=w_up[l].astype(BF16), wd=w_down[l].astype(BF16))


def kernel(x, rel_bias, hg_lb_logits, norm_mix, w_in, hg_out_gain, nsa_qk_gain, cmp_pos, cmp_w1, cmp_b1, cmp_w2, w_branch_a, w_branch_b, w_out, norm_mlp, w_up, w_down):
    B, S, D = x.shape
    T = B * S
    depth = w_in.shape[0]
    assert D == D_MODEL and S % SEL_CHUNK == 0 and S >= WINDOW and S // SEL_BLOCK <= SEL_ROWS
    assert S % Q_BLOCK == 0 and Q_BLOCK % WIN_KB == 0 and WINDOW % WIN_KB == 0
    assert FORCE_BONUS > NSA_HPG and SEL_TOPK >= 3
    lb_cum = jnp.cumsum(jax.nn.softmax(hg_lb_logits.astype(F32), axis=0), axis=0)
    lower_bounds = lb_cum - lb_cum[0:1]

    dsat, strip_len = _sel_table_geometry()
    tabc, tabw, tabs = _bias_tables(rel_bias, S, dsat, strip_len)
    ovl = _overlap_matrix(S)
    nc = S // CMP_STRIDE
    tm = min(1024, T)

    x2 = x.reshape(T, D)
    for l in range(depth):
        p = _layer_params(l, lower_bounds, norm_mix, w_in, hg_out_gain, nsa_qk_gain, cmp_pos, cmp_w1,
                          cmp_b1, cmp_w2, w_branch_a, w_branch_b, w_out, norm_mlp, w_up, w_down)
        proj, z, kc, vc = _rms_proj(x2, p["norm_mix"], p["w_main"], p["w_cmp"], tm, P_TILE)
        proj3 = proj.reshape(B, S, P_COLS)
        kc2 = kc.reshape(B, nc, CMP_STRIDE * NSA_KVW)
        vc2 = vc.reshape(B, nc, CMP_STRIDE * NSA_KVW)
        o_a = _hgrn(proj3, z.reshape(B, S, HG_K), p["lbp"], p["hg_gain"])
        prep = _nsa_prep(proj3, kc2, vc2, p["gsel"], p["gwin"], p["gcmp"], p["pos8"], p["w1"],
                         p["w1c"], p["b1"], p["w2p"])
        o_b = _nsa_attn(proj3, prep, tabc, tabw, tabs, ovl, p["qgain"], dsat)
        x2 = _merge(x2, o_a.reshape(T, HG_V), o_b.reshape(T, NSA_Q), proj, p["wa"], p["wb"], p["wo"],
                    min(512, T))
        x2 = _mlp(x2, p["norm_mlp"], p["wu"], p["wd"], tm, 1024)
    return x2.reshape(B, S, D)
```

```python
import functools
import math

import numpy as np
import jax
import jax.numpy as jnp
from jax import lax
from jax.experimental import pallas as pl
from jax.experimental.pallas import tpu as pltpu

F32 = jnp.float32
BF16 = jnp.bfloat16

D_MODEL = 1024
HG_HEADS = 4
HG_DK = 128
HG_DV = 128
HG_K = HG_HEADS * HG_DK
HG_V = HG_HEADS * HG_DV
NSA_HEADS = 8
NSA_KV = 2
NSA_DH = 64
NSA_HPG = NSA_HEADS // NSA_KV
NSA_Q = NSA_HEADS * NSA_DH
NSA_KVW = NSA_KV * NSA_DH
CMP_BLOCK = 32
CMP_STRIDE = 16
CMP_HID = 128
SEL_BLOCK = 64
SEL_TOPK = 8
WINDOW = 512
FORCE_BONUS = 1000.0
REL_BUCKETS = 32
REL_MAX_DIST = 1024
D_FF = 4 * D_MODEL
EPS = 1e-6
NEG_BIG = -1e30
LOG2E = math.log2(math.e)

LANES = 128
VMEM_LIMIT = 56 * 1024 * 1024

P_GAB = 0
P_HG = P_GAB + 2 * D_MODEL
P_KV = P_HG + 3 * HG_K
P_NQ = P_KV + 4 * NSA_KVW
P_GATE = P_NQ + NSA_Q
P_COLS = P_GATE + NSA_KV * LANES
P_TILE = P_COLS // 2

HG_CHUNK = 128
HG_TS = 1024
Q_BLOCK = 256
WIN_KB = 128
SEL_CHUNK = 512
WIN_BLOCKS = (WINDOW + Q_BLOCK) // WIN_KB
SEL_SHIFT = int(math.log2(SEL_BLOCK))
SEL_ROWS = LANES - NSA_DH


def _cparams(sem):
    return pltpu.CompilerParams(dimension_semantics=sem, vmem_limit_bytes=VMEM_LIMIT)


def _rms_rows(x, gain):
    ms = jnp.mean(x * x, axis=-1, keepdims=True)
    return x * lax.rsqrt(ms + EPS) * gain


def _proj_kernel(x_ref, g_ref, w_ref, wx_ref, o_ref, z_ref, kc_ref, vc_ref, h_ref):
    @pl.when(pl.program_id(1) == 0)
    def _():
        h = _rms_rows(x_ref[...], g_ref[...]).astype(BF16)
        h_ref[...] = h
        c = jnp.dot(h, wx_ref[...], preferred_element_type=F32)
        z_ref[...] = c[:, :HG_K]
        kc_ref[...] = c[:, HG_K:HG_K + NSA_KVW].astype(BF16)
        vc_ref[...] = c[:, HG_K + NSA_KVW:].astype(BF16)

    o_ref[...] = jnp.dot(h_ref[...], w_ref[...], preferred_element_type=F32).astype(BF16)


def _rms_proj(x2, gain, w, wx, tm, tn):
    T, D = x2.shape
    N = w.shape[1]
    cmp_sds = jax.ShapeDtypeStruct((T, NSA_KVW), BF16)
    return pl.pallas_call(
        _proj_kernel,
        grid=(T // tm, N // tn),
        in_specs=[
            pl.BlockSpec((tm, D), lambda i, j: (i, 0)),
            pl.BlockSpec((1, D), lambda i, j: (0, 0)),
            pl.BlockSpec((D, tn), lambda i, j: (0, j)),
            pl.BlockSpec(wx.shape, lambda i, j: (0, 0)),
        ],
        out_specs=[
            pl.BlockSpec((tm, tn), lambda i, j: (i, j)),
            pl.BlockSpec((tm, HG_K), lambda i, j: (i, 0)),
            pl.BlockSpec((tm, NSA_KVW), lambda i, j: (i, 0)),
            pl.BlockSpec((tm, NSA_KVW), lambda i, j: (i, 0)),
        ],
        out_shape=[jax.ShapeDtypeStruct((T, N), BF16), jax.ShapeDtypeStruct((T, HG_K), F32),
                   cmp_sds, cmp_sds],
        scratch_shapes=[pltpu.VMEM((tm, D), BF16)],
        compiler_params=_cparams(("parallel", "arbitrary")),
        name="rms_proj",
    )(x2, gain, w, wx)


def _mlp_kernel(x_ref, g_ref, wu_ref, wd_ref, o_ref, h_ref):
    f = pl.program_id(1)

    @pl.when(f == 0)
    def _():
        x = x_ref[...]
        h_ref[...] = _rms_rows(x, g_ref[...]).astype(BF16)
        o_ref[...] = x

    u = jnp.dot(h_ref[...], wu_ref[...], preferred_element_type=F32)
    u = jnp.square(jnp.maximum(u, 0.0)).astype(BF16)
    o_ref[...] += jnp.dot(u, wd_ref[...], preferred_element_type=F32)


def _mlp(x2, gain, wu, wd, tm, tf):
    T, D = x2.shape
    FF = wu.shape[1]
    return pl.pallas_call(
        _mlp_kernel,
        grid=(T // tm, FF // tf),
        in_specs=[
            pl.BlockSpec((tm, D), lambda i, f: (i, 0)),
            pl.BlockSpec((1, D), lambda i, f: (0, 0)),
            pl.BlockSpec((D, tf), lambda i, f: (0, f)),
            pl.BlockSpec((tf, D), lambda i, f: (f, 0)),
        ],
        out_specs=pl.BlockSpec((tm, D), lambda i, f: (i, 0)),
        out_shape=jax.ShapeDtypeStruct((T, D), F32),
        scratch_shapes=[pltpu.VMEM((tm, D), BF16)],
        compiler_params=_cparams(("parallel", "arbitrary")),
        name="mlp",
    )(x2, gain, wu, wd)


def _merge_kernel(x_ref, oa_ref, ob_ref, ga_ref, gb_ref, wa_ref, wb_ref, wo_ref, o_ref):
    ya = jnp.dot(oa_ref[...], wa_ref[...], preferred_element_type=F32)
    yb = jnp.dot(ob_ref[...], wb_ref[...], preferred_element_type=F32)
    mixed = (jax.nn.sigmoid(ga_ref[...].astype(F32)) * ya
             + jax.nn.sigmoid(gb_ref[...].astype(F32)) * yb)
    o_ref[...] = x_ref[...] + jnp.dot(mixed.astype(BF16), wo_ref[...], preferred_element_type=F32)


def _merge(x2, oa, ob, proj, wa, wb, wo, tm):
    T, D = x2.shape
    ga_blk = P_GAB // D
    return pl.pallas_call(
        _merge_kernel,
        grid=(T // tm,),
        in_specs=[
            pl.BlockSpec((tm, D), lambda i: (i, 0)),
            pl.BlockSpec((tm, HG_V), lambda i: (i, 0)),
            pl.BlockSpec((tm, NSA_Q), lambda i: (i, 0)),
            pl.BlockSpec((tm, D), lambda i: (i, ga_blk)),
            pl.BlockSpec((tm, D), lambda i: (i, ga_blk + 1)),
            pl.BlockSpec((HG_V, D), lambda i: (0, 0)),
            pl.BlockSpec((NSA_Q, D), lambda i: (0, 0)),
            pl.BlockSpec((D, D), lambda i: (0, 0)),
        ],
        out_specs=pl.BlockSpec((tm, D), lambda i: (i, 0)),
        out_shape=jax.ShapeDtypeStruct((T, D), F32),
        compiler_params=_cparams(("parallel",)),
        name="merge",
    )(x2, oa, ob, proj, proj, wa, wb, wo)


def _hgrn_consts(C):
    L = int(math.log2(C))
    idx = np.arange(C)
    mats = [(idx[None, :] <= idx[:, None]),
            (idx[None, :] > idx[:, None])]
    bmask = []
    for l in range(L):
        m = 1 << l
        r = (idx & ~(2 * m - 1)) + m
        lo = np.minimum(idx, r)[:, None]
        hi = np.maximum(idx, r)[:, None]
        mats.append((idx[None, :] > lo) & (idx[None, :] <= hi))
        blk = idx >> (l + 1)
        bmask.append(blk[:, None] == blk[None, :])
    bmask.append(idx[:, None] == idx[None, :])
    seg = np.concatenate(mats, axis=0).astype(np.float32)
    seg = np.concatenate([seg, seg], axis=1)
    return seg, np.stack(bmask).astype(np.float32), L


def _hgrn_kernel(q_ref, f_ref, i_ref, g_ref, lbp_ref, gain_ref, seg_ref, bm_ref, o_ref, st_ref,
                 k_ref, sg_ref, *, C, L, nchunk):
    @pl.when(pl.program_id(1) == 0)
    def _():
        st_ref[...] = jnp.zeros_like(st_ref)

    rowi = lax.broadcasted_iota(jnp.int32, (C, HG_DK), 0)
    nt = (((1,), (1,)), ((), ()))
    tn = (((0,), (0,)), ((), ()))

    H = range(HG_HEADS)
    css = [slice(h * HG_DK, (h + 1) * HG_DK) for h in H]

    def gates(c, k_ref, sg_ref):
        r0 = pl.multiple_of(c * C, C)
        z = f_ref[0, pl.ds(r0, C), :]
        ez = jnp.exp(-jnp.abs(z))
        y = lbp_ref[1:2, :] + (jnp.minimum(z, 0.0) - jnp.log(1.0 + ez))
        log_lb = lbp_ref[0:1, :]
        logf = (jnp.maximum(log_lb, y) + jnp.log(1.0 + jnp.exp(-jnp.abs(log_lb - y)))) * LOG2E
        k_ref[...] = lbp_ref[2:3, :] * (jnp.where(z >= 0.0, ez, 1.0) / (1.0 + ez))
        g_hi = logf.astype(BF16)
        g_lo = (logf - g_hi.astype(F32)).astype(BF16)
        sg_ref[...] = jnp.dot(seg_ref[...], jnp.concatenate([g_hi, g_lo], axis=0),
                              preferred_element_type=F32)

    def mix(c, k_ref, sg_ref):
        r0 = pl.multiple_of(c * C, C)
        qbs = [q_ref[0, pl.ds(r0, C), cs] for cs in css]
        qs = [qb.astype(F32) for qb in qbs]
        ks = [k_ref[:, cs] for cs in css]
        ss = [lax.dot_general(qbs[h], ks[h].astype(BF16), nt, preferred_element_type=F32) * bm_ref[L]
              for h in H]
        for l in range(L):
            second = (rowi & (1 << l)) != 0
            for h in H:
                e = jnp.exp2(sg_ref[(2 + l) * C:(3 + l) * C, css[h]])
                ql = jnp.where(second, qs[h] * e, 0.0).astype(BF16)
                kl = jnp.where(second, 0.0, ks[h] * e).astype(BF16)
                ss[h] = ss[h] + lax.dot_general(ql, kl, nt, preferred_element_type=F32) * bm_ref[l]
        vbs = [i_ref[0, pl.ds(r0, C), cs] for cs in css]
        bs = [sg_ref[0:C, cs] for cs in css]
        sts = [st_ref[h] for h in H]
        outs = []
        for h in H:
            qd = (qs[h] * jnp.exp2(bs[h])).astype(BF16)
            o = lax.dot_general(qd, sts[h].astype(BF16), nt, preferred_element_type=F32)
            outs.append(o + jnp.dot(ss[h].astype(BF16), vbs[h], preferred_element_type=F32))
        for h in H:
            kd = (ks[h] * jnp.exp2(sg_ref[C:2 * C, css[h]])).astype(BF16)
            st_ref[h] = (sts[h] * jnp.exp2(bs[h][C - 1:C, :])
                         + lax.dot_general(vbs[h], kd, tn, preferred_element_type=F32))
        for h in H:
            gg = g_ref[0, pl.ds(r0, C), css[h]].astype(F32)
            o = _rms_rows(outs[h], gain_ref[...]) * (gg * jax.nn.sigmoid(gg))
            o_ref[0, pl.ds(r0, C), css[h]] = o.astype(BF16)

    def chunk(c, carry):
        gates(c, k_ref, sg_ref)
        mix(c, k_ref, sg_ref)
        return carry

    lax.fori_loop(0, nchunk, chunk, 0)


def _hgrn(proj3, z3, lbp, gain, C=HG_CHUNK, ts=HG_TS):
    B, S, _ = proj3.shape
    ts = min(ts, S)
    seg, bm, L = _hgrn_consts(C)
    blk = lambda k: pl.BlockSpec((1, ts, HG_K), lambda b, s: (b, s, P_HG // HG_K + k))
    return pl.pallas_call(
        functools.partial(_hgrn_kernel, C=C, L=L, nchunk=ts // C),
        grid=(B, S // ts),
        in_specs=[
            blk(0), pl.BlockSpec((1, ts, HG_K), lambda b, s: (b, s, 0)), blk(1), blk(2),
            pl.BlockSpec((3, HG_K), lambda b, s: (0, 0)),
            pl.BlockSpec((1, HG_DV), lambda b, s: (0, 0)),
            pl.BlockSpec(seg.shape, lambda b, s: (0, 0)),
            pl.BlockSpec(bm.shape, lambda b, s: (0, 0, 0)),
        ],
        out_specs=pl.BlockSpec((1, ts, HG_V), lambda b, s: (b, s, 0)),
        out_shape=jax.ShapeDtypeStruct((B, S, HG_V), BF16),
        scratch_shapes=[pltpu.VMEM((HG_HEADS, HG_DV, HG_DK), F32), pltpu.VMEM((C, HG_K), F32),
                        pltpu.VMEM(((2 + L) * C, HG_K), F32)],
        compiler_params=_cparams(("parallel", "arbitrary")),
        name="hgrn2",
    )(proj3, z3, proj3, proj3, lbp, gain, jnp.asarray(seg, BF16), jnp.asarray(bm))


def _half_rms(x, gain2, lane):
    sq = x * x
    s_lo = jnp.sum(jnp.where(lane < NSA_DH, sq, 0.0), axis=-1, keepdims=True)
    s_hi = jnp.sum(jnp.where(lane < NSA_DH, 0.0, sq), axis=-1, keepdims=True)
    ms = jnp.where(lane < NSA_DH, s_lo, s_hi) * (1.0 / NSA_DH)
    return x * lax.rsqrt(ms + EPS) * gain2


def _gelu_tanh(x):
    return 0.5 * x * (1.0 + jnp.tanh(math.sqrt(2.0 / math.pi) * (x + 0.044715 * (x * x * x))))


def _prep_kernel(kv_ref, nq_ref, kc_ref, vc_ref, gsel_ref, gwin_ref, gcmp_ref, qg_ref, pos_ref, w1_ref,
                 w1c_ref, b1_ref, w2_ref, ksa_ref, vst_ref, kwa_ref, vwt_ref, kca_ref, vct_ref, qt_ref,
                 *, S, rows):
    lane = lax.broadcasted_iota(jnp.int32, (rows, LANES), 1)
    rowi = lax.broadcasted_iota(jnp.int32, (rows, LANES), 0)
    ones_row = (lax.broadcasted_iota(jnp.int32, (NSA_DH, rows), 0) == 0).astype(F32)
    low = lane < NSA_DH

    def body(i, carry):
        r0 = pl.multiple_of(i * rows, rows)
        blk_hot = jnp.where(lane - NSA_DH == ((r0 + rowi) >> SEL_SHIFT), 1.0, 0.0)
        for src, gain_ref, aug, k_dst, v_dst, vrows in (
                (0, gsel_ref, blk_hot, ksa_ref, vst_ref, rows),
                (2, gwin_ref, jnp.zeros((rows, LANES), F32), kwa_ref, vwt_ref, WIN_KB)):
            kk = kv_ref[0, pl.ds(r0, rows), src * LANES:(src + 1) * LANES].astype(F32)
            vv = kv_ref[0, pl.ds(r0, rows), (src + 1) * LANES:(src + 2) * LANES].astype(F32)
            kn = _half_rms(kk, gain_ref[...], lane)
            k_dst[0, 0, pl.ds(r0, rows), :] = jnp.where(low, kn, aug).astype(BF16)
            k_dst[0, 1, pl.ds(r0, rows), :] = jnp.where(low, pltpu.roll(kn, NSA_DH, 1), aug).astype(BF16)
            vt = vv.T
            for g in range(NSA_KV):
                vg = jnp.concatenate([vt[g * NSA_DH:(g + 1) * NSA_DH], ones_row], axis=0).astype(BF16)
                for u in range(rows // vrows):
                    v_dst[0, g, i * (rows // vrows) + u] = vg[:, u * vrows:(u + 1) * vrows]
        for u in range(rows // Q_BLOCK):
            xt = nq_ref[0, pl.ds(r0 + u * Q_BLOCK, Q_BLOCK), :].astype(F32).T
            for g in range(NSA_KV):
                cols = []
                for h in range(NSA_HPG):
                    xh = xt[(g * NSA_HPG + h) * NSA_DH:(g * NSA_HPG + h + 1) * NSA_DH]
                    ms = jnp.mean(xh * xh, axis=0, keepdims=True)
                    cols.append(xh * lax.rsqrt(ms + EPS) * qg_ref[...])
                qt_ref[0, g, i * (rows // Q_BLOCK) + u] = jnp.concatenate(cols, axis=1).astype(BF16)
        return carry

    lax.fori_loop(0, S // rows, body, 0)

    nc = S // CMP_STRIDE
    outs = []
    for j, src_ref in ((0, kc_ref), (1, vc_ref)):
        bias = jnp.dot(pos_ref[j].astype(BF16), w1_ref[j], preferred_element_type=F32)[0:1] + b1_ref[j]
        pj = jnp.dot(src_ref[0], w1c_ref[j], preferred_element_type=F32)
        per_g = []
        for g in range(NSA_KV):
            first = pj[:, (2 * g) * CMP_HID:(2 * g + 1) * CMP_HID]
            second = pj[:, (2 * g + 1) * CMP_HID:(2 * g + 2) * CMP_HID]
            hid = _gelu_tanh(first + pltpu.roll(second, nc - 1, 0) + bias).astype(BF16)
            if j == 0:
                per_g.append(jnp.dot(hid, w2_ref[0], preferred_element_type=F32))
            else:
                per_g.append(lax.dot_general(w2_ref[1], hid, (((1,), (1,)), ((), ())),
                                             preferred_element_type=F32))
        outs.append(per_g)
    for g in range(NSA_KV):
        kc = outs[0][g]
        ms = jnp.sum(kc * kc, axis=-1, keepdims=True) * (1.0 / NSA_DH)
        kca_ref[0, g] = (kc * lax.rsqrt(ms + EPS) * gcmp_ref[...]).astype(BF16)
        vct_ref[0, g] = outs[1][g].astype(BF16)


def _nsa_prep(proj3, kc2, vc2, gsel, gwin, gcmp, qgain, pos8, w1, w1c, b1, w2p):
    B, S, _ = proj3.shape
    nc = S // CMP_STRIDE
    rows = SEL_CHUNK
    full = lambda shp: pl.BlockSpec(shp, lambda b: (0,) * len(shp))
    whole = lambda shp: pl.BlockSpec((1,) + shp, lambda b: (b,) + (0,) * len(shp))
    shapes = [(NSA_KV, S, LANES), (NSA_KV, S // SEL_CHUNK, LANES, SEL_CHUNK),
              (NSA_KV, S, LANES), (NSA_KV, S // WIN_KB, LANES, WIN_KB),
              (NSA_KV, nc, LANES), (NSA_KV, LANES, nc),
              (NSA_KV, S // Q_BLOCK, NSA_DH, NSA_HPG * Q_BLOCK)]
    return pl.pallas_call(
        functools.partial(_prep_kernel, S=S, rows=rows),
        grid=(B,),
        in_specs=[
            pl.BlockSpec((1, S, 4 * NSA_KVW), lambda b: (b, 0, P_KV // (4 * NSA_KVW))),
            pl.BlockSpec((1, S, NSA_Q), lambda b: (b, 0, P_NQ // NSA_Q)),
            pl.BlockSpec((1, nc, CMP_STRIDE * NSA_KVW), lambda b: (b, 0, 0)),
            pl.BlockSpec((1, nc, CMP_STRIDE * NSA_KVW), lambda b: (b, 0, 0)),
            full((1, LANES)), full((1, LANES)), full((1, LANES)), full(qgain.shape),
            full(pos8.shape), full(w1.shape), full(w1c.shape), full(b1.shape), full(w2p.shape),
        ],
        out_specs=[whole(s) for s in shapes],
        out_shape=[jax.ShapeDtypeStruct((B,) + s, BF16) for s in shapes],
        compiler_params=_cparams(("parallel",)),
        name="nsa_prep",
    )(proj3, proj3, kc2, vc2, gsel, gwin, gcmp, qgain, pos8, w1, w1c, b1, w2p)


def _sel_table_geometry():
    last_start = (REL_BUCKETS // 2) * (REL_MAX_DIST / (REL_BUCKETS // 2)) ** (
        (REL_BUCKETS - REL_BUCKETS // 2 - 1) / (REL_BUCKETS - REL_BUCKETS // 2))
    dsat = int(math.ceil((last_start + 16 + SEL_CHUNK) / LANES)) * LANES
    return dsat, dsat + SEL_CHUNK


def _attn_kernel(q_ref, gate_ref, kca_ref, vct_ref, ksa_ref, vst_ref, kwa_ref, vwt_ref, tabc_ref,
                 tabw_ref, tabs_ref, ovl_ref, o_ref, m_ref, acc_ref, sa_ref, sb_ref, sw_ref,
                 *, nsel, dsat):
    qi = pl.program_id(2)
    QB = Q_BLOCK
    R = NSA_HPG * QB
    DH = NSA_DH

    qt = q_ref[0, 0, 0]
    qta = jnp.concatenate([qt, jnp.zeros_like(qt)], axis=0)

    vts, slabs = [], []
    for r in range(WIN_BLOCKS):
        kb = qi * (QB // WIN_KB) - WINDOW // WIN_KB + r
        kbc = jnp.maximum(kb, 0)
        kw = kwa_ref[0, 0, pl.ds(pl.multiple_of(kbc * WIN_KB, WIN_KB), WIN_KB), :]
        vts.append(vwt_ref[0, 0, kbc])
        slabs.append(pl.multiple_of(jnp.where(kb >= 0, r, WIN_BLOCKS) * WIN_KB, WIN_KB))
        sw_ref[r * WIN_KB:(r + 1) * WIN_KB, :] = jnp.dot(kw, qta, preferred_element_type=F32)

    def scores(c, q_aug):
        k0 = pl.multiple_of(c * SEL_CHUNK, SEL_CHUNK)
        delta = qi * QB - c * SEL_CHUNK
        start = pl.multiple_of(dsat - jnp.minimum(delta, dsat), LANES)
        return (jnp.dot(ksa_ref[0, 0, pl.ds(k0, SEL_CHUNK), :], q_aug, preferred_element_type=F32)
                + tabs_ref[0, 0, pl.ds(start, SEL_CHUNK), :])

    s = jnp.dot(kca_ref[0, 0], qta, preferred_element_type=F32) + tabc_ref[0, 0]
    mx = jnp.max(s, axis=0, keepdims=True)
    e = jnp.exp2(s - mx)
    inv = jnp.where(mx > 0.5 * NEG_BIG, 1.0 / jnp.sum(e, axis=0, keepdims=True), 0.0)
    p = e * inv
    o_c = jnp.dot(vct_ref[0, 0], p.astype(BF16), preferred_element_type=F32)

    psum = p[:, 0:QB] + p[:, QB:2 * QB] + p[:, 2 * QB:3 * QB] + p[:, 3 * QB:4 * QB]
    p_hi = psum.astype(BF16)
    p_lo = (psum - p_hi.astype(F32)).astype(BF16)
    imp = (jnp.dot(ovl_ref[...], p_hi, preferred_element_type=F32)
           + jnp.dot(ovl_ref[...], p_lo, preferred_element_type=F32))
    t = qi * QB + lax.broadcasted_iota(jnp.int32, (SEL_ROWS, QB), 1)
    jrow = lax.broadcasted_iota(jnp.int32, (SEL_ROWS, QB), 0)
    jcur = t >> SEL_SHIFT
    n_top = min(SEL_TOPK, nsel)
    forced = (jrow == 0) | (jrow == jcur) | (jrow == jcur - 1)
    work = jnp.where(forced, -1.0, jnp.where(jrow * SEL_BLOCK <= t, imp, -1.0))
    work = jnp.where(jrow < nsel, work, -5.0)
    jf = jrow.astype(F32)
    sel = jnp.where(forced, 1.0, 0.0)
    for _ in range(n_top - 3):
        best = jnp.max(work, axis=0, keepdims=True)
        first = jnp.min(jnp.where(work == best, jf, float(SEL_ROWS)), axis=0, keepdims=True)
        hit = jf == first
        sel = jnp.where(hit, 1.0, sel)
        work = jnp.where(hit, -3.0, work)
    sel = jnp.where(jcur < n_top, jnp.where(jrow < n_top, 1.0, 0.0), sel)
    selneg = jnp.concatenate([jnp.where(sel > 0.5, 0.0, NEG_BIG)] * NSA_HPG, axis=1)
    qts = jnp.concatenate([qt, selneg.astype(BF16)], axis=0)
    sa_ref[...] = scores(0, qts)

    sw = jnp.concatenate([sw_ref[r * WIN_KB:(r + 1) * WIN_KB, :] + tabw_ref[0, 0, pl.ds(slabs[r], WIN_KB), :]
                          for r in range(WIN_BLOCKS)], axis=0)
    ew = jnp.exp2(sw - jnp.max(sw, axis=0, keepdims=True)).astype(BF16)
    o_w = jnp.dot(jnp.concatenate(vts, axis=1), ew, preferred_element_type=F32)

    m_ref[...] = jnp.full_like(m_ref, NEG_BIG)
    acc_ref[...] = jnp.zeros_like(acc_ref)

    def consume(s_ref, c):
        sc = s_ref[...]
        m_old = m_ref[...]
        m_new = jnp.maximum(m_old, jnp.max(sc, axis=0, keepdims=True))
        alpha = jnp.exp2(m_old - m_new)
        pe = jnp.exp2(sc - m_new).astype(BF16)
        acc_ref[...] = alpha * acc_ref[...] + jnp.dot(vst_ref[0, 0, c], pe, preferred_element_type=F32)
        m_ref[...] = m_new

    n_chunks = (qi * QB + QB - 1) // SEL_CHUNK + 1
    n_pairs = (n_chunks - 1) // 2

    def sel_pair(pr, carry):
        c = 2 * pr
        sb_ref[...] = scores(c + 1, qts)
        consume(sa_ref, c)
        sa_ref[...] = scores(c + 2, qts)
        consume(sb_ref, c + 1)
        return carry

    lax.fori_loop(0, n_pairs, sel_pair, 0)

    @pl.when(n_chunks % 2 == 0)
    def _():
        sb_ref[...] = scores(n_chunks - 1, qts)
        consume(sa_ref, n_chunks - 2)
        consume(sb_ref, n_chunks - 1)

    @pl.when(n_chunks % 2 == 1)
    def _():
        consume(sa_ref, n_chunks - 1)

    acc = acc_ref[...]

    gt = jax.nn.sigmoid(gate_ref[0].astype(F32)).T
    o_s = acc[0:DH] / acc[DH:DH + 1]
    o_w = o_w[0:DH] / o_w[DH:DH + 1]
    heads = []
    for h in range(NSA_HPG):
        cs = slice(h * QB, (h + 1) * QB)
        heads.append(gt[3 * h:3 * h + 1] * o_c[0:DH, cs] + gt[3 * h + 1:3 * h + 2] * o_s[:, cs]
                     + gt[3 * h + 2:3 * h + 3] * o_w[:, cs])
    o_ref[0] = jnp.concatenate(heads, axis=0).T.astype(BF16)


def _nsa_attn(proj3, prep, tabc, tabw, tabs, ovl, dsat):
    B, S, _ = proj3.shape
    ksa, vst, kwa, vwt, kca, vct, qt = prep
    nq = S // Q_BLOCK
    nc = S // CMP_STRIDE
    nsel = S // SEL_BLOCK
    R = NSA_HPG * Q_BLOCK
    gw = NSA_HPG * NSA_DH
    per_bg = lambda a: pl.BlockSpec((1, 1) + a.shape[2:], lambda b, g, i: (b, g) + (0,) * (a.ndim - 2))
    return pl.pallas_call(
        functools.partial(_attn_kernel, nsel=nsel, dsat=dsat),
        grid=(B, NSA_KV, nq),
        in_specs=[
            pl.BlockSpec((1, 1, 1, NSA_DH, R), lambda b, g, i: (b, g, i, 0, 0)),
            pl.BlockSpec((1, Q_BLOCK, LANES), lambda b, g, i: (b, i, P_GATE // LANES + g)),
            per_bg(kca), per_bg(vct), per_bg(ksa), per_bg(vst), per_bg(kwa), per_bg(vwt),
            pl.BlockSpec((1, 1, nc, R), lambda b, g, i: (g, i, 0, 0)),
            pl.BlockSpec((1, 1) + tabw.shape[2:], lambda b, g, i: (g, 0, 0, 0)),
            pl.BlockSpec((1, 1) + tabs.shape[2:], lambda b, g, i: (g, 0, 0, 0)),
            pl.BlockSpec(ovl.shape, lambda b, g, i: (0, 0)),
        ],
        out_specs=pl.BlockSpec((1, Q_BLOCK, gw), lambda b, g, i: (b, i, g)),
        out_shape=jax.ShapeDtypeStruct((B, S, NSA_Q), BF16),
        scratch_shapes=[pltpu.VMEM((1, R), F32), pltpu.VMEM((LANES, R), F32),
                        pltpu.VMEM((SEL_CHUNK, R), F32), pltpu.VMEM((SEL_CHUNK, R), F32),
                        pltpu.VMEM((WIN_BLOCKS * WIN_KB, R), F32)],
        compiler_params=_cparams(("parallel", "parallel", "arbitrary")),
        name="nsa_attn",
    )(qt, proj3, kca, vct, ksa, vst, kwa, vwt, tabc, tabw, tabs, ovl)


def _rel_bucket(d):
    max_exact = REL_BUCKETS // 2
    d = jnp.maximum(d, 0)
    df = jnp.maximum(d, 1).astype(F32)
    large = max_exact + (jnp.log(df / max_exact) / math.log(REL_MAX_DIST / max_exact)
                         * (REL_BUCKETS - max_exact)).astype(jnp.int32)
    return jnp.where(d < max_exact, d, jnp.minimum(large, REL_BUCKETS - 1))


def _bias_kernel(rb_ref, bkt_ref, o_ref, *, rows):
    g = pl.program_id(0)
    step = 32
    for r0 in range(0, rows, step):
        bkt = bkt_ref[0, r0:r0 + step, :]
        acc = [jnp.full(bkt.shape, NEG_BIG, F32) for _ in range(NSA_HPG)]
        for b in range(REL_BUCKETS):
            hit = bkt == b
            for h in range(NSA_HPG):
                acc[h] = jnp.where(hit, rb_ref[b, g * NSA_HPG + h] * LOG2E, acc[h])
        for h in range(NSA_HPG):
            o_ref[0, 0, r0:r0 + step, h * Q_BLOCK:(h + 1) * Q_BLOCK] = acc[h]


def _bias_table(rel_bias, bucket):
    nt, rows, cols = bucket.shape
    return pl.pallas_call(
        functools.partial(_bias_kernel, rows=rows),
        grid=(NSA_KV, nt),
        in_specs=[
            pl.BlockSpec(memory_space=pltpu.SMEM),
            pl.BlockSpec((1, rows, cols), lambda g, t: (t, 0, 0)),
        ],
        out_specs=pl.BlockSpec((1, 1, rows, NSA_HPG * cols), lambda g, t: (g, t, 0, 0)),
        out_shape=jax.ShapeDtypeStruct((NSA_KV, nt, rows, NSA_HPG * cols), F32),
        compiler_params=_cparams(("parallel", "parallel")),
        name="bias_table",
    )(rel_bias, bucket)


def _bias_tables(rel_bias, S, dsat, strip_len):
    nq = S // Q_BLOCK
    nc = S // CMP_STRIDE
    n_cmp = (S - CMP_BLOCK) // CMP_STRIDE + 1
    ar = lambda n: jnp.arange(n, dtype=jnp.int32)
    bucket = lambda d, valid: jnp.where(valid, _rel_bucket(d), -1)
    i = ar(Q_BLOCK)[None, None, :]
    c = ar(nc)[None, :, None]
    d_c = ar(nq)[:, None, None] * Q_BLOCK + i - (c * CMP_STRIDE + CMP_BLOCK - 1)
    tabc = _bias_table(rel_bias, bucket(d_c, (d_c >= 0) & (c < n_cmp)))
    jw = ar((WIN_BLOCKS + 1) * WIN_KB)[None, :, None]
    d_w = i + WINDOW - jw
    tabw = _bias_table(rel_bias, bucket(d_w, (d_w >= 0) & (d_w < WINDOW) & (jw < WIN_BLOCKS * WIN_KB)))
    d_s = i - ar(strip_len)[None, :, None] + dsat
    tabs = _bias_table(rel_bias, bucket(d_s, d_s >= 0))
    return tabc, tabw, tabs


def _overlap_matrix(S):
    nc = S // CMP_STRIDE
    nsel = S // SEL_BLOCK
    n_cmp = (S - CMP_BLOCK) // CMP_STRIDE + 1
    c = np.arange(nc)[None, :]
    sel_start = (np.arange(SEL_ROWS) * SEL_BLOCK)[:, None]
    ovl = ((c * CMP_STRIDE < sel_start + SEL_BLOCK) & (c * CMP_STRIDE + CMP_BLOCK - 1 >= sel_start)
           & (c < n_cmp) & (np.arange(SEL_ROWS)[:, None] < nsel))
    return jnp.asarray(ovl, BF16)


def _layer_params(l, lower_bounds, norm_mix, w_in, hg_out_gain, nsa_qk_gain, cmp_pos, cmp_w1, cmp_b1,
                  cmp_w2, w_branch_a, w_branch_b, w_out, norm_mlp, w_up, w_down):
    splits = (HG_K, HG_K, HG_V, HG_V, NSA_Q, NSA_KVW, NSA_KVW, NSA_KVW, NSA_KVW, NSA_KVW, NSA_KVW,
              3 * NSA_HEADS, D_MODEL, D_MODEL)
    offs = np.concatenate([[0], np.cumsum(splits)])
    col = lambda k: w_in[l][:, offs[k]:offs[k + 1]]
    gate_w = col(11)
    zpad = jnp.zeros((D_MODEL, LANES - 3 * NSA_HPG), F32)
    gate_cols = []
    for g in range(NSA_KV):
        gate_cols += [gate_w[:, g * 3 * NSA_HPG:(g + 1) * 3 * NSA_HPG], zpad]
    w_main = jnp.concatenate([col(12), col(13), col(0), col(2), col(3),
                              col(7), col(8), col(9), col(10), col(4)] + gate_cols, axis=1).astype(BF16)
    w_cmp = jnp.concatenate([col(1), col(5), col(6)], axis=1).astype(BF16)

    lb = lower_bounds[l]
    lbp = jnp.stack([jnp.log(lb), jnp.log1p(-lb), 1.0 - lb])

    w1 = cmp_w1[l].reshape(2, 2, CMP_STRIDE, NSA_DH, CMP_HID)
    w1c = jnp.zeros((2, CMP_STRIDE, NSA_KV, NSA_DH, NSA_KV, 2, CMP_HID), F32)
    for g in range(NSA_KV):
        w1c = w1c.at[:, :, g, :, g, :, :].set(w1.transpose(0, 2, 3, 1, 4))
    w1c = w1c.reshape(2, CMP_STRIDE * NSA_KVW, NSA_KV * 2 * CMP_HID).astype(BF16)
    pos8 = jnp.zeros((2, 8, CMP_BLOCK * NSA_DH), F32).at[:, 0, :].set(cmp_pos[l].reshape(2, -1))
    w2p = jnp.zeros((2, CMP_HID, LANES), F32)
    w2p = w2p.at[0, :, :NSA_DH].set(cmp_w2[l, 0]).at[1, :NSA_DH, :].set(cmp_w2[l, 1].T).astype(BF16)
    qk = nsa_qk_gain[l]
    z64 = jnp.zeros((NSA_DH,), F32)
    return dict(
        norm_mix=norm_mix[l][None, :], w_main=w_main, w_cmp=w_cmp, lbp=lbp,
        hg_gain=hg_out_gain[l][None, :],
        gsel=jnp.concatenate([qk[2], qk[2]])[None, :], gwin=jnp.concatenate([qk[3], qk[3]])[None, :],
        gcmp=jnp.concatenate([qk[1], z64])[None, :],
        qgain=jnp.broadcast_to((qk[0] * (NSA_DH ** -0.5 * LOG2E))[:, None], (NSA_DH, Q_BLOCK)),
        pos8=pos8, w1=cmp_w1[l].astype(BF16), w1c=w1c, b1=cmp_b1[l][:, None, :], w2p=w2p,
        wa=w_branch_a[l].astype(BF16), wb=w_branch_b[l].astype(BF16), wo=w_out[l].astype(BF16),
        norm_mlp=norm_mlp[l][None, :], wu=w_up[l].astype(BF16), wd=w_down[l].astype(BF16))


def kernel(x, rel_bias, hg_lb_logits, norm_mix, w_in, hg_out_gain, nsa_qk_gain, cmp_pos, cmp_w1, cmp_b1, cmp_w2, w_branch_a, w_branch_b, w_out, norm_mlp, w_up, w_down):
    B, S, D = x.shape
    T = B * S
    depth = w_in.shape[0]
    assert D == D_MODEL and S % SEL_CHUNK == 0 and S >= WINDOW and S // SEL_BLOCK <= SEL_ROWS
    assert S % Q_BLOCK == 0 and Q_BLOCK % WIN_KB == 0 and WINDOW % WIN_KB == 0
    assert FORCE_BONUS > NSA_HPG and SEL_TOPK >= 3
    lb_cum = jnp.cumsum(jax.nn.softmax(hg_lb_logits.astype(F32), axis=0), axis=0)
    lower_bounds = lb_cum - lb_cum[0:1]

    dsat, strip_len = _sel_table_geometry()
    tabc, tabw, tabs = _bias_tables(rel_bias, S, dsat, strip_len)
    ovl = _overlap_matrix(S)
    nc = S // CMP_STRIDE
    tm = min(1024, T)

    x2 = x.reshape(T, D)
    for l in range(depth):
        p = _layer_params(l, lower_bounds, norm_mix, w_in, hg_out_gain, nsa_qk_gain, cmp_pos, cmp_w1,
                          cmp_b1, cmp_w2, w_branch_a, w_branch_b, w_out, norm_mlp, w_up, w_down)
        proj, z, kc, vc = _rms_proj(x2, p["norm_mix"], p["w_main"], p["w_cmp"], tm, P_TILE)
        proj3 = proj.reshape(B, S, P_COLS)
        kc2 = kc.reshape(B, nc, CMP_STRIDE * NSA_KVW)
        vc2 = vc.reshape(B, nc, CMP_STRIDE * NSA_KVW)
        o_a = _hgrn(proj3, z.reshape(B, S, HG_K), p["lbp"], p["hg_gain"])
        prep = _nsa_prep(proj3, kc2, vc2, p["gsel"], p["gwin"], p["gcmp"], p["qgain"], p["pos8"], p["w1"],
                         p["w1c"], p["b1"], p["w2p"])
        o_b = _nsa_attn(proj3, prep, tabc, tabw, tabs, ovl, dsat)
        x2 = _merge(x2, o_a.reshape(T, HG_V), o_b.reshape(T, NSA_Q), proj, p["wa"], p["wb"], p["wo"],
                    min(512, T))
        x2 = _mlp(x2, p["norm_mlp"], p["wu"], p["wd"], tm, 1024)
    return x2.reshape(B, S, D)
```

```python
import functools
import math

import numpy as np
import jax
import jax.numpy as jnp
from jax import lax
from jax.experimental import pallas as pl
from jax.experimental.pallas import tpu as pltpu

F32 = jnp.float32
BF16 = jnp.bfloat16

D_MODEL = 1024
HG_HEADS = 4
HG_DK = 128
HG_DV = 128
HG_K = HG_HEADS * HG_DK
HG_V = HG_HEADS * HG_DV
NSA_HEADS = 8
NSA_KV = 2
NSA_DH = 64
NSA_HPG = NSA_HEADS // NSA_KV
NSA_Q = NSA_HEADS * NSA_DH
NSA_KVW = NSA_KV * NSA_DH
CMP_BLOCK = 32
CMP_STRIDE = 16
CMP_HID = 128
SEL_BLOCK = 64
SEL_TOPK = 8
WINDOW = 512
FORCE_BONUS = 1000.0
REL_BUCKETS = 32
REL_MAX_DIST = 1024
D_FF = 4 * D_MODEL
EPS = 1e-6
NEG_BIG = -1e30
LOG2E = math.log2(math.e)

LANES = 128
VMEM_LIMIT = 56 * 1024 * 1024

P_GAB = 0
P_HG = P_GAB + 2 * D_MODEL
P_KV = P_HG + 3 * HG_K
P_NQ = P_KV + 4 * NSA_KVW
P_GATE = P_NQ + NSA_Q
P_COLS = P_GATE + NSA_KV * LANES
P_TILE = P_COLS // 2

HG_CHUNK = 128
HG_TS = 1024
Q_BLOCK = 256
WIN_KB = 128
SEL_CHUNK = 512
WIN_BLOCKS = (WINDOW + Q_BLOCK) // WIN_KB
SEL_SHIFT = int(math.log2(SEL_BLOCK))
SEL_ROWS = LANES - NSA_DH
V_ROWS = NSA_DH + 16


def _cparams(sem):
    return pltpu.CompilerParams(dimension_semantics=sem, vmem_limit_bytes=VMEM_LIMIT)


def _rms_rows(x, gain):
    ms = jnp.mean(x * x, axis=-1, keepdims=True)
    return x * lax.rsqrt(ms + EPS) * gain


def _proj_kernel(x_ref, g_ref, w_ref, wx_ref, o_ref, z_ref, kc_ref, vc_ref, h_ref):
    @pl.when(pl.program_id(1) == 0)
    def _():
        h = _rms_rows(x_ref[...], g_ref[...]).astype(BF16)
        h_ref[...] = h
        c = jnp.dot(h, wx_ref[...], preferred_element_type=F32)
        z_ref[...] = c[:, :HG_K]
        kc_ref[...] = c[:, HG_K:HG_K + NSA_KVW].astype(BF16)
        vc_ref[...] = c[:, HG_K + NSA_KVW:].astype(BF16)

    o_ref[...] = jnp.dot(h_ref[...], w_ref[...], preferred_element_type=F32).astype(BF16)


def _rms_proj(x2, gain, w, wx, tm, tn):
    T, D = x2.shape
    N = w.shape[1]
    cmp_sds = jax.ShapeDtypeStruct((T, NSA_KVW), BF16)
    return pl.pallas_call(
        _proj_kernel,
        grid=(T // tm, N // tn),
        in_specs=[
            pl.BlockSpec((tm, D), lambda i, j: (i, 0)),
            pl.BlockSpec((1, D), lambda i, j: (0, 0)),
            pl.BlockSpec((D, tn), lambda i, j: (0, j)),
            pl.BlockSpec(wx.shape, lambda i, j: (0, 0)),
        ],
        out_specs=[
            pl.BlockSpec((tm, tn), lambda i, j: (i, j)),
            pl.BlockSpec((tm, HG_K), lambda i, j: (i, 0)),
            pl.BlockSpec((tm, NSA_KVW), lambda i, j: (i, 0)),
            pl.BlockSpec((tm, NSA_KVW), lambda i, j: (i, 0)),
        ],
        out_shape=[jax.ShapeDtypeStruct((T, N), BF16), jax.ShapeDtypeStruct((T, HG_K), F32),
                   cmp_sds, cmp_sds],
        scratch_shapes=[pltpu.VMEM((tm, D), BF16)],
        compiler_params=_cparams(("parallel", "arbitrary")),
        name="rms_proj",
    )(x2, gain, w, wx)


def _mlp_kernel(x_ref, g_ref, wu_ref, wd_ref, o_ref, h_ref):
    f = pl.program_id(1)

    @pl.when(f == 0)
    def _():
        x = x_ref[...]
        h_ref[...] = _rms_rows(x, g_ref[...]).astype(BF16)
        o_ref[...] = x

    u = jnp.dot(h_ref[...], wu_ref[...], preferred_element_type=F32)
    u = jnp.square(jnp.maximum(u, 0.0)).astype(BF16)
    o_ref[...] += jnp.dot(u, wd_ref[...], preferred_element_type=F32)


def _mlp(x2, gain, wu, wd, tm, tf):
    T, D = x2.shape
    FF = wu.shape[1]
    return pl.pallas_call(
        _mlp_kernel,
        grid=(T // tm, FF // tf),
        in_specs=[
            pl.BlockSpec((tm, D), lambda i, f: (i, 0)),
            pl.BlockSpec((1, D), lambda i, f: (0, 0)),
            pl.BlockSpec((D, tf), lambda i, f: (0, f)),
            pl.BlockSpec((tf, D), lambda i, f: (f, 0)),
        ],
        out_specs=pl.BlockSpec((tm, D), lambda i, f: (i, 0)),
        out_shape=jax.ShapeDtypeStruct((T, D), F32),
        scratch_shapes=[pltpu.VMEM((tm, D), BF16)],
        compiler_params=_cparams(("parallel", "arbitrary")),
        name="mlp",
    )(x2, gain, wu, wd)


def _merge_kernel(x_ref, oa_ref, ob_ref, ga_ref, gb_ref, wa_ref, wb_ref, wo_ref, o_ref):
    ya = jnp.dot(oa_ref[...], wa_ref[...], preferred_element_type=F32)
    yb = jnp.dot(ob_ref[...], wb_ref[...], preferred_element_type=F32)
    mixed = (jax.nn.sigmoid(ga_ref[...].astype(F32)) * ya
             + jax.nn.sigmoid(gb_ref[...].astype(F32)) * yb)
    o_ref[...] = x_ref[...] + jnp.dot(mixed.astype(BF16), wo_ref[...], preferred_element_type=F32)


def _merge(x2, oa, ob, proj, wa, wb, wo, tm):
    T, D = x2.shape
    ga_blk = P_GAB // D
    return pl.pallas_call(
        _merge_kernel,
        grid=(T // tm,),
        in_specs=[
            pl.BlockSpec((tm, D), lambda i: (i, 0)),
            pl.BlockSpec((tm, HG_V), lambda i: (i, 0)),
            pl.BlockSpec((tm, NSA_Q), lambda i: (i, 0)),
            pl.BlockSpec((tm, D), lambda i: (i, ga_blk)),
            pl.BlockSpec((tm, D), lambda i: (i, ga_blk + 1)),
            pl.BlockSpec((HG_V, D), lambda i: (0, 0)),
            pl.BlockSpec((NSA_Q, D), lambda i: (0, 0)),
            pl.BlockSpec((D, D), lambda i: (0, 0)),
        ],
        out_specs=pl.BlockSpec((tm, D), lambda i: (i, 0)),
        out_shape=jax.ShapeDtypeStruct((T, D), F32),
        compiler_params=_cparams(("parallel",)),
        name="merge",
    )(x2, oa, ob, proj, proj, wa, wb, wo)


def _hgrn_consts(C):
    L = int(math.log2(C))
    idx = np.arange(C)
    mats = [(idx[None, :] <= idx[:, None]),
            (idx[None, :] > idx[:, None])]
    bmask = []
    for l in range(L):
        m = 1 << l
        r = (idx & ~(2 * m - 1)) + m
        lo = np.minimum(idx, r)[:, None]
        hi = np.maximum(idx, r)[:, None]
        mats.append((idx[None, :] > lo) & (idx[None, :] <= hi))
        blk = idx >> (l + 1)
        bmask.append(blk[:, None] == blk[None, :])
    bmask.append(idx[:, None] == idx[None, :])
    seg = np.concatenate(mats, axis=0).astype(np.float32)
    seg = np.concatenate([seg, seg], axis=1)
    return seg, np.stack(bmask).astype(np.float32), L


def _hgrn_kernel(q_ref, f_ref, i_ref, g_ref, lbp_ref, gain_ref, seg_ref, bm_ref, o_ref, st_ref,
                 k_ref, sg_ref, *, C, L, nchunk):
    @pl.when(pl.program_id(1) == 0)
    def _():
        st_ref[...] = jnp.zeros_like(st_ref)

    rowi = lax.broadcasted_iota(jnp.int32, (C, HG_DK), 0)
    nt = (((1,), (1,)), ((), ()))
    tn = (((0,), (0,)), ((), ()))

    H = range(HG_HEADS)
    css = [slice(h * HG_DK, (h + 1) * HG_DK) for h in H]

    def gates(c, k_ref, sg_ref):
        r0 = pl.multiple_of(c * C, C)
        z = f_ref[0, pl.ds(r0, C), :]
        ez = jnp.exp(-jnp.abs(z))
        y = lbp_ref[1:2, :] + (jnp.minimum(z, 0.0) - jnp.log(1.0 + ez))
        log_lb = lbp_ref[0:1, :]
        logf = (jnp.maximum(log_lb, y) + jnp.log(1.0 + jnp.exp(-jnp.abs(log_lb - y)))) * LOG2E
        k_ref[...] = lbp_ref[2:3, :] * (jnp.where(z >= 0.0, ez, 1.0) / (1.0 + ez))
        g_hi = logf.astype(BF16)
        g_lo = (logf - g_hi.astype(F32)).astype(BF16)
        sg_ref[...] = jnp.dot(seg_ref[...], jnp.concatenate([g_hi, g_lo], axis=0),
                              preferred_element_type=F32)

    def mix(c, k_ref, sg_ref):
        r0 = pl.multiple_of(c * C, C)
        qbs = [q_ref[0, pl.ds(r0, C), cs] for cs in css]
        qs = [qb.astype(F32) for qb in qbs]
        ks = [k_ref[:, cs] for cs in css]
        ss = [lax.dot_general(qbs[h], ks[h].astype(BF16), nt, preferred_element_type=F32) * bm_ref[L]
              for h in H]
        for l in range(L):
            second = (rowi & (1 << l)) != 0
            for h in H:
                e = jnp.exp2(sg_ref[(2 + l) * C:(3 + l) * C, css[h]])
                ql = jnp.where(second, qs[h] * e, 0.0).astype(BF16)
                kl = jnp.where(second, 0.0, ks[h] * e).astype(BF16)
                ss[h] = ss[h] + lax.dot_general(ql, kl, nt, preferred_element_type=F32) * bm_ref[l]
        vbs = [i_ref[0, pl.ds(r0, C), cs] for cs in css]
        bs = [sg_ref[0:C, cs] for cs in css]
        sts = [st_ref[h] for h in H]
        outs = []
        for h in H:
            qd = (qs[h] * jnp.exp2(bs[h])).astype(BF16)
            o = lax.dot_general(qd, sts[h].astype(BF16), nt, preferred_element_type=F32)
            outs.append(o + jnp.dot(ss[h].astype(BF16), vbs[h], preferred_element_type=F32))
        for h in H:
            kd = (ks[h] * jnp.exp2(sg_ref[C:2 * C, css[h]])).astype(BF16)
            st_ref[h] = (sts[h] * jnp.exp2(bs[h][C - 1:C, :])
                         + lax.dot_general(vbs[h], kd, tn, preferred_element_type=F32))
        for h in H:
            gg = g_ref[0, pl.ds(r0, C), css[h]].astype(F32)
            o = _rms_rows(outs[h], gain_ref[...]) * (gg * jax.nn.sigmoid(gg))
            o_ref[0, pl.ds(r0, C), css[h]] = o.astype(BF16)

    def chunk(c, carry):
        gates(c, k_ref, sg_ref)
        mix(c, k_ref, sg_ref)
        return carry

    lax.fori_loop(0, nchunk, chunk, 0)


def _hgrn(proj3, z3, lbp, gain, C=HG_CHUNK, ts=HG_TS):
    B, S, _ = proj3.shape
    ts = min(ts, S)
    seg, bm, L = _hgrn_consts(C)
    blk = lambda k: pl.BlockSpec((1, ts, HG_K), lambda b, s: (b, s, P_HG // HG_K + k))
    return pl.pallas_call(
        functools.partial(_hgrn_kernel, C=C, L=L, nchunk=ts // C),
        grid=(B, S // ts),
        in_specs=[
            blk(0), pl.BlockSpec((1, ts, HG_K), lambda b, s: (b, s, 0)), blk(1), blk(2),
            pl.BlockSpec((3, HG_K), lambda b, s: (0, 0)),
            pl.BlockSpec((1, HG_DV), lambda b, s: (0, 0)),
            pl.BlockSpec(seg.shape, lambda b, s: (0, 0)),
            pl.BlockSpec(bm.shape, lambda b, s: (0, 0, 0)),
        ],
        out_specs=pl.BlockSpec((1, ts, HG_V), lambda b, s: (b, s, 0)),
        out_shape=jax.ShapeDtypeStruct((B, S, HG_V), BF16),
        scratch_shapes=[pltpu.VMEM((HG_HEADS, HG_DV, HG_DK), F32), pltpu.VMEM((C, HG_K), F32),
                        pltpu.VMEM(((2 + L) * C, HG_K), F32)],
        compiler_params=_cparams(("parallel", "arbitrary")),
        name="hgrn2",
    )(proj3, z3, proj3, proj3, lbp, gain, jnp.asarray(seg, BF16), jnp.asarray(bm))


def _half_rms(x, gain2, lane):
    sq = x * x
    s_lo = jnp.sum(jnp.where(lane < NSA_DH, sq, 0.0), axis=-1, keepdims=True)
    s_hi = jnp.sum(jnp.where(lane < NSA_DH, 0.0, sq), axis=-1, keepdims=True)
    ms = jnp.where(lane < NSA_DH, s_lo, s_hi) * (1.0 / NSA_DH)
    return x * lax.rsqrt(ms + EPS) * gain2


def _gelu_tanh(x):
    return 0.5 * x * (1.0 + jnp.tanh(math.sqrt(2.0 / math.pi) * (x + 0.044715 * (x * x * x))))


def _prep_kernel(kv_ref, nq_ref, kc_ref, vc_ref, gsel_ref, gwin_ref, gcmp_ref, qg_ref, pos_ref, w1_ref,
                 w1c_ref, b1_ref, w2_ref, ksa_ref, vst_ref, kwa_ref, vwt_ref, kca_ref, vct_ref, qt_ref,
                 *, S, rows):
    lane = lax.broadcasted_iota(jnp.int32, (rows, LANES), 1)
    rowi = lax.broadcasted_iota(jnp.int32, (rows, LANES), 0)
    ones_row = (lax.broadcasted_iota(jnp.int32, (V_ROWS - NSA_DH, rows), 0) == 0).astype(F32)
    low = lane < NSA_DH

    def body(i, carry):
        r0 = pl.multiple_of(i * rows, rows)
        blk_hot = jnp.where(lane - NSA_DH == ((r0 + rowi) >> SEL_SHIFT), 1.0, 0.0)
        for src, gain_ref, aug, k_dst, v_dst, vrows in (
                (0, gsel_ref, blk_hot, ksa_ref, vst_ref, rows),
                (2, gwin_ref, jnp.zeros((rows, LANES), F32), kwa_ref, vwt_ref, WIN_KB)):
            kk = kv_ref[0, pl.ds(r0, rows), src * LANES:(src + 1) * LANES].astype(F32)
            vv = kv_ref[0, pl.ds(r0, rows), (src + 1) * LANES:(src + 2) * LANES].astype(F32)
            kn = _half_rms(kk, gain_ref[...], lane)
            k_dst[0, 0, pl.ds(r0, rows), :] = jnp.where(low, kn, aug).astype(BF16)
            k_dst[0, 1, pl.ds(r0, rows), :] = jnp.where(low, pltpu.roll(kn, NSA_DH, 1), aug).astype(BF16)
            vt = vv.T
            for g in range(NSA_KV):
                vg = jnp.concatenate([vt[g * NSA_DH:(g + 1) * NSA_DH], ones_row], axis=0).astype(BF16)
                for u in range(rows // vrows):
                    v_dst[0, g, i * (rows // vrows) + u] = vg[:, u * vrows:(u + 1) * vrows]
        for u in range(rows // Q_BLOCK):
            xt = nq_ref[0, pl.ds(r0 + u * Q_BLOCK, Q_BLOCK), :].astype(F32).T
            for g in range(NSA_KV):
                cols = []
                for h in range(NSA_HPG):
                    xh = xt[(g * NSA_HPG + h) * NSA_DH:(g * NSA_HPG + h + 1) * NSA_DH]
                    ms = jnp.mean(xh * xh, axis=0, keepdims=True)
                    cols.append(xh * lax.rsqrt(ms + EPS) * qg_ref[...])
                qt_ref[0, g, i * (rows // Q_BLOCK) + u] = jnp.concatenate(cols, axis=1).astype(BF16)
        return carry

    lax.fori_loop(0, S // rows, body, 0)

    nc = S // CMP_STRIDE
    outs = []
    for j, src_ref in ((0, kc_ref), (1, vc_ref)):
        bias = jnp.dot(pos_ref[j].astype(BF16), w1_ref[j], preferred_element_type=F32)[0:1] + b1_ref[j]
        pj = jnp.dot(src_ref[0], w1c_ref[j], preferred_element_type=F32)
        per_g = []
        for g in range(NSA_KV):
            first = pj[:, (2 * g) * CMP_HID:(2 * g + 1) * CMP_HID]
            second = pj[:, (2 * g + 1) * CMP_HID:(2 * g + 2) * CMP_HID]
            hid = _gelu_tanh(first + pltpu.roll(second, nc - 1, 0) + bias).astype(BF16)
            if j == 0:
                per_g.append(jnp.dot(hid, w2_ref[0], preferred_element_type=F32))
            else:
                per_g.append(lax.dot_general(w2_ref[1, :NSA_DH, :], hid, (((1,), (1,)), ((), ())),
                                             preferred_element_type=F32))
        outs.append(per_g)
    for g in range(NSA_KV):
        kc = outs[0][g]
        ms = jnp.sum(kc * kc, axis=-1, keepdims=True) * (1.0 / NSA_DH)
        kca_ref[0, g] = (kc * lax.rsqrt(ms + EPS) * gcmp_ref[...]).astype(BF16)
        vct_ref[0, g] = outs[1][g].astype(BF16)


def _nsa_prep(proj3, kc2, vc2, gsel, gwin, gcmp, qgain, pos8, w1, w1c, b1, w2p):
    B, S, _ = proj3.shape
    nc = S // CMP_STRIDE
    rows = SEL_CHUNK
    full = lambda shp: pl.BlockSpec(shp, lambda b: (0,) * len(shp))
    whole = lambda shp: pl.BlockSpec((1,) + shp, lambda b: (b,) + (0,) * len(shp))
    shapes = [(NSA_KV, S, LANES), (NSA_KV, S // SEL_CHUNK, V_ROWS, SEL_CHUNK),
              (NSA_KV, S, LANES), (NSA_KV, S // WIN_KB, V_ROWS, WIN_KB),
              (NSA_KV, nc, LANES), (NSA_KV, NSA_DH, nc),
              (NSA_KV, S // Q_BLOCK, NSA_DH, NSA_HPG * Q_BLOCK)]
    return pl.pallas_call(
        functools.partial(_prep_kernel, S=S, rows=rows),
        grid=(B,),
        in_specs=[
            pl.BlockSpec((1, S, 4 * NSA_KVW), lambda b: (b, 0, P_KV // (4 * NSA_KVW))),
            pl.BlockSpec((1, S, NSA_Q), lambda b: (b, 0, P_NQ // NSA_Q)),
            pl.BlockSpec((1, nc, CMP_STRIDE * NSA_KVW), lambda b: (b, 0, 0)),
            pl.BlockSpec((1, nc, CMP_STRIDE * NSA_KVW), lambda b: (b, 0, 0)),
            full((1, LANES)), full((1, LANES)), full((1, LANES)), full(qgain.shape),
            full(pos8.shape), full(w1.shape), full(w1c.shape), full(b1.shape), full(w2p.shape),
        ],
        out_specs=[whole(s) for s in shapes],
        out_shape=[jax.ShapeDtypeStruct((B,) + s, BF16) for s in shapes],
        compiler_params=_cparams(("parallel",)),
        name="nsa_prep",
    )(proj3, proj3, kc2, vc2, gsel, gwin, gcmp, qgain, pos8, w1, w1c, b1, w2p)


def _sel_table_geometry():
    last_start = (REL_BUCKETS // 2) * (REL_MAX_DIST / (REL_BUCKETS // 2)) ** (
        (REL_BUCKETS - REL_BUCKETS // 2 - 1) / (REL_BUCKETS - REL_BUCKETS // 2))
    dsat = int(math.ceil((last_start + 16 + SEL_CHUNK) / LANES)) * LANES
    return dsat, dsat + SEL_CHUNK


def _attn_kernel(q_ref, gate_ref, kca_ref, vct_ref, ksa_ref, vst_ref, kwa_ref, vwt_ref, tabc_ref,
                 tabw_ref, tabs_ref, ovl_ref, o_ref, m_ref, acc_ref, sa_ref, sb_ref, sw_ref,
                 *, nsel, dsat):
    qi = pl.program_id(2)
    QB = Q_BLOCK
    R = NSA_HPG * QB
    DH = NSA_DH

    qt = q_ref[0, 0, 0]
    qta = jnp.concatenate([qt, jnp.zeros_like(qt)], axis=0)

    vts, slabs = [], []
    for r in range(WIN_BLOCKS):
        kb = qi * (QB // WIN_KB) - WINDOW // WIN_KB + r
        kbc = jnp.maximum(kb, 0)
        kw = kwa_ref[0, 0, pl.ds(pl.multiple_of(kbc * WIN_KB, WIN_KB), WIN_KB), :]
        vts.append(vwt_ref[0, 0, kbc])
        slabs.append(pl.multiple_of(jnp.where(kb >= 0, r, WIN_BLOCKS) * WIN_KB, WIN_KB))
        sw_ref[r * WIN_KB:(r + 1) * WIN_KB, :] = jnp.dot(kw, qta, preferred_element_type=F32)

    def scores(c, q_aug):
        k0 = pl.multiple_of(c * SEL_CHUNK, SEL_CHUNK)
        delta = qi * QB - c * SEL_CHUNK
        start = pl.multiple_of(dsat - jnp.minimum(delta, dsat), LANES)
        return (jnp.dot(ksa_ref[0, 0, pl.ds(k0, SEL_CHUNK), :], q_aug, preferred_element_type=F32)
                + tabs_ref[0, 0, pl.ds(start, SEL_CHUNK), :])

    s = jnp.dot(kca_ref[0, 0], qta, preferred_element_type=F32) + tabc_ref[0, 0]
    mx = jnp.max(s, axis=0, keepdims=True)
    e = jnp.exp2(s - mx)
    inv = jnp.where(mx > 0.5 * NEG_BIG, 1.0 / jnp.sum(e, axis=0, keepdims=True), 0.0)
    p = e * inv
    o_c = jnp.dot(vct_ref[0, 0], p.astype(BF16), preferred_element_type=F32)

    psum = p[:, 0:QB] + p[:, QB:2 * QB] + p[:, 2 * QB:3 * QB] + p[:, 3 * QB:4 * QB]
    p_hi = psum.astype(BF16)
    p_lo = (psum - p_hi.astype(F32)).astype(BF16)
    imp = (jnp.dot(ovl_ref[...], p_hi, preferred_element_type=F32)
           + jnp.dot(ovl_ref[...], p_lo, preferred_element_type=F32))
    t = qi * QB + lax.broadcasted_iota(jnp.int32, (SEL_ROWS, QB), 1)
    jrow = lax.broadcasted_iota(jnp.int32, (SEL_ROWS, QB), 0)
    jcur = t >> SEL_SHIFT
    n_top = min(SEL_TOPK, nsel)
    forced = (jrow == 0) | (jrow == jcur) | (jrow == jcur - 1)
    work = jnp.where(forced, -1.0, jnp.where(jrow * SEL_BLOCK <= t, imp, -1.0))
    work = jnp.where(jrow < nsel, work, -5.0)
    jf = jrow.astype(F32)
    sel = jnp.where(forced, 1.0, 0.0)
    for _ in range(n_top - 3):
        best = jnp.max(work, axis=0, keepdims=True)
        first = jnp.min(jnp.where(work == best, jf, float(SEL_ROWS)), axis=0, keepdims=True)
        hit = jf == first
        sel = jnp.where(hit, 1.0, sel)
        work = jnp.where(hit, -3.0, work)
    sel = jnp.where(jcur < n_top, jnp.where(jrow < n_top, 1.0, 0.0), sel)
    selneg = jnp.concatenate([jnp.where(sel > 0.5, 0.0, NEG_BIG)] * NSA_HPG, axis=1)
    qts = jnp.concatenate([qt, selneg.astype(BF16)], axis=0)
    sa_ref[...] = scores(0, qts)

    sw = jnp.concatenate([sw_ref[r * WIN_KB:(r + 1) * WIN_KB, :] + tabw_ref[0, 0, pl.ds(slabs[r], WIN_KB), :]
                          for r in range(WIN_BLOCKS)], axis=0)
    ew = jnp.exp2(sw - jnp.max(sw, axis=0, keepdims=True)).astype(BF16)
    o_w = jnp.dot(jnp.concatenate(vts, axis=1), ew, preferred_element_type=F32)

    m_ref[...] = jnp.full_like(m_ref, NEG_BIG)
    acc_ref[...] = jnp.zeros_like(acc_ref)

    def consume(s_ref, c):
        sc = s_ref[...]
        m_old = m_ref[...]
        m_new = jnp.maximum(m_old, jnp.max(sc, axis=0, keepdims=True))
        alpha = jnp.exp2(m_old - m_new)
        pe = jnp.exp2(sc - m_new).astype(BF16)
        acc_ref[...] = alpha * acc_ref[...] + jnp.dot(vst_ref[0, 0, c], pe, preferred_element_type=F32)
        m_ref[...] = m_new

    n_chunks = (qi * QB + QB - 1) // SEL_CHUNK + 1
    n_pairs = (n_chunks - 1) // 2

    def sel_pair(pr, carry):
        c = 2 * pr
        sb_ref[...] = scores(c + 1, qts)
        consume(sa_ref, c)
        sa_ref[...] = scores(c + 2, qts)
        consume(sb_ref, c + 1)
        return carry

    lax.fori_loop(0, n_pairs, sel_pair, 0)

    @pl.when(n_chunks % 2 == 0)
    def _():
        sb_ref[...] = scores(n_chunks - 1, qts)
        consume(sa_ref, n_chunks - 2)
        consume(sb_ref, n_chunks - 1)

    @pl.when(n_chunks % 2 == 1)
    def _():
        consume(sa_ref, n_chunks - 1)

    acc = acc_ref[...]

    gt = jax.nn.sigmoid(gate_ref[0].astype(F32)).T
    o_s = acc[0:DH] / acc[DH:DH + 1]
    o_w = o_w[0:DH] / o_w[DH:DH + 1]
    heads = []
    for h in range(NSA_HPG):
        cs = slice(h * QB, (h + 1) * QB)
        heads.append(gt[3 * h:3 * h + 1] * o_c[:, cs] + gt[3 * h + 1:3 * h + 2] * o_s[:, cs]
                     + gt[3 * h + 2:3 * h + 3] * o_w[:, cs])
    o_ref[0] = jnp.concatenate(heads, axis=0).T.astype(BF16)


def _nsa_attn(proj3, prep, tabc, tabw, tabs, ovl, dsat):
    B, S, _ = proj3.shape
    ksa, vst, kwa, vwt, kca, vct, qt = prep
    nq = S // Q_BLOCK
    nc = S // CMP_STRIDE
    nsel = S // SEL_BLOCK
    R = NSA_HPG * Q_BLOCK
    gw = NSA_HPG * NSA_DH
    per_bg = lambda a: pl.BlockSpec((1, 1) + a.shape[2:], lambda b, g, i: (b, g) + (0,) * (a.ndim - 2))
    return pl.pallas_call(
        functools.partial(_attn_kernel, nsel=nsel, dsat=dsat),
        grid=(B, NSA_KV, nq),
        in_specs=[
            pl.BlockSpec((1, 1, 1, NSA_DH, R), lambda b, g, i: (b, g, i, 0, 0)),
            pl.BlockSpec((1, Q_BLOCK, LANES), lambda b, g, i: (b, i, P_GATE // LANES + g)),
            per_bg(kca), per_bg(vct), per_bg(ksa), per_bg(vst), per_bg(kwa), per_bg(vwt),
            pl.BlockSpec((1, 1, nc, R), lambda b, g, i: (g, i, 0, 0)),
            pl.BlockSpec((1, 1) + tabw.shape[2:], lambda b, g, i: (g, 0, 0, 0)),
            pl.BlockSpec((1, 1) + tabs.shape[2:], lambda b, g, i: (g, 0, 0, 0)),
            pl.BlockSpec(ovl.shape, lambda b, g, i: (0, 0)),
        ],
        out_specs=pl.BlockSpec((1, Q_BLOCK, gw), lambda b, g, i: (b, i, g)),
        out_shape=jax.ShapeDtypeStruct((B, S, NSA_Q), BF16),
        scratch_shapes=[pltpu.VMEM((1, R), F32), pltpu.VMEM((V_ROWS, R), F32),
                        pltpu.VMEM((SEL_CHUNK, R), F32), pltpu.VMEM((SEL_CHUNK, R), F32),
                        pltpu.VMEM((WIN_BLOCKS * WIN_KB, R), F32)],
        compiler_params=_cparams(("parallel", "parallel", "arbitrary")),
        name="nsa_attn",
    )(qt, proj3, kca, vct, ksa, vst, kwa, vwt, tabc, tabw, tabs, ovl)


def _rel_bucket(d):
    max_exact = REL_BUCKETS // 2
    d = jnp.maximum(d, 0)
    df = jnp.maximum(d, 1).astype(F32)
    large = max_exact + (jnp.log(df / max_exact) / math.log(REL_MAX_DIST / max_exact)
                         * (REL_BUCKETS - max_exact)).astype(jnp.int32)
    return jnp.where(d < max_exact, d, jnp.minimum(large, REL_BUCKETS - 1))


def _bias_kernel(rb_ref, bkt_ref, o_ref, *, rows):
    g = pl.program_id(0)
    step = 32
    for r0 in range(0, rows, step):
        bkt = bkt_ref[0, r0:r0 + step, :]
        acc = [jnp.full(bkt.shape, NEG_BIG, F32) for _ in range(NSA_HPG)]
        for b in range(REL_BUCKETS):
            hit = bkt == b
            for h in range(NSA_HPG):
                acc[h] = jnp.where(hit, rb_ref[b, g * NSA_HPG + h] * LOG2E, acc[h])
        for h in range(NSA_HPG):
            o_ref[0, 0, r0:r0 + step, h * Q_BLOCK:(h + 1) * Q_BLOCK] = acc[h]


def _bias_table(rel_bias, bucket):
    nt, rows, cols = bucket.shape
    return pl.pallas_call(
        functools.partial(_bias_kernel, rows=rows),
        grid=(NSA_KV, nt),
        in_specs=[
            pl.BlockSpec(memory_space=pltpu.SMEM),
            pl.BlockSpec((1, rows, cols), lambda g, t: (t, 0, 0)),
        ],
        out_specs=pl.BlockSpec((1, 1, rows, NSA_HPG * cols), lambda g, t: (g, t, 0, 0)),
        out_shape=jax.ShapeDtypeStruct((NSA_KV, nt, rows, NSA_HPG * cols), F32),
        compiler_params=_cparams(("parallel", "parallel")),
        name="bias_table",
    )(rel_bias, bucket)


def _bias_tables(rel_bias, S, dsat, strip_len):
    nq = S // Q_BLOCK
    nc = S // CMP_STRIDE
    n_cmp = (S - CMP_BLOCK) // CMP_STRIDE + 1
    ar = lambda n: jnp.arange(n, dtype=jnp.int32)
    bucket = lambda d, valid: jnp.where(valid, _rel_bucket(d), -1)
    i = ar(Q_BLOCK)[None, None, :]
    c = ar(nc)[None, :, None]
    d_c = ar(nq)[:, None, None] * Q_BLOCK + i - (c * CMP_STRIDE + CMP_BLOCK - 1)
    tabc = _bias_table(rel_bias, bucket(d_c, (d_c >= 0) & (c < n_cmp)))
    jw = ar((WIN_BLOCKS + 1) * WIN_KB)[None, :, None]
    d_w = i + WINDOW - jw
    tabw = _bias_table(rel_bias, bucket(d_w, (d_w >= 0) & (d_w < WINDOW) & (jw < WIN_BLOCKS * WIN_KB)))
    d_s = i - ar(strip_len)[None, :, None] + dsat
    tabs = _bias_table(rel_bias, bucket(d_s, d_s >= 0))
    return tabc, tabw, tabs


def _overlap_matrix(S):
    nc = S // CMP_STRIDE
    nsel = S // SEL_BLOCK
    n_cmp = (S - CMP_BLOCK) // CMP_STRIDE + 1
    c = np.arange(nc)[None, :]
    sel_start = (np.arange(SEL_ROWS) * SEL_BLOCK)[:, None]
    ovl = ((c * CMP_STRIDE < sel_start + SEL_BLOCK) & (c * CMP_STRIDE + CMP_BLOCK - 1 >= sel_start)
           & (c < n_cmp) & (np.arange(SEL_ROWS)[:, None] < nsel))
    return jnp.asarray(ovl, BF16)


def _layer_params(l, lower_bounds, norm_mix, w_in, hg_out_gain, nsa_qk_gain, cmp_pos, cmp_w1, cmp_b1,
                  cmp_w2, w_branch_a, w_branch_b, w_out, norm_mlp, w_up, w_down):
    splits = (HG_K, HG_K, HG_V, HG_V, NSA_Q, NSA_KVW, NSA_KVW, NSA_KVW, NSA_KVW, NSA_KVW, NSA_KVW,
              3 * NSA_HEADS, D_MODEL, D_MODEL)
    offs = np.concatenate([[0], np.cumsum(splits)])
    col = lambda k: w_in[l][:, offs[k]:offs[k + 1]]
    gate_w = col(11)
    zpad = jnp.zeros((D_MODEL, LANES - 3 * NSA_HPG), F32)
    gate_cols = []
    for g in range(NSA_KV):
        gate_cols += [gate_w[:, g * 3 * NSA_HPG:(g + 1) * 3 * NSA_HPG], zpad]
    w_main = jnp.concatenate([col(12), col(13), col(0), col(2), col(3),
                              col(7), col(8), col(9), col(10), col(4)] + gate_cols, axis=1).astype(BF16)
    w_cmp = jnp.concatenate([col(1), col(5), col(6)], axis=1).astype(BF16)

    lb = lower_bounds[l]
    lbp = jnp.stack([jnp.log(lb), jnp.log1p(-lb), 1.0 - lb])

    w1 = cmp_w1[l].reshape(2, 2, CMP_STRIDE, NSA_DH, CMP_HID)
    w1c = jnp.zeros((2, CMP_STRIDE, NSA_KV, NSA_DH, NSA_KV, 2, CMP_HID), F32)
    for g in range(NSA_KV):
        w1c = w1c.at[:, :, g, :, g, :, :].set(w1.transpose(0, 2, 3, 1, 4))
    w1c = w1c.reshape(2, CMP_STRIDE * NSA_KVW, NSA_KV * 2 * CMP_HID).astype(BF16)
    pos8 = jnp.zeros((2, 8, CMP_BLOCK * NSA_DH), F32).at[:, 0, :].set(cmp_pos[l].reshape(2, -1))
    w2p = jnp.zeros((2, CMP_HID, LANES), F32)
    w2p = w2p.at[0, :, :NSA_DH].set(cmp_w2[l, 0]).at[1, :NSA_DH, :].set(cmp_w2[l, 1].T).astype(BF16)
    qk = nsa_qk_gain[l]
    z64 = jnp.zeros((NSA_DH,), F32)
    return dict(
        norm_mix=norm_mix[l][None, :], w_main=w_main, w_cmp=w_cmp, lbp=lbp,
        hg_gain=hg_out_gain[l][None, :],
        gsel=jnp.concatenate([qk[2], qk[2]])[None, :], gwin=jnp.concatenate([qk[3], qk[3]])[None, :],
        gcmp=jnp.concatenate([qk[1], z64])[None, :],
        qgain=jnp.broadcast_to((qk[0] * (NSA_DH ** -0.5 * LOG2E))[:, None], (NSA_DH, Q_BLOCK)),
        pos8=pos8, w1=cmp_w1[l].astype(BF16), w1c=w1c, b1=cmp_b1[l][:, None, :], w2p=w2p,
        wa=w_branch_a[l].astype(BF16), wb=w_branch_b[l].astype(BF16), wo=w_out[l].astype(BF16),
        norm_mlp=norm_mlp[l][None, :], wu=w_up[l].astype(BF16), wd=w_down[l].astype(BF16))


def kernel(x, rel_bias, hg_lb_logits, norm_mix, w_in, hg_out_gain, nsa_qk_gain, cmp_pos, cmp_w1, cmp_b1, cmp_w2, w_branch_a, w_branch_b, w_out, norm_mlp, w_up, w_down):
    B, S, D = x.shape
    T = B * S
    depth = w_in.shape[0]
    assert D == D_MODEL and S % SEL_CHUNK == 0 and S >= WINDOW and S // SEL_BLOCK <= SEL_ROWS
    assert S % Q_BLOCK == 0 and Q_BLOCK % WIN_KB == 0 and WINDOW % WIN_KB == 0
    assert FORCE_BONUS > NSA_HPG and SEL_TOPK >= 3
    lb_cum = jnp.cumsum(jax.nn.softmax(hg_lb_logits.astype(F32), axis=0), axis=0)
    lower_bounds = lb_cum - lb_cum[0:1]

    dsat, strip_len = _sel_table_geometry()
    tabc, tabw, tabs = _bias_tables(rel_bias, S, dsat, strip_len)
    ovl = _overlap_matrix(S)
    nc = S // CMP_STRIDE
    tm = min(1024, T)

    x2 = x.reshape(T, D)
    for l in range(depth):
        p = _layer_params(l, lower_bounds, norm_mix, w_in, hg_out_gain, nsa_qk_gain, cmp_pos, cmp_w1,
                          cmp_b1, cmp_w2, w_branch_a, w_branch_b, w_out, norm_mlp, w_up, w_down)
        proj, z, kc, vc = _rms_proj(x2, p["norm_mix"], p["w_main"], p["w_cmp"], tm, P_TILE)
        proj3 = proj.reshape(B, S, P_COLS)
        kc2 = kc.reshape(B, nc, CMP_STRIDE * NSA_KVW)
        vc2 = vc.reshape(B, nc, CMP_STRIDE * NSA_KVW)
        o_a = _hgrn(proj3, z.reshape(B, S, HG_K), p["lbp"], p["hg_gain"])
        prep = _nsa_prep(proj3, kc2, vc2, p["gsel"], p["gwin"], p["gcmp"], p["qgain"], p["pos8"], p["w1"],
                         p["w1c"], p["b1"], p["w2p"])
        o_b = _nsa_attn(proj3, prep, tabc, tabw, tabs, ovl, dsat)
        x2 = _merge(x2, o_a.reshape(T, HG_V), o_b.reshape(T, NSA_Q), proj, p["wa"], p["wb"], p["wo"],
                    min(512, T))
        x2 = _mlp(x2, p["norm_mlp"], p["wu"], p["wd"], tm, 1024)
    return x2.reshape(B, S, D)
```

```python
import functools
import math

import numpy as np
import jax
import jax.numpy as jnp
from jax import lax
from jax.experimental import pallas as pl
from jax.experimental.pallas import tpu as pltpu

F32 = jnp.float32
BF16 = jnp.bfloat16

D_MODEL = 1024
HG_HEADS = 4
HG_DK = 128
HG_DV = 128
HG_K = HG_HEADS * HG_DK
HG_V = HG_HEADS * HG_DV
NSA_HEADS = 8
NSA_KV = 2
NSA_DH = 64
NSA_HPG = NSA_HEADS // NSA_KV
NSA_Q = NSA_HEADS * NSA_DH
NSA_KVW = NSA_KV * NSA_DH
CMP_BLOCK = 32
CMP_STRIDE = 16
CMP_HID = 128
SEL_BLOCK = 64
SEL_TOPK = 8
WINDOW = 512
FORCE_BONUS = 1000.0
REL_BUCKETS = 32
REL_MAX_DIST = 1024
D_FF = 4 * D_MODEL
EPS = 1e-6
NEG_BIG = -1e30
LOG2E = math.log2(math.e)

LANES = 128
VMEM_LIMIT = 56 * 1024 * 1024

P_GAB = 0
P_HG = P_GAB + 2 * D_MODEL
P_KV = P_HG + 3 * HG_K
P_NQ = P_KV + 4 * NSA_KVW
P_GATE = P_NQ + NSA_Q
P_COLS = P_GATE + NSA_KV * LANES
P_TILE = P_COLS // 2

HG_CHUNK = 128
HG_TS = 2048
Q_BLOCK = 256
WIN_KB = 128
SEL_CHUNK = 512
WIN_BLOCKS = (WINDOW + Q_BLOCK) // WIN_KB
SEL_SHIFT = int(math.log2(SEL_BLOCK))
SEL_ROWS = LANES - NSA_DH
V_ROWS = NSA_DH + 16


def _cparams(sem):
    return pltpu.CompilerParams(dimension_semantics=sem, vmem_limit_bytes=VMEM_LIMIT)


def _rms_rows(x, gain):
    ms = jnp.mean(x * x, axis=-1, keepdims=True)
    return x * lax.rsqrt(ms + EPS) * gain


def _proj_kernel(x_ref, g_ref, w_ref, wx_ref, o_ref, z_ref, kc_ref, vc_ref, h_ref, ck_ref, cv_ref):
    @pl.when(pl.program_id(1) == 0)
    def _():
        h = _rms_rows(x_ref[...], g_ref[...]).astype(BF16)
        h_ref[...] = h
        c = jnp.dot(h, wx_ref[...], preferred_element_type=F32)
        z_ref[...] = c[:, :HG_K]
        ck_ref[...] = c[:, HG_K:HG_K + NSA_KVW]
        cv_ref[...] = c[:, HG_K + NSA_KVW:]
        rows = ck_ref.shape[0] // CMP_STRIDE
        for t in range(CMP_STRIDE):
            lanes = slice(t * NSA_KVW, (t + 1) * NSA_KVW)
            kc_ref[:, lanes] = ck_ref[pl.ds(t, rows, stride=CMP_STRIDE), :].astype(BF16)
            vc_ref[:, lanes] = cv_ref[pl.ds(t, rows, stride=CMP_STRIDE), :].astype(BF16)

    o_ref[...] = jnp.dot(h_ref[...], w_ref[...], preferred_element_type=F32).astype(BF16)


def _rms_proj(x2, gain, w, wx, tm, tn):
    T, D = x2.shape
    N = w.shape[1]
    cmp_sds = jax.ShapeDtypeStruct((T // CMP_STRIDE, CMP_STRIDE * NSA_KVW), BF16)
    return pl.pallas_call(
        _proj_kernel,
        grid=(T // tm, N // tn),
        in_specs=[
            pl.BlockSpec((tm, D), lambda i, j: (i, 0)),
            pl.BlockSpec((1, D), lambda i, j: (0, 0)),
            pl.BlockSpec((D, tn), lambda i, j: (0, j)),
            pl.BlockSpec(wx.shape, lambda i, j: (0, 0)),
        ],
        out_specs=[
            pl.BlockSpec((tm, tn), lambda i, j: (i, j)),
            pl.BlockSpec((tm, HG_K), lambda i, j: (i, 0)),
            pl.BlockSpec((tm // CMP_STRIDE, CMP_STRIDE * NSA_KVW), lambda i, j: (i, 0)),
            pl.BlockSpec((tm // CMP_STRIDE, CMP_STRIDE * NSA_KVW), lambda i, j: (i, 0)),
        ],
        out_shape=[jax.ShapeDtypeStruct((T, N), BF16), jax.ShapeDtypeStruct((T, HG_K), F32),
                   cmp_sds, cmp_sds],
        scratch_shapes=[pltpu.VMEM((tm, D), BF16), pltpu.VMEM((tm, NSA_KVW), F32),
                        pltpu.VMEM((tm, NSA_KVW), F32)],
        compiler_params=_cparams(("parallel", "arbitrary")),
        name="rms_proj",
    )(x2, gain, w, wx)


def _mlp_kernel(x_ref, g_ref, wu_ref, wd_ref, o_ref, h_ref):
    f = pl.program_id(1)

    @pl.when(f == 0)
    def _():
        x = x_ref[...]
        h_ref[...] = _rms_rows(x, g_ref[...]).astype(BF16)
        o_ref[...] = x

    u = jnp.dot(h_ref[...], wu_ref[...], preferred_element_type=F32)
    u = jnp.square(jnp.maximum(u, 0.0)).astype(BF16)
    o_ref[...] += jnp.dot(u, wd_ref[...], preferred_element_type=F32)


def _mlp(x2, gain, wu, wd, tm, tf):
    T, D = x2.shape
    FF = wu.shape[1]
    return pl.pallas_call(
        _mlp_kernel,
        grid=(T // tm, FF // tf),
        in_specs=[
            pl.BlockSpec((tm, D), lambda i, f: (i, 0)),
            pl.BlockSpec((1, D), lambda i, f: (0, 0)),
            pl.BlockSpec((D, tf), lambda i, f: (0, f)),
            pl.BlockSpec((tf, D), lambda i, f: (f, 0)),
        ],
        out_specs=pl.BlockSpec((tm, D), lambda i, f: (i, 0)),
        out_shape=jax.ShapeDtypeStruct((T, D), F32),
        scratch_shapes=[pltpu.VMEM((tm, D), BF16)],
        compiler_params=_cparams(("parallel", "arbitrary")),
        name="mlp",
    )(x2, gain, wu, wd)


def _merge_kernel(x_ref, oa_ref, ob_ref, ga_ref, gb_ref, wa_ref, wb_ref, wo_ref, o_ref):
    ya = jnp.dot(oa_ref[...], wa_ref[...], preferred_element_type=F32)
    yb = jnp.dot(ob_ref[...], wb_ref[...], preferred_element_type=F32)
    mixed = (jax.nn.sigmoid(ga_ref[...].astype(F32)) * ya
             + jax.nn.sigmoid(gb_ref[...].astype(F32)) * yb)
    o_ref[...] = x_ref[...] + jnp.dot(mixed.astype(BF16), wo_ref[...], preferred_element_type=F32)


def _merge(x2, oa, ob, proj, wa, wb, wo, tm):
    T, D = x2.shape
    ga_blk = P_GAB // D
    return pl.pallas_call(
        _merge_kernel,
        grid=(T // tm,),
        in_specs=[
            pl.BlockSpec((tm, D), lambda i: (i, 0)),
            pl.BlockSpec((tm, HG_V), lambda i: (i, 0)),
            pl.BlockSpec((tm, NSA_Q), lambda i: (i, 0)),
            pl.BlockSpec((tm, D), lambda i: (i, ga_blk)),
            pl.BlockSpec((tm, D), lambda i: (i, ga_blk + 1)),
            pl.BlockSpec((HG_V, D), lambda i: (0, 0)),
            pl.BlockSpec((NSA_Q, D), lambda i: (0, 0)),
            pl.BlockSpec((D, D), lambda i: (0, 0)),
        ],
        out_specs=pl.BlockSpec((tm, D), lambda i: (i, 0)),
        out_shape=jax.ShapeDtypeStruct((T, D), F32),
        compiler_params=_cparams(("parallel",)),
        name="merge",
    )(x2, oa, ob, proj, proj, wa, wb, wo)


def _hgrn_consts(C):
    L = int(math.log2(C))
    idx = np.arange(C)
    mats = [(idx[None, :] <= idx[:, None]),
            (idx[None, :] > idx[:, None])]
    bmask = []
    for l in range(L):
        m = 1 << l
        r = (idx & ~(2 * m - 1)) + m
        lo = np.minimum(idx, r)[:, None]
        hi = np.maximum(idx, r)[:, None]
        mats.append((idx[None, :] > lo) & (idx[None, :] <= hi))
        blk = idx >> (l + 1)
        bmask.append(blk[:, None] == blk[None, :])
    bmask.append(idx[:, None] == idx[None, :])
    seg = np.concatenate(mats, axis=0).astype(np.float32)
    seg = np.concatenate([seg, seg], axis=1)
    return seg, np.stack(bmask).astype(np.float32), L


def _hgrn_kernel(q_ref, f_ref, i_ref, g_ref, lbp_ref, gain_ref, seg_ref, bm_ref, o_ref, st_ref,
                 k0_ref, sg0_ref, k1_ref, sg1_ref, *, C, L, nchunk):
    @pl.when(pl.program_id(1) == 0)
    def _():
        st_ref[...] = jnp.zeros_like(st_ref)

    rowi = lax.broadcasted_iota(jnp.int32, (C, HG_DK), 0)
    nt = (((1,), (1,)), ((), ()))
    tn = (((0,), (0,)), ((), ()))

    H = range(HG_HEADS)
    css = [slice(h * HG_DK, (h + 1) * HG_DK) for h in H]

    def gates(c, k_ref, sg_ref):
        r0 = pl.multiple_of(c * C, C)
        z = f_ref[0, pl.ds(r0, C), :]
        ez = jnp.exp(-jnp.abs(z))
        y = lbp_ref[1:2, :] + (jnp.minimum(z, 0.0) - jnp.log(1.0 + ez))
        log_lb = lbp_ref[0:1, :]
        logf = (jnp.maximum(log_lb, y) + jnp.log(1.0 + jnp.exp(-jnp.abs(log_lb - y)))) * LOG2E
        k_ref[...] = lbp_ref[2:3, :] * (jnp.where(z >= 0.0, ez, 1.0) / (1.0 + ez))
        g_hi = logf.astype(BF16)
        g_lo = (logf - g_hi.astype(F32)).astype(BF16)
        sg_ref[...] = jnp.dot(seg_ref[...], jnp.concatenate([g_hi, g_lo], axis=0),
                              preferred_element_type=F32)

    def mix(c, k_ref, sg_ref):
        r0 = pl.multiple_of(c * C, C)
        qbs = [q_ref[0, pl.ds(r0, C), cs] for cs in css]
        qs = [qb.astype(F32) for qb in qbs]
        ks = [k_ref[:, cs] for cs in css]
        ss = [lax.dot_general(qbs[h], ks[h].astype(BF16), nt, preferred_element_type=F32) * bm_ref[L]
              for h in H]
        for l in range(L):
            second = (rowi & (1 << l)) != 0
            for h in H:
                e = jnp.exp2(sg_ref[(2 + l) * C:(3 + l) * C, css[h]])
                ql = jnp.where(second, qs[h] * e, 0.0).astype(BF16)
                kl = jnp.where(second, 0.0, ks[h] * e).astype(BF16)
                ss[h] = ss[h] + lax.dot_general(ql, kl, nt, preferred_element_type=F32) * bm_ref[l]
        return r0, qs, ks, ss

    def mix_out(scores, k_ref, sg_ref):
        r0, qs, ks, ss = scores
        vbs = [i_ref[0, pl.ds(r0, C), cs] for cs in css]
        bs = [sg_ref[0:C, cs] for cs in css]
        sts = [st_ref[h] for h in H]
        outs = []
        for h in H:
            qd = (qs[h] * jnp.exp2(bs[h])).astype(BF16)
            o = lax.dot_general(qd, sts[h].astype(BF16), nt, preferred_element_type=F32)
            outs.append(o + jnp.dot(ss[h].astype(BF16), vbs[h], preferred_element_type=F32))
        for h in H:
            kd = (ks[h] * jnp.exp2(sg_ref[C:2 * C, css[h]])).astype(BF16)
            st_ref[h] = (sts[h] * jnp.exp2(bs[h][C - 1:C, :])
                         + lax.dot_general(vbs[h], kd, tn, preferred_element_type=F32))
        for h in H:
            gg = g_ref[0, pl.ds(r0, C), css[h]].astype(F32)
            o = _rms_rows(outs[h], gain_ref[...]) * (gg * jax.nn.sigmoid(gg))
            o_ref[0, pl.ds(r0, C), css[h]] = o.astype(BF16)

    gates(0, k0_ref, sg0_ref)

    def pair(pr, carry):
        c = 2 * pr
        sc0 = mix(c, k0_ref, sg0_ref)
        gates(c + 1, k1_ref, sg1_ref)
        mix_out(sc0, k0_ref, sg0_ref)
        sc1 = mix(c + 1, k1_ref, sg1_ref)
        gates(jnp.minimum(c + 2, nchunk - 1), k0_ref, sg0_ref)
        mix_out(sc1, k1_ref, sg1_ref)
        return carry

    lax.fori_loop(0, nchunk // 2, pair, 0)


def _hgrn(proj3, z3, lbp, gain, C=HG_CHUNK, ts=HG_TS):
    B, S, _ = proj3.shape
    ts = min(ts, S)
    assert (ts // C) % 2 == 0
    seg, bm, L = _hgrn_consts(C)
    blk = lambda k: pl.BlockSpec((1, ts, HG_K), lambda b, s: (b, s, P_HG // HG_K + k))
    return pl.pallas_call(
        functools.partial(_hgrn_kernel, C=C, L=L, nchunk=ts // C),
        grid=(B, S // ts),
        in_specs=[
            blk(0), pl.BlockSpec((1, ts, HG_K), lambda b, s: (b, s, 0)), blk(1), blk(2),
            pl.BlockSpec((3, HG_K), lambda b, s: (0, 0)),
            pl.BlockSpec((1, HG_DV), lambda b, s: (0, 0)),
            pl.BlockSpec(seg.shape, lambda b, s: (0, 0)),
            pl.BlockSpec(bm.shape, lambda b, s: (0, 0, 0)),
        ],
        out_specs=pl.BlockSpec((1, ts, HG_V), lambda b, s: (b, s, 0)),
        out_shape=jax.ShapeDtypeStruct((B, S, HG_V), BF16),
        scratch_shapes=[pltpu.VMEM((HG_HEADS, HG_DV, HG_DK), F32)]
        + 2 * [pltpu.VMEM((C, HG_K), F32), pltpu.VMEM(((2 + L) * C, HG_K), F32)],
        compiler_params=_cparams(("parallel", "arbitrary")),
        name="hgrn2",
    )(proj3, z3, proj3, proj3, lbp, gain, jnp.asarray(seg, BF16), jnp.asarray(bm))


def _half_rms(x, gain2, lane):
    sq = x * x
    s_lo = jnp.sum(jnp.where(lane < NSA_DH, sq, 0.0), axis=-1, keepdims=True)
    s_hi = jnp.sum(jnp.where(lane < NSA_DH, 0.0, sq), axis=-1, keepdims=True)
    ms = jnp.where(lane < NSA_DH, s_lo, s_hi) * (1.0 / NSA_DH)
    return x * lax.rsqrt(ms + EPS) * gain2


def _gelu_tanh(x):
    return 0.5 * x * (1.0 + jnp.tanh(math.sqrt(2.0 / math.pi) * (x + 0.044715 * (x * x * x))))


def _prep_kernel(kv_ref, nq_ref, kc_ref, vc_ref, gsel_ref, gwin_ref, gcmp_ref, qg_ref, pos_ref, w1_ref,
                 w1c_ref, b1_ref, w2_ref, ksa_ref, vst_ref, kwa_ref, vwt_ref, kca_ref, vct_ref, qt_ref,
                 *, S, rows):
    lane = lax.broadcasted_iota(jnp.int32, (rows, LANES), 1)
    rowi = lax.broadcasted_iota(jnp.int32, (rows, LANES), 0)
    ones_row = (lax.broadcasted_iota(jnp.int32, (V_ROWS - NSA_DH, rows), 0) == 0).astype(F32)
    low = lane < NSA_DH

    def body(i, carry):
        r0 = pl.multiple_of(i * rows, rows)
        blk_hot = jnp.where(lane - NSA_DH == ((r0 + rowi) >> SEL_SHIFT), 1.0, 0.0)
        for src, gain_ref, aug, k_dst, v_dst, vrows in (
                (0, gsel_ref, blk_hot, ksa_ref, vst_ref, rows),
                (2, gwin_ref, jnp.zeros((rows, LANES), F32), kwa_ref, vwt_ref, WIN_KB)):
            kk = kv_ref[0, pl.ds(r0, rows), src * LANES:(src + 1) * LANES].astype(F32)
            vv = kv_ref[0, pl.ds(r0, rows), (src + 1) * LANES:(src + 2) * LANES].astype(F32)
            kn = _half_rms(kk, gain_ref[...], lane)
            k_dst[0, 0, pl.ds(r0, rows), :] = jnp.where(low, kn, aug).astype(BF16)
            k_dst[0, 1, pl.ds(r0, rows), :] = jnp.where(low, pltpu.roll(kn, NSA_DH, 1), aug).astype(BF16)
            vt = vv.T
            for g in range(NSA_KV):
                vg = jnp.concatenate([vt[g * NSA_DH:(g + 1) * NSA_DH], ones_row], axis=0).astype(BF16)
                for u in range(rows // vrows):
                    v_dst[0, g, i * (rows // vrows) + u] = vg[:, u * vrows:(u + 1) * vrows]
        for u in range(rows // Q_BLOCK):
            xt = nq_ref[0, pl.ds(r0 + u * Q_BLOCK, Q_BLOCK), :].astype(F32).T
            for g in range(NSA_KV):
                cols = []
                for h in range(NSA_HPG):
                    xh = xt[(g * NSA_HPG + h) * NSA_DH:(g * NSA_HPG + h + 1) * NSA_DH]
                    ms = jnp.mean(xh * xh, axis=0, keepdims=True)
                    cols.append(xh * lax.rsqrt(ms + EPS) * qg_ref[...])
                qt_ref[0, g, i * (rows // Q_BLOCK) + u] = jnp.concatenate(cols, axis=1).astype(BF16)
        return carry

    lax.fori_loop(0, S // rows, body, 0)

    nc = S // CMP_STRIDE
    outs = []
    for j, src_ref in ((0, kc_ref), (1, vc_ref)):
        bias = jnp.dot(pos_ref[j].astype(BF16), w1_ref[j], preferred_element_type=F32)[0:1] + b1_ref[j]
        pj = jnp.dot(src_ref[0], w1c_ref[j], preferred_element_type=F32)
        per_g = []
        for g in range(NSA_KV):
            first = pj[:, (2 * g) * CMP_HID:(2 * g + 1) * CMP_HID]
            second = pj[:, (2 * g + 1) * CMP_HID:(2 * g + 2) * CMP_HID]
            hid = _gelu_tanh(first + pltpu.roll(second, nc - 1, 0) + bias).astype(BF16)
            if j == 0:
                per_g.append(jnp.dot(hid, w2_ref[0], preferred_element_type=F32))
            else:
                per_g.append(lax.dot_general(w2_ref[1, :NSA_DH, :], hid, (((1,), (1,)), ((), ())),
                                             preferred_element_type=F32))
        outs.append(per_g)
    for g in range(NSA_KV):
        kc = outs[0][g]
        ms = jnp.sum(kc * kc, axis=-1, keepdims=True) * (1.0 / NSA_DH)
        kca_ref[0, g] = (kc * lax.rsqrt(ms + EPS) * gcmp_ref[...]).astype(BF16)
        vct_ref[0, g] = outs[1][g].astype(BF16)


def _nsa_prep(proj3, kc2, vc2, gsel, gwin, gcmp, qgain, pos8, w1, w1c, b1, w2p):
    B, S, _ = proj3.shape
    nc = S // CMP_STRIDE
    rows = SEL_CHUNK
    full = lambda shp: pl.BlockSpec(shp, lambda b: (0,) * len(shp))
    whole = lambda shp: pl.BlockSpec((1,) + shp, lambda b: (b,) + (0,) * len(shp))
    shapes = [(NSA_KV, S, LANES), (NSA_KV, S // SEL_CHUNK, V_ROWS, SEL_CHUNK),
              (NSA_KV, S, LANES), (NSA_KV, S // WIN_KB, V_ROWS, WIN_KB),
              (NSA_KV, nc, LANES), (NSA_KV, NSA_DH, nc),
              (NSA_KV, S // Q_BLOCK, NSA_DH, NSA_HPG * Q_BLOCK)]
    return pl.pallas_call(
        functools.partial(_prep_kernel, S=S, rows=rows),
        grid=(B,),
        in_specs=[
            pl.BlockSpec((1, S, 4 * NSA_KVW), lambda b: (b, 0, P_KV // (4 * NSA_KVW))),
            pl.BlockSpec((1, S, NSA_Q), lambda b: (b, 0, P_NQ // NSA_Q)),
            pl.BlockSpec((1, nc, CMP_STRIDE * NSA_KVW), lambda b: (b, 0, 0)),
            pl.BlockSpec((1, nc, CMP_STRIDE * NSA_KVW), lambda b: (b, 0, 0)),
            full((1, LANES)), full((1, LANES)), full((1, LANES)), full(qgain.shape),
            full(pos8.shape), full(w1.shape), full(w1c.shape), full(b1.shape), full(w2p.shape),
        ],
        out_specs=[whole(s) for s in shapes],
        out_shape=[jax.ShapeDtypeStruct((B,) + s, BF16) for s in shapes],
        compiler_params=_cparams(("parallel",)),
        name="nsa_prep",
    )(proj3, proj3, kc2, vc2, gsel, gwin, gcmp, qgain, pos8, w1, w1c, b1, w2p)


def _sel_table_geometry():
    last_start = (REL_BUCKETS // 2) * (REL_MAX_DIST / (REL_BUCKETS // 2)) ** (
        (REL_BUCKETS - REL_BUCKETS // 2 - 1) / (REL_BUCKETS - REL_BUCKETS // 2))
    dsat = int(math.ceil((last_start + 16 + SEL_CHUNK) / LANES)) * LANES
    return dsat, dsat + SEL_CHUNK


def _attn_kernel(q_ref, gate_ref, kca_ref, vct_ref, ksa_ref, vst_ref, kwa_ref, vwt_ref, tabc_ref,
                 tabw_ref, tabs_ref, ovl_ref, o_ref, m_ref, acc_ref, sa_ref, sb_ref, sw_ref,
                 *, nsel, dsat):
    qi = pl.program_id(2)
    QB = Q_BLOCK
    R = NSA_HPG * QB
    DH = NSA_DH

    qt = q_ref[0, 0, 0]
    qta = jnp.concatenate([qt, jnp.zeros_like(qt)], axis=0)

    vts, slabs = [], []
    for r in range(WIN_BLOCKS):
        kb = qi * (QB // WIN_KB) - WINDOW // WIN_KB + r
        kbc = jnp.maximum(kb, 0)
        kw = kwa_ref[0, 0, pl.ds(pl.multiple_of(kbc * WIN_KB, WIN_KB), WIN_KB), :]
        vts.append(vwt_ref[0, 0, kbc])
        slabs.append(pl.multiple_of(jnp.where(kb >= 0, r, WIN_BLOCKS) * WIN_KB, WIN_KB))
        sw_ref[r * WIN_KB:(r + 1) * WIN_KB, :] = jnp.dot(kw, qta, preferred_element_type=F32)

    def scores(c, q_aug):
        k0 = pl.multiple_of(c * SEL_CHUNK, SEL_CHUNK)
        delta = qi * QB - c * SEL_CHUNK
        start = pl.multiple_of(dsat - jnp.minimum(delta, dsat), LANES)
        return (jnp.dot(ksa_ref[0, 0, pl.ds(k0, SEL_CHUNK), :], q_aug, preferred_element_type=F32)
                + tabs_ref[0, 0, pl.ds(start, SEL_CHUNK), :])

    s = jnp.dot(kca_ref[0, 0], qta, preferred_element_type=F32) + tabc_ref[0, 0]
    mx = jnp.max(s, axis=0, keepdims=True)
    e = jnp.exp2(s - mx)
    inv = jnp.where(mx > 0.5 * NEG_BIG, 1.0 / jnp.sum(e, axis=0, keepdims=True), 0.0)
    p = e * inv
    o_c = jnp.dot(vct_ref[0, 0], p.astype(BF16), preferred_element_type=F32)

    psum = p[:, 0:QB] + p[:, QB:2 * QB] + p[:, 2 * QB:3 * QB] + p[:, 3 * QB:4 * QB]
    p_hi = psum.astype(BF16)
    p_lo = (psum - p_hi.astype(F32)).astype(BF16)
    imp = (jnp.dot(ovl_ref[...], p_hi, preferred_element_type=F32)
           + jnp.dot(ovl_ref[...], p_lo, preferred_element_type=F32))
    t = qi * QB + lax.broadcasted_iota(jnp.int32, (SEL_ROWS, QB), 1)
    jrow = lax.broadcasted_iota(jnp.int32, (SEL_ROWS, QB), 0)
    jcur = t >> SEL_SHIFT
    n_top = min(SEL_TOPK, nsel)
    forced = (jrow == 0) | (jrow == jcur) | (jrow == jcur - 1)
    work = jnp.where(forced, -1.0, jnp.where(jrow * SEL_BLOCK <= t, imp, -1.0))
    work = jnp.where(jrow < nsel, work, -5.0)
    jf = jrow.astype(F32)
    sel = jnp.where(forced, 1.0, 0.0)
    for _ in range(n_top - 3):
        best = jnp.max(work, axis=0, keepdims=True)
        first = jnp.min(jnp.where(work == best, jf, float(SEL_ROWS)), axis=0, keepdims=True)
        hit = jf == first
        sel = jnp.where(hit, 1.0, sel)
        work = jnp.where(hit, -3.0, work)
    sel = jnp.where(jcur < n_top, jnp.where(jrow < n_top, 1.0, 0.0), sel)
    selneg = jnp.concatenate([jnp.where(sel > 0.5, 0.0, NEG_BIG)] * NSA_HPG, axis=1)
    qts = jnp.concatenate([qt, selneg.astype(BF16)], axis=0)
    sa_ref[...] = scores(0, qts)

    sw = jnp.concatenate([sw_ref[r * WIN_KB:(r + 1) * WIN_KB, :] + tabw_ref[0, 0, pl.ds(slabs[r], WIN_KB), :]
                          for r in range(WIN_BLOCKS)], axis=0)
    ew = jnp.exp2(sw - jnp.max(sw, axis=0, keepdims=True)).astype(BF16)
    o_w = jnp.dot(jnp.concatenate(vts, axis=1), ew, preferred_element_type=F32)

    m_ref[...] = jnp.full_like(m_ref, NEG_BIG)
    acc_ref[...] = jnp.zeros_like(acc_ref)

    def consume(s_ref, c):
        vt = vst_ref[0, 0, c]
        for h in range(NSA_HPG):
            cs = slice(h * QB, (h + 1) * QB)
            sc = s_ref[:, cs]
            m_old = m_ref[:, cs]
            m_new = jnp.maximum(m_old, jnp.max(sc, axis=0, keepdims=True))
            alpha = jnp.exp2(m_old - m_new)
            pe = jnp.exp2(sc - m_new).astype(BF16)
            acc_ref[:, cs] = alpha * acc_ref[:, cs] + jnp.dot(vt, pe, preferred_element_type=F32)
            m_ref[:, cs] = m_new

    n_chunks = (qi * QB + QB - 1) // SEL_CHUNK + 1
    n_pairs = (n_chunks - 1) // 2

    def sel_pair(pr, carry):
        c = 2 * pr
        sb_ref[...] = scores(c + 1, qts)
        consume(sa_ref, c)
        sa_ref[...] = scores(c + 2, qts)
        consume(sb_ref, c + 1)
        return carry

    lax.fori_loop(0, n_pairs, sel_pair, 0)

    @pl.when(n_chunks % 2 == 0)
    def _():
        sb_ref[...] = scores(n_chunks - 1, qts)
        consume(sa_ref, n_chunks - 2)
        consume(sb_ref, n_chunks - 1)

    @pl.when(n_chunks % 2 == 1)
    def _():
        consume(sa_ref, n_chunks - 1)

    acc = acc_ref[...]

    gt = jax.nn.sigmoid(gate_ref[0].astype(F32)).T
    o_s = acc[0:DH] / acc[DH:DH + 1]
    o_w = o_w[0:DH] / o_w[DH:DH + 1]
    heads = []
    for h in range(NSA_HPG):
        cs = slice(h * QB, (h + 1) * QB)
        heads.append(gt[3 * h:3 * h + 1] * o_c[:, cs] + gt[3 * h + 1:3 * h + 2] * o_s[:, cs]
                     + gt[3 * h + 2:3 * h + 3] * o_w[:, cs])
    o_ref[0] = jnp.concatenate(heads, axis=0).T.astype(BF16)


def _nsa_attn(proj3, prep, tabc, tabw, tabs, ovl, dsat):
    B, S, _ = proj3.shape
    ksa, vst, kwa, vwt, kca, vct, qt = prep
    nq = S // Q_BLOCK
    nc = S // CMP_STRIDE
    nsel = S // SEL_BLOCK
    R = NSA_HPG * Q_BLOCK
    gw = NSA_HPG * NSA_DH
    per_bg = lambda a: pl.BlockSpec((1, 1) + a.shape[2:], lambda b, g, i: (b, g) + (0,) * (a.ndim - 2))
    return pl.pallas_call(
        functools.partial(_attn_kernel, nsel=nsel, dsat=dsat),
        grid=(B, NSA_KV, nq),
        in_specs=[
            pl.BlockSpec((1, 1, 1, NSA_DH, R), lambda b, g, i: (b, g, i, 0, 0)),
            pl.BlockSpec((1, Q_BLOCK, LANES), lambda b, g, i: (b, i, P_GATE // LANES + g)),
            per_bg(kca), per_bg(vct), per_bg(ksa), per_bg(vst), per_bg(kwa), per_bg(vwt),
            pl.BlockSpec((1, 1, nc, R), lambda b, g, i: (g, i, 0, 0)),
            pl.BlockSpec((1, 1) + tabw.shape[2:], lambda b, g, i: (g, 0, 0, 0)),
            pl.BlockSpec((1, 1) + tabs.shape[2:], lambda b, g, i: (g, 0, 0, 0)),
            pl.BlockSpec(ovl.shape, lambda b, g, i: (0, 0)),
        ],
        out_specs=pl.BlockSpec((1, Q_BLOCK, gw), lambda b, g, i: (b, i, g)),
        out_shape=jax.ShapeDtypeStruct((B, S, NSA_Q), BF16),
        scratch_shapes=[pltpu.VMEM((1, R), F32), pltpu.VMEM((V_ROWS, R), F32),
                        pltpu.VMEM((SEL_CHUNK, R), F32), pltpu.VMEM((SEL_CHUNK, R), F32),
                        pltpu.VMEM((WIN_BLOCKS * WIN_KB, R), F32)],
        compiler_params=_cparams(("parallel", "parallel", "arbitrary")),
        name="nsa_attn",
    )(qt, proj3, kca, vct, ksa, vst, kwa, vwt, tabc, tabw, tabs, ovl)


def _rel_bucket(d):
    max_exact = REL_BUCKETS // 2
    d = jnp.maximum(d, 0)
    df = jnp.maximum(d, 1).astype(F32)
    large = max_exact + (jnp.log(df / max_exact) / math.log(REL_MAX_DIST / max_exact)
                         * (REL_BUCKETS - max_exact)).astype(jnp.int32)
    return jnp.where(d < max_exact, d, jnp.minimum(large, REL_BUCKETS - 1))


def _bias_kernel(rb_ref, bkt_ref, o_ref, *, rows):
    g = pl.program_id(0)
    step = 32
    for r0 in range(0, rows, step):
        bkt = bkt_ref[0, r0:r0 + step, :]
        acc = [jnp.full(bkt.shape, NEG_BIG, F32) for _ in range(NSA_HPG)]
        for b in range(REL_BUCKETS):
            hit = bkt == b
            for h in range(NSA_HPG):
                acc[h] = jnp.where(hit, rb_ref[b, g * NSA_HPG + h] * LOG2E, acc[h])
        for h in range(NSA_HPG):
            o_ref[0, 0, r0:r0 + step, h * Q_BLOCK:(h + 1) * Q_BLOCK] = acc[h]


def _bias_table(rel_bias, bucket):
    nt, rows, cols = bucket.shape
    return pl.pallas_call(
        functools.partial(_bias_kernel, rows=rows),
        grid=(NSA_KV, nt),
        in_specs=[
            pl.BlockSpec(memory_space=pltpu.SMEM),
            pl.BlockSpec((1, rows, cols), lambda g, t: (t, 0, 0)),
        ],
        out_specs=pl.BlockSpec((1, 1, rows, NSA_HPG * cols), lambda g, t: (g, t, 0, 0)),
        out_shape=jax.ShapeDtypeStruct((NSA_KV, nt, rows, NSA_HPG * cols), F32),
        compiler_params=_cparams(("parallel", "parallel")),
        name="bias_table",
    )(rel_bias, bucket)


def _bias_tables(rel_bias, S, dsat, strip_len):
    nq = S // Q_BLOCK
    nc = S // CMP_STRIDE
    n_cmp = (S - CMP_BLOCK) // CMP_STRIDE + 1
    ar = lambda n: jnp.arange(n, dtype=jnp.int32)
    bucket = lambda d, valid: jnp.where(valid, _rel_bucket(d), -1)
    i = ar(Q_BLOCK)[None, None, :]
    c = ar(nc)[None, :, None]
    d_c = ar(nq)[:, None, None] * Q_BLOCK + i - (c * CMP_STRIDE + CMP_BLOCK - 1)
    tabc = _bias_table(rel_bias, bucket(d_c, (d_c >= 0) & (c < n_cmp)))
    jw = ar((WIN_BLOCKS + 1) * WIN_KB)[None, :, None]
    d_w = i + WINDOW - jw
    tabw = _bias_table(rel_bias, bucket(d_w, (d_w >= 0) & (d_w < WINDOW) & (jw < WIN_BLOCKS * WIN_KB)))
    d_s = i - ar(strip_len)[None, :, None] + dsat
    tabs = _bias_table(rel_bias, bucket(d_s, d_s >= 0))
    return tabc, tabw, tabs


def _overlap_matrix(S):
    nc = S // CMP_STRIDE
    nsel = S // SEL_BLOCK
    n_cmp = (S - CMP_BLOCK) // CMP_STRIDE + 1
    c = np.arange(nc)[None, :]
    sel_start = (np.arange(SEL_ROWS) * SEL_BLOCK)[:, None]
    ovl = ((c * CMP_STRIDE < sel_start + SEL_BLOCK) & (c * CMP_STRIDE + CMP_BLOCK - 1 >= sel_start)
           & (c < n_cmp) & (np.arange(SEL_ROWS)[:, None] < nsel))
    return jnp.asarray(ovl, BF16)


def _layer_params(l, lower_bounds, norm_mix, w_in, hg_out_gain, nsa_qk_gain, cmp_pos, cmp_w1, cmp_b1,
                  cmp_w2, w_branch_a, w_branch_b, w_out, norm_mlp, w_up, w_down):
    splits = (HG_K, HG_K, HG_V, HG_V, NSA_Q, NSA_KVW, NSA_KVW, NSA_KVW, NSA_KVW, NSA_KVW, NSA_KVW,
              3 * NSA_HEADS, D_MODEL, D_MODEL)
    offs = np.concatenate([[0], np.cumsum(splits)])
    col = lambda k: w_in[l][:, offs[k]:offs[k + 1]]
    gate_w = col(11)
    zpad = jnp.zeros((D_MODEL, LANES - 3 * NSA_HPG), F32)
    gate_cols = []
    for g in range(NSA_KV):
        gate_cols += [gate_w[:, g * 3 * NSA_HPG:(g + 1) * 3 * NSA_HPG], zpad]
    w_main = jnp.concatenate([col(12), col(13), col(0), col(2), col(3),
                              col(7), col(8), col(9), col(10), col(4)] + gate_cols, axis=1).astype(BF16)
    w_cmp = jnp.concatenate([col(1), col(5), col(6)], axis=1).astype(BF16)

    lb = lower_bounds[l]
    lbp = jnp.stack([jnp.log(lb), jnp.log1p(-lb), 1.0 - lb])

    w1 = cmp_w1[l].reshape(2, 2, CMP_STRIDE, NSA_DH, CMP_HID)
    w1c = jnp.zeros((2, CMP_STRIDE, NSA_KV, NSA_DH, NSA_KV, 2, CMP_HID), F32)
    for g in range(NSA_KV):
        w1c = w1c.at[:, :, g, :, g, :, :].set(w1.transpose(0, 2, 3, 1, 4))
    w1c = w1c.reshape(2, CMP_STRIDE * NSA_KVW, NSA_KV * 2 * CMP_HID).astype(BF16)
    pos8 = jnp.zeros((2, 8, CMP_BLOCK * NSA_DH), F32).at[:, 0, :].set(cmp_pos[l].reshape(2, -1))
    w2p = jnp.zeros((2, CMP_HID, LANES), F32)
    w2p = w2p.at[0, :, :NSA_DH].set(cmp_w2[l, 0]).at[1, :NSA_DH, :].set(cmp_w2[l, 1].T).astype(BF16)
    qk = nsa_qk_gain[l]
    z64 = jnp.zeros((NSA_DH,), F32)
    return dict(
        norm_mix=norm_mix[l][None, :], w_main=w_main, w_cmp=w_cmp, lbp=lbp,
        hg_gain=hg_out_gain[l][None, :],
        gsel=jnp.concatenate([qk[2], qk[2]])[None, :], gwin=jnp.concatenate([qk[3], qk[3]])[None, :],
        gcmp=jnp.concatenate([qk[1], z64])[None, :],
        qgain=jnp.broadcast_to((qk[0] * (NSA_DH ** -0.5 * LOG2E))[:, None], (NSA_DH, Q_BLOCK)),
        pos8=pos8, w1=cmp_w1[l].astype(BF16), w1c=w1c, b1=cmp_b1[l][:, None, :], w2p=w2p,
        wa=w_branch_a[l].astype(BF16), wb=w_branch_b[l].astype(BF16), wo=w_out[l].astype(BF16),
        norm_mlp=norm_mlp[l][None, :], wu=w_up[l].astype(BF16), wd=w_down[l].astype(BF16))


def kernel(x, rel_bias, hg_lb_logits, norm_mix, w_in, hg_out_gain, nsa_qk_gain, cmp_pos, cmp_w1, cmp_b1, cmp_w2, w_branch_a, w_branch_b, w_out, norm_mlp, w_up, w_down):
    B, S, D = x.shape
    T = B * S
    depth = w_in.shape[0]
    assert D == D_MODEL and S % SEL_CHUNK == 0 and S >= WINDOW and S // SEL_BLOCK <= SEL_ROWS
    assert S % Q_BLOCK == 0 and Q_BLOCK % WIN_KB == 0 and WINDOW % WIN_KB == 0
    assert FORCE_BONUS > NSA_HPG and SEL_TOPK >= 3
    lb_cum = jnp.cumsum(jax.nn.softmax(hg_lb_logits.astype(F32), axis=0), axis=0)
    lower_bounds = lb_cum - lb_cum[0:1]

    dsat, strip_len = _sel_table_geometry()
    tabc, tabw, tabs = _bias_tables(rel_bias, S, dsat, strip_len)
    ovl = _overlap_matrix(S)
    nc = S // CMP_STRIDE
    tm = min(1024, T)

    x2 = x.reshape(T, D)
    for l in range(depth):
        p = _layer_params(l, lower_bounds, norm_mix, w_in, hg_out_gain, nsa_qk_gain, cmp_pos, cmp_w1,
                          cmp_b1, cmp_w2, w_branch_a, w_branch_b, w_out, norm_mlp, w_up, w_down)
        proj, z, kc, vc = _rms_proj(x2, p["norm_mix"], p["w_main"], p["w_cmp"], tm, P_TILE)
        proj3 = proj.reshape(B, S, P_COLS)
        kc2 = kc.reshape(B, nc, CMP_STRIDE * NSA_KVW)
        vc2 = vc.reshape(B, nc, CMP_STRIDE * NSA_KVW)
        o_a = _hgrn(proj3, z.reshape(B, S, HG_K), p["lbp"], p["hg_gain"])
        prep = _nsa_prep(proj3, kc2, vc2, p["gsel"], p["gwin"], p["gcmp"], p["qgain"], p["pos8"], p["w1"],
                         p["w1c"], p["b1"], p["w2p"])
        o_b = _nsa_attn(proj3, prep, tabc, tabw, tabs, ovl, dsat)
        x2 = _merge(x2, o_a.reshape(T, HG_V), o_b.reshape(T, NSA_Q), proj, p["wa"], p["wb"], p["wo"],
                    min(1024, T))
        x2 = _mlp(x2, p["norm_mlp"], p["wu"], p["wd"], tm, 1024)
    return x2.reshape(B, S, D)
```

```python
import functools
import math

import numpy as np
import jax
import jax.numpy as jnp
from jax import lax
from jax.experimental import pallas as pl
from jax.experimental.pallas import tpu as pltpu

F32 = jnp.float32
BF16 = jnp.bfloat16

D_MODEL = 1024
HG_HEADS = 4
HG_DK = 128
HG_DV = 128
HG_K = HG_HEADS * HG_DK
HG_V = HG_HEADS * HG_DV
NSA_HEADS = 8
NSA_KV = 2
NSA_DH = 64
NSA_HPG = NSA_HEADS // NSA_KV
NSA_Q = NSA_HEADS * NSA_DH
NSA_KVW = NSA_KV * NSA_DH
CMP_BLOCK = 32
CMP_STRIDE = 16
CMP_HID = 128
SEL_BLOCK = 64
SEL_TOPK = 8
WINDOW = 512
FORCE_BONUS = 1000.0
REL_BUCKETS = 32
REL_MAX_DIST = 1024
D_FF = 4 * D_MODEL
EPS = 1e-6
NEG_BIG = -1e30
LOG2E = math.log2(math.e)

LANES = 128
VMEM_LIMIT = 56 * 1024 * 1024

P_GAB = 0
P_HG = P_GAB + 2 * D_MODEL
P_KV = P_HG + 3 * HG_K
P_NQ = P_KV + 4 * NSA_KVW
P_GATE = P_NQ + NSA_Q
P_COLS = P_GATE + NSA_KV * LANES
P_TILE = P_COLS // 2

HG_CHUNK = 128
HG_TS = 2048
Q_BLOCK = 256
WIN_KB = 128
SEL_CHUNK = 512
WIN_BLOCKS = (WINDOW + Q_BLOCK) // WIN_KB
SEL_SHIFT = int(math.log2(SEL_BLOCK))
SEL_ROWS = LANES - NSA_DH
V_ROWS = NSA_DH + 16


def _cparams(sem):
    return pltpu.CompilerParams(dimension_semantics=sem, vmem_limit_bytes=VMEM_LIMIT)


def _rms_rows(x, gain):
    ms = jnp.mean(x * x, axis=-1, keepdims=True)
    return x * lax.rsqrt(ms + EPS) * gain


def _proj_kernel(x_ref, g_ref, w_ref, wx_ref, o_ref, z_ref, kc_ref, vc_ref, h_ref, ck_ref, cv_ref):
    @pl.when(pl.program_id(1) == 0)
    def _():
        h = _rms_rows(x_ref[...], g_ref[...]).astype(BF16)
        h_ref[...] = h
        c = jnp.dot(h, wx_ref[...], preferred_element_type=F32)
        z_ref[...] = c[:, :HG_K]
        ck_ref[...] = c[:, HG_K:HG_K + NSA_KVW]
        cv_ref[...] = c[:, HG_K + NSA_KVW:]
        rows = ck_ref.shape[0] // CMP_STRIDE
        for t in range(CMP_STRIDE):
            lanes = slice(t * NSA_KVW, (t + 1) * NSA_KVW)
            kc_ref[:, lanes] = ck_ref[pl.ds(t, rows, stride=CMP_STRIDE), :].astype(BF16)
            vc_ref[:, lanes] = cv_ref[pl.ds(t, rows, stride=CMP_STRIDE), :].astype(BF16)

    o_ref[...] = jnp.dot(h_ref[...], w_ref[...], preferred_element_type=F32).astype(BF16)


def _rms_proj(x2, gain, w, wx, tm, tn):
    T, D = x2.shape
    N = w.shape[1]
    cmp_sds = jax.ShapeDtypeStruct((T // CMP_STRIDE, CMP_STRIDE * NSA_KVW), BF16)
    return pl.pallas_call(
        _proj_kernel,
        grid=(T // tm, N // tn),
        in_specs=[
            pl.BlockSpec((tm, D), lambda i, j: (i, 0)),
            pl.BlockSpec((1, D), lambda i, j: (0, 0)),
            pl.BlockSpec((D, tn), lambda i, j: (0, j)),
            pl.BlockSpec(wx.shape, lambda i, j: (0, 0)),
        ],
        out_specs=[
            pl.BlockSpec((tm, tn), lambda i, j: (i, j)),
            pl.BlockSpec((tm, HG_K), lambda i, j: (i, 0)),
            pl.BlockSpec((tm // CMP_STRIDE, CMP_STRIDE * NSA_KVW), lambda i, j: (i, 0)),
            pl.BlockSpec((tm // CMP_STRIDE, CMP_STRIDE * NSA_KVW), lambda i, j: (i, 0)),
        ],
        out_shape=[jax.ShapeDtypeStruct((T, N), BF16), jax.ShapeDtypeStruct((T, HG_K), F32),
                   cmp_sds, cmp_sds],
        scratch_shapes=[pltpu.VMEM((tm, D), BF16), pltpu.VMEM((tm, NSA_KVW), F32),
                        pltpu.VMEM((tm, NSA_KVW), F32)],
        compiler_params=_cparams(("parallel", "arbitrary")),
        name="rms_proj",
    )(x2, gain, w, wx)


def _mlp_kernel(x_ref, g_ref, wu_ref, wd_ref, o_ref, h_ref):
    f = pl.program_id(1)

    @pl.when(f == 0)
    def _():
        x = x_ref[...]
        h_ref[...] = _rms_rows(x, g_ref[...]).astype(BF16)
        o_ref[...] = x

    u = jnp.dot(h_ref[...], wu_ref[...], preferred_element_type=F32)
    u = jnp.square(jnp.maximum(u, 0.0)).astype(BF16)
    o_ref[...] += jnp.dot(u, wd_ref[...], preferred_element_type=F32)


def _mlp(x2, gain, wu, wd, tm, tf):
    T, D = x2.shape
    FF = wu.shape[1]
    return pl.pallas_call(
        _mlp_kernel,
        grid=(T // tm, FF // tf),
        in_specs=[
            pl.BlockSpec((tm, D), lambda i, f: (i, 0)),
            pl.BlockSpec((1, D), lambda i, f: (0, 0)),
            pl.BlockSpec((D, tf), lambda i, f: (0, f)),
            pl.BlockSpec((tf, D), lambda i, f: (f, 0)),
        ],
        out_specs=pl.BlockSpec((tm, D), lambda i, f: (i, 0)),
        out_shape=jax.ShapeDtypeStruct((T, D), F32),
        scratch_shapes=[pltpu.VMEM((tm, D), BF16)],
        compiler_params=_cparams(("parallel", "arbitrary")),
        name="mlp",
    )(x2, gain, wu, wd)


def _merge_kernel(x_ref, oa_ref, ob_ref, ga_ref, gb_ref, wa_ref, wb_ref, wo_ref, o_ref):
    ya = jnp.dot(oa_ref[...], wa_ref[...], preferred_element_type=F32)
    yb = jnp.dot(ob_ref[...], wb_ref[...], preferred_element_type=F32)
    mixed = (jax.nn.sigmoid(ga_ref[...].astype(F32)) * ya
             + jax.nn.sigmoid(gb_ref[...].astype(F32)) * yb)
    o_ref[...] = x_ref[...] + jnp.dot(mixed.astype(BF16), wo_ref[...], preferred_element_type=F32)


def _merge(x2, oa, ob, proj, wa, wb, wo, tm):
    T, D = x2.shape
    ga_blk = P_GAB // D
    return pl.pallas_call(
        _merge_kernel,
        grid=(T // tm,),
        in_specs=[
            pl.BlockSpec((tm, D), lambda i: (i, 0)),
            pl.BlockSpec((tm, HG_V), lambda i: (i, 0)),
            pl.BlockSpec((tm, NSA_Q), lambda i: (i, 0)),
            pl.BlockSpec((tm, D), lambda i: (i, ga_blk)),
            pl.BlockSpec((tm, D), lambda i: (i, ga_blk + 1)),
            pl.BlockSpec((HG_V, D), lambda i: (0, 0)),
            pl.BlockSpec((NSA_Q, D), lambda i: (0, 0)),
            pl.BlockSpec((D, D), lambda i: (0, 0)),
        ],
        out_specs=pl.BlockSpec((tm, D), lambda i: (i, 0)),
        out_shape=jax.ShapeDtypeStruct((T, D), F32),
        compiler_params=_cparams(("parallel",)),
        name="merge",
    )(x2, oa, ob, proj, proj, wa, wb, wo)


def _hgrn_consts(C):
    L = int(math.log2(C))
    idx = np.arange(C)
    mats = [(idx[None, :] <= idx[:, None]),
            (idx[None, :] > idx[:, None])]
    bmask = []
    for l in range(L):
        m = 1 << l
        r = (idx & ~(2 * m - 1)) + m
        lo = np.minimum(idx, r)[:, None]
        hi = np.maximum(idx, r)[:, None]
        mats.append((idx[None, :] > lo) & (idx[None, :] <= hi))
        blk = idx >> (l + 1)
        bmask.append(blk[:, None] == blk[None, :])
    bmask.append(idx[:, None] == idx[None, :])
    seg = np.concatenate(mats, axis=0).astype(np.float32)
    seg = np.concatenate([seg, seg], axis=1)
    return seg, np.stack(bmask).astype(np.float32), L


def _hgrn_kernel(q_ref, f_ref, i_ref, g_ref, lbp_ref, gain_ref, seg_ref, bm_ref, o_ref, st_ref,
                 k0_ref, sg0_ref, k1_ref, sg1_ref, *, C, L, nchunk):
    @pl.when(pl.program_id(1) == 0)
    def _():
        st_ref[...] = jnp.zeros_like(st_ref)

    rowi = lax.broadcasted_iota(jnp.int32, (C, HG_DK), 0)
    nt = (((1,), (1,)), ((), ()))
    tn = (((0,), (0,)), ((), ()))

    H = range(HG_HEADS)
    css = [slice(h * HG_DK, (h + 1) * HG_DK) for h in H]

    def gates(c, k_ref, sg_ref):
        r0 = pl.multiple_of(c * C, C)
        z = f_ref[0, pl.ds(r0, C), :]
        ez = jnp.exp(-jnp.abs(z))
        y = lbp_ref[1:2, :] + (jnp.minimum(z, 0.0) - jnp.log(1.0 + ez))
        log_lb = lbp_ref[0:1, :]
        logf = (jnp.maximum(log_lb, y) + jnp.log(1.0 + jnp.exp(-jnp.abs(log_lb - y)))) * LOG2E
        k_ref[...] = lbp_ref[2:3, :] * (jnp.where(z >= 0.0, ez, 1.0) / (1.0 + ez))
        g_hi = logf.astype(BF16)
        g_lo = (logf - g_hi.astype(F32)).astype(BF16)
        sg_ref[...] = jnp.dot(seg_ref[...], jnp.concatenate([g_hi, g_lo], axis=0),
                              preferred_element_type=F32)

    def mix(c, k_ref, sg_ref):
        r0 = pl.multiple_of(c * C, C)
        qbs = [q_ref[0, pl.ds(r0, C), cs] for cs in css]
        qs = [qb.astype(F32) for qb in qbs]
        ks = [k_ref[:, cs] for cs in css]
        ss = [lax.dot_general(qbs[h], ks[h].astype(BF16), nt, preferred_element_type=F32) * bm_ref[L]
              for h in H]
        for l in range(L):
            second = (rowi & (1 << l)) != 0
            for h in H:
                e = jnp.exp2(sg_ref[(2 + l) * C:(3 + l) * C, css[h]])
                ql = jnp.where(second, qs[h] * e, 0.0).astype(BF16)
                kl = jnp.where(second, 0.0, ks[h] * e).astype(BF16)
                ss[h] = ss[h] + lax.dot_general(ql, kl, nt, preferred_element_type=F32) * bm_ref[l]
        return r0, qs, ks, ss

    def mix_out(scores, k_ref, sg_ref):
        r0, qs, ks, ss = scores
        vbs = [i_ref[0, pl.ds(r0, C), cs] for cs in css]
        bs = [sg_ref[0:C, cs] for cs in css]
        sts = [st_ref[h] for h in H]
        outs = []
        for h in H:
            qd = (qs[h] * jnp.exp2(bs[h])).astype(BF16)
            o = lax.dot_general(qd, sts[h].astype(BF16), nt, preferred_element_type=F32)
            outs.append(o + jnp.dot(ss[h].astype(BF16), vbs[h], preferred_element_type=F32))
        for h in H:
            kd = (ks[h] * jnp.exp2(sg_ref[C:2 * C, css[h]])).astype(BF16)
            st_ref[h] = (sts[h] * jnp.exp2(bs[h][C - 1:C, :])
                         + lax.dot_general(vbs[h], kd, tn, preferred_element_type=F32))
        for h in H:
            gg = g_ref[0, pl.ds(r0, C), css[h]].astype(F32)
            o = _rms_rows(outs[h], gain_ref[...]) * (gg * jax.nn.sigmoid(gg))
            o_ref[0, pl.ds(r0, C), css[h]] = o.astype(BF16)

    gates(0, k0_ref, sg0_ref)

    def pair(pr, carry):
        c = 2 * pr
        sc0 = mix(c, k0_ref, sg0_ref)
        gates(c + 1, k1_ref, sg1_ref)
        mix_out(sc0, k0_ref, sg0_ref)
        sc1 = mix(c + 1, k1_ref, sg1_ref)
        gates(jnp.minimum(c + 2, nchunk - 1), k0_ref, sg0_ref)
        mix_out(sc1, k1_ref, sg1_ref)
        return carry

    lax.fori_loop(0, nchunk // 2, pair, 0)


def _hgrn(proj3, z3, lbp, gain, C=HG_CHUNK, ts=HG_TS):
    B, S, _ = proj3.shape
    ts = min(ts, S)
    assert (ts // C) % 2 == 0
    seg, bm, L = _hgrn_consts(C)
    blk = lambda k: pl.BlockSpec((1, ts, HG_K), lambda b, s: (b, s, P_HG // HG_K + k))
    return pl.pallas_call(
        functools.partial(_hgrn_kernel, C=C, L=L, nchunk=ts // C),
        grid=(B, S // ts),
        in_specs=[
            blk(0), pl.BlockSpec((1, ts, HG_K), lambda b, s: (b, s, 0)), blk(1), blk(2),
            pl.BlockSpec((3, HG_K), lambda b, s: (0, 0)),
            pl.BlockSpec((1, HG_DV), lambda b, s: (0, 0)),
            pl.BlockSpec(seg.shape, lambda b, s: (0, 0)),
            pl.BlockSpec(bm.shape, lambda b, s: (0, 0, 0)),
        ],
        out_specs=pl.BlockSpec((1, ts, HG_V), lambda b, s: (b, s, 0)),
        out_shape=jax.ShapeDtypeStruct((B, S, HG_V), BF16),
        scratch_shapes=[pltpu.VMEM((HG_HEADS, HG_DV, HG_DK), F32)]
        + 2 * [pltpu.VMEM((C, HG_K), F32), pltpu.VMEM(((2 + L) * C, HG_K), F32)],
        compiler_params=_cparams(("parallel", "arbitrary")),
        name="hgrn2",
    )(proj3, z3, proj3, proj3, lbp, gain, jnp.asarray(seg, BF16), jnp.asarray(bm))


def _half_rms(x, gain2, lane):
    sq = x * x
    s_lo = jnp.sum(jnp.where(lane < NSA_DH, sq, 0.0), axis=-1, keepdims=True)
    s_hi = jnp.sum(jnp.where(lane < NSA_DH, 0.0, sq), axis=-1, keepdims=True)
    ms = jnp.where(lane < NSA_DH, s_lo, s_hi) * (1.0 / NSA_DH)
    return x * lax.rsqrt(ms + EPS) * gain2


def _gelu_tanh(x):
    return 0.5 * x * (1.0 + jnp.tanh(math.sqrt(2.0 / math.pi) * (x + 0.044715 * (x * x * x))))


def _prep_kernel(kv_ref, nq_ref, kc_ref, vc_ref, gsel_ref, gwin_ref, gcmp_ref, qg_ref, pos_ref, w1_ref,
                 w1c_ref, b1_ref, w2_ref, ksa_ref, vst_ref, kwa_ref, vwt_ref, kca_ref, vct_ref, qt_ref,
                 *, S, rows):
    lane = lax.broadcasted_iota(jnp.int32, (rows, LANES), 1)
    rowi = lax.broadcasted_iota(jnp.int32, (rows, LANES), 0)
    ones_row = (lax.broadcasted_iota(jnp.int32, (V_ROWS - NSA_DH, rows), 0) == 0).astype(F32)
    low = lane < NSA_DH

    def body(i, carry):
        r0 = pl.multiple_of(i * rows, rows)
        blk_hot = jnp.where(lane - NSA_DH == ((r0 + rowi) >> SEL_SHIFT), 1.0, 0.0)
        for src, gain_ref, aug, k_dst, v_dst, vrows in (
                (0, gsel_ref, blk_hot, ksa_ref, vst_ref, rows),
                (2, gwin_ref, jnp.zeros((rows, LANES), F32), kwa_ref, vwt_ref, WIN_KB)):
            kk = kv_ref[0, pl.ds(r0, rows), src * LANES:(src + 1) * LANES].astype(F32)
            vv = kv_ref[0, pl.ds(r0, rows), (src + 1) * LANES:(src + 2) * LANES].astype(F32)
            kn = _half_rms(kk, gain_ref[...], lane)
            k_dst[0, 0, pl.ds(r0, rows), :] = jnp.where(low, kn, aug).astype(BF16)
            k_dst[0, 1, pl.ds(r0, rows), :] = jnp.where(low, pltpu.roll(kn, NSA_DH, 1), aug).astype(BF16)
            vt = vv.T
            for g in range(NSA_KV):
                vg = jnp.concatenate([vt[g * NSA_DH:(g + 1) * NSA_DH], ones_row], axis=0).astype(BF16)
                for u in range(rows // vrows):
                    v_dst[0, g, i * (rows // vrows) + u] = vg[:, u * vrows:(u + 1) * vrows]
        for u in range(rows // Q_BLOCK):
            xt = nq_ref[0, pl.ds(r0 + u * Q_BLOCK, Q_BLOCK), :].astype(F32).T
            for g in range(NSA_KV):
                cols = []
                for h in range(NSA_HPG):
                    xh = xt[(g * NSA_HPG + h) * NSA_DH:(g * NSA_HPG + h + 1) * NSA_DH]
                    ms = jnp.mean(xh * xh, axis=0, keepdims=True)
                    cols.append(xh * lax.rsqrt(ms + EPS) * qg_ref[...])
                qt_ref[0, g, i * (rows // Q_BLOCK) + u] = jnp.concatenate(cols, axis=1).astype(BF16)
        return carry

    lax.fori_loop(0, S // rows, body, 0)

    nc = S // CMP_STRIDE
    outs = []
    for j, src_ref in ((0, kc_ref), (1, vc_ref)):
        bias = jnp.dot(pos_ref[j].astype(BF16), w1_ref[j], preferred_element_type=F32)[0:1] + b1_ref[j]
        pj = jnp.dot(src_ref[0], w1c_ref[j], preferred_element_type=F32)
        per_g = []
        for g in range(NSA_KV):
            first = pj[:, (2 * g) * CMP_HID:(2 * g + 1) * CMP_HID]
            second = pj[:, (2 * g + 1) * CMP_HID:(2 * g + 2) * CMP_HID]
            hid = _gelu_tanh(first + pltpu.roll(second, nc - 1, 0) + bias).astype(BF16)
            if j == 0:
                per_g.append(jnp.dot(hid, w2_ref[0], preferred_element_type=F32))
            else:
                per_g.append(lax.dot_general(w2_ref[1, :NSA_DH, :], hid, (((1,), (1,)), ((), ())),
                                             preferred_element_type=F32))
        outs.append(per_g)
    for g in range(NSA_KV):
        kc = outs[0][g]
        ms = jnp.sum(kc * kc, axis=-1, keepdims=True) * (1.0 / NSA_DH)
        kca_ref[0, g] = (kc * lax.rsqrt(ms + EPS) * gcmp_ref[...]).astype(BF16)
        vct_ref[0, g] = outs[1][g].astype(BF16)


def _nsa_prep(proj3, kc2, vc2, gsel, gwin, gcmp, qgain, pos8, w1, w1c, b1, w2p):
    B, S, _ = proj3.shape
    nc = S // CMP_STRIDE
    rows = SEL_CHUNK
    full = lambda shp: pl.BlockSpec(shp, lambda b: (0,) * len(shp))
    whole = lambda shp: pl.BlockSpec((1,) + shp, lambda b: (b,) + (0,) * len(shp))
    shapes = [(NSA_KV, S, LANES), (NSA_KV, S // SEL_CHUNK, V_ROWS, SEL_CHUNK),
              (NSA_KV, S, LANES), (NSA_KV, S // WIN_KB, V_ROWS, WIN_KB),
              (NSA_KV, nc, LANES), (NSA_KV, NSA_DH, nc),
              (NSA_KV, S // Q_BLOCK, NSA_DH, NSA_HPG * Q_BLOCK)]
    return pl.pallas_call(
        functools.partial(_prep_kernel, S=S, rows=rows),
        grid=(B,),
        in_specs=[
            pl.BlockSpec((1, S, 4 * NSA_KVW), lambda b: (b, 0, P_KV // (4 * NSA_KVW))),
            pl.BlockSpec((1, S, NSA_Q), lambda b: (b, 0, P_NQ // NSA_Q)),
            pl.BlockSpec((1, nc, CMP_STRIDE * NSA_KVW), lambda b: (b, 0, 0)),
            pl.BlockSpec((1, nc, CMP_STRIDE * NSA_KVW), lambda b: (b, 0, 0)),
            full((1, LANES)), full((1, LANES)), full((1, LANES)), full(qgain.shape),
            full(pos8.shape), full(w1.shape), full(w1c.shape), full(b1.shape), full(w2p.shape),
        ],
        out_specs=[whole(s) for s in shapes],
        out_shape=[jax.ShapeDtypeStruct((B,) + s, BF16) for s in shapes],
        compiler_params=_cparams(("parallel",)),
        name="nsa_prep",
    )(proj3, proj3, kc2, vc2, gsel, gwin, gcmp, qgain, pos8, w1, w1c, b1, w2p)


def _sel_table_geometry():
    last_start = (REL_BUCKETS // 2) * (REL_MAX_DIST / (REL_BUCKETS // 2)) ** (
        (REL_BUCKETS - REL_BUCKETS // 2 - 1) / (REL_BUCKETS - REL_BUCKETS // 2))
    dsat = int(math.ceil((last_start + 16 + SEL_CHUNK) / LANES)) * LANES
    return dsat, dsat + SEL_CHUNK


def _attn_kernel(q_ref, gate_ref, kca_ref, vct_ref, ksa_ref, vst_ref, kwa_ref, vwt_ref, tabc_ref,
                 tabw_ref, tabs_ref, ovl_ref, o_ref, m_ref, acc_ref, sa_ref, sb_ref, sw_ref,
                 *, nsel, dsat):
    qi = pl.program_id(2)
    QB = Q_BLOCK
    R = NSA_HPG * QB
    DH = NSA_DH

    qt = q_ref[0, 0, 0]
    qta = jnp.concatenate([qt, jnp.zeros_like(qt)], axis=0)

    vts, slabs = [], []
    for r in range(WIN_BLOCKS):
        kb = qi * (QB // WIN_KB) - WINDOW // WIN_KB + r
        kbc = jnp.maximum(kb, 0)
        kw = kwa_ref[0, 0, pl.ds(pl.multiple_of(kbc * WIN_KB, WIN_KB), WIN_KB), :]
        vts.append(vwt_ref[0, 0, kbc])
        slabs.append(pl.multiple_of(jnp.where(kb >= 0, r, WIN_BLOCKS) * WIN_KB, WIN_KB))
        sw_ref[r * WIN_KB:(r + 1) * WIN_KB, :] = jnp.dot(kw, qta, preferred_element_type=F32)

    def hcols(h):
        return slice(h * QB, (h + 1) * QB)

    def scores(c, q_aug, h=None):
        k0 = pl.multiple_of(c * SEL_CHUNK, SEL_CHUNK)
        delta = qi * QB - c * SEL_CHUNK
        start = pl.multiple_of(dsat - jnp.minimum(delta, dsat), LANES)
        cs = slice(None) if h is None else hcols(h)
        return (jnp.dot(ksa_ref[0, 0, pl.ds(k0, SEL_CHUNK), :], q_aug[:, cs], preferred_element_type=F32)
                + tabs_ref[0, 0, pl.ds(start, SEL_CHUNK), cs])

    s = jnp.dot(kca_ref[0, 0], qta, preferred_element_type=F32) + tabc_ref[0, 0]
    mx = jnp.max(s, axis=0, keepdims=True)
    e = jnp.exp2(s - mx)
    inv = jnp.where(mx > 0.5 * NEG_BIG, 1.0 / jnp.sum(e, axis=0, keepdims=True), 0.0)
    p = e * inv
    o_c = jnp.dot(vct_ref[0, 0], p.astype(BF16), preferred_element_type=F32)

    psum = p[:, 0:QB] + p[:, QB:2 * QB] + p[:, 2 * QB:3 * QB] + p[:, 3 * QB:4 * QB]
    p_hi = psum.astype(BF16)
    p_lo = (psum - p_hi.astype(F32)).astype(BF16)
    imp = (jnp.dot(ovl_ref[...], p_hi, preferred_element_type=F32)
           + jnp.dot(ovl_ref[...], p_lo, preferred_element_type=F32))
    t = qi * QB + lax.broadcasted_iota(jnp.int32, (SEL_ROWS, QB), 1)
    jrow = lax.broadcasted_iota(jnp.int32, (SEL_ROWS, QB), 0)
    jcur = t >> SEL_SHIFT
    n_top = min(SEL_TOPK, nsel)
    forced = (jrow == 0) | (jrow == jcur) | (jrow == jcur - 1)
    work = jnp.where(forced, -1.0, jnp.where(jrow * SEL_BLOCK <= t, imp, -1.0))
    work = jnp.where(jrow < nsel, work, -5.0)
    jf = jrow.astype(F32)
    sel = jnp.where(forced, 1.0, 0.0)
    for _ in range(n_top - 3):
        best = jnp.max(work, axis=0, keepdims=True)
        first = jnp.min(jnp.where(work == best, jf, float(SEL_ROWS)), axis=0, keepdims=True)
        hit = jf == first
        sel = jnp.where(hit, 1.0, sel)
        work = jnp.where(hit, -3.0, work)
    sel = jnp.where(jcur < n_top, jnp.where(jrow < n_top, 1.0, 0.0), sel)
    selneg = jnp.concatenate([jnp.where(sel > 0.5, 0.0, NEG_BIG)] * NSA_HPG, axis=1)
    qts = jnp.concatenate([qt, selneg.astype(BF16)], axis=0)
    sa_ref[...] = scores(0, qts)

    sw = jnp.concatenate([sw_ref[r * WIN_KB:(r + 1) * WIN_KB, :] + tabw_ref[0, 0, pl.ds(slabs[r], WIN_KB), :]
                          for r in range(WIN_BLOCKS)], axis=0)
    ew = jnp.exp2(sw - jnp.max(sw, axis=0, keepdims=True)).astype(BF16)
    o_w = jnp.dot(jnp.concatenate(vts, axis=1), ew, preferred_element_type=F32)

    m_ref[...] = jnp.full_like(m_ref, NEG_BIG)
    acc_ref[...] = jnp.zeros_like(acc_ref)

    def consume(s_ref, c, heads=range(NSA_HPG)):
        vt = vst_ref[0, 0, c]
        for h in heads:
            cs = hcols(h)
            sc = s_ref[:, cs]
            m_old = m_ref[:, cs]
            m_new = jnp.maximum(m_old, jnp.max(sc, axis=0, keepdims=True))
            alpha = jnp.exp2(m_old - m_new)
            pe = jnp.exp2(sc - m_new).astype(BF16)
            acc_ref[:, cs] = alpha * acc_ref[:, cs] + jnp.dot(vt, pe, preferred_element_type=F32)
            m_ref[:, cs] = m_new

    def fill_and_consume(dst_ref, c_next, src_ref, c):
        dst_ref[:, hcols(0)] = scores(c_next, qts, 0)
        for h in range(NSA_HPG):
            if h + 1 < NSA_HPG:
                dst_ref[:, hcols(h + 1)] = scores(c_next, qts, h + 1)
            consume(src_ref, c, [h])

    n_chunks = (qi * QB + QB - 1) // SEL_CHUNK + 1
    n_pairs = (n_chunks - 1) // 2

    def sel_pair(pr, carry):
        c = 2 * pr
        fill_and_consume(sb_ref, c + 1, sa_ref, c)
        fill_and_consume(sa_ref, c + 2, sb_ref, c + 1)
        return carry

    lax.fori_loop(0, n_pairs, sel_pair, 0)

    @pl.when(n_chunks % 2 == 0)
    def _():
        fill_and_consume(sb_ref, n_chunks - 1, sa_ref, n_chunks - 2)
        consume(sb_ref, n_chunks - 1)

    @pl.when(n_chunks % 2 == 1)
    def _():
        consume(sa_ref, n_chunks - 1)

    acc = acc_ref[...]

    gt = jax.nn.sigmoid(gate_ref[0].astype(F32)).T
    o_s = acc[0:DH] / acc[DH:DH + 1]
    o_w = o_w[0:DH] / o_w[DH:DH + 1]
    heads = []
    for h in range(NSA_HPG):
        cs = slice(h * QB, (h + 1) * QB)
        heads.append(gt[3 * h:3 * h + 1] * o_c[:, cs] + gt[3 * h + 1:3 * h + 2] * o_s[:, cs]
                     + gt[3 * h + 2:3 * h + 3] * o_w[:, cs])
    o_ref[0] = jnp.concatenate(heads, axis=0).T.astype(BF16)


def _nsa_attn(proj3, prep, tabc, tabw, tabs, ovl, dsat):
    B, S, _ = proj3.shape
    ksa, vst, kwa, vwt, kca, vct, qt = prep
    nq = S // Q_BLOCK
    nc = S // CMP_STRIDE
    nsel = S // SEL_BLOCK
    R = NSA_HPG * Q_BLOCK
    gw = NSA_HPG * NSA_DH
    per_bg = lambda a: pl.BlockSpec((1, 1) + a.shape[2:], lambda b, g, i: (b, g) + (0,) * (a.ndim - 2))
    return pl.pallas_call(
        functools.partial(_attn_kernel, nsel=nsel, dsat=dsat),
        grid=(B, NSA_KV, nq),
        in_specs=[
            pl.BlockSpec((1, 1, 1, NSA_DH, R), lambda b, g, i: (b, g, i, 0, 0)),
            pl.BlockSpec((1, Q_BLOCK, LANES), lambda b, g, i: (b, i, P_GATE // LANES + g)),
            per_bg(kca), per_bg(vct), per_bg(ksa), per_bg(vst), per_bg(kwa), per_bg(vwt),
            pl.BlockSpec((1, 1, nc, R), lambda b, g, i: (g, i, 0, 0)),
            pl.BlockSpec((1, 1) + tabw.shape[2:], lambda b, g, i: (g, 0, 0, 0)),
            pl.BlockSpec((1, 1) + tabs.shape[2:], lambda b, g, i: (g, 0, 0, 0)),
            pl.BlockSpec(ovl.shape, lambda b, g, i: (0, 0)),
        ],
        out_specs=pl.BlockSpec((1, Q_BLOCK, gw), lambda b, g, i: (b, i, g)),
        out_shape=jax.ShapeDtypeStruct((B, S, NSA_Q), BF16),
        scratch_shapes=[pltpu.VMEM((1, R), F32), pltpu.VMEM((V_ROWS, R), F32),
                        pltpu.VMEM((SEL_CHUNK, R), F32), pltpu.VMEM((SEL_CHUNK, R), F32),
                        pltpu.VMEM((WIN_BLOCKS * WIN_KB, R), F32)],
        compiler_params=_cparams(("parallel", "parallel", "arbitrary")),
        name="nsa_attn",
    )(qt, proj3, kca, vct, ksa, vst, kwa, vwt, tabc, tabw, tabs, ovl)


def _rel_bucket(d):
    max_exact = REL_BUCKETS // 2
    d = jnp.maximum(d, 0)
    df = jnp.maximum(d, 1).astype(F32)
    large = max_exact + (jnp.log(df / max_exact) / math.log(REL_MAX_DIST / max_exact)
                         * (REL_BUCKETS - max_exact)).astype(jnp.int32)
    return jnp.where(d < max_exact, d, jnp.minimum(large, REL_BUCKETS - 1))


def _bias_kernel(rb_ref, bkt_ref, o_ref, *, rows):
    g = pl.program_id(0)
    step = 32
    for r0 in range(0, rows, step):
        bkt = bkt_ref[0, r0:r0 + step, :]
        acc = [jnp.full(bkt.shape, NEG_BIG, F32) for _ in range(NSA_HPG)]
        for b in range(REL_BUCKETS):
            hit = bkt == b
            for h in range(NSA_HPG):
                acc[h] = jnp.where(hit, rb_ref[b, g * NSA_HPG + h] * LOG2E, acc[h])
        for h in range(NSA_HPG):
            o_ref[0, 0, r0:r0 + step, h * Q_BLOCK:(h + 1) * Q_BLOCK] = acc[h]


def _bias_table(rel_bias, bucket):
    nt, rows, cols = bucket.shape
    return pl.pallas_call(
        functools.partial(_bias_kernel, rows=rows),
        grid=(NSA_KV, nt),
        in_specs=[
            pl.BlockSpec(memory_space=pltpu.SMEM),
            pl.BlockSpec((1, rows, cols), lambda g, t: (t, 0, 0)),
        ],
        out_specs=pl.BlockSpec((1, 1, rows, NSA_HPG * cols), lambda g, t: (g, t, 0, 0)),
        out_shape=jax.ShapeDtypeStruct((NSA_KV, nt, rows, NSA_HPG * cols), F32),
        compiler_params=_cparams(("parallel", "parallel")),
        name="bias_table",
    )(rel_bias, bucket)


def _bias_tables(rel_bias, S, dsat, strip_len):
    nq = S // Q_BLOCK
    nc = S // CMP_STRIDE
    n_cmp = (S - CMP_BLOCK) // CMP_STRIDE + 1
    ar = lambda n: jnp.arange(n, dtype=jnp.int32)
    bucket = lambda d, valid: jnp.where(valid, _rel_bucket(d), -1)
    i = ar(Q_BLOCK)[None, None, :]
    c = ar(nc)[None, :, None]
    d_c = ar(nq)[:, None, None] * Q_BLOCK + i - (c * CMP_STRIDE + CMP_BLOCK - 1)
    tabc = _bias_table(rel_bias, bucket(d_c, (d_c >= 0) & (c < n_cmp)))
    jw = ar((WIN_BLOCKS + 1) * WIN_KB)[None, :, None]
    d_w = i + WINDOW - jw
    tabw = _bias_table(rel_bias, bucket(d_w, (d_w >= 0) & (d_w < WINDOW) & (jw < WIN_BLOCKS * WIN_KB)))
    d_s = i - ar(strip_len)[None, :, None] + dsat
    tabs = _bias_table(rel_bias, bucket(d_s, d_s >= 0))
    return tabc, tabw, tabs


def _overlap_matrix(S):
    nc = S // CMP_STRIDE
    nsel = S // SEL_BLOCK
    n_cmp = (S - CMP_BLOCK) // CMP_STRIDE + 1
    c = np.arange(nc)[None, :]
    sel_start = (np.arange(SEL_ROWS) * SEL_BLOCK)[:, None]
    ovl = ((c * CMP_STRIDE < sel_start + SEL_BLOCK) & (c * CMP_STRIDE + CMP_BLOCK - 1 >= sel_start)
           & (c < n_cmp) & (np.arange(SEL_ROWS)[:, None] < nsel))
    return jnp.asarray(ovl, BF16)


def _layer_params(l, lower_bounds, norm_mix, w_in, hg_out_gain, nsa_qk_gain, cmp_pos, cmp_w1, cmp_b1,
                  cmp_w2, w_branch_a, w_branch_b, w_out, norm_mlp, w_up, w_down):
    splits = (HG_K, HG_K, HG_V, HG_V, NSA_Q, NSA_KVW, NSA_KVW, NSA_KVW, NSA_KVW, NSA_KVW, NSA_KVW,
              3 * NSA_HEADS, D_MODEL, D_MODEL)
    offs = np.concatenate([[0], np.cumsum(splits)])
    col = lambda k: w_in[l][:, offs[k]:offs[k + 1]]
    gate_w = col(11)
    zpad = jnp.zeros((D_MODEL, LANES - 3 * NSA_HPG), F32)
    gate_cols = []
    for g in range(NSA_KV):
        gate_cols += [gate_w[:, g * 3 * NSA_HPG:(g + 1) * 3 * NSA_HPG], zpad]
    w_main = jnp.concatenate([col(12), col(13), col(0), col(2), col(3),
                              col(7), col(8), col(9), col(10), col(4)] + gate_cols, axis=1).astype(BF16)
    w_cmp = jnp.concatenate([col(1), col(5), col(6)], axis=1).astype(BF16)

    lb = lower_bounds[l]
    lbp = jnp.stack([jnp.log(lb), jnp.log1p(-lb), 1.0 - lb])

    w1 = cmp_w1[l].reshape(2, 2, CMP_STRIDE, NSA_DH, CMP_HID)
    w1c = jnp.zeros((2, CMP_STRIDE, NSA_KV, NSA_DH, NSA_KV, 2, CMP_HID), F32)
    for g in range(NSA_KV):
        w1c = w1c.at[:, :, g, :, g, :, :].set(w1.transpose(0, 2, 3, 1, 4))
    w1c = w1c.reshape(2, CMP_STRIDE * NSA_KVW, NSA_KV * 2 * CMP_HID).astype(BF16)
    pos8 = jnp.zeros((2, 8, CMP_BLOCK * NSA_DH), F32).at[:, 0, :].set(cmp_pos[l].reshape(2, -1))
    w2p = jnp.zeros((2, CMP_HID, LANES), F32)
    w2p = w2p.at[0, :, :NSA_DH].set(cmp_w2[l, 0]).at[1, :NSA_DH, :].set(cmp_w2[l, 1].T).astype(BF16)
    qk = nsa_qk_gain[l]
    z64 = jnp.zeros((NSA_DH,), F32)
    return dict(
        norm_mix=norm_mix[l][None, :], w_main=w_main, w_cmp=w_cmp, lbp=lbp,
        hg_gain=hg_out_gain[l][None, :],
        gsel=jnp.concatenate([qk[2], qk[2]])[None, :], gwin=jnp.concatenate([qk[3], qk[3]])[None, :],
        gcmp=jnp.concatenate([qk[1], z64])[None, :],
        qgain=jnp.broadcast_to((qk[0] * (NSA_DH ** -0.5 * LOG2E))[:, None], (NSA_DH, Q_BLOCK)),
        pos8=pos8, w1=cmp_w1[l].astype(BF16), w1c=w1c, b1=cmp_b1[l][:, None, :], w2p=w2p,
        wa=w_branch_a[l].astype(BF16), wb=w_branch_b[l].astype(BF16), wo=w_out[l].astype(BF16),
        norm_mlp=norm_mlp[l][None, :], wu=w_up[l].astype(BF16), wd=w_down[l].astype(BF16))


def kernel(x, rel_bias, hg_lb_logits, norm_mix, w_in, hg_out_gain, nsa_qk_gain, cmp_pos, cmp_w1, cmp_b1, cmp_w2, w_branch_a, w_branch_b, w_out, norm_mlp, w_up, w_down):
    B, S, D = x.shape
    T = B * S
    depth = w_in.shape[0]
    assert D == D_MODEL and S % SEL_CHUNK == 0 and S >= WINDOW and S // SEL_BLOCK <= SEL_ROWS
    assert S % Q_BLOCK == 0 and Q_BLOCK % WIN_KB == 0 and WINDOW % WIN_KB == 0
    assert FORCE_BONUS > NSA_HPG and SEL_TOPK >= 3
    lb_cum = jnp.cumsum(jax.nn.softmax(hg_lb_logits.astype(F32), axis=0), axis=0)
    lower_bounds = lb_cum - lb_cum[0:1]

    dsat, strip_len = _sel_table_geometry()
    tabc, tabw, tabs = _bias_tables(rel_bias, S, dsat, strip_len)
    ovl = _overlap_matrix(S)
    nc = S // CMP_STRIDE
    tm = min(1024, T)

    x2 = x.reshape(T, D)
    for l in range(depth):
        p = _layer_params(l, lower_bounds, norm_mix, w_in, hg_out_gain, nsa_qk_gain, cmp_pos, cmp_w1,
                          cmp_b1, cmp_w2, w_branch_a, w_branch_b, w_out, norm_mlp, w_up, w_down)
        proj, z, kc, vc = _rms_proj(x2, p["norm_mix"], p["w_main"], p["w_cmp"], tm, P_TILE)
        proj3 = proj.reshape(B, S, P_COLS)
        kc2 = kc.reshape(B, nc, CMP_STRIDE * NSA_KVW)
        vc2 = vc.reshape(B, nc, CMP_STRIDE * NSA_KVW)
        o_a = _hgrn(proj3, z.reshape(B, S, HG_K), p["lbp"], p["hg_gain"])
        prep = _nsa_prep(proj3, kc2, vc2, p["gsel"], p["gwin"], p["gcmp"], p["qgain"], p["pos8"], p["w1"],
                         p["w1c"], p["b1"], p["w2p"])
        o_b = _nsa_attn(proj3, prep, tabc, tabw, tabs, ovl, dsat)
        x2 = _merge(x2, o_a.reshape(T, HG_V), o_b.reshape(T, NSA_Q), proj, p["wa"], p["wb"], p["wo"],
                    min(1024, T))
        x2 = _mlp(x2, p["norm_mlp"], p["wu"], p["wd"], tm, 1024)
    return x2.reshape(B, S, D)
```

```python
import functools
import math

import numpy as np
import jax
import jax.numpy as jnp
from jax import lax
from jax.experimental import pallas as pl
from jax.experimental.pallas import tpu as pltpu

F32 = jnp.float32
BF16 = jnp.bfloat16

D_MODEL = 1024
HG_HEADS = 4
HG_DK = 128
HG_DV = 128
HG_K = HG_HEADS * HG_DK
HG_V = HG_HEADS * HG_DV
NSA_HEADS = 8
NSA_KV = 2
NSA_DH = 64
NSA_HPG = NSA_HEADS // NSA_KV
NSA_Q = NSA_HEADS * NSA_DH
NSA_KVW = NSA_KV * NSA_DH
CMP_BLOCK = 32
CMP_STRIDE = 16
CMP_HID = 128
SEL_BLOCK = 64
SEL_TOPK = 8
WINDOW = 512
FORCE_BONUS = 1000.0
REL_BUCKETS = 32
REL_MAX_DIST = 1024
D_FF = 4 * D_MODEL
EPS = 1e-6
NEG_BIG = -1e30
LOG2E = math.log2(math.e)

LANES = 128
VMEM_LIMIT = 56 * 1024 * 1024

P_GAB = 0
P_HG = P_GAB + 2 * D_MODEL
P_KV = P_HG + 3 * HG_K
P_NQ = P_KV + 4 * NSA_KVW
P_GATE = P_NQ + NSA_Q
P_COLS = P_GATE + NSA_KV * LANES
P_TILE = P_COLS // 2

HG_CHUNK = 128
HG_TS = 2048
Q_BLOCK = 256
WIN_KB = 128
SEL_CHUNK = 512
WIN_BLOCKS = (WINDOW + Q_BLOCK) // WIN_KB
WIN_SUB = WINDOW // WIN_KB + 1
SEL_SHIFT = int(math.log2(SEL_BLOCK))
SEL_ROWS = LANES - NSA_DH
V_ROWS = NSA_DH + 16


def _cparams(sem):
    return pltpu.CompilerParams(dimension_semantics=sem, vmem_limit_bytes=VMEM_LIMIT)


def _rms_rows(x, gain):
    ms = jnp.mean(x * x, axis=-1, keepdims=True)
    return x * lax.rsqrt(ms + EPS) * gain


def _proj_kernel(x_ref, g_ref, w_ref, wx_ref, o_ref, z_ref, kc_ref, vc_ref, h_ref, ck_ref, cv_ref):
    @pl.when(pl.program_id(1) == 0)
    def _():
        h = _rms_rows(x_ref[...], g_ref[...]).astype(BF16)
        h_ref[...] = h
        c = jnp.dot(h, wx_ref[...], preferred_element_type=F32)
        z_ref[...] = c[:, :HG_K]
        ck_ref[...] = c[:, HG_K:HG_K + NSA_KVW]
        cv_ref[...] = c[:, HG_K + NSA_KVW:]
        rows = ck_ref.shape[0] // CMP_STRIDE
        for t in range(CMP_STRIDE):
            lanes = slice(t * NSA_KVW, (t + 1) * NSA_KVW)
            kc_ref[:, lanes] = ck_ref[pl.ds(t, rows, stride=CMP_STRIDE), :].astype(BF16)
            vc_ref[:, lanes] = cv_ref[pl.ds(t, rows, stride=CMP_STRIDE), :].astype(BF16)

    o_ref[...] = jnp.dot(h_ref[...], w_ref[...], preferred_element_type=F32).astype(BF16)


def _rms_proj(x2, gain, w, wx, tm, tn):
    T, D = x2.shape
    N = w.shape[1]
    cmp_sds = jax.ShapeDtypeStruct((T // CMP_STRIDE, CMP_STRIDE * NSA_KVW), BF16)
    return pl.pallas_call(
        _proj_kernel,
        grid=(T // tm, N // tn),
        in_specs=[
            pl.BlockSpec((tm, D), lambda i, j: (i, 0)),
            pl.BlockSpec((1, D), lambda i, j: (0, 0)),
            pl.BlockSpec((D, tn), lambda i, j: (0, j)),
            pl.BlockSpec(wx.shape, lambda i, j: (0, 0)),
        ],
        out_specs=[
            pl.BlockSpec((tm, tn), lambda i, j: (i, j)),
            pl.BlockSpec((tm, HG_K), lambda i, j: (i, 0)),
            pl.BlockSpec((tm // CMP_STRIDE, CMP_STRIDE * NSA_KVW), lambda i, j: (i, 0)),
            pl.BlockSpec((tm // CMP_STRIDE, CMP_STRIDE * NSA_KVW), lambda i, j: (i, 0)),
        ],
        out_shape=[jax.ShapeDtypeStruct((T, N), BF16), jax.ShapeDtypeStruct((T, HG_K), F32),
                   cmp_sds, cmp_sds],
        scratch_shapes=[pltpu.VMEM((tm, D), BF16), pltpu.VMEM((tm, NSA_KVW), F32),
                        pltpu.VMEM((tm, NSA_KVW), F32)],
        compiler_params=_cparams(("parallel", "arbitrary")),
        name="rms_proj",
    )(x2, gain, w, wx)


def _mlp_kernel(x_ref, g_ref, wu_ref, wd_ref, o_ref, h_ref):
    f = pl.program_id(1)

    @pl.when(f == 0)
    def _():
        x = x_ref[...]
        h_ref[...] = _rms_rows(x, g_ref[...]).astype(BF16)
        o_ref[...] = x

    u = jnp.dot(h_ref[...], wu_ref[...], preferred_element_type=F32)
    u = jnp.square(jnp.maximum(u, 0.0)).astype(BF16)
    o_ref[...] += jnp.dot(u, wd_ref[...], preferred_element_type=F32)


def _mlp(x2, gain, wu, wd, tm, tf):
    T, D = x2.shape
    FF = wu.shape[1]
    return pl.pallas_call(
        _mlp_kernel,
        grid=(T // tm, FF // tf),
        in_specs=[
            pl.BlockSpec((tm, D), lambda i, f: (i, 0)),
            pl.BlockSpec((1, D), lambda i, f: (0, 0)),
            pl.BlockSpec((D, tf), lambda i, f: (0, f)),
            pl.BlockSpec((tf, D), lambda i, f: (f, 0)),
        ],
        out_specs=pl.BlockSpec((tm, D), lambda i, f: (i, 0)),
        out_shape=jax.ShapeDtypeStruct((T, D), F32),
        scratch_shapes=[pltpu.VMEM((tm, D), BF16)],
        compiler_params=_cparams(("parallel", "arbitrary")),
        name="mlp",
    )(x2, gain, wu, wd)


def _merge_kernel(x_ref, oa_ref, ob_ref, ga_ref, gb_ref, wa_ref, wb_ref, wo_ref, o_ref):
    ya = jnp.dot(oa_ref[...], wa_ref[...], preferred_element_type=F32)
    yb = jnp.dot(ob_ref[...], wb_ref[...], preferred_element_type=F32)
    mixed = (jax.nn.sigmoid(ga_ref[...].astype(F32)) * ya
             + jax.nn.sigmoid(gb_ref[...].astype(F32)) * yb)
    o_ref[...] = x_ref[...] + jnp.dot(mixed.astype(BF16), wo_ref[...], preferred_element_type=F32)


def _merge(x2, oa, ob, proj, wa, wb, wo, tm):
    T, D = x2.shape
    ga_blk = P_GAB // D
    return pl.pallas_call(
        _merge_kernel,
        grid=(T // tm,),
        in_specs=[
            pl.BlockSpec((tm, D), lambda i: (i, 0)),
            pl.BlockSpec((tm, HG_V), lambda i: (i, 0)),
            pl.BlockSpec((tm, NSA_Q), lambda i: (i, 0)),
            pl.BlockSpec((tm, D), lambda i: (i, ga_blk)),
            pl.BlockSpec((tm, D), lambda i: (i, ga_blk + 1)),
            pl.BlockSpec((HG_V, D), lambda i: (0, 0)),
            pl.BlockSpec((NSA_Q, D), lambda i: (0, 0)),
            pl.BlockSpec((D, D), lambda i: (0, 0)),
        ],
        out_specs=pl.BlockSpec((tm, D), lambda i: (i, 0)),
        out_shape=jax.ShapeDtypeStruct((T, D), F32),
        compiler_params=_cparams(("parallel",)),
        name="merge",
    )(x2, oa, ob, proj, proj, wa, wb, wo)


def _hgrn_consts(C):
    L = int(math.log2(C))
    idx = np.arange(C)
    mats = [(idx[None, :] <= idx[:, None]),
            (idx[None, :] > idx[:, None])]
    bmask = []
    for l in range(L):
        m = 1 << l
        r = (idx & ~(2 * m - 1)) + m
        lo = np.minimum(idx, r)[:, None]
        hi = np.maximum(idx, r)[:, None]
        mats.append((idx[None, :] > lo) & (idx[None, :] <= hi))
        blk = idx >> (l + 1)
        bmask.append(blk[:, None] == blk[None, :])
    bmask.append(idx[:, None] == idx[None, :])
    seg = np.concatenate(mats, axis=0).astype(np.float32)
    seg = np.concatenate([seg, seg], axis=1)
    return seg, np.stack(bmask).astype(np.float32), L


def _hgrn_kernel(q_ref, f_ref, i_ref, g_ref, lbp_ref, gain_ref, seg_ref, bm_ref, o_ref, st_ref,
                 k0_ref, sg0_ref, k1_ref, sg1_ref, *, C, L, nchunk):
    @pl.when(pl.program_id(1) == 0)
    def _():
        st_ref[...] = jnp.zeros_like(st_ref)

    rowi = lax.broadcasted_iota(jnp.int32, (C, HG_DK), 0)
    nt = (((1,), (1,)), ((), ()))
    tn = (((0,), (0,)), ((), ()))

    H = range(HG_HEADS)
    css = [slice(h * HG_DK, (h + 1) * HG_DK) for h in H]

    def gates(c, k_ref, sg_ref):
        r0 = pl.multiple_of(c * C, C)
        z = f_ref[0, pl.ds(r0, C), :]
        ez = jnp.exp(-jnp.abs(z))
        y = lbp_ref[1:2, :] + (jnp.minimum(z, 0.0) - jnp.log(1.0 + ez))
        log_lb = lbp_ref[0:1, :]
        logf = (jnp.maximum(log_lb, y) + jnp.log(1.0 + jnp.exp(-jnp.abs(log_lb - y)))) * LOG2E
        k_ref[...] = lbp_ref[2:3, :] * (jnp.where(z >= 0.0, ez, 1.0) / (1.0 + ez))
        g_hi = logf.astype(BF16)
        g_lo = (logf - g_hi.astype(F32)).astype(BF16)
        sg_ref[...] = jnp.dot(seg_ref[...], jnp.concatenate([g_hi, g_lo], axis=0),
                              preferred_element_type=F32)

    def mix(c, k_ref, sg_ref):
        r0 = pl.multiple_of(c * C, C)
        qbs = [q_ref[0, pl.ds(r0, C), cs] for cs in css]
        qs = [qb.astype(F32) for qb in qbs]
        ks = [k_ref[:, cs] for cs in css]
        ss = [lax.dot_general(qbs[h], ks[h].astype(BF16), nt, preferred_element_type=F32) * bm_ref[L]
              for h in H]
        for l in range(L):
            second = (rowi & (1 << l)) != 0
            for h in H:
                e = jnp.exp2(sg_ref[(2 + l) * C:(3 + l) * C, css[h]])
                ql = jnp.where(second, qs[h] * e, 0.0).astype(BF16)
                kl = jnp.where(second, 0.0, ks[h] * e).astype(BF16)
                ss[h] = ss[h] + lax.dot_general(ql, kl, nt, preferred_element_type=F32) * bm_ref[l]
        return r0, qs, ks, ss

    def mix_out(scores, k_ref, sg_ref):
        r0, qs, ks, ss = scores
        vbs = [i_ref[0, pl.ds(r0, C), cs] for cs in css]
        bs = [sg_ref[0:C, cs] for cs in css]
        sts = [st_ref[h] for h in H]
        outs = []
        for h in H:
            qd = (qs[h] * jnp.exp2(bs[h])).astype(BF16)
            o = lax.dot_general(qd, sts[h].astype(BF16), nt, preferred_element_type=F32)
            outs.append(o + jnp.dot(ss[h].astype(BF16), vbs[h], preferred_element_type=F32))
        for h in H:
            kd = (ks[h] * jnp.exp2(sg_ref[C:2 * C, css[h]])).astype(BF16)
            st_ref[h] = (sts[h] * jnp.exp2(bs[h][C - 1:C, :])
                         + lax.dot_general(vbs[h], kd, tn, preferred_element_type=F32))
        for h in H:
            gg = g_ref[0, pl.ds(r0, C), css[h]].astype(F32)
            o = _rms_rows(outs[h], gain_ref[...]) * (gg * jax.nn.sigmoid(gg))
            o_ref[0, pl.ds(r0, C), css[h]] = o.astype(BF16)

    gates(0, k0_ref, sg0_ref)

    def pair(pr, carry):
        c = 2 * pr
        sc0 = mix(c, k0_ref, sg0_ref)
        gates(c + 1, k1_ref, sg1_ref)
        mix_out(sc0, k0_ref, sg0_ref)
        sc1 = mix(c + 1, k1_ref, sg1_ref)
        gates(jnp.minimum(c + 2, nchunk - 1), k0_ref, sg0_ref)
        mix_out(sc1, k1_ref, sg1_ref)
        return carry

    lax.fori_loop(0, nchunk // 2, pair, 0)


def _hgrn(proj3, z3, lbp, gain, C=HG_CHUNK, ts=HG_TS):
    B, S, _ = proj3.shape
    ts = min(ts, S)
    assert (ts // C) % 2 == 0
    seg, bm, L = _hgrn_consts(C)
    blk = lambda k: pl.BlockSpec((1, ts, HG_K), lambda b, s: (b, s, P_HG // HG_K + k))
    return pl.pallas_call(
        functools.partial(_hgrn_kernel, C=C, L=L, nchunk=ts // C),
        grid=(B, S // ts),
        in_specs=[
            blk(0), pl.BlockSpec((1, ts, HG_K), lambda b, s: (b, s, 0)), blk(1), blk(2),
            pl.BlockSpec((3, HG_K), lambda b, s: (0, 0)),
            pl.BlockSpec((1, HG_DV), lambda b, s: (0, 0)),
            pl.BlockSpec(seg.shape, lambda b, s: (0, 0)),
            pl.BlockSpec(bm.shape, lambda b, s: (0, 0, 0)),
        ],
        out_specs=pl.BlockSpec((1, ts, HG_V), lambda b, s: (b, s, 0)),
        out_shape=jax.ShapeDtypeStruct((B, S, HG_V), BF16),
        scratch_shapes=[pltpu.VMEM((HG_HEADS, HG_DV, HG_DK), F32)]
        + 2 * [pltpu.VMEM((C, HG_K), F32), pltpu.VMEM(((2 + L) * C, HG_K), F32)],
        compiler_params=_cparams(("parallel", "arbitrary")),
        name="hgrn2",
    )(proj3, z3, proj3, proj3, lbp, gain, jnp.asarray(seg, BF16), jnp.asarray(bm))


def _half_rms(x, gain2, lane):
    sq = x * x
    s_lo = jnp.sum(jnp.where(lane < NSA_DH, sq, 0.0), axis=-1, keepdims=True)
    s_hi = jnp.sum(jnp.where(lane < NSA_DH, 0.0, sq), axis=-1, keepdims=True)
    ms = jnp.where(lane < NSA_DH, s_lo, s_hi) * (1.0 / NSA_DH)
    return x * lax.rsqrt(ms + EPS) * gain2


def _gelu_tanh(x):
    return 0.5 * x * (1.0 + jnp.tanh(math.sqrt(2.0 / math.pi) * (x + 0.044715 * (x * x * x))))


def _prep_kernel(kv_ref, nq_ref, kc_ref, vc_ref, gsel_ref, gwin_ref, gcmp_ref, qg_ref, pos_ref, w1_ref,
                 w1c_ref, b1_ref, w2_ref, ksa_ref, vst_ref, kwa_ref, vwt_ref, kca_ref, vct_ref, qt_ref,
                 *, S, rows):
    lane = lax.broadcasted_iota(jnp.int32, (rows, LANES), 1)
    rowi = lax.broadcasted_iota(jnp.int32, (rows, LANES), 0)
    ones_row = (lax.broadcasted_iota(jnp.int32, (V_ROWS - NSA_DH, rows), 0) == 0).astype(F32)
    low = lane < NSA_DH

    def body(i, carry):
        r0 = pl.multiple_of(i * rows, rows)
        blk_hot = jnp.where(lane - NSA_DH == ((r0 + rowi) >> SEL_SHIFT), 1.0, 0.0)
        for src, gain_ref, aug, k_dst, v_dst, vrows in (
                (0, gsel_ref, blk_hot, ksa_ref, vst_ref, rows),
                (2, gwin_ref, jnp.zeros((rows, LANES), F32), kwa_ref, vwt_ref, WIN_KB)):
            kk = kv_ref[0, pl.ds(r0, rows), src * LANES:(src + 1) * LANES].astype(F32)
            vv = kv_ref[0, pl.ds(r0, rows), (src + 1) * LANES:(src + 2) * LANES].astype(F32)
            kn = _half_rms(kk, gain_ref[...], lane)
            k_dst[0, 0, pl.ds(r0, rows), :] = jnp.where(low, kn, aug).astype(BF16)
            k_dst[0, 1, pl.ds(r0, rows), :] = jnp.where(low, pltpu.roll(kn, NSA_DH, 1), aug).astype(BF16)
            vt = vv.T
            for g in range(NSA_KV):
                vg = jnp.concatenate([vt[g * NSA_DH:(g + 1) * NSA_DH], ones_row], axis=0).astype(BF16)
                for u in range(rows // vrows):
                    v_dst[0, g, i * (rows // vrows) + u] = vg[:, u * vrows:(u + 1) * vrows]
        for u in range(rows // Q_BLOCK):
            xt = nq_ref[0, pl.ds(r0 + u * Q_BLOCK, Q_BLOCK), :].astype(F32).T
            for g in range(NSA_KV):
                cols = []
                for h in range(NSA_HPG):
                    xh = xt[(g * NSA_HPG + h) * NSA_DH:(g * NSA_HPG + h + 1) * NSA_DH]
                    ms = jnp.mean(xh * xh, axis=0, keepdims=True)
                    cols.append(xh * lax.rsqrt(ms + EPS) * qg_ref[...])
                qt_ref[0, g, i * (rows // Q_BLOCK) + u] = jnp.concatenate(cols, axis=1).astype(BF16)
        return carry

    lax.fori_loop(0, S // rows, body, 0)

    nc = S // CMP_STRIDE
    outs = []
    for j, src_ref in ((0, kc_ref), (1, vc_ref)):
        bias = jnp.dot(pos_ref[j].astype(BF16), w1_ref[j], preferred_element_type=F32)[0:1] + b1_ref[j]
        pj = jnp.dot(src_ref[0], w1c_ref[j], preferred_element_type=F32)
        per_g = []
        for g in range(NSA_KV):
            first = pj[:, (2 * g) * CMP_HID:(2 * g + 1) * CMP_HID]
            second = pj[:, (2 * g + 1) * CMP_HID:(2 * g + 2) * CMP_HID]
            hid = _gelu_tanh(first + pltpu.roll(second, nc - 1, 0) + bias).astype(BF16)
            if j == 0:
                per_g.append(jnp.dot(hid, w2_ref[0], preferred_element_type=F32))
            else:
                per_g.append(lax.dot_general(w2_ref[1, :NSA_DH, :], hid, (((1,), (1,)), ((), ())),
                                             preferred_element_type=F32))
        outs.append(per_g)
    for g in range(NSA_KV):
        kc = outs[0][g]
        ms = jnp.sum(kc * kc, axis=-1, keepdims=True) * (1.0 / NSA_DH)
        kca_ref[0, g] = (kc * lax.rsqrt(ms + EPS) * gcmp_ref[...]).astype(BF16)
        vct_ref[0, g] = outs[1][g].astype(BF16)


def _nsa_prep(proj3, kc2, vc2, gsel, gwin, gcmp, qgain, pos8, w1, w1c, b1, w2p):
    B, S, _ = proj3.shape
    nc = S // CMP_STRIDE
    rows = SEL_CHUNK
    full = lambda shp: pl.BlockSpec(shp, lambda b: (0,) * len(shp))
    whole = lambda shp: pl.BlockSpec((1,) + shp, lambda b: (b,) + (0,) * len(shp))
    shapes = [(NSA_KV, S, LANES), (NSA_KV, S // SEL_CHUNK, V_ROWS, SEL_CHUNK),
              (NSA_KV, S, LANES), (NSA_KV, S // WIN_KB, V_ROWS, WIN_KB),
              (NSA_KV, nc, LANES), (NSA_KV, NSA_DH, nc),
              (NSA_KV, S // Q_BLOCK, NSA_DH, NSA_HPG * Q_BLOCK)]
    return pl.pallas_call(
        functools.partial(_prep_kernel, S=S, rows=rows),
        grid=(B,),
        in_specs=[
            pl.BlockSpec((1, S, 4 * NSA_KVW), lambda b: (b, 0, P_KV // (4 * NSA_KVW))),
            pl.BlockSpec((1, S, NSA_Q), lambda b: (b, 0, P_NQ // NSA_Q)),
            pl.BlockSpec((1, nc, CMP_STRIDE * NSA_KVW), lambda b: (b, 0, 0)),
            pl.BlockSpec((1, nc, CMP_STRIDE * NSA_KVW), lambda b: (b, 0, 0)),
            full((1, LANES)), full((1, LANES)), full((1, LANES)), full(qgain.shape),
            full(pos8.shape), full(w1.shape), full(w1c.shape), full(b1.shape), full(w2p.shape),
        ],
        out_specs=[whole(s) for s in shapes],
        out_shape=[jax.ShapeDtypeStruct((B,) + s, BF16) for s in shapes],
        compiler_params=_cparams(("parallel",)),
        name="nsa_prep",
    )(proj3, proj3, kc2, vc2, gsel, gwin, gcmp, qgain, pos8, w1, w1c, b1, w2p)


def _sel_table_geometry():
    last_start = (REL_BUCKETS // 2) * (REL_MAX_DIST / (REL_BUCKETS // 2)) ** (
        (REL_BUCKETS - REL_BUCKETS // 2 - 1) / (REL_BUCKETS - REL_BUCKETS // 2))
    dsat = int(math.ceil((last_start + 16 + SEL_CHUNK) / LANES)) * LANES
    return dsat, dsat + SEL_CHUNK


def _attn_kernel(q_ref, gate_ref, kca_ref, vct_ref, ksa_ref, vst_ref, kwa_ref, vwt_ref, tabc_ref,
                 tabw_ref, tabs_ref, ovl_ref, o_ref, m_ref, acc_ref, sa_ref, sb_ref, sw_ref,
                 *, nsel, dsat):
    qi = pl.program_id(2)
    QB = Q_BLOCK
    R = NSA_HPG * QB
    DH = NSA_DH

    qt = q_ref[0, 0, 0]
    qta = jnp.concatenate([qt, jnp.zeros_like(qt)], axis=0)

    n_sub = QB // WIN_KB
    vts, kws, kvalid = [], [], []
    for r in range(WIN_BLOCKS):
        kb = qi * n_sub - WINDOW // WIN_KB + r
        kbc = jnp.maximum(kb, 0)
        kws.append(kwa_ref[0, 0, pl.ds(pl.multiple_of(kbc * WIN_KB, WIN_KB), WIN_KB), :])
        vts.append(vwt_ref[0, 0, kbc])
        kvalid.append(kb >= 0)
    for u in range(n_sub):
        q_u = jnp.concatenate([qta[:, h * QB + u * WIN_KB:h * QB + (u + 1) * WIN_KB] for h in range(NSA_HPG)],
                              axis=1)
        for r in range(WIN_SUB):
            row = (u * WIN_SUB + r) * WIN_KB
            sw_ref[row:row + WIN_KB, :] = jnp.dot(kws[u + r], q_u, preferred_element_type=F32)

    def hcols(h):
        return slice(h * QB, (h + 1) * QB)

    def scores(c, q_aug, h=None):
        k0 = pl.multiple_of(c * SEL_CHUNK, SEL_CHUNK)
        delta = qi * QB - c * SEL_CHUNK
        start = pl.multiple_of(dsat - jnp.minimum(delta, dsat), LANES)
        cs = slice(None) if h is None else hcols(h)
        return (jnp.dot(ksa_ref[0, 0, pl.ds(k0, SEL_CHUNK), :], q_aug[:, cs], preferred_element_type=F32)
                + tabs_ref[0, 0, pl.ds(start, SEL_CHUNK), cs])

    s = jnp.dot(kca_ref[0, 0], qta, preferred_element_type=F32) + tabc_ref[0, 0]
    mx = jnp.max(s, axis=0, keepdims=True)
    e = jnp.exp2(s - mx)
    inv = jnp.where(mx > 0.5 * NEG_BIG, 1.0 / jnp.sum(e, axis=0, keepdims=True), 0.0)
    p = e * inv
    o_c = jnp.dot(vct_ref[0, 0], p.astype(BF16), preferred_element_type=F32)

    psum = p[:, 0:QB] + p[:, QB:2 * QB] + p[:, 2 * QB:3 * QB] + p[:, 3 * QB:4 * QB]
    p_hi = psum.astype(BF16)
    p_lo = (psum - p_hi.astype(F32)).astype(BF16)
    imp = (jnp.dot(ovl_ref[...], p_hi, preferred_element_type=F32)
           + jnp.dot(ovl_ref[...], p_lo, preferred_element_type=F32))
    t = qi * QB + lax.broadcasted_iota(jnp.int32, (SEL_ROWS, QB), 1)
    jrow = lax.broadcasted_iota(jnp.int32, (SEL_ROWS, QB), 0)
    jcur = t >> SEL_SHIFT
    n_top = min(SEL_TOPK, nsel)
    forced = (jrow == 0) | (jrow == jcur) | (jrow == jcur - 1)
    work = jnp.where(forced, -1.0, jnp.where(jrow * SEL_BLOCK <= t, imp, -1.0))
    work = jnp.where(jrow < nsel, work, -5.0)
    jf = jrow.astype(F32)
    sel = jnp.where(forced, 1.0, 0.0)
    for _ in range(n_top - 3):
        best = jnp.max(work, axis=0, keepdims=True)
        first = jnp.min(jnp.where(work == best, jf, float(SEL_ROWS)), axis=0, keepdims=True)
        hit = jf == first
        sel = jnp.where(hit, 1.0, sel)
        work = jnp.where(hit, -3.0, work)
    sel = jnp.where(jcur < n_top, jnp.where(jrow < n_top, 1.0, 0.0), sel)
    selneg = jnp.concatenate([jnp.where(sel > 0.5, 0.0, NEG_BIG)] * NSA_HPG, axis=1)
    qts = jnp.concatenate([qt, selneg.astype(BF16)], axis=0)
    sa_ref[...] = scores(0, qts)

    o_wu = []
    for u in range(n_sub):
        tiles = []
        for r in range(WIN_SUB):
            row = (u * WIN_SUB + r) * WIN_KB
            slab = pl.multiple_of(jnp.where(kvalid[u + r], r, WIN_SUB) * WIN_KB, WIN_KB)
            tiles.append(sw_ref[row:row + WIN_KB, :] + tabw_ref[0, 0, pl.ds(slab, WIN_KB), :])
        sw = jnp.concatenate(tiles, axis=0)
        ew = jnp.exp2(sw - jnp.max(sw, axis=0, keepdims=True)).astype(BF16)
        o_wu.append(jnp.dot(jnp.concatenate(vts[u:u + WIN_SUB], axis=1), ew, preferred_element_type=F32))
    o_w = jnp.concatenate([o_wu[u][:, h * WIN_KB:(h + 1) * WIN_KB]
                           for h in range(NSA_HPG) for u in range(n_sub)], axis=1)

    m_ref[...] = jnp.full_like(m_ref, NEG_BIG)
    acc_ref[...] = jnp.zeros_like(acc_ref)

    def consume(s_ref, c, heads=range(NSA_HPG)):
        vt = vst_ref[0, 0, c]
        for h in heads:
            cs = hcols(h)
            sc = s_ref[:, cs]
            m_old = m_ref[:, cs]
            m_new = jnp.maximum(m_old, jnp.max(sc, axis=0, keepdims=True))
            alpha = jnp.exp2(m_old - m_new)
            pe = jnp.exp2(sc - m_new).astype(BF16)
            acc_ref[:, cs] = alpha * acc_ref[:, cs] + jnp.dot(vt, pe, preferred_element_type=F32)
            m_ref[:, cs] = m_new

    def fill_and_consume(dst_ref, c_next, src_ref, c):
        dst_ref[:, hcols(0)] = scores(c_next, qts, 0)
        for h in range(NSA_HPG):
            if h + 1 < NSA_HPG:
                dst_ref[:, hcols(h + 1)] = scores(c_next, qts, h + 1)
            consume(src_ref, c, [h])

    n_chunks = (qi * QB + QB - 1) // SEL_CHUNK + 1
    n_pairs = (n_chunks - 1) // 2

    def sel_pair(pr, carry):
        c = 2 * pr
        fill_and_consume(sb_ref, c + 1, sa_ref, c)
        fill_and_consume(sa_ref, c + 2, sb_ref, c + 1)
        return carry

    lax.fori_loop(0, n_pairs, sel_pair, 0)

    @pl.when(n_chunks % 2 == 0)
    def _():
        fill_and_consume(sb_ref, n_chunks - 1, sa_ref, n_chunks - 2)
        consume(sb_ref, n_chunks - 1)

    @pl.when(n_chunks % 2 == 1)
    def _():
        consume(sa_ref, n_chunks - 1)

    acc = acc_ref[...]

    gt = jax.nn.sigmoid(gate_ref[0].astype(F32)).T
    o_s = acc[0:DH] / acc[DH:DH + 1]
    o_w = o_w[0:DH] / o_w[DH:DH + 1]
    heads = []
    for h in range(NSA_HPG):
        cs = slice(h * QB, (h + 1) * QB)
        heads.append(gt[3 * h:3 * h + 1] * o_c[:, cs] + gt[3 * h + 1:3 * h + 2] * o_s[:, cs]
                     + gt[3 * h + 2:3 * h + 3] * o_w[:, cs])
    o_ref[0] = jnp.concatenate(heads, axis=0).T.astype(BF16)


def _nsa_attn(proj3, prep, tabc, tabw, tabs, ovl, dsat):
    B, S, _ = proj3.shape
    ksa, vst, kwa, vwt, kca, vct, qt = prep
    nq = S // Q_BLOCK
    nc = S // CMP_STRIDE
    nsel = S // SEL_BLOCK
    R = NSA_HPG * Q_BLOCK
    gw = NSA_HPG * NSA_DH
    per_bg = lambda a: pl.BlockSpec((1, 1) + a.shape[2:], lambda b, g, i: (b, g) + (0,) * (a.ndim - 2))
    return pl.pallas_call(
        functools.partial(_attn_kernel, nsel=nsel, dsat=dsat),
        grid=(B, NSA_KV, nq),
        in_specs=[
            pl.BlockSpec((1, 1, 1, NSA_DH, R), lambda b, g, i: (b, g, i, 0, 0)),
            pl.BlockSpec((1, Q_BLOCK, LANES), lambda b, g, i: (b, i, P_GATE // LANES + g)),
            per_bg(kca), per_bg(vct), per_bg(ksa), per_bg(vst), per_bg(kwa), per_bg(vwt),
            pl.BlockSpec((1, 1, nc, R), lambda b, g, i: (g, i, 0, 0)),
            pl.BlockSpec((1, 1) + tabw.shape[2:], lambda b, g, i: (g, 0, 0, 0)),
            pl.BlockSpec((1, 1) + tabs.shape[2:], lambda b, g, i: (g, 0, 0, 0)),
            pl.BlockSpec(ovl.shape, lambda b, g, i: (0, 0)),
        ],
        out_specs=pl.BlockSpec((1, Q_BLOCK, gw), lambda b, g, i: (b, i, g)),
        out_shape=jax.ShapeDtypeStruct((B, S, NSA_Q), BF16),
        scratch_shapes=[pltpu.VMEM((1, R), F32), pltpu.VMEM((V_ROWS, R), F32),
                        pltpu.VMEM((SEL_CHUNK, R), F32), pltpu.VMEM((SEL_CHUNK, R), F32),
                        pltpu.VMEM((Q_BLOCK // WIN_KB * WIN_SUB * WIN_KB, NSA_HPG * WIN_KB), F32)],
        compiler_params=_cparams(("parallel", "parallel", "arbitrary")),
        name="nsa_attn",
    )(qt, proj3, kca, vct, ksa, vst, kwa, vwt, tabc, tabw, tabs, ovl)


def _rel_bucket(d):
    max_exact = REL_BUCKETS // 2
    d = jnp.maximum(d, 0)
    df = jnp.maximum(d, 1).astype(F32)
    large = max_exact + (jnp.log(df / max_exact) / math.log(REL_MAX_DIST / max_exact)
                         * (REL_BUCKETS - max_exact)).astype(jnp.int32)
    return jnp.where(d < max_exact, d, jnp.minimum(large, REL_BUCKETS - 1))


def _bias_kernel(rb_ref, bkt_ref, o_ref, *, rows, cols):
    g = pl.program_id(0)
    step = 32
    for r0 in range(0, rows, step):
        bkt = bkt_ref[0, r0:r0 + step, :]
        acc = [jnp.full(bkt.shape, NEG_BIG, F32) for _ in range(NSA_HPG)]
        for b in range(REL_BUCKETS):
            hit = bkt == b
            for h in range(NSA_HPG):
                acc[h] = jnp.where(hit, rb_ref[b, g * NSA_HPG + h] * LOG2E, acc[h])
        for h in range(NSA_HPG):
            o_ref[0, 0, r0:r0 + step, h * cols:(h + 1) * cols] = acc[h]


def _bias_table(rel_bias, bucket):
    nt, rows, cols = bucket.shape
    return pl.pallas_call(
        functools.partial(_bias_kernel, rows=rows, cols=cols),
        grid=(NSA_KV, nt),
        in_specs=[
            pl.BlockSpec(memory_space=pltpu.SMEM),
            pl.BlockSpec((1, rows, cols), lambda g, t: (t, 0, 0)),
        ],
        out_specs=pl.BlockSpec((1, 1, rows, NSA_HPG * cols), lambda g, t: (g, t, 0, 0)),
        out_shape=jax.ShapeDtypeStruct((NSA_KV, nt, rows, NSA_HPG * cols), F32),
        compiler_params=_cparams(("parallel", "parallel")),
        name="bias_table",
    )(rel_bias, bucket)


def _bias_tables(rel_bias, S, dsat, strip_len):
    nq = S // Q_BLOCK
    nc = S // CMP_STRIDE
    n_cmp = (S - CMP_BLOCK) // CMP_STRIDE + 1
    ar = lambda n: jnp.arange(n, dtype=jnp.int32)
    bucket = lambda d, valid: jnp.where(valid, _rel_bucket(d), -1)
    i = ar(Q_BLOCK)[None, None, :]
    c = ar(nc)[None, :, None]
    d_c = ar(nq)[:, None, None] * Q_BLOCK + i - (c * CMP_STRIDE + CMP_BLOCK - 1)
    tabc = _bias_table(rel_bias, bucket(d_c, (d_c >= 0) & (c < n_cmp)))
    jw = ar((WIN_SUB + 1) * WIN_KB)[None, :, None]
    d_w = ar(WIN_KB)[None, None, :] + WINDOW - jw
    tabw = _bias_table(rel_bias, bucket(d_w, (d_w >= 0) & (d_w < WINDOW) & (jw < WIN_SUB * WIN_KB)))
    d_s = i - ar(strip_len)[None, :, None] + dsat
    tabs = _bias_table(rel_bias, bucket(d_s, d_s >= 0))
    return tabc, tabw, tabs


def _overlap_matrix(S):
    nc = S // CMP_STRIDE
    nsel = S // SEL_BLOCK
    n_cmp = (S - CMP_BLOCK) // CMP_STRIDE + 1
    c = np.arange(nc)[None, :]
    sel_start = (np.arange(SEL_ROWS) * SEL_BLOCK)[:, None]
    ovl = ((c * CMP_STRIDE < sel_start + SEL_BLOCK) & (c * CMP_STRIDE + CMP_BLOCK - 1 >= sel_start)
           & (c < n_cmp) & (np.arange(SEL_ROWS)[:, None] < nsel))
    return jnp.asarray(ovl, BF16)


def _layer_params(l, lower_bounds, norm_mix, w_in, hg_out_gain, nsa_qk_gain, cmp_pos, cmp_w1, cmp_b1,
                  cmp_w2, w_branch_a, w_branch_b, w_out, norm_mlp, w_up, w_down):
    splits = (HG_K, HG_K, HG_V, HG_V, NSA_Q, NSA_KVW, NSA_KVW, NSA_KVW, NSA_KVW, NSA_KVW, NSA_KVW,
              3 * NSA_HEADS, D_MODEL, D_MODEL)
    offs = np.concatenate([[0], np.cumsum(splits)])
    col = lambda k: w_in[l][:, offs[k]:offs[k + 1]]
    gate_w = col(11)
    zpad = jnp.zeros((D_MODEL, LANES - 3 * NSA_HPG), F32)
    gate_cols = []
    for g in range(NSA_KV):
        gate_cols += [gate_w[:, g * 3 * NSA_HPG:(g + 1) * 3 * NSA_HPG], zpad]
    w_main = jnp.concatenate([col(12), col(13), col(0), col(2), col(3),
                              col(7), col(8), col(9), col(10), col(4)] + gate_cols, axis=1).astype(BF16)
    w_cmp = jnp.concatenate([col(1), col(5), col(6)], axis=1).astype(BF16)

    lb = lower_bounds[l]
    lbp = jnp.stack([jnp.log(lb), jnp.log1p(-lb), 1.0 - lb])

    w1 = cmp_w1[l].reshape(2, 2, CMP_STRIDE, NSA_DH, CMP_HID)
    w1c = jnp.zeros((2, CMP_STRIDE, NSA_KV, NSA_DH, NSA_KV, 2, CMP_HID), F32)
    for g in range(NSA_KV):
        w1c = w1c.at[:, :, g, :, g, :, :].set(w1.transpose(0, 2, 3, 1, 4))
    w1c = w1c.reshape(2, CMP_STRIDE * NSA_KVW, NSA_KV * 2 * CMP_HID).astype(BF16)
    pos8 = jnp.zeros((2, 8, CMP_BLOCK * NSA_DH), F32).at[:, 0, :].set(cmp_pos[l].reshape(2, -1))
    w2p = jnp.zeros((2, CMP_HID, LANES), F32)
    w2p = w2p.at[0, :, :NSA_DH].set(cmp_w2[l, 0]).at[1, :NSA_DH, :].set(cmp_w2[l, 1].T).astype(BF16)
    qk = nsa_qk_gain[l]
    z64 = jnp.zeros((NSA_DH,), F32)
    return dict(
        norm_mix=norm_mix[l][None, :], w_main=w_main, w_cmp=w_cmp, lbp=lbp,
        hg_gain=hg_out_gain[l][None, :],
        gsel=jnp.concatenate([qk[2], qk[2]])[None, :], gwin=jnp.concatenate([qk[3], qk[3]])[None, :],
        gcmp=jnp.concatenate([qk[1], z64])[None, :],
        qgain=jnp.broadcast_to((qk[0] * (NSA_DH ** -0.5 * LOG2E))[:, None], (NSA_DH, Q_BLOCK)),
        pos8=pos8, w1=cmp_w1[l].astype(BF16), w1c=w1c, b1=cmp_b1[l][:, None, :], w2p=w2p,
        wa=w_branch_a[l].astype(BF16), wb=w_branch_b[l].astype(BF16), wo=w_out[l].astype(BF16),
        norm_mlp=norm_mlp[l][None, :], wu=w_up[l].astype(BF16), wd=w_down[l].astype(BF16))


def kernel(x, rel_bias, hg_lb_logits, norm_mix, w_in, hg_out_gain, nsa_qk_gain, cmp_pos, cmp_w1, cmp_b1, cmp_w2, w_branch_a, w_branch_b, w_out, norm_mlp, w_up, w_down):
    B, S, D = x.shape
    T = B * S
    depth = w_in.shape[0]
    assert D == D_MODEL and S % SEL_CHUNK == 0 and S >= WINDOW and S // SEL_BLOCK <= SEL_ROWS
    assert S % Q_BLOCK == 0 and Q_BLOCK % WIN_KB == 0 and WINDOW % WIN_KB == 0
    assert FORCE_BONUS > NSA_HPG and SEL_TOPK >= 3
    lb_cum = jnp.cumsum(jax.nn.softmax(hg_lb_logits.astype(F32), axis=0), axis=0)
    lower_bounds = lb_cum - lb_cum[0:1]

    dsat, strip_len = _sel_table_geometry()
    tabc, tabw, tabs = _bias_tables(rel_bias, S, dsat, strip_len)
    ovl = _overlap_matrix(S)
    nc = S // CMP_STRIDE
    tm = min(1024, T)

    x2 = x.reshape(T, D)
    for l in range(depth):
        p = _layer_params(l, lower_bounds, norm_mix, w_in, hg_out_gain, nsa_qk_gain, cmp_pos, cmp_w1,
                          cmp_b1, cmp_w2, w_branch_a, w_branch_b, w_out, norm_mlp, w_up, w_down)
        proj, z, kc, vc = _rms_proj(x2, p["norm_mix"], p["w_main"], p["w_cmp"], tm, P_TILE)
        proj3 = proj.reshape(B, S, P_COLS)
        kc2 = kc.reshape(B, nc, CMP_STRIDE * NSA_KVW)
        vc2 = vc.reshape(B, nc, CMP_STRIDE * NSA_KVW)
        o_a = _hgrn(proj3, z.reshape(B, S, HG_K), p["lbp"], p["hg_gain"])
        prep = _nsa_prep(proj3, kc2, vc2, p["gsel"], p["gwin"], p["gcmp"], p["qgain"], p["pos8"], p["w1"],
                         p["w1c"], p["b1"], p["w2p"])
        o_b = _nsa_attn(proj3, prep, tabc, tabw, tabs, ovl, dsat)
        x2 = _merge(x2, o_a.reshape(T, HG_V), o_b.reshape(T, NSA_Q), proj, p["wa"], p["wb"], p["wo"],
                    min(1024, T))
        x2 = _mlp(x2, p["norm_mlp"], p["wu"], p["wd"], tm, 1024)
    return x2.reshape(B, S, D)
```

```python
import functools
import math

import numpy as np
import jax
import jax.numpy as jnp
from jax import lax
from jax.experimental import pallas as pl
from jax.experimental.pallas import tpu as pltpu

F32 = jnp.float32
BF16 = jnp.bfloat16

D_MODEL = 1024
HG_HEADS = 4
HG_DK = 128
HG_DV = 128
HG_K = HG_HEADS * HG_DK
HG_V = HG_HEADS * HG_DV
NSA_HEADS = 8
NSA_KV = 2
NSA_DH = 64
NSA_HPG = NSA_HEADS // NSA_KV
NSA_Q = NSA_HEADS * NSA_DH
NSA_KVW = NSA_KV * NSA_DH
CMP_BLOCK = 32
CMP_STRIDE = 16
CMP_HID = 128
SEL_BLOCK = 64
SEL_TOPK = 8
WINDOW = 512
FORCE_BONUS = 1000.0
REL_BUCKETS = 32
REL_MAX_DIST = 1024
D_FF = 4 * D_MODEL
EPS = 1e-6
NEG_BIG = -1e30
LOG2E = math.log2(math.e)

LANES = 128
VMEM_LIMIT = 56 * 1024 * 1024

P_GAB = 0
P_HG = P_GAB + 2 * D_MODEL
P_KV = P_HG + 3 * HG_K
P_NQ = P_KV + 4 * NSA_KVW
P_GATE = P_NQ + NSA_Q
P_COLS = P_GATE + NSA_KV * LANES
P_TILE = P_COLS // 2
ROW_BLOCKS = 4

HG_CHUNK = 128
HG_TS = 2048
Q_BLOCK = 256
WIN_KB = 128
SEL_CHUNK = 512
WIN_BLOCKS = (WINDOW + Q_BLOCK) // WIN_KB
WIN_SUB = WINDOW // WIN_KB + 1
SEL_SHIFT = int(math.log2(SEL_BLOCK))
SEL_ROWS = LANES - NSA_DH
V_ROWS = NSA_DH + 16


def _cparams(sem):
    return pltpu.CompilerParams(dimension_semantics=sem, vmem_limit_bytes=VMEM_LIMIT)


def _rms_rows(x, gain):
    ms = jnp.mean(x * x, axis=-1, keepdims=True)
    return x * lax.rsqrt(ms + EPS) * gain


def _proj_kernel(x_ref, g_ref, w_ref, wx_ref, o_ref, z_ref, kc_ref, vc_ref, h_ref, ck_ref, cv_ref):
    j = pl.program_id(1)
    rb = x_ref.shape[0] // ROW_BLOCKS

    @pl.when(j == 0)
    def _():
        for r in range(ROW_BLOCKS):
            rows = slice(r * rb, (r + 1) * rb)
            h = _rms_rows(x_ref[rows, :], g_ref[...]).astype(BF16)
            h_ref[rows, :] = h
            o_ref[rows, :] = jnp.dot(h, w_ref[...], preferred_element_type=F32).astype(BF16)
            c = jnp.dot(h, wx_ref[...], preferred_element_type=F32)
            z_ref[rows, :] = c[:, :HG_K]
            ck_ref[rows, :] = c[:, HG_K:HG_K + NSA_KVW]
            cv_ref[rows, :] = c[:, HG_K + NSA_KVW:]
        n16 = ck_ref.shape[0] // CMP_STRIDE
        for t in range(CMP_STRIDE):
            lanes = slice(t * NSA_KVW, (t + 1) * NSA_KVW)
            kc_ref[:, lanes] = ck_ref[pl.ds(t, n16, stride=CMP_STRIDE), :].astype(BF16)
            vc_ref[:, lanes] = cv_ref[pl.ds(t, n16, stride=CMP_STRIDE), :].astype(BF16)

    @pl.when(j != 0)
    def _():
        o_ref[...] = jnp.dot(h_ref[...], w_ref[...], preferred_element_type=F32).astype(BF16)


def _rms_proj(x2, gain, w, wx, tm, tn):
    T, D = x2.shape
    N = w.shape[1]
    cmp_sds = jax.ShapeDtypeStruct((T // CMP_STRIDE, CMP_STRIDE * NSA_KVW), BF16)
    return pl.pallas_call(
        _proj_kernel,
        grid=(T // tm, N // tn),
        in_specs=[
            pl.BlockSpec((tm, D), lambda i, j: (i, 0)),
            pl.BlockSpec((1, D), lambda i, j: (0, 0)),
            pl.BlockSpec((D, tn), lambda i, j: (0, j)),
            pl.BlockSpec(wx.shape, lambda i, j: (0, 0)),
        ],
        out_specs=[
            pl.BlockSpec((tm, tn), lambda i, j: (i, j)),
            pl.BlockSpec((tm, HG_K), lambda i, j: (i, 0)),
            pl.BlockSpec((tm // CMP_STRIDE, CMP_STRIDE * NSA_KVW), lambda i, j: (i, 0)),
            pl.BlockSpec((tm // CMP_STRIDE, CMP_STRIDE * NSA_KVW), lambda i, j: (i, 0)),
        ],
        out_shape=[jax.ShapeDtypeStruct((T, N), BF16), jax.ShapeDtypeStruct((T, HG_K), F32),
                   cmp_sds, cmp_sds],
        scratch_shapes=[pltpu.VMEM((tm, D), BF16), pltpu.VMEM((tm, NSA_KVW), F32),
                        pltpu.VMEM((tm, NSA_KVW), F32)],
        compiler_params=_cparams(("parallel", "arbitrary")),
        name="rms_proj",
    )(x2, gain, w, wx)


def _mlp_kernel(x_ref, g_ref, wu_ref, wd_ref, o_ref, h_ref):
    f = pl.program_id(1)

    rb = x_ref.shape[0] // ROW_BLOCKS

    def up_down(h):
        u = jnp.dot(h, wu_ref[...], preferred_element_type=F32)
        u = jnp.square(jnp.maximum(u, 0.0)).astype(BF16)
        return jnp.dot(u, wd_ref[...], preferred_element_type=F32)

    @pl.when(f == 0)
    def _():
        for r in range(ROW_BLOCKS):
            rows = slice(r * rb, (r + 1) * rb)
            x = x_ref[rows, :]
            h = _rms_rows(x, g_ref[...]).astype(BF16)
            h_ref[rows, :] = h
            o_ref[rows, :] = x + up_down(h)

    @pl.when(f != 0)
    def _():
        o_ref[...] += up_down(h_ref[...])


def _mlp(x2, gain, wu, wd, tm, tf):
    T, D = x2.shape
    FF = wu.shape[1]
    return pl.pallas_call(
        _mlp_kernel,
        grid=(T // tm, FF // tf),
        in_specs=[
            pl.BlockSpec((tm, D), lambda i, f: (i, 0)),
            pl.BlockSpec((1, D), lambda i, f: (0, 0)),
            pl.BlockSpec((D, tf), lambda i, f: (0, f)),
            pl.BlockSpec((tf, D), lambda i, f: (f, 0)),
        ],
        out_specs=pl.BlockSpec((tm, D), lambda i, f: (i, 0)),
        out_shape=jax.ShapeDtypeStruct((T, D), F32),
        scratch_shapes=[pltpu.VMEM((tm, D), BF16)],
        compiler_params=_cparams(("parallel", "arbitrary")),
        name="mlp",
    )(x2, gain, wu, wd)


def _merge_kernel(x_ref, oa_ref, ob_ref, ga_ref, gb_ref, wa_ref, wb_ref, wo_ref, o_ref):
    ya = jnp.dot(oa_ref[...], wa_ref[...], preferred_element_type=F32)
    yb = jnp.dot(ob_ref[...], wb_ref[...], preferred_element_type=F32)
    mixed = (jax.nn.sigmoid(ga_ref[...].astype(F32)) * ya
             + jax.nn.sigmoid(gb_ref[...].astype(F32)) * yb)
    o_ref[...] = x_ref[...] + jnp.dot(mixed.astype(BF16), wo_ref[...], preferred_element_type=F32)


def _merge(x2, oa, ob, proj, wa, wb, wo, tm):
    T, D = x2.shape
    ga_blk = P_GAB // D
    return pl.pallas_call(
        _merge_kernel,
        grid=(T // tm,),
        in_specs=[
            pl.BlockSpec((tm, D), lambda i: (i, 0)),
            pl.BlockSpec((tm, HG_V), lambda i: (i, 0)),
            pl.BlockSpec((tm, NSA_Q), lambda i: (i, 0)),
            pl.BlockSpec((tm, D), lambda i: (i, ga_blk)),
            pl.BlockSpec((tm, D), lambda i: (i, ga_blk + 1)),
            pl.BlockSpec((HG_V, D), lambda i: (0, 0)),
            pl.BlockSpec((NSA_Q, D), lambda i: (0, 0)),
            pl.BlockSpec((D, D), lambda i: (0, 0)),
        ],
        out_specs=pl.BlockSpec((tm, D), lambda i: (i, 0)),
        out_shape=jax.ShapeDtypeStruct((T, D), F32),
        compiler_params=_cparams(("parallel",)),
        name="merge",
    )(x2, oa, ob, proj, proj, wa, wb, wo)


def _hgrn_consts(C):
    L = int(math.log2(C))
    idx = np.arange(C)
    mats = [(idx[None, :] <= idx[:, None]),
            (idx[None, :] > idx[:, None])]
    bmask = []
    for l in range(L):
        m = 1 << l
        r = (idx & ~(2 * m - 1)) + m
        lo = np.minimum(idx, r)[:, None]
        hi = np.maximum(idx, r)[:, None]
        mats.append((idx[None, :] > lo) & (idx[None, :] <= hi))
        blk = idx >> (l + 1)
        bmask.append(blk[:, None] == blk[None, :])
    bmask.append(idx[:, None] == idx[None, :])
    seg = np.concatenate(mats, axis=0).astype(np.float32)
    seg = np.concatenate([seg, seg], axis=1)
    return seg, np.stack(bmask).astype(np.float32), L


def _hgrn_kernel(q_ref, f_ref, i_ref, g_ref, lbp_ref, gain_ref, seg_ref, bm_ref, o_ref, st_ref,
                 k0_ref, sg0_ref, k1_ref, sg1_ref, *, C, L, nchunk):
    @pl.when(pl.program_id(1) == 0)
    def _():
        st_ref[...] = jnp.zeros_like(st_ref)

    rowi = lax.broadcasted_iota(jnp.int32, (C, HG_DK), 0)
    nt = (((1,), (1,)), ((), ()))
    tn = (((0,), (0,)), ((), ()))

    H = range(HG_HEADS)
    css = [slice(h * HG_DK, (h + 1) * HG_DK) for h in H]

    def gates(c, k_ref, sg_ref):
        r0 = pl.multiple_of(c * C, C)
        z = f_ref[0, pl.ds(r0, C), :]
        ez = jnp.exp(-jnp.abs(z))
        y = lbp_ref[1:2, :] + (jnp.minimum(z, 0.0) - jnp.log(1.0 + ez))
        log_lb = lbp_ref[0:1, :]
        logf = (jnp.maximum(log_lb, y) + jnp.log(1.0 + jnp.exp(-jnp.abs(log_lb - y)))) * LOG2E
        k_ref[...] = lbp_ref[2:3, :] * (jnp.where(z >= 0.0, ez, 1.0) / (1.0 + ez))
        g_hi = logf.astype(BF16)
        g_lo = (logf - g_hi.astype(F32)).astype(BF16)
        sg_ref[...] = jnp.dot(seg_ref[...], jnp.concatenate([g_hi, g_lo], axis=0),
                              preferred_element_type=F32)

    def mix(c, k_ref, sg_ref):
        r0 = pl.multiple_of(c * C, C)
        qbs = [q_ref[0, pl.ds(r0, C), cs] for cs in css]
        qs = [qb.astype(F32) for qb in qbs]
        ks = [k_ref[:, cs] for cs in css]
        ss = [lax.dot_general(qbs[h], ks[h].astype(BF16), nt, preferred_element_type=F32) * bm_ref[L]
              for h in H]
        for l in range(L):
            second = (rowi & (1 << l)) != 0
            for h in H:
                e = jnp.exp2(sg_ref[(2 + l) * C:(3 + l) * C, css[h]])
                ql = jnp.where(second, qs[h] * e, 0.0).astype(BF16)
                kl = jnp.where(second, 0.0, ks[h] * e).astype(BF16)
                ss[h] = ss[h] + lax.dot_general(ql, kl, nt, preferred_element_type=F32) * bm_ref[l]
        return r0, qs, ks, ss

    def mix_out(scores, k_ref, sg_ref):
        r0, qs, ks, ss = scores
        vbs = [i_ref[0, pl.ds(r0, C), cs] for cs in css]
        bs = [sg_ref[0:C, cs] for cs in css]
        sts = [st_ref[h] for h in H]
        outs = []
        for h in H:
            qd = (qs[h] * jnp.exp2(bs[h])).astype(BF16)
            o = lax.dot_general(qd, sts[h].astype(BF16), nt, preferred_element_type=F32)
            outs.append(o + jnp.dot(ss[h].astype(BF16), vbs[h], preferred_element_type=F32))
        for h in H:
            kd = (ks[h] * jnp.exp2(sg_ref[C:2 * C, css[h]])).astype(BF16)
            st_ref[h] = (sts[h] * jnp.exp2(bs[h][C - 1:C, :])
                         + lax.dot_general(vbs[h], kd, tn, preferred_element_type=F32))
        for h in H:
            gg = g_ref[0, pl.ds(r0, C), css[h]].astype(F32)
            o = _rms_rows(outs[h], gain_ref[...]) * (gg * jax.nn.sigmoid(gg))
            o_ref[0, pl.ds(r0, C), css[h]] = o.astype(BF16)

    gates(0, k0_ref, sg0_ref)

    def pair(pr, carry):
        c = 2 * pr
        sc0 = mix(c, k0_ref, sg0_ref)
        gates(c + 1, k1_ref, sg1_ref)
        mix_out(sc0, k0_ref, sg0_ref)
        sc1 = mix(c + 1, k1_ref, sg1_ref)
        gates(jnp.minimum(c + 2, nchunk - 1), k0_ref, sg0_ref)
        mix_out(sc1, k1_ref, sg1_ref)
        return carry

    lax.fori_loop(0, nchunk // 2, pair, 0)


def _hgrn(proj3, z3, lbp, gain, C=HG_CHUNK, ts=HG_TS):
    B, S, _ = proj3.shape
    ts = min(ts, S)
    assert (ts // C) % 2 == 0
    seg, bm, L = _hgrn_consts(C)
    blk = lambda k: pl.BlockSpec((1, ts, HG_K), lambda b, s: (b, s, P_HG // HG_K + k))
    return pl.pallas_call(
        functools.partial(_hgrn_kernel, C=C, L=L, nchunk=ts // C),
        grid=(B, S // ts),
        in_specs=[
            blk(0), pl.BlockSpec((1, ts, HG_K), lambda b, s: (b, s, 0)), blk(1), blk(2),
            pl.BlockSpec((3, HG_K), lambda b, s: (0, 0)),
            pl.BlockSpec((1, HG_DV), lambda b, s: (0, 0)),
            pl.BlockSpec(seg.shape, lambda b, s: (0, 0)),
            pl.BlockSpec(bm.shape, lambda b, s: (0, 0, 0)),
        ],
        out_specs=pl.BlockSpec((1, ts, HG_V), lambda b, s: (b, s, 0)),
        out_shape=jax.ShapeDtypeStruct((B, S, HG_V), BF16),
        scratch_shapes=[pltpu.VMEM((HG_HEADS, HG_DV, HG_DK), F32)]
        + 2 * [pltpu.VMEM((C, HG_K), F32), pltpu.VMEM(((2 + L) * C, HG_K), F32)],
        compiler_params=_cparams(("parallel", "arbitrary")),
        name="hgrn2",
    )(proj3, z3, proj3, proj3, lbp, gain, jnp.asarray(seg, BF16), jnp.asarray(bm))


def _half_rms(x, gain2, lane):
    sq = x * x
    s_lo = jnp.sum(jnp.where(lane < NSA_DH, sq, 0.0), axis=-1, keepdims=True)
    s_hi = jnp.sum(jnp.where(lane < NSA_DH, 0.0, sq), axis=-1, keepdims=True)
    ms = jnp.where(lane < NSA_DH, s_lo, s_hi) * (1.0 / NSA_DH)
    return x * lax.rsqrt(ms + EPS) * gain2


def _gelu_tanh(x):
    return 0.5 * x * (1.0 + jnp.tanh(math.sqrt(2.0 / math.pi) * (x + 0.044715 * (x * x * x))))


def _prep_kernel(kv_ref, nq_ref, kc_ref, vc_ref, gsel_ref, gwin_ref, gcmp_ref, qg_ref, pos_ref, w1_ref,
                 w1c_ref, b1_ref, w2_ref, ksa_ref, vst_ref, kwa_ref, vwt_ref, kca_ref, vct_ref, qt_ref,
                 *, S, rows):
    lane = lax.broadcasted_iota(jnp.int32, (rows, LANES), 1)
    rowi = lax.broadcasted_iota(jnp.int32, (rows, LANES), 0)
    ones_row = (lax.broadcasted_iota(jnp.int32, (V_ROWS - NSA_DH, rows), 0) == 0).astype(F32)
    low = lane < NSA_DH

    def body(i, carry):
        r0 = pl.multiple_of(i * rows, rows)
        blk_hot = jnp.where(lane - NSA_DH == ((r0 + rowi) >> SEL_SHIFT), 1.0, 0.0)
        for src, gain_ref, aug, k_dst, v_dst, vrows in (
                (0, gsel_ref, blk_hot, ksa_ref, vst_ref, rows),
                (2, gwin_ref, jnp.zeros((rows, LANES), F32), kwa_ref, vwt_ref, WIN_KB)):
            kk = kv_ref[0, pl.ds(r0, rows), src * LANES:(src + 1) * LANES].astype(F32)
            vv = kv_ref[0, pl.ds(r0, rows), (src + 1) * LANES:(src + 2) * LANES].astype(F32)
            kn = _half_rms(kk, gain_ref[...], lane)
            k_dst[0, 0, pl.ds(r0, rows), :] = jnp.where(low, kn, aug).astype(BF16)
            k_dst[0, 1, pl.ds(r0, rows), :] = jnp.where(low, pltpu.roll(kn, NSA_DH, 1), aug).astype(BF16)
            vt = vv.T
            for g in range(NSA_KV):
                vg = jnp.concatenate([vt[g * NSA_DH:(g + 1) * NSA_DH], ones_row], axis=0).astype(BF16)
                for u in range(rows // vrows):
                    v_dst[0, g, i * (rows // vrows) + u] = vg[:, u * vrows:(u + 1) * vrows]
        for u in range(rows // Q_BLOCK):
            xt = nq_ref[0, pl.ds(r0 + u * Q_BLOCK, Q_BLOCK), :].astype(F32).T
            for g in range(NSA_KV):
                cols = []
                for h in range(NSA_HPG):
                    xh = xt[(g * NSA_HPG + h) * NSA_DH:(g * NSA_HPG + h + 1) * NSA_DH]
                    ms = jnp.mean(xh * xh, axis=0, keepdims=True)
                    cols.append(xh * lax.rsqrt(ms + EPS) * qg_ref[...])
                qt_ref[0, g, i * (rows // Q_BLOCK) + u] = jnp.concatenate(cols, axis=1).astype(BF16)
        return carry

    lax.fori_loop(0, S // rows, body, 0)

    nc = S // CMP_STRIDE
    outs = []
    for j, src_ref in ((0, kc_ref), (1, vc_ref)):
        bias = jnp.dot(pos_ref[j].astype(BF16), w1_ref[j], preferred_element_type=F32)[0:1] + b1_ref[j]
        pj = jnp.dot(src_ref[0], w1c_ref[j], preferred_element_type=F32)
        per_g = []
        for g in range(NSA_KV):
            first = pj[:, (2 * g) * CMP_HID:(2 * g + 1) * CMP_HID]
            second = pj[:, (2 * g + 1) * CMP_HID:(2 * g + 2) * CMP_HID]
            hid = _gelu_tanh(first + pltpu.roll(second, nc - 1, 0) + bias).astype(BF16)
            if j == 0:
                per_g.append(jnp.dot(hid, w2_ref[0], preferred_element_type=F32))
            else:
                per_g.append(lax.dot_general(w2_ref[1, :NSA_DH, :], hid, (((1,), (1,)), ((), ())),
                                             preferred_element_type=F32))
        outs.append(per_g)
    for g in range(NSA_KV):
        kc = outs[0][g]
        ms = jnp.sum(kc * kc, axis=-1, keepdims=True) * (1.0 / NSA_DH)
        kca_ref[0, g] = (kc * lax.rsqrt(ms + EPS) * gcmp_ref[...]).astype(BF16)
        vct_ref[0, g] = outs[1][g].astype(BF16)


def _nsa_prep(proj3, kc2, vc2, gsel, gwin, gcmp, qgain, pos8, w1, w1c, b1, w2p):
    B, S, _ = proj3.shape
    nc = S // CMP_STRIDE
    rows = SEL_CHUNK
    full = lambda shp: pl.BlockSpec(shp, lambda b: (0,) * len(shp))
    whole = lambda shp: pl.BlockSpec((1,) + shp, lambda b: (b,) + (0,) * len(shp))
    shapes = [(NSA_KV, S, LANES), (NSA_KV, S // SEL_CHUNK, V_ROWS, SEL_CHUNK),
              (NSA_KV, S, LANES), (NSA_KV, S // WIN_KB, V_ROWS, WIN_KB),
              (NSA_KV, nc, LANES), (NSA_KV, NSA_DH, nc),
              (NSA_KV, S // Q_BLOCK, NSA_DH, NSA_HPG * Q_BLOCK)]
    return pl.pallas_call(
        functools.partial(_prep_kernel, S=S, rows=rows),
        grid=(B,),
        in_specs=[
            pl.BlockSpec((1, S, 4 * NSA_KVW), lambda b: (b, 0, P_KV // (4 * NSA_KVW))),
            pl.BlockSpec((1, S, NSA_Q), lambda b: (b, 0, P_NQ // NSA_Q)),
            pl.BlockSpec((1, nc, CMP_STRIDE * NSA_KVW), lambda b: (b, 0, 0)),
            pl.BlockSpec((1, nc, CMP_STRIDE * NSA_KVW), lambda b: (b, 0, 0)),
            full((1, LANES)), full((1, LANES)), full((1, LANES)), full(qgain.shape),
            full(pos8.shape), full(w1.shape), full(w1c.shape), full(b1.shape), full(w2p.shape),
        ],
        out_specs=[whole(s) for s in shapes],
        out_shape=[jax.ShapeDtypeStruct((B,) + s, BF16) for s in shapes],
        compiler_params=_cparams(("parallel",)),
        name="nsa_prep",
    )(proj3, proj3, kc2, vc2, gsel, gwin, gcmp, qgain, pos8, w1, w1c, b1, w2p)


def _sel_table_geometry():
    last_start = (REL_BUCKETS // 2) * (REL_MAX_DIST / (REL_BUCKETS // 2)) ** (
        (REL_BUCKETS - REL_BUCKETS // 2 - 1) / (REL_BUCKETS - REL_BUCKETS // 2))
    dsat = int(math.ceil((last_start + 16 + SEL_CHUNK) / LANES)) * LANES
    return dsat, dsat + SEL_CHUNK


def _attn_kernel(q_ref, gate_ref, kca_ref, vct_ref, ksa_ref, vst_ref, kwa_ref, vwt_ref, tabc_ref,
                 tabw_ref, tabs_ref, ovl_ref, o_ref, m_ref, acc_ref, sa_ref, sb_ref, sw_ref,
                 *, nsel, dsat):
    qi = pl.program_id(2)
    QB = Q_BLOCK
    R = NSA_HPG * QB
    DH = NSA_DH

    qt = q_ref[0, 0, 0]
    qta = jnp.concatenate([qt, jnp.zeros_like(qt)], axis=0)

    n_sub = QB // WIN_KB
    vts, kws, kvalid = [], [], []
    for r in range(WIN_BLOCKS):
        kb = qi * n_sub - WINDOW // WIN_KB + r
        kbc = jnp.maximum(kb, 0)
        kws.append(kwa_ref[0, 0, pl.ds(pl.multiple_of(kbc * WIN_KB, WIN_KB), WIN_KB), :])
        vts.append(vwt_ref[0, 0, kbc])
        kvalid.append(kb >= 0)
    for u in range(n_sub):
        q_u = jnp.concatenate([qta[:, h * QB + u * WIN_KB:h * QB + (u + 1) * WIN_KB] for h in range(NSA_HPG)],
                              axis=1)
        for r in range(WIN_SUB):
            row = (u * WIN_SUB + r) * WIN_KB
            sw_ref[row:row + WIN_KB, :] = jnp.dot(kws[u + r], q_u, preferred_element_type=F32)

    def hcols(h):
        return slice(h * QB, (h + 1) * QB)

    def scores(c, q_aug, h=None):
        k0 = pl.multiple_of(c * SEL_CHUNK, SEL_CHUNK)
        delta = qi * QB - c * SEL_CHUNK
        start = pl.multiple_of(dsat - jnp.minimum(delta, dsat), LANES)
        cs = slice(None) if h is None else hcols(h)
        return (jnp.dot(ksa_ref[0, 0, pl.ds(k0, SEL_CHUNK), :], q_aug[:, cs], preferred_element_type=F32)
                + tabs_ref[0, 0, pl.ds(start, SEL_CHUNK), cs])

    s = jnp.dot(kca_ref[0, 0], qta, preferred_element_type=F32) + tabc_ref[0, 0]
    mx = jnp.max(s, axis=0, keepdims=True)
    e = jnp.exp2(s - mx)
    inv = jnp.where(mx > 0.5 * NEG_BIG, 1.0 / jnp.sum(e, axis=0, keepdims=True), 0.0)
    p = e * inv
    o_c = jnp.dot(vct_ref[0, 0], p.astype(BF16), preferred_element_type=F32)

    psum = p[:, 0:QB] + p[:, QB:2 * QB] + p[:, 2 * QB:3 * QB] + p[:, 3 * QB:4 * QB]
    p_hi = psum.astype(BF16)
    p_lo = (psum - p_hi.astype(F32)).astype(BF16)
    imp = (jnp.dot(ovl_ref[...], p_hi, preferred_element_type=F32)
           + jnp.dot(ovl_ref[...], p_lo, preferred_element_type=F32))
    t = qi * QB + lax.broadcasted_iota(jnp.int32, (SEL_ROWS, QB), 1)
    jrow = lax.broadcasted_iota(jnp.int32, (SEL_ROWS, QB), 0)
    jcur = t >> SEL_SHIFT
    n_top = min(SEL_TOPK, nsel)
    forced = (jrow == 0) | (jrow == jcur) | (jrow == jcur - 1)
    work = jnp.where(forced, -1.0, jnp.where(jrow * SEL_BLOCK <= t, imp, -1.0))
    work = jnp.where(jrow < nsel, work, -5.0)
    jf = jrow.astype(F32)
    sel = jnp.where(forced, 1.0, 0.0)
    for _ in range(n_top - 3):
        best = jnp.max(work, axis=0, keepdims=True)
        first = jnp.min(jnp.where(work == best, jf, float(SEL_ROWS)), axis=0, keepdims=True)
        hit = jf == first
        sel = jnp.where(hit, 1.0, sel)
        work = jnp.where(hit, -3.0, work)
    sel = jnp.where(jcur < n_top, jnp.where(jrow < n_top, 1.0, 0.0), sel)
    selneg = jnp.concatenate([jnp.where(sel > 0.5, 0.0, NEG_BIG)] * NSA_HPG, axis=1)
    qts = jnp.concatenate([qt, selneg.astype(BF16)], axis=0)
    sa_ref[...] = scores(0, qts)

    o_wu = []
    for u in range(n_sub):
        tiles = []
        for r in range(WIN_SUB):
            row = (u * WIN_SUB + r) * WIN_KB
            slab = pl.multiple_of(jnp.where(kvalid[u + r], r, WIN_SUB) * WIN_KB, WIN_KB)
            tiles.append(sw_ref[row:row + WIN_KB, :] + tabw_ref[0, 0, pl.ds(slab, WIN_KB), :])
        sw = jnp.concatenate(tiles, axis=0)
        ew = jnp.exp2(sw - jnp.max(sw, axis=0, keepdims=True)).astype(BF16)
        o_wu.append(jnp.dot(jnp.concatenate(vts[u:u + WIN_SUB], axis=1), ew, preferred_element_type=F32))
    o_w = jnp.concatenate([o_wu[u][:, h * WIN_KB:(h + 1) * WIN_KB]
                           for h in range(NSA_HPG) for u in range(n_sub)], axis=1)

    m_ref[...] = jnp.full_like(m_ref, NEG_BIG)
    acc_ref[...] = jnp.zeros_like(acc_ref)

    def consume(s_ref, c, heads=range(NSA_HPG)):
        vt = vst_ref[0, 0, c]
        for h in heads:
            cs = hcols(h)
            sc = s_ref[:, cs]
            m_old = m_ref[:, cs]
            m_new = jnp.maximum(m_old, jnp.max(sc, axis=0, keepdims=True))
            alpha = jnp.exp2(m_old - m_new)
            pe = jnp.exp2(sc - m_new).astype(BF16)
            acc_ref[:, cs] = alpha * acc_ref[:, cs] + jnp.dot(vt, pe, preferred_element_type=F32)
            m_ref[:, cs] = m_new

    def fill_and_consume(dst_ref, c_next, src_ref, c):
        dst_ref[:, hcols(0)] = scores(c_next, qts, 0)
        for h in range(NSA_HPG):
            if h + 1 < NSA_HPG:
                dst_ref[:, hcols(h + 1)] = scores(c_next, qts, h + 1)
            consume(src_ref, c, [h])

    n_chunks = (qi * QB + QB - 1) // SEL_CHUNK + 1
    n_pairs = (n_chunks - 1) // 2

    def sel_pair(pr, carry):
        c = 2 * pr
        fill_and_consume(sb_ref, c + 1, sa_ref, c)
        fill_and_consume(sa_ref, c + 2, sb_ref, c + 1)
        return carry

    lax.fori_loop(0, n_pairs, sel_pair, 0)

    @pl.when(n_chunks % 2 == 0)
    def _():
        fill_and_consume(sb_ref, n_chunks - 1, sa_ref, n_chunks - 2)
        consume(sb_ref, n_chunks - 1)

    @pl.when(n_chunks % 2 == 1)
    def _():
        consume(sa_ref, n_chunks - 1)

    acc = acc_ref[...]

    gt = jax.nn.sigmoid(gate_ref[0].astype(F32)).T
    o_s = acc[0:DH] / acc[DH:DH + 1]
    o_w = o_w[0:DH] / o_w[DH:DH + 1]
    heads = []
    for h in range(NSA_HPG):
        cs = slice(h * QB, (h + 1) * QB)
        heads.append(gt[3 * h:3 * h + 1] * o_c[:, cs] + gt[3 * h + 1:3 * h + 2] * o_s[:, cs]
                     + gt[3 * h + 2:3 * h + 3] * o_w[:, cs])
    o_ref[0] = jnp.concatenate(heads, axis=0).T.astype(BF16)


def _nsa_attn(proj3, prep, tabc, tabw, tabs, ovl, dsat):
    B, S, _ = proj3.shape
    ksa, vst, kwa, vwt, kca, vct, qt = prep
    nq = S // Q_BLOCK
    nc = S // CMP_STRIDE
    nsel = S // SEL_BLOCK
    R = NSA_HPG * Q_BLOCK
    gw = NSA_HPG * NSA_DH
    per_bg = lambda a: pl.BlockSpec((1, 1) + a.shape[2:], lambda b, g, i: (b, g) + (0,) * (a.ndim - 2))
    return pl.pallas_call(
        functools.partial(_attn_kernel, nsel=nsel, dsat=dsat),
        grid=(B, NSA_KV, nq),
        in_specs=[
            pl.BlockSpec((1, 1, 1, NSA_DH, R), lambda b, g, i: (b, g, i, 0, 0)),
            pl.BlockSpec((1, Q_BLOCK, LANES), lambda b, g, i: (b, i, P_GATE // LANES + g)),
            per_bg(kca), per_bg(vct), per_bg(ksa), per_bg(vst), per_bg(kwa), per_bg(vwt),
            pl.BlockSpec((1, 1, nc, R), lambda b, g, i: (g, i, 0, 0)),
            pl.BlockSpec((1, 1) + tabw.shape[2:], lambda b, g, i: (g, 0, 0, 0)),
            pl.BlockSpec((1, 1) + tabs.shape[2:], lambda b, g, i: (g, 0, 0, 0)),
            pl.BlockSpec(ovl.shape, lambda b, g, i: (0, 0)),
        ],
        out_specs=pl.BlockSpec((1, Q_BLOCK, gw), lambda b, g, i: (b, i, g)),
        out_shape=jax.ShapeDtypeStruct((B, S, NSA_Q), BF16),
        scratch_shapes=[pltpu.VMEM((1, R), F32), pltpu.VMEM((V_ROWS, R), F32),
                        pltpu.VMEM((SEL_CHUNK, R), F32), pltpu.VMEM((SEL_CHUNK, R), F32),
                        pltpu.VMEM((Q_BLOCK // WIN_KB * WIN_SUB * WIN_KB, NSA_HPG * WIN_KB), F32)],
        compiler_params=_cparams(("parallel", "parallel", "arbitrary")),
        name="nsa_attn",
    )(qt, proj3, kca, vct, ksa, vst, kwa, vwt, tabc, tabw, tabs, ovl)


def _rel_bucket(d):
    max_exact = REL_BUCKETS // 2
    d = jnp.maximum(d, 0)
    df = jnp.maximum(d, 1).astype(F32)
    large = max_exact + (jnp.log(df / max_exact) / math.log(REL_MAX_DIST / max_exact)
                         * (REL_BUCKETS - max_exact)).astype(jnp.int32)
    return jnp.where(d < max_exact, d, jnp.minimum(large, REL_BUCKETS - 1))


def _bias_kernel(rb_ref, bkt_ref, o_ref, *, rows, cols):
    g = pl.program_id(0)
    step = 32
    for r0 in range(0, rows, step):
        bkt = bkt_ref[0, r0:r0 + step, :]
        acc = [jnp.full(bkt.shape, NEG_BIG, F32) for _ in range(NSA_HPG)]
        for b in range(REL_BUCKETS):
            hit = bkt == b
            for h in range(NSA_HPG):
                acc[h] = jnp.where(hit, rb_ref[b, g * NSA_HPG + h] * LOG2E, acc[h])
        for h in range(NSA_HPG):
            o_ref[0, 0, r0:r0 + step, h * cols:(h + 1) * cols] = acc[h]


def _bias_table(rel_bias, bucket):
    nt, rows, cols = bucket.shape
    return pl.pallas_call(
        functools.partial(_bias_kernel, rows=rows, cols=cols),
        grid=(NSA_KV, nt),
        in_specs=[
            pl.BlockSpec(memory_space=pltpu.SMEM),
            pl.BlockSpec((1, rows, cols), lambda g, t: (t, 0, 0)),
        ],
        out_specs=pl.BlockSpec((1, 1, rows, NSA_HPG * cols), lambda g, t: (g, t, 0, 0)),
        out_shape=jax.ShapeDtypeStruct((NSA_KV, nt, rows, NSA_HPG * cols), F32),
        compiler_params=_cparams(("parallel", "parallel")),
        name="bias_table",
    )(rel_bias, bucket)


def _bias_tables(rel_bias, S, dsat, strip_len):
    nq = S // Q_BLOCK
    nc = S // CMP_STRIDE
    n_cmp = (S - CMP_BLOCK) // CMP_STRIDE + 1
    ar = lambda n: jnp.arange(n, dtype=jnp.int32)
    bucket = lambda d, valid: jnp.where(valid, _rel_bucket(d), -1)
    i = ar(Q_BLOCK)[None, None, :]
    c = ar(nc)[None, :, None]
    d_c = ar(nq)[:, None, None] * Q_BLOCK + i - (c * CMP_STRIDE + CMP_BLOCK - 1)
    tabc = _bias_table(rel_bias, bucket(d_c, (d_c >= 0) & (c < n_cmp)))
    jw = ar((WIN_SUB + 1) * WIN_KB)[None, :, None]
    d_w = ar(WIN_KB)[None, None, :] + WINDOW - jw
    tabw = _bias_table(rel_bias, bucket(d_w, (d_w >= 0) & (d_w < WINDOW) & (jw < WIN_SUB * WIN_KB)))
    d_s = i - ar(strip_len)[None, :, None] + dsat
    tabs = _bias_table(rel_bias, bucket(d_s, d_s >= 0))
    return tabc, tabw, tabs


def _overlap_matrix(S):
    nc = S // CMP_STRIDE
    nsel = S // SEL_BLOCK
    n_cmp = (S - CMP_BLOCK) // CMP_STRIDE + 1
    c = np.arange(nc)[None, :]
    sel_start = (np.arange(SEL_ROWS) * SEL_BLOCK)[:, None]
    ovl = ((c * CMP_STRIDE < sel_start + SEL_BLOCK) & (c * CMP_STRIDE + CMP_BLOCK - 1 >= sel_start)
           & (c < n_cmp) & (np.arange(SEL_ROWS)[:, None] < nsel))
    return jnp.asarray(ovl, BF16)


def _layer_params(l, lower_bounds, norm_mix, w_in, hg_out_gain, nsa_qk_gain, cmp_pos, cmp_w1, cmp_b1,
                  cmp_w2, w_branch_a, w_branch_b, w_out, norm_mlp, w_up, w_down):
    splits = (HG_K, HG_K, HG_V, HG_V, NSA_Q, NSA_KVW, NSA_KVW, NSA_KVW, NSA_KVW, NSA_KVW, NSA_KVW,
              3 * NSA_HEADS, D_MODEL, D_MODEL)
    offs = np.concatenate([[0], np.cumsum(splits)])
    col = lambda k: w_in[l][:, offs[k]:offs[k + 1]]
    gate_w = col(11)
    zpad = jnp.zeros((D_MODEL, LANES - 3 * NSA_HPG), F32)
    gate_cols = []
    for g in range(NSA_KV):
        gate_cols += [gate_w[:, g * 3 * NSA_HPG:(g + 1) * 3 * NSA_HPG], zpad]
    w_main = jnp.concatenate([col(12), col(13), col(0), col(2), col(3),
                              col(7), col(8), col(9), col(10), col(4)] + gate_cols, axis=1).astype(BF16)
    w_cmp = jnp.concatenate([col(1), col(5), col(6)], axis=1).astype(BF16)

    lb = lower_bounds[l]
    lbp = jnp.stack([jnp.log(lb), jnp.log1p(-lb), 1.0 - lb])

    w1 = cmp_w1[l].reshape(2, 2, CMP_STRIDE, NSA_DH, CMP_HID)
    w1c = jnp.zeros((2, CMP_STRIDE, NSA_KV, NSA_DH, NSA_KV, 2, CMP_HID), F32)
    for g in range(NSA_KV):
        w1c = w1c.at[:, :, g, :, g, :, :].set(w1.transpose(0, 2, 3, 1, 4))
    w1c = w1c.reshape(2, CMP_STRIDE * NSA_KVW, NSA_KV * 2 * CMP_HID).astype(BF16)
    pos8 = jnp.zeros((2, 8, CMP_BLOCK * NSA_DH), F32).at[:, 0, :].set(cmp_pos[l].reshape(2, -1))
    w2p = jnp.zeros((2, CMP_HID, LANES), F32)
    w2p = w2p.at[0, :, :NSA_DH].set(cmp_w2[l, 0]).at[1, :NSA_DH, :].set(cmp_w2[l, 1].T).astype(BF16)
    qk = nsa_qk_gain[l]
    z64 = jnp.zeros((NSA_DH,), F32)
    return dict(
        norm_mix=norm_mix[l][None, :], w_main=w_main, w_cmp=w_cmp, lbp=lbp,
        hg_gain=hg_out_gain[l][None, :],
        gsel=jnp.concatenate([qk[2], qk[2]])[None, :], gwin=jnp.concatenate([qk[3], qk[3]])[None, :],
        gcmp=jnp.concatenate([qk[1], z64])[None, :],
        qgain=jnp.broadcast_to((qk[0] * (NSA_DH ** -0.5 * LOG2E))[:, None], (NSA_DH, Q_BLOCK)),
        pos8=pos8, w1=cmp_w1[l].astype(BF16), w1c=w1c, b1=cmp_b1[l][:, None, :], w2p=w2p,
        wa=w_branch_a[l].astype(BF16), wb=w_branch_b[l].astype(BF16), wo=w_out[l].astype(BF16),
        norm_mlp=norm_mlp[l][None, :], wu=w_up[l].astype(BF16), wd=w_down[l].astype(BF16))


def kernel(x, rel_bias, hg_lb_logits, norm_mix, w_in, hg_out_gain, nsa_qk_gain, cmp_pos, cmp_w1, cmp_b1, cmp_w2, w_branch_a, w_branch_b, w_out, norm_mlp, w_up, w_down):
    B, S, D = x.shape
    T = B * S
    depth = w_in.shape[0]
    assert D == D_MODEL and S % SEL_CHUNK == 0 and S >= WINDOW and S // SEL_BLOCK <= SEL_ROWS
    assert S % Q_BLOCK == 0 and Q_BLOCK % WIN_KB == 0 and WINDOW % WIN_KB == 0
    assert FORCE_BONUS > NSA_HPG and SEL_TOPK >= 3
    lb_cum = jnp.cumsum(jax.nn.softmax(hg_lb_logits.astype(F32), axis=0), axis=0)
    lower_bounds = lb_cum - lb_cum[0:1]

    dsat, strip_len = _sel_table_geometry()
    tabc, tabw, tabs = _bias_tables(rel_bias, S, dsat, strip_len)
    ovl = _overlap_matrix(S)
    nc = S // CMP_STRIDE
    tm = min(1024, T)

    x2 = x.reshape(T, D)
    for l in range(depth):
        p = _layer_params(l, lower_bounds, norm_mix, w_in, hg_out_gain, nsa_qk_gain, cmp_pos, cmp_w1,
                          cmp_b1, cmp_w2, w_branch_a, w_branch_b, w_out, norm_mlp, w_up, w_down)
        proj, z, kc, vc = _rms_proj(x2, p["norm_mix"], p["w_main"], p["w_cmp"], tm, P_TILE)
        proj3 = proj.reshape(B, S, P_COLS)
        kc2 = kc.reshape(B, nc, CMP_STRIDE * NSA_KVW)
        vc2 = vc.reshape(B, nc, CMP_STRIDE * NSA_KVW)
        o_a = _hgrn(proj3, z.reshape(B, S, HG_K), p["lbp"], p["hg_gain"])
        prep = _nsa_prep(proj3, kc2, vc2, p["gsel"], p["gwin"], p["gcmp"], p["qgain"], p["pos8"], p["w1"],
                         p["w1c"], p["b1"], p["w2p"])
        o_b = _nsa_attn(proj3, prep, tabc, tabw, tabs, ovl, dsat)
        x2 = _merge(x2, o_a.reshape(T, HG_V), o_b.reshape(T, NSA_Q), proj, p["wa"], p["wb"], p["wo"],
                    min(1024, T))
        x2 = _mlp(x2, p["norm_mlp"], p["wu"], p["wd"], tm, 1024)
    return x2.reshape(B, S, D)
```

```python
import functools
import math

import numpy as np
import jax
import jax.numpy as jnp
from jax import lax
from jax.experimental import pallas as pl
from jax.experimental.pallas import tpu as pltpu

F32 = jnp.float32
BF16 = jnp.bfloat16

D_MODEL = 1024
HG_HEADS = 4
HG_DK = 128
HG_DV = 128
HG_K = HG_HEADS * HG_DK
HG_V = HG_HEADS * HG_DV
NSA_HEADS = 8
NSA_KV = 2
NSA_DH = 64
NSA_HPG = NSA_HEADS // NSA_KV
NSA_Q = NSA_HEADS * NSA_DH
NSA_KVW = NSA_KV * NSA_DH
CMP_BLOCK = 32
CMP_STRIDE = 16
CMP_HID = 128
SEL_BLOCK = 64
SEL_TOPK = 8
WINDOW = 512
FORCE_BONUS = 1000.0
REL_BUCKETS = 32
REL_MAX_DIST = 1024
D_FF = 4 * D_MODEL
EPS = 1e-6
NEG_BIG = -1e30
LOG2E = math.log2(math.e)

LANES = 128
VMEM_LIMIT = 56 * 1024 * 1024

P_GAB = 0
P_HG = P_GAB + 2 * D_MODEL
P_KV = P_HG + 3 * HG_K
P_NQ = P_KV + 4 * NSA_KVW
P_GATE = P_NQ + NSA_Q
P_COLS = P_GATE + NSA_KV * LANES
P_TILE = P_COLS // 2
ROW_BLOCKS = 4

HG_CHUNK = 128
HG_TS = 2048
Q_BLOCK = 256
WIN_KB = 128
SEL_CHUNK = 512
WIN_BLOCKS = (WINDOW + Q_BLOCK) // WIN_KB
WIN_SUB = WINDOW // WIN_KB + 1
SEL_SHIFT = int(math.log2(SEL_BLOCK))
SEL_ROWS = LANES - NSA_DH
V_ROWS = NSA_DH + 16


def _cparams(sem):
    return pltpu.CompilerParams(dimension_semantics=sem, vmem_limit_bytes=VMEM_LIMIT)


def _rms_rows(x, gain):
    ms = jnp.mean(x * x, axis=-1, keepdims=True)
    return x * lax.rsqrt(ms + EPS) * gain


def _proj_kernel(x_ref, g_ref, w_ref, wx_ref, o_ref, z_ref, kc_ref, vc_ref, h_ref, ck_ref, cv_ref):
    j = pl.program_id(1)
    rb = x_ref.shape[0] // ROW_BLOCKS

    @pl.when(j == 0)
    def _():
        for r in range(ROW_BLOCKS):
            rows = slice(r * rb, (r + 1) * rb)
            h = _rms_rows(x_ref[rows, :], g_ref[...]).astype(BF16)
            h_ref[rows, :] = h
            o_ref[rows, :] = jnp.dot(h, w_ref[...], preferred_element_type=F32).astype(BF16)
            c = jnp.dot(h, wx_ref[...], preferred_element_type=F32)
            z_ref[rows, :] = c[:, :HG_K]
            ck_ref[rows, :] = c[:, HG_K:HG_K + NSA_KVW]
            cv_ref[rows, :] = c[:, HG_K + NSA_KVW:]
        n16 = ck_ref.shape[0] // CMP_STRIDE
        for t in range(CMP_STRIDE):
            lanes = slice(t * NSA_KVW, (t + 1) * NSA_KVW)
            kc_ref[:, lanes] = ck_ref[pl.ds(t, n16, stride=CMP_STRIDE), :].astype(BF16)
            vc_ref[:, lanes] = cv_ref[pl.ds(t, n16, stride=CMP_STRIDE), :].astype(BF16)

    @pl.when(j != 0)
    def _():
        o_ref[...] = jnp.dot(h_ref[...], w_ref[...], preferred_element_type=F32).astype(BF16)


def _rms_proj(x2, gain, w, wx, tm, tn):
    T, D = x2.shape
    N = w.shape[1]
    cmp_sds = jax.ShapeDtypeStruct((T // CMP_STRIDE, CMP_STRIDE * NSA_KVW), BF16)
    return pl.pallas_call(
        _proj_kernel,
        grid=(T // tm, N // tn),
        in_specs=[
            pl.BlockSpec((tm, D), lambda i, j: (i, 0)),
            pl.BlockSpec((1, D), lambda i, j: (0, 0)),
            pl.BlockSpec((D, tn), lambda i, j: (0, j)),
            pl.BlockSpec(wx.shape, lambda i, j: (0, 0)),
        ],
        out_specs=[
            pl.BlockSpec((tm, tn), lambda i, j: (i, j)),
            pl.BlockSpec((tm, HG_K), lambda i, j: (i, 0)),
            pl.BlockSpec((tm // CMP_STRIDE, CMP_STRIDE * NSA_KVW), lambda i, j: (i, 0)),
            pl.BlockSpec((tm // CMP_STRIDE, CMP_STRIDE * NSA_KVW), lambda i, j: (i, 0)),
        ],
        out_shape=[jax.ShapeDtypeStruct((T, N), BF16), jax.ShapeDtypeStruct((T, HG_K), F32),
                   cmp_sds, cmp_sds],
        scratch_shapes=[pltpu.VMEM((tm, D), BF16), pltpu.VMEM((tm, NSA_KVW), F32),
                        pltpu.VMEM((tm, NSA_KVW), F32)],
        compiler_params=_cparams(("parallel", "arbitrary")),
        name="rms_proj",
    )(x2, gain, w, wx)


def _mlp_kernel(x_ref, g_ref, wu_ref, wd_ref, o_ref, h_ref):
    f = pl.program_id(1)

    rb = x_ref.shape[0] // ROW_BLOCKS

    def up_down(h):
        u = jnp.dot(h, wu_ref[...], preferred_element_type=F32)
        u = jnp.square(jnp.maximum(u, 0.0)).astype(BF16)
        return jnp.dot(u, wd_ref[...], preferred_element_type=F32)

    @pl.when(f == 0)
    def _():
        for r in range(ROW_BLOCKS):
            rows = slice(r * rb, (r + 1) * rb)
            x = x_ref[rows, :]
            h = _rms_rows(x, g_ref[...]).astype(BF16)
            h_ref[rows, :] = h
            o_ref[rows, :] = x + up_down(h)

    @pl.when(f != 0)
    def _():
        o_ref[...] += up_down(h_ref[...])


def _mlp(x2, gain, wu, wd, tm, tf):
    T, D = x2.shape
    FF = wu.shape[1]
    return pl.pallas_call(
        _mlp_kernel,
        grid=(T // tm, FF // tf),
        in_specs=[
            pl.BlockSpec((tm, D), lambda i, f: (i, 0)),
            pl.BlockSpec((1, D), lambda i, f: (0, 0)),
            pl.BlockSpec((D, tf), lambda i, f: (0, f)),
            pl.BlockSpec((tf, D), lambda i, f: (f, 0)),
        ],
        out_specs=pl.BlockSpec((tm, D), lambda i, f: (i, 0)),
        out_shape=jax.ShapeDtypeStruct((T, D), F32),
        scratch_shapes=[pltpu.VMEM((tm, D), BF16)],
        compiler_params=_cparams(("parallel", "arbitrary")),
        name="mlp",
    )(x2, gain, wu, wd)


def _merge_kernel(x_ref, oa_ref, ob_ref, ga_ref, gb_ref, wa_ref, wb_ref, wo_ref, o_ref):
    ya = jnp.dot(oa_ref[...], wa_ref[...], preferred_element_type=F32)
    yb = lax.dot_general(ob_ref[0], wb_ref[...], (((0,), (0,)), ((), ())), preferred_element_type=F32)
    mixed = (jax.nn.sigmoid(ga_ref[...].astype(F32)) * ya
             + jax.nn.sigmoid(gb_ref[...].astype(F32)) * yb)
    o_ref[...] = x_ref[...] + jnp.dot(mixed.astype(BF16), wo_ref[...], preferred_element_type=F32)


def _merge(x2, oa, ob, proj, wa, wb, wo, tm):
    T, D = x2.shape
    ga_blk = P_GAB // D
    per_b = ob.shape[2] // tm
    return pl.pallas_call(
        _merge_kernel,
        grid=(T // tm,),
        in_specs=[
            pl.BlockSpec((tm, D), lambda i: (i, 0)),
            pl.BlockSpec((tm, HG_V), lambda i: (i, 0)),
            pl.BlockSpec((1, NSA_Q, tm), lambda i: (i // per_b, 0, i % per_b)),
            pl.BlockSpec((tm, D), lambda i: (i, ga_blk)),
            pl.BlockSpec((tm, D), lambda i: (i, ga_blk + 1)),
            pl.BlockSpec((HG_V, D), lambda i: (0, 0)),
            pl.BlockSpec((NSA_Q, D), lambda i: (0, 0)),
            pl.BlockSpec((D, D), lambda i: (0, 0)),
        ],
        out_specs=pl.BlockSpec((tm, D), lambda i: (i, 0)),
        out_shape=jax.ShapeDtypeStruct((T, D), F32),
        compiler_params=_cparams(("parallel",)),
        name="merge",
    )(x2, oa, ob, proj, proj, wa, wb, wo)


def _hgrn_consts(C):
    L = int(math.log2(C))
    idx = np.arange(C)
    mats = [(idx[None, :] <= idx[:, None]),
            (idx[None, :] > idx[:, None])]
    bmask = []
    for l in range(L):
        m = 1 << l
        r = (idx & ~(2 * m - 1)) + m
        lo = np.minimum(idx, r)[:, None]
        hi = np.maximum(idx, r)[:, None]
        mats.append((idx[None, :] > lo) & (idx[None, :] <= hi))
        blk = idx >> (l + 1)
        bmask.append(blk[:, None] == blk[None, :])
    bmask.append(idx[:, None] == idx[None, :])
    seg = np.concatenate(mats, axis=0).astype(np.float32)
    seg = np.concatenate([seg, seg], axis=1)
    return seg, np.stack(bmask).astype(np.float32), L


def _hgrn_kernel(q_ref, f_ref, i_ref, g_ref, lbp_ref, gain_ref, seg_ref, bm_ref, o_ref, st_ref,
                 k0_ref, sg0_ref, k1_ref, sg1_ref, *, C, L, nchunk):
    @pl.when(pl.program_id(1) == 0)
    def _():
        st_ref[...] = jnp.zeros_like(st_ref)

    rowi = lax.broadcasted_iota(jnp.int32, (C, HG_DK), 0)
    nt = (((1,), (1,)), ((), ()))
    tn = (((0,), (0,)), ((), ()))

    H = range(HG_HEADS)
    css = [slice(h * HG_DK, (h + 1) * HG_DK) for h in H]

    def gates(c, k_ref, sg_ref):
        r0 = pl.multiple_of(c * C, C)
        z = f_ref[0, pl.ds(r0, C), :]
        ez = jnp.exp(-jnp.abs(z))
        y = lbp_ref[1:2, :] + (jnp.minimum(z, 0.0) - jnp.log(1.0 + ez))
        log_lb = lbp_ref[0:1, :]
        logf = (jnp.maximum(log_lb, y) + jnp.log(1.0 + jnp.exp(-jnp.abs(log_lb - y)))) * LOG2E
        k_ref[...] = lbp_ref[2:3, :] * (jnp.where(z >= 0.0, ez, 1.0) / (1.0 + ez))
        g_hi = logf.astype(BF16)
        g_lo = (logf - g_hi.astype(F32)).astype(BF16)
        sg_ref[...] = jnp.dot(seg_ref[...], jnp.concatenate([g_hi, g_lo], axis=0),
                              preferred_element_type=F32)

    def mix(c, k_ref, sg_ref):
        r0 = pl.multiple_of(c * C, C)
        qbs = [q_ref[0, pl.ds(r0, C), cs] for cs in css]
        qs = [qb.astype(F32) for qb in qbs]
        ks = [k_ref[:, cs] for cs in css]
        ss = [lax.dot_general(qbs[h], ks[h].astype(BF16), nt, preferred_element_type=F32) * bm_ref[L]
              for h in H]
        for l in range(L):
            second = (rowi & (1 << l)) != 0
            for h in H:
                e = jnp.exp2(sg_ref[(2 + l) * C:(3 + l) * C, css[h]])
                ql = jnp.where(second, qs[h] * e, 0.0).astype(BF16)
                kl = jnp.where(second, 0.0, ks[h] * e).astype(BF16)
                ss[h] = ss[h] + lax.dot_general(ql, kl, nt, preferred_element_type=F32) * bm_ref[l]
        return r0, qs, ks, ss

    def mix_out(scores, k_ref, sg_ref):
        r0, qs, ks, ss = scores
        vbs = [i_ref[0, pl.ds(r0, C), cs] for cs in css]
        bs = [sg_ref[0:C, cs] for cs in css]
        sts = [st_ref[h] for h in H]
        outs = []
        for h in H:
            qd = (qs[h] * jnp.exp2(bs[h])).astype(BF16)
            o = lax.dot_general(qd, sts[h].astype(BF16), nt, preferred_element_type=F32)
            outs.append(o + jnp.dot(ss[h].astype(BF16), vbs[h], preferred_element_type=F32))
        for h in H:
            kd = (ks[h] * jnp.exp2(sg_ref[C:2 * C, css[h]])).astype(BF16)
            st_ref[h] = (sts[h] * jnp.exp2(bs[h][C - 1:C, :])
                         + lax.dot_general(vbs[h], kd, tn, preferred_element_type=F32))
        for h in H:
            gg = g_ref[0, pl.ds(r0, C), css[h]].astype(F32)
            o = _rms_rows(outs[h], gain_ref[...]) * (gg * jax.nn.sigmoid(gg))
            o_ref[0, pl.ds(r0, C), css[h]] = o.astype(BF16)

    gates(0, k0_ref, sg0_ref)

    def pair(pr, carry):
        c = 2 * pr
        sc0 = mix(c, k0_ref, sg0_ref)
        gates(c + 1, k1_ref, sg1_ref)
        mix_out(sc0, k0_ref, sg0_ref)
        sc1 = mix(c + 1, k1_ref, sg1_ref)
        gates(jnp.minimum(c + 2, nchunk - 1), k0_ref, sg0_ref)
        mix_out(sc1, k1_ref, sg1_ref)
        return carry

    lax.fori_loop(0, nchunk // 2, pair, 0)


def _hgrn(proj3, z3, lbp, gain, C=HG_CHUNK, ts=HG_TS):
    B, S, _ = proj3.shape
    ts = min(ts, S)
    assert (ts // C) % 2 == 0
    seg, bm, L = _hgrn_consts(C)
    blk = lambda k: pl.BlockSpec((1, ts, HG_K), lambda b, s: (b, s, P_HG // HG_K + k))
    return pl.pallas_call(
        functools.partial(_hgrn_kernel, C=C, L=L, nchunk=ts // C),
        grid=(B, S // ts),
        in_specs=[
            blk(0), pl.BlockSpec((1, ts, HG_K), lambda b, s: (b, s, 0)), blk(1), blk(2),
            pl.BlockSpec((3, HG_K), lambda b, s: (0, 0)),
            pl.BlockSpec((1, HG_DV), lambda b, s: (0, 0)),
            pl.BlockSpec(seg.shape, lambda b, s: (0, 0)),
            pl.BlockSpec(bm.shape, lambda b, s: (0, 0, 0)),
        ],
        out_specs=pl.BlockSpec((1, ts, HG_V), lambda b, s: (b, s, 0)),
        out_shape=jax.ShapeDtypeStruct((B, S, HG_V), BF16),
        scratch_shapes=[pltpu.VMEM((HG_HEADS, HG_DV, HG_DK), F32)]
        + 2 * [pltpu.VMEM((C, HG_K), F32), pltpu.VMEM(((2 + L) * C, HG_K), F32)],
        compiler_params=_cparams(("parallel", "arbitrary")),
        name="hgrn2",
    )(proj3, z3, proj3, proj3, lbp, gain, jnp.asarray(seg, BF16), jnp.asarray(bm))


def _half_rms(x, gain2, lane):
    sq = x * x
    s_lo = jnp.sum(jnp.where(lane < NSA_DH, sq, 0.0), axis=-1, keepdims=True)
    s_hi = jnp.sum(jnp.where(lane < NSA_DH, 0.0, sq), axis=-1, keepdims=True)
    ms = jnp.where(lane < NSA_DH, s_lo, s_hi) * (1.0 / NSA_DH)
    return x * lax.rsqrt(ms + EPS) * gain2


def _gelu_tanh(x):
    return 0.5 * x * (1.0 + jnp.tanh(math.sqrt(2.0 / math.pi) * (x + 0.044715 * (x * x * x))))


def _prep_kernel(kv_ref, nq_ref, kc_ref, vc_ref, gsel_ref, gwin_ref, gcmp_ref, qg_ref, pos_ref, w1_ref,
                 w1c_ref, b1_ref, w2_ref, ksa_ref, vst_ref, kwa_ref, vwt_ref, kca_ref, vct_ref, qt_ref,
                 *, S, rows):
    lane = lax.broadcasted_iota(jnp.int32, (rows, LANES), 1)
    rowi = lax.broadcasted_iota(jnp.int32, (rows, LANES), 0)
    ones_row = (lax.broadcasted_iota(jnp.int32, (V_ROWS - NSA_DH, rows), 0) == 0).astype(F32)
    low = lane < NSA_DH

    def body(i, carry):
        r0 = pl.multiple_of(i * rows, rows)
        blk_hot = jnp.where(lane - NSA_DH == ((r0 + rowi) >> SEL_SHIFT), 1.0, 0.0)
        for src, gain_ref, aug, k_dst, v_dst, vrows in (
                (0, gsel_ref, blk_hot, ksa_ref, vst_ref, rows),
                (2, gwin_ref, jnp.zeros((rows, LANES), F32), kwa_ref, vwt_ref, WIN_KB)):
            kk = kv_ref[0, pl.ds(r0, rows), src * LANES:(src + 1) * LANES].astype(F32)
            vv = kv_ref[0, pl.ds(r0, rows), (src + 1) * LANES:(src + 2) * LANES].astype(F32)
            kn = _half_rms(kk, gain_ref[...], lane)
            k_dst[0, 0, pl.ds(r0, rows), :] = jnp.where(low, kn, aug).astype(BF16)
            k_dst[0, 1, pl.ds(r0, rows), :] = jnp.where(low, pltpu.roll(kn, NSA_DH, 1), aug).astype(BF16)
            vt = vv.T
            for g in range(NSA_KV):
                vg = jnp.concatenate([vt[g * NSA_DH:(g + 1) * NSA_DH], ones_row], axis=0).astype(BF16)
                for u in range(rows // vrows):
                    v_dst[0, g, i * (rows // vrows) + u] = vg[:, u * vrows:(u + 1) * vrows]
        for u in range(rows // Q_BLOCK):
            xt = nq_ref[0, pl.ds(r0 + u * Q_BLOCK, Q_BLOCK), :].astype(F32).T
            for g in range(NSA_KV):
                cols = []
                for h in range(NSA_HPG):
                    xh = xt[(g * NSA_HPG + h) * NSA_DH:(g * NSA_HPG + h + 1) * NSA_DH]
                    ms = jnp.mean(xh * xh, axis=0, keepdims=True)
                    cols.append(xh * lax.rsqrt(ms + EPS) * qg_ref[...])
                qt_ref[0, g, i * (rows // Q_BLOCK) + u] = jnp.concatenate(cols, axis=1).astype(BF16)
        return carry

    lax.fori_loop(0, S // rows, body, 0)

    nc = S // CMP_STRIDE
    outs = []
    for j, src_ref in ((0, kc_ref), (1, vc_ref)):
        bias = jnp.dot(pos_ref[j].astype(BF16), w1_ref[j], preferred_element_type=F32)[0:1] + b1_ref[j]
        pj = jnp.dot(src_ref[0], w1c_ref[j], preferred_element_type=F32)
        per_g = []
        for g in range(NSA_KV):
            first = pj[:, (2 * g) * CMP_HID:(2 * g + 1) * CMP_HID]
            second = pj[:, (2 * g + 1) * CMP_HID:(2 * g + 2) * CMP_HID]
            hid = _gelu_tanh(first + pltpu.roll(second, nc - 1, 0) + bias).astype(BF16)
            if j == 0:
                per_g.append(jnp.dot(hid, w2_ref[0], preferred_element_type=F32))
            else:
                per_g.append(lax.dot_general(w2_ref[1, :NSA_DH, :], hid, (((1,), (1,)), ((), ())),
                                             preferred_element_type=F32))
        outs.append(per_g)
    for g in range(NSA_KV):
        kc = outs[0][g]
        ms = jnp.sum(kc * kc, axis=-1, keepdims=True) * (1.0 / NSA_DH)
        kca_ref[0, g] = (kc * lax.rsqrt(ms + EPS) * gcmp_ref[...]).astype(BF16)
        vct_ref[0, g] = outs[1][g].astype(BF16)


def _nsa_prep(proj3, kc2, vc2, gsel, gwin, gcmp, qgain, pos8, w1, w1c, b1, w2p):
    B, S, _ = proj3.shape
    nc = S // CMP_STRIDE
    rows = SEL_CHUNK
    full = lambda shp: pl.BlockSpec(shp, lambda b: (0,) * len(shp))
    whole = lambda shp: pl.BlockSpec((1,) + shp, lambda b: (b,) + (0,) * len(shp))
    shapes = [(NSA_KV, S, LANES), (NSA_KV, S // SEL_CHUNK, V_ROWS, SEL_CHUNK),
              (NSA_KV, S, LANES), (NSA_KV, S // WIN_KB, V_ROWS, WIN_KB),
              (NSA_KV, nc, LANES), (NSA_KV, NSA_DH, nc),
              (NSA_KV, S // Q_BLOCK, NSA_DH, NSA_HPG * Q_BLOCK)]
    return pl.pallas_call(
        functools.partial(_prep_kernel, S=S, rows=rows),
        grid=(B,),
        in_specs=[
            pl.BlockSpec((1, S, 4 * NSA_KVW), lambda b: (b, 0, P_KV // (4 * NSA_KVW))),
            pl.BlockSpec((1, S, NSA_Q), lambda b: (b, 0, P_NQ // NSA_Q)),
            pl.BlockSpec((1, nc, CMP_STRIDE * NSA_KVW), lambda b: (b, 0, 0)),
            pl.BlockSpec((1, nc, CMP_STRIDE * NSA_KVW), lambda b: (b, 0, 0)),
            full((1, LANES)), full((1, LANES)), full((1, LANES)), full(qgain.shape),
            full(pos8.shape), full(w1.shape), full(w1c.shape), full(b1.shape), full(w2p.shape),
        ],
        out_specs=[whole(s) for s in shapes],
        out_shape=[jax.ShapeDtypeStruct((B,) + s, BF16) for s in shapes],
        compiler_params=_cparams(("parallel",)),
        name="nsa_prep",
    )(proj3, proj3, kc2, vc2, gsel, gwin, gcmp, qgain, pos8, w1, w1c, b1, w2p)


def _sel_table_geometry():
    last_start = (REL_BUCKETS // 2) * (REL_MAX_DIST / (REL_BUCKETS // 2)) ** (
        (REL_BUCKETS - REL_BUCKETS // 2 - 1) / (REL_BUCKETS - REL_BUCKETS // 2))
    dsat = int(math.ceil((last_start + 16 + SEL_CHUNK) / LANES)) * LANES
    return dsat, dsat + SEL_CHUNK


def _attn_kernel(q_ref, gate_ref, kca_ref, vct_ref, ksa_ref, vst_ref, kwa_ref, vwt_ref, tabc_ref,
                 tabw_ref, tabs_ref, ovl_ref, o_ref, m_ref, acc_ref, sa_ref, sb_ref, sw_ref,
                 *, nsel, dsat):
    qi = pl.program_id(2)
    QB = Q_BLOCK
    R = NSA_HPG * QB
    DH = NSA_DH

    qt = q_ref[0, 0, 0]
    qta = jnp.concatenate([qt, jnp.zeros_like(qt)], axis=0)

    n_sub = QB // WIN_KB
    vts, kws, kvalid = [], [], []
    for r in range(WIN_BLOCKS):
        kb = qi * n_sub - WINDOW // WIN_KB + r
        kbc = jnp.maximum(kb, 0)
        kws.append(kwa_ref[0, 0, pl.ds(pl.multiple_of(kbc * WIN_KB, WIN_KB), WIN_KB), :])
        vts.append(vwt_ref[0, 0, kbc])
        kvalid.append(kb >= 0)
    for u in range(n_sub):
        q_u = jnp.concatenate([qta[:, h * QB + u * WIN_KB:h * QB + (u + 1) * WIN_KB] for h in range(NSA_HPG)],
                              axis=1)
        for r in range(WIN_SUB):
            row = (u * WIN_SUB + r) * WIN_KB
            sw_ref[row:row + WIN_KB, :] = jnp.dot(kws[u + r], q_u, preferred_element_type=F32)

    def hcols(h):
        return slice(h * QB, (h + 1) * QB)

    def scores(c, q_aug, h=None):
        k0 = pl.multiple_of(c * SEL_CHUNK, SEL_CHUNK)
        delta = qi * QB - c * SEL_CHUNK
        start = pl.multiple_of(dsat - jnp.minimum(delta, dsat), LANES)
        cs = slice(None) if h is None else hcols(h)
        return (jnp.dot(ksa_ref[0, 0, pl.ds(k0, SEL_CHUNK), :], q_aug[:, cs], preferred_element_type=F32)
                + tabs_ref[0, 0, pl.ds(start, SEL_CHUNK), cs])

    s = jnp.dot(kca_ref[0, 0], qta, preferred_element_type=F32) + tabc_ref[0, 0]
    mx = jnp.max(s, axis=0, keepdims=True)
    e = jnp.exp2(s - mx)
    inv = jnp.where(mx > 0.5 * NEG_BIG, 1.0 / jnp.sum(e, axis=0, keepdims=True), 0.0)
    p = e * inv
    o_c = jnp.dot(vct_ref[0, 0], p.astype(BF16), preferred_element_type=F32)

    psum = p[:, 0:QB] + p[:, QB:2 * QB] + p[:, 2 * QB:3 * QB] + p[:, 3 * QB:4 * QB]
    p_hi = psum.astype(BF16)
    p_lo = (psum - p_hi.astype(F32)).astype(BF16)
    imp = (jnp.dot(ovl_ref[...], p_hi, preferred_element_type=F32)
           + jnp.dot(ovl_ref[...], p_lo, preferred_element_type=F32))
    t = qi * QB + lax.broadcasted_iota(jnp.int32, (SEL_ROWS, QB), 1)
    jrow = lax.broadcasted_iota(jnp.int32, (SEL_ROWS, QB), 0)
    jcur = t >> SEL_SHIFT
    n_top = min(SEL_TOPK, nsel)
    forced = (jrow == 0) | (jrow == jcur) | (jrow == jcur - 1)
    work = jnp.where(forced, -1.0, jnp.where(jrow * SEL_BLOCK <= t, imp, -1.0))
    work = jnp.where(jrow < nsel, work, -5.0)
    jf = jrow.astype(F32)
    sel = jnp.where(forced, 1.0, 0.0)
    for _ in range(n_top - 3):
        best = jnp.max(work, axis=0, keepdims=True)
        first = jnp.min(jnp.where(work == best, jf, float(SEL_ROWS)), axis=0, keepdims=True)
        hit = jf == first
        sel = jnp.where(hit, 1.0, sel)
        work = jnp.where(hit, -3.0, work)
    sel = jnp.where(jcur < n_top, jnp.where(jrow < n_top, 1.0, 0.0), sel)
    selneg = jnp.concatenate([jnp.where(sel > 0.5, 0.0, NEG_BIG)] * NSA_HPG, axis=1)
    qts = jnp.concatenate([qt, selneg.astype(BF16)], axis=0)
    sa_ref[...] = scores(0, qts)

    o_wu = []
    for u in range(n_sub):
        tiles = []
        for r in range(WIN_SUB):
            row = (u * WIN_SUB + r) * WIN_KB
            slab = pl.multiple_of(jnp.where(kvalid[u + r], r, WIN_SUB) * WIN_KB, WIN_KB)
            tiles.append(sw_ref[row:row + WIN_KB, :] + tabw_ref[0, 0, pl.ds(slab, WIN_KB), :])
        sw = jnp.concatenate(tiles, axis=0)
        ew = jnp.exp2(sw - jnp.max(sw, axis=0, keepdims=True)).astype(BF16)
        o_wu.append(jnp.dot(jnp.concatenate(vts[u:u + WIN_SUB], axis=1), ew, preferred_element_type=F32))
    o_w = jnp.concatenate([o_wu[u][:, h * WIN_KB:(h + 1) * WIN_KB]
                           for h in range(NSA_HPG) for u in range(n_sub)], axis=1)

    m_ref[...] = jnp.full_like(m_ref, NEG_BIG)
    acc_ref[...] = jnp.zeros_like(acc_ref)

    def consume(s_ref, c, heads=range(NSA_HPG)):
        vt = vst_ref[0, 0, c]
        for h in heads:
            cs = hcols(h)
            sc = s_ref[:, cs]
            m_old = m_ref[:, cs]
            m_new = jnp.maximum(m_old, jnp.max(sc, axis=0, keepdims=True))
            alpha = jnp.exp2(m_old - m_new)
            pe = jnp.exp2(sc - m_new).astype(BF16)
            acc_ref[:, cs] = alpha * acc_ref[:, cs] + jnp.dot(vt, pe, preferred_element_type=F32)
            m_ref[:, cs] = m_new

    def fill_and_consume(dst_ref, c_next, src_ref, c):
        ahead = 2
        for h in range(ahead):
            dst_ref[:, hcols(h)] = scores(c_next, qts, h)
        for h in range(NSA_HPG):
            if h + ahead < NSA_HPG:
                dst_ref[:, hcols(h + ahead)] = scores(c_next, qts, h + ahead)
            consume(src_ref, c, [h])

    n_chunks = (qi * QB + QB - 1) // SEL_CHUNK + 1
    n_pairs = (n_chunks - 1) // 2

    def sel_pair(pr, carry):
        c = 2 * pr
        fill_and_consume(sb_ref, c + 1, sa_ref, c)
        fill_and_consume(sa_ref, c + 2, sb_ref, c + 1)
        return carry

    lax.fori_loop(0, n_pairs, sel_pair, 0)

    @pl.when(n_chunks % 2 == 0)
    def _():
        fill_and_consume(sb_ref, n_chunks - 1, sa_ref, n_chunks - 2)
        consume(sb_ref, n_chunks - 1)

    @pl.when(n_chunks % 2 == 1)
    def _():
        consume(sa_ref, n_chunks - 1)

    acc = acc_ref[...]

    gt = jax.nn.sigmoid(gate_ref[0].astype(F32)).T
    o_s = acc[0:DH] / acc[DH:DH + 1]
    o_w = o_w[0:DH] / o_w[DH:DH + 1]
    heads = []
    for h in range(NSA_HPG):
        cs = slice(h * QB, (h + 1) * QB)
        heads.append(gt[3 * h:3 * h + 1] * o_c[:, cs] + gt[3 * h + 1:3 * h + 2] * o_s[:, cs]
                     + gt[3 * h + 2:3 * h + 3] * o_w[:, cs])
    o_ref[0] = jnp.concatenate(heads, axis=0).astype(BF16)


def _nsa_attn(proj3, prep, tabc, tabw, tabs, ovl, dsat):
    B, S, _ = proj3.shape
    ksa, vst, kwa, vwt, kca, vct, qt = prep
    nq = S // Q_BLOCK
    nc = S // CMP_STRIDE
    nsel = S // SEL_BLOCK
    R = NSA_HPG * Q_BLOCK
    gw = NSA_HPG * NSA_DH
    per_bg = lambda a: pl.BlockSpec((1, 1) + a.shape[2:], lambda b, g, i: (b, g) + (0,) * (a.ndim - 2))
    return pl.pallas_call(
        functools.partial(_attn_kernel, nsel=nsel, dsat=dsat),
        grid=(B, NSA_KV, nq),
        in_specs=[
            pl.BlockSpec((1, 1, 1, NSA_DH, R), lambda b, g, i: (b, g, i, 0, 0)),
            pl.BlockSpec((1, Q_BLOCK, LANES), lambda b, g, i: (b, i, P_GATE // LANES + g)),
            per_bg(kca), per_bg(vct), per_bg(ksa), per_bg(vst), per_bg(kwa), per_bg(vwt),
            pl.BlockSpec((1, 1, nc, R), lambda b, g, i: (g, i, 0, 0)),
            pl.BlockSpec((1, 1) + tabw.shape[2:], lambda b, g, i: (g, 0, 0, 0)),
            pl.BlockSpec((1, 1) + tabs.shape[2:], lambda b, g, i: (g, 0, 0, 0)),
            pl.BlockSpec(ovl.shape, lambda b, g, i: (0, 0)),
        ],
        out_specs=pl.BlockSpec((1, gw, Q_BLOCK), lambda b, g, i: (b, g, i)),
        out_shape=jax.ShapeDtypeStruct((B, NSA_Q, S), BF16),
        scratch_shapes=[pltpu.VMEM((1, R), F32), pltpu.VMEM((V_ROWS, R), F32),
                        pltpu.VMEM((SEL_CHUNK, R), F32), pltpu.VMEM((SEL_CHUNK, R), F32),
                        pltpu.VMEM((Q_BLOCK // WIN_KB * WIN_SUB * WIN_KB, NSA_HPG * WIN_KB), F32)],
        compiler_params=_cparams(("parallel", "parallel", "arbitrary")),
        name="nsa_attn",
    )(qt, proj3, kca, vct, ksa, vst, kwa, vwt, tabc, tabw, tabs, ovl)


def _rel_bucket(d):
    max_exact = REL_BUCKETS // 2
    d = jnp.maximum(d, 0)
    df = jnp.maximum(d, 1).astype(F32)
    large = max_exact + (jnp.log(df / max_exact) / math.log(REL_MAX_DIST / max_exact)
                         * (REL_BUCKETS - max_exact)).astype(jnp.int32)
    return jnp.where(d < max_exact, d, jnp.minimum(large, REL_BUCKETS - 1))


def _bias_kernel(rb_ref, bkt_ref, o_ref, *, rows, cols):
    g = pl.program_id(0)
    step = 32
    for r0 in range(0, rows, step):
        bkt = bkt_ref[0, r0:r0 + step, :]
        acc = [jnp.full(bkt.shape, NEG_BIG, F32) for _ in range(NSA_HPG)]
        for b in range(REL_BUCKETS):
            hit = bkt == b
            for h in range(NSA_HPG):
                acc[h] = jnp.where(hit, rb_ref[b, g * NSA_HPG + h] * LOG2E, acc[h])
        for h in range(NSA_HPG):
            o_ref[0, 0, r0:r0 + step, h * cols:(h + 1) * cols] = acc[h]


def _bias_table(rel_bias, bucket):
    nt, rows, cols = bucket.shape
    return pl.pallas_call(
        functools.partial(_bias_kernel, rows=rows, cols=cols),
        grid=(NSA_KV, nt),
        in_specs=[
            pl.BlockSpec(memory_space=pltpu.SMEM),
            pl.BlockSpec((1, rows, cols), lambda g, t: (t, 0, 0)),
        ],
        out_specs=pl.BlockSpec((1, 1, rows, NSA_HPG * cols), lambda g, t: (g, t, 0, 0)),
        out_shape=jax.ShapeDtypeStruct((NSA_KV, nt, rows, NSA_HPG * cols), F32),
        compiler_params=_cparams(("parallel", "parallel")),
        name="bias_table",
    )(rel_bias, bucket)


def _bias_tables(rel_bias, S, dsat, strip_len):
    nq = S // Q_BLOCK
    nc = S // CMP_STRIDE
    n_cmp = (S - CMP_BLOCK) // CMP_STRIDE + 1
    ar = lambda n: jnp.arange(n, dtype=jnp.int32)
    bucket = lambda d, valid: jnp.where(valid, _rel_bucket(d), -1)
    i = ar(Q_BLOCK)[None, None, :]
    c = ar(nc)[None, :, None]
    d_c = ar(nq)[:, None, None] * Q_BLOCK + i - (c * CMP_STRIDE + CMP_BLOCK - 1)
    tabc = _bias_table(rel_bias, bucket(d_c, (d_c >= 0) & (c < n_cmp)))
    jw = ar((WIN_SUB + 1) * WIN_KB)[None, :, None]
    d_w = ar(WIN_KB)[None, None, :] + WINDOW - jw
    tabw = _bias_table(rel_bias, bucket(d_w, (d_w >= 0) & (d_w < WINDOW) & (jw < WIN_SUB * WIN_KB)))
    d_s = i - ar(strip_len)[None, :, None] + dsat
    tabs = _bias_table(rel_bias, bucket(d_s, d_s >= 0))
    return tabc, tabw, tabs


def _overlap_matrix(S):
    nc = S // CMP_STRIDE
    nsel = S // SEL_BLOCK
    n_cmp = (S - CMP_BLOCK) // CMP_STRIDE + 1
    c = np.arange(nc)[None, :]
    sel_start = (np.arange(SEL_ROWS) * SEL_BLOCK)[:, None]
    ovl = ((c * CMP_STRIDE < sel_start + SEL_BLOCK) & (c * CMP_STRIDE + CMP_BLOCK - 1 >= sel_start)
           & (c < n_cmp) & (np.arange(SEL_ROWS)[:, None] < nsel))
    return jnp.asarray(ovl, BF16)


def _layer_params(l, lower_bounds, norm_mix, w_in, hg_out_gain, nsa_qk_gain, cmp_pos, cmp_w1, cmp_b1,
                  cmp_w2, w_branch_a, w_branch_b, w_out, norm_mlp, w_up, w_down):
    splits = (HG_K, HG_K, HG_V, HG_V, NSA_Q, NSA_KVW, NSA_KVW, NSA_KVW, NSA_KVW, NSA_KVW, NSA_KVW,
              3 * NSA_HEADS, D_MODEL, D_MODEL)
    offs = np.concatenate([[0], np.cumsum(splits)])
    col = lambda k: w_in[l][:, offs[k]:offs[k + 1]]
    gate_w = col(11)
    zpad = jnp.zeros((D_MODEL, LANES - 3 * NSA_HPG), F32)
    gate_cols = []
    for g in range(NSA_KV):
        gate_cols += [gate_w[:, g * 3 * NSA_HPG:(g + 1) * 3 * NSA_HPG], zpad]
    w_main = jnp.concatenate([col(12), col(13), col(0), col(2), col(3),
                              col(7), col(8), col(9), col(10), col(4)] + gate_cols, axis=1).astype(BF16)
    w_cmp = jnp.concatenate([col(1), col(5), col(6)], axis=1).astype(BF16)

    lb = lower_bounds[l]
    lbp = jnp.stack([jnp.log(lb), jnp.log1p(-lb), 1.0 - lb])

    w1 = cmp_w1[l].reshape(2, 2, CMP_STRIDE, NSA_DH, CMP_HID)
    w1c = jnp.zeros((2, CMP_STRIDE, NSA_KV, NSA_DH, NSA_KV, 2, CMP_HID), F32)
    for g in range(NSA_KV):
        w1c = w1c.at[:, :, g, :, g, :, :].set(w1.transpose(0, 2, 3, 1, 4))
    w1c = w1c.reshape(2, CMP_STRIDE * NSA_KVW, NSA_KV * 2 * CMP_HID).astype(BF16)
    pos8 = jnp.zeros((2, 8, CMP_BLOCK * NSA_DH), F32).at[:, 0, :].set(cmp_pos[l].reshape(2, -1))
    w2p = jnp.zeros((2, CMP_HID, LANES), F32)
    w2p = w2p.at[0, :, :NSA_DH].set(cmp_w2[l, 0]).at[1, :NSA_DH, :].set(cmp_w2[l, 1].T).astype(BF16)
    qk = nsa_qk_gain[l]
    z64 = jnp.zeros((NSA_DH,), F32)
    return dict(
        norm_mix=norm_mix[l][None, :], w_main=w_main, w_cmp=w_cmp, lbp=lbp,
        hg_gain=hg_out_gain[l][None, :],
        gsel=jnp.concatenate([qk[2], qk[2]])[None, :], gwin=jnp.concatenate([qk[3], qk[3]])[None, :],
        gcmp=jnp.concatenate([qk[1], z64])[None, :],
        qgain=jnp.broadcast_to((qk[0] * (NSA_DH ** -0.5 * LOG2E))[:, None], (NSA_DH, Q_BLOCK)),
        pos8=pos8, w1=cmp_w1[l].astype(BF16), w1c=w1c, b1=cmp_b1[l][:, None, :], w2p=w2p,
        wa=w_branch_a[l].astype(BF16), wb=w_branch_b[l].astype(BF16), wo=w_out[l].astype(BF16),
        norm_mlp=norm_mlp[l][None, :], wu=w_up[l].astype(BF16), wd=w_down[l].astype(BF16))


def kernel(x, rel_bias, hg_lb_logits, norm_mix, w_in, hg_out_gain, nsa_qk_gain, cmp_pos, cmp_w1, cmp_b1, cmp_w2, w_branch_a, w_branch_b, w_out, norm_mlp, w_up, w_down):
    B, S, D = x.shape
    T = B * S
    depth = w_in.shape[0]
    assert D == D_MODEL and S % SEL_CHUNK == 0 and S >= WINDOW and S // SEL_BLOCK <= SEL_ROWS
    assert S % Q_BLOCK == 0 and Q_BLOCK % WIN_KB == 0 and WINDOW % WIN_KB == 0
    assert FORCE_BONUS > NSA_HPG and SEL_TOPK >= 3
    lb_cum = jnp.cumsum(jax.nn.softmax(hg_lb_logits.astype(F32), axis=0), axis=0)
    lower_bounds = lb_cum - lb_cum[0:1]

    dsat, strip_len = _sel_table_geometry()
    tabc, tabw, tabs = _bias_tables(rel_bias, S, dsat, strip_len)
    ovl = _overlap_matrix(S)
    nc = S // CMP_STRIDE
    tm = min(1024, T)

    x2 = x.reshape(T, D)
    for l in range(depth):
        p = _layer_params(l, lower_bounds, norm_mix, w_in, hg_out_gain, nsa_qk_gain, cmp_pos, cmp_w1,
                          cmp_b1, cmp_w2, w_branch_a, w_branch_b, w_out, norm_mlp, w_up, w_down)
        proj, z, kc, vc = _rms_proj(x2, p["norm_mix"], p["w_main"], p["w_cmp"], tm, P_TILE)
        proj3 = proj.reshape(B, S, P_COLS)
        kc2 = kc.reshape(B, nc, CMP_STRIDE * NSA_KVW)
        vc2 = vc.reshape(B, nc, CMP_STRIDE * NSA_KVW)
        o_a = _hgrn(proj3, z.reshape(B, S, HG_K), p["lbp"], p["hg_gain"])
        prep = _nsa_prep(proj3, kc2, vc2, p["gsel"], p["gwin"], p["gcmp"], p["qgain"], p["pos8"], p["w1"],
                         p["w1c"], p["b1"], p["w2p"])
        o_b = _nsa_attn(proj3, prep, tabc, tabw, tabs, ovl, dsat)
        x2 = _merge(x2, o_a.reshape(T, HG_V), o_b, proj, p["wa"], p["wb"], p["wo"],
                    min(1024, T))
        x2 = _mlp(x2, p["norm_mlp"], p["wu"], p["wd"], tm, 1024)
    return x2.reshape(B, S, D)
```

```python
import functools
import math

import numpy as np
import jax
import jax.numpy as jnp
from jax import lax
from jax.experimental import pallas as pl
from jax.experimental.pallas import tpu as pltpu

F32 = jnp.float32
BF16 = jnp.bfloat16

D_MODEL = 1024
HG_HEADS = 4
HG_DK = 128
HG_DV = 128
HG_K = HG_HEADS * HG_DK
HG_V = HG_HEADS * HG_DV
NSA_HEADS = 8
NSA_KV = 2
NSA_DH = 64
NSA_HPG = NSA_HEADS // NSA_KV
NSA_Q = NSA_HEADS * NSA_DH
NSA_KVW = NSA_KV * NSA_DH
CMP_BLOCK = 32
CMP_STRIDE = 16
CMP_HID = 128
SEL_BLOCK = 64
SEL_TOPK = 8
WINDOW = 512
FORCE_BONUS = 1000.0
REL_BUCKETS = 32
REL_MAX_DIST = 1024
D_FF = 4 * D_MODEL
EPS = 1e-6
NEG_BIG = -1e30
LOG2E = math.log2(math.e)

LANES = 128
VMEM_LIMIT = 56 * 1024 * 1024

P_GAB = 0
P_HG = P_GAB + 2 * D_MODEL
P_KV = P_HG + 3 * HG_K
P_NQ = P_KV + 4 * NSA_KVW
P_GATE = P_NQ + NSA_Q
P_COLS = P_GATE + NSA_KV * LANES
P_TILE = P_COLS
ROW_BLOCKS = 4

HG_CHUNK = 128
HG_TS = 2048
Q_BLOCK = 256
WIN_KB = 128
SEL_CHUNK = 512
WIN_BLOCKS = (WINDOW + Q_BLOCK) // WIN_KB
WIN_SUB = WINDOW // WIN_KB + 1
SEL_SHIFT = int(math.log2(SEL_BLOCK))
SEL_ROWS = LANES - NSA_DH
V_ROWS = NSA_DH + 16


def _cparams(sem):
    return pltpu.CompilerParams(dimension_semantics=sem, vmem_limit_bytes=VMEM_LIMIT)


def _rms_rows(x, gain):
    ms = jnp.mean(x * x, axis=-1, keepdims=True)
    return x * lax.rsqrt(ms + EPS) * gain


def _proj_kernel(x_ref, g_ref, w_ref, wx_ref, o_ref, z_ref, kc_ref, vc_ref, h_ref, ck_ref, cv_ref):
    j = pl.program_id(1)
    rb = x_ref.shape[0] // ROW_BLOCKS

    @pl.when(j == 0)
    def _():
        for r in range(ROW_BLOCKS):
            rows = slice(r * rb, (r + 1) * rb)
            h = _rms_rows(x_ref[rows, :], g_ref[...]).astype(BF16)
            h_ref[rows, :] = h
            o_ref[rows, :] = jnp.dot(h, w_ref[...], preferred_element_type=F32).astype(BF16)
            c = jnp.dot(h, wx_ref[...], preferred_element_type=F32)
            z_ref[rows, :] = c[:, :HG_K]
            ck_ref[rows, :] = c[:, HG_K:HG_K + NSA_KVW]
            cv_ref[rows, :] = c[:, HG_K + NSA_KVW:]
        n16 = ck_ref.shape[0] // CMP_STRIDE
        for t in range(CMP_STRIDE):
            lanes = slice(t * NSA_KVW, (t + 1) * NSA_KVW)
            kc_ref[:, lanes] = ck_ref[pl.ds(t, n16, stride=CMP_STRIDE), :].astype(BF16)
            vc_ref[:, lanes] = cv_ref[pl.ds(t, n16, stride=CMP_STRIDE), :].astype(BF16)

    @pl.when(j != 0)
    def _():
        o_ref[...] = jnp.dot(h_ref[...], w_ref[...], preferred_element_type=F32).astype(BF16)


def _rms_proj(x2, gain, w, wx, tm, tn):
    T, D = x2.shape
    N = w.shape[1]
    cmp_sds = jax.ShapeDtypeStruct((T // CMP_STRIDE, CMP_STRIDE * NSA_KVW), BF16)
    return pl.pallas_call(
        _proj_kernel,
        grid=(T // tm, N // tn),
        in_specs=[
            pl.BlockSpec((tm, D), lambda i, j: (i, 0)),
            pl.BlockSpec((1, D), lambda i, j: (0, 0)),
            pl.BlockSpec((D, tn), lambda i, j: (0, j), pipeline_mode=pl.Buffered(1)),
            pl.BlockSpec(wx.shape, lambda i, j: (0, 0), pipeline_mode=pl.Buffered(1)),
        ],
        out_specs=[
            pl.BlockSpec((tm, tn), lambda i, j: (i, j)),
            pl.BlockSpec((tm, HG_K), lambda i, j: (i, 0)),
            pl.BlockSpec((tm // CMP_STRIDE, CMP_STRIDE * NSA_KVW), lambda i, j: (i, 0)),
            pl.BlockSpec((tm // CMP_STRIDE, CMP_STRIDE * NSA_KVW), lambda i, j: (i, 0)),
        ],
        out_shape=[jax.ShapeDtypeStruct((T, N), BF16), jax.ShapeDtypeStruct((T, HG_K), F32),
                   cmp_sds, cmp_sds],
        scratch_shapes=[pltpu.VMEM((tm, D), BF16), pltpu.VMEM((tm, NSA_KVW), F32),
                        pltpu.VMEM((tm, NSA_KVW), F32)],
        compiler_params=_cparams(("parallel", "arbitrary")),
        name="rms_proj",
    )(x2, gain, w, wx)


def _mlp_kernel(x_ref, g_ref, wu_ref, wd_ref, o_ref, h_ref):
    f = pl.program_id(1)

    rb = x_ref.shape[0] // ROW_BLOCKS

    def up_down(h):
        u = jnp.dot(h, wu_ref[...], preferred_element_type=F32)
        u = jnp.square(jnp.maximum(u, 0.0)).astype(BF16)
        return jnp.dot(u, wd_ref[...], preferred_element_type=F32)

    @pl.when(f == 0)
    def _():
        for r in range(ROW_BLOCKS):
            rows = slice(r * rb, (r + 1) * rb)
            x = x_ref[rows, :]
            h = _rms_rows(x, g_ref[...]).astype(BF16)
            h_ref[rows, :] = h
            o_ref[rows, :] = x + up_down(h)

    @pl.when(f != 0)
    def _():
        o_ref[...] += up_down(h_ref[...])


def _mlp(x2, gain, wu, wd, tm, tf):
    T, D = x2.shape
    FF = wu.shape[1]
    return pl.pallas_call(
        _mlp_kernel,
        grid=(T // tm, FF // tf),
        in_specs=[
            pl.BlockSpec((tm, D), lambda i, f: (i, 0)),
            pl.BlockSpec((1, D), lambda i, f: (0, 0)),
            pl.BlockSpec((D, tf), lambda i, f: (0, f)),
            pl.BlockSpec((tf, D), lambda i, f: (f, 0)),
        ],
        out_specs=pl.BlockSpec((tm, D), lambda i, f: (i, 0)),
        out_shape=jax.ShapeDtypeStruct((T, D), F32),
        scratch_shapes=[pltpu.VMEM((tm, D), BF16)],
        compiler_params=_cparams(("parallel", "arbitrary")),
        name="mlp",
    )(x2, gain, wu, wd)


def _merge_kernel(x_ref, oa_ref, ob_ref, ga_ref, gb_ref, wa_ref, wb_ref, wo_ref, o_ref):
    ya = jnp.dot(oa_ref[...], wa_ref[...], preferred_element_type=F32)
    yb = jnp.dot(ob_ref[...], wb_ref[...], preferred_element_type=F32)
    mixed = (jax.nn.sigmoid(ga_ref[...].astype(F32)) * ya
             + jax.nn.sigmoid(gb_ref[...].astype(F32)) * yb)
    o_ref[...] = x_ref[...] + jnp.dot(mixed.astype(BF16), wo_ref[...], preferred_element_type=F32)


def _merge(x2, oa, ob, proj, wa, wb, wo, tm):
    T, D = x2.shape
    ga_blk = P_GAB // D
    return pl.pallas_call(
        _merge_kernel,
        grid=(T // tm,),
        in_specs=[
            pl.BlockSpec((tm, D), lambda i: (i, 0)),
            pl.BlockSpec((tm, HG_V), lambda i: (i, 0)),
            pl.BlockSpec((tm, NSA_Q), lambda i: (i, 0)),
            pl.BlockSpec((tm, D), lambda i: (i, ga_blk)),
            pl.BlockSpec((tm, D), lambda i: (i, ga_blk + 1)),
            pl.BlockSpec((HG_V, D), lambda i: (0, 0)),
            pl.BlockSpec((NSA_Q, D), lambda i: (0, 0)),
            pl.BlockSpec((D, D), lambda i: (0, 0)),
        ],
        out_specs=pl.BlockSpec((tm, D), lambda i: (i, 0)),
        out_shape=jax.ShapeDtypeStruct((T, D), F32),
        compiler_params=_cparams(("parallel",)),
        name="merge",
    )(x2, oa, ob, proj, proj, wa, wb, wo)


def _hgrn_consts(C):
    L = int(math.log2(C))
    idx = np.arange(C)
    mats = [(idx[None, :] <= idx[:, None]),
            (idx[None, :] > idx[:, None])]
    bmask = []
    for l in range(L):
        m = 1 << l
        r = (idx & ~(2 * m - 1)) + m
        lo = np.minimum(idx, r)[:, None]
        hi = np.maximum(idx, r)[:, None]
        mats.append((idx[None, :] > lo) & (idx[None, :] <= hi))
        blk = idx >> (l + 1)
        bmask.append(blk[:, None] == blk[None, :])
    bmask.append(idx[:, None] == idx[None, :])
    seg = np.concatenate(mats, axis=0).astype(np.float32)
    seg = np.concatenate([seg, seg], axis=1)
    return seg, np.stack(bmask).astype(np.float32), L


def _hgrn_kernel(q_ref, f_ref, i_ref, g_ref, lbp_ref, gain_ref, seg_ref, bm_ref, o_ref, st_ref,
                 k0_ref, sg0_ref, k1_ref, sg1_ref, *, C, L, nchunk):
    @pl.when(pl.program_id(1) == 0)
    def _():
        st_ref[...] = jnp.zeros_like(st_ref)

    rowi = lax.broadcasted_iota(jnp.int32, (C, HG_DK), 0)
    nt = (((1,), (1,)), ((), ()))
    tn = (((0,), (0,)), ((), ()))

    H = range(HG_HEADS)
    css = [slice(h * HG_DK, (h + 1) * HG_DK) for h in H]

    def gates(c, k_ref, sg_ref):
        r0 = pl.multiple_of(c * C, C)
        z = f_ref[0, pl.ds(r0, C), :]
        ez = jnp.exp(-jnp.abs(z))
        y = lbp_ref[1:2, :] + (jnp.minimum(z, 0.0) - jnp.log(1.0 + ez))
        log_lb = lbp_ref[0:1, :]
        logf = (jnp.maximum(log_lb, y) + jnp.log(1.0 + jnp.exp(-jnp.abs(log_lb - y)))) * LOG2E
        k_ref[...] = lbp_ref[2:3, :] * (jnp.where(z >= 0.0, ez, 1.0) / (1.0 + ez))
        g_hi = logf.astype(BF16)
        g_lo = (logf - g_hi.astype(F32)).astype(BF16)
        sg_ref[...] = jnp.dot(seg_ref[...], jnp.concatenate([g_hi, g_lo], axis=0),
                              preferred_element_type=F32)

    def mix(c, k_ref, sg_ref):
        r0 = pl.multiple_of(c * C, C)
        qbs = [q_ref[0, pl.ds(r0, C), cs] for cs in css]
        qs = [qb.astype(F32) for qb in qbs]
        ks = [k_ref[:, cs] for cs in css]
        ss = [lax.dot_general(qbs[h], ks[h].astype(BF16), nt, preferred_element_type=F32) * bm_ref[L]
              for h in H]
        for l in range(L):
            second = (rowi & (1 << l)) != 0
            for h in H:
                e = jnp.exp2(sg_ref[(2 + l) * C:(3 + l) * C, css[h]])
                ql = jnp.where(second, qs[h] * e, 0.0).astype(BF16)
                kl = jnp.where(second, 0.0, ks[h] * e).astype(BF16)
                ss[h] = ss[h] + lax.dot_general(ql, kl, nt, preferred_element_type=F32) * bm_ref[l]
        return r0, qs, ks, ss

    def mix_out(scores, k_ref, sg_ref):
        r0, qs, ks, ss = scores
        vbs = [i_ref[0, pl.ds(r0, C), cs] for cs in css]
        bs = [sg_ref[0:C, cs] for cs in css]
        sts = [st_ref[h] for h in H]
        outs = []
        for h in H:
            qd = (qs[h] * jnp.exp2(bs[h])).astype(BF16)
            o = lax.dot_general(qd, sts[h].astype(BF16), nt, preferred_element_type=F32)
            outs.append(o + jnp.dot(ss[h].astype(BF16), vbs[h], preferred_element_type=F32))
        for h in H:
            kd = (ks[h] * jnp.exp2(sg_ref[C:2 * C, css[h]])).astype(BF16)
            st_ref[h] = (sts[h] * jnp.exp2(bs[h][C - 1:C, :])
                         + lax.dot_general(vbs[h], kd, tn, preferred_element_type=F32))
        for h in H:
            gg = g_ref[0, pl.ds(r0, C), css[h]].astype(F32)
            o = _rms_rows(outs[h], gain_ref[...]) * (gg * jax.nn.sigmoid(gg))
            o_ref[0, pl.ds(r0, C), css[h]] = o.astype(BF16)

    gates(0, k0_ref, sg0_ref)

    def pair(pr, carry):
        c = 2 * pr
        sc0 = mix(c, k0_ref, sg0_ref)
        gates(c + 1, k1_ref, sg1_ref)
        mix_out(sc0, k0_ref, sg0_ref)
        sc1 = mix(c + 1, k1_ref, sg1_ref)
        gates(jnp.minimum(c + 2, nchunk - 1), k0_ref, sg0_ref)
        mix_out(sc1, k1_ref, sg1_ref)
        return carry

    lax.fori_loop(0, nchunk // 2, pair, 0)


def _hgrn(proj3, z3, lbp, gain, C=HG_CHUNK, ts=HG_TS):
    B, S, _ = proj3.shape
    ts = min(ts, S)
    assert (ts // C) % 2 == 0
    seg, bm, L = _hgrn_consts(C)
    blk = lambda k: pl.BlockSpec((1, ts, HG_K), lambda b, s: (b, s, P_HG // HG_K + k))
    return pl.pallas_call(
        functools.partial(_hgrn_kernel, C=C, L=L, nchunk=ts // C),
        grid=(B, S // ts),
        in_specs=[
            blk(0), pl.BlockSpec((1, ts, HG_K), lambda b, s: (b, s, 0)), blk(1), blk(2),
            pl.BlockSpec((3, HG_K), lambda b, s: (0, 0)),
            pl.BlockSpec((1, HG_DV), lambda b, s: (0, 0)),
            pl.BlockSpec(seg.shape, lambda b, s: (0, 0)),
            pl.BlockSpec(bm.shape, lambda b, s: (0, 0, 0)),
        ],
        out_specs=pl.BlockSpec((1, ts, HG_V), lambda b, s: (b, s, 0)),
        out_shape=jax.ShapeDtypeStruct((B, S, HG_V), BF16),
        scratch_shapes=[pltpu.VMEM((HG_HEADS, HG_DV, HG_DK), F32)]
        + 2 * [pltpu.VMEM((C, HG_K), F32), pltpu.VMEM(((2 + L) * C, HG_K), F32)],
        compiler_params=_cparams(("parallel", "arbitrary")),
        name="hgrn2",
    )(proj3, z3, proj3, proj3, lbp, gain, jnp.asarray(seg, BF16), jnp.asarray(bm))


def _half_rms(x, gain2, lane):
    sq = x * x
    s_lo = jnp.sum(jnp.where(lane < NSA_DH, sq, 0.0), axis=-1, keepdims=True)
    s_hi = jnp.sum(jnp.where(lane < NSA_DH, 0.0, sq), axis=-1, keepdims=True)
    ms = jnp.where(lane < NSA_DH, s_lo, s_hi) * (1.0 / NSA_DH)
    return x * lax.rsqrt(ms + EPS) * gain2


def _gelu_tanh(x):
    return 0.5 * x * (1.0 + jnp.tanh(math.sqrt(2.0 / math.pi) * (x + 0.044715 * (x * x * x))))


def _prep_kernel(kv_ref, nq_ref, kc_ref, vc_ref, gsel_ref, gwin_ref, gcmp_ref, qg_ref, pos_ref, w1_ref,
                 w1c_ref, b1_ref, w2_ref, ksa_ref, vst_ref, kwa_ref, vwt_ref, kca_ref, vct_ref, qt_ref,
                 *, S, rows):
    lane = lax.broadcasted_iota(jnp.int32, (rows, LANES), 1)
    rowi = lax.broadcasted_iota(jnp.int32, (rows, LANES), 0)
    ones_row = (lax.broadcasted_iota(jnp.int32, (V_ROWS - NSA_DH, rows), 0) == 0).astype(F32)
    low = lane < NSA_DH

    def body(i, carry):
        r0 = pl.multiple_of(i * rows, rows)
        blk_hot = jnp.where(lane - NSA_DH == ((r0 + rowi) >> SEL_SHIFT), 1.0, 0.0)
        for src, gain_ref, aug, k_dst, v_dst, vrows in (
                (0, gsel_ref, blk_hot, ksa_ref, vst_ref, rows),
                (2, gwin_ref, jnp.zeros((rows, LANES), F32), kwa_ref, vwt_ref, WIN_KB)):
            kk = kv_ref[0, pl.ds(r0, rows), src * LANES:(src + 1) * LANES].astype(F32)
            vv = kv_ref[0, pl.ds(r0, rows), (src + 1) * LANES:(src + 2) * LANES].astype(F32)
            kn = _half_rms(kk, gain_ref[...], lane)
            k_dst[0, 0, pl.ds(r0, rows), :] = jnp.where(low, kn, aug).astype(BF16)
            k_dst[0, 1, pl.ds(r0, rows), :] = jnp.where(low, pltpu.roll(kn, NSA_DH, 1), aug).astype(BF16)
            vt = vv.T
            for g in range(NSA_KV):
                vg = jnp.concatenate([vt[g * NSA_DH:(g + 1) * NSA_DH], ones_row], axis=0).astype(BF16)
                for u in range(rows // vrows):
                    v_dst[0, g, i * (rows // vrows) + u] = vg[:, u * vrows:(u + 1) * vrows]
        for u in range(rows // Q_BLOCK):
            xt = nq_ref[0, pl.ds(r0 + u * Q_BLOCK, Q_BLOCK), :].astype(F32).T
            for g in range(NSA_KV):
                cols = []
                for h in range(NSA_HPG):
                    xh = xt[(g * NSA_HPG + h) * NSA_DH:(g * NSA_HPG + h + 1) * NSA_DH]
                    ms = jnp.mean(xh * xh, axis=0, keepdims=True)
                    cols.append(xh * lax.rsqrt(ms + EPS) * qg_ref[...])
                qt_ref[0, g, i * (rows // Q_BLOCK) + u] = jnp.concatenate(cols, axis=1).astype(BF16)
        return carry

    lax.fori_loop(0, S // rows, body, 0)

    nc = S // CMP_STRIDE
    outs = []
    for j, src_ref in ((0, kc_ref), (1, vc_ref)):
        bias = jnp.dot(pos_ref[j].astype(BF16), w1_ref[j], preferred_element_type=F32)[0:1] + b1_ref[j]
        pj = jnp.dot(src_ref[0], w1c_ref[j], preferred_element_type=F32)
        per_g = []
        for g in range(NSA_KV):
            first = pj[:, (2 * g) * CMP_HID:(2 * g + 1) * CMP_HID]
            second = pj[:, (2 * g + 1) * CMP_HID:(2 * g + 2) * CMP_HID]
            hid = _gelu_tanh(first + pltpu.roll(second, nc - 1, 0) + bias).astype(BF16)
            if j == 0:
                per_g.append(jnp.dot(hid, w2_ref[0], preferred_element_type=F32))
            else:
                per_g.append(lax.dot_general(w2_ref[1, :NSA_DH, :], hid, (((1,), (1,)), ((), ())),
                                             preferred_element_type=F32))
        outs.append(per_g)
    for g in range(NSA_KV):
        kc = outs[0][g]
        ms = jnp.sum(kc * kc, axis=-1, keepdims=True) * (1.0 / NSA_DH)
        kca_ref[0, g] = (kc * lax.rsqrt(ms + EPS) * gcmp_ref[...]).astype(BF16)
        vct_ref[0, g] = outs[1][g].astype(BF16)


def _nsa_prep(proj3, kc2, vc2, gsel, gwin, gcmp, qgain, pos8, w1, w1c, b1, w2p):
    B, S, _ = proj3.shape
    nc = S // CMP_STRIDE
    rows = SEL_CHUNK
    full = lambda shp: pl.BlockSpec(shp, lambda b: (0,) * len(shp))
    whole = lambda shp: pl.BlockSpec((1,) + shp, lambda b: (b,) + (0,) * len(shp))
    shapes = [(NSA_KV, S, LANES), (NSA_KV, S // SEL_CHUNK, V_ROWS, SEL_CHUNK),
              (NSA_KV, S, LANES), (NSA_KV, S // WIN_KB, V_ROWS, WIN_KB),
              (NSA_KV, nc, LANES), (NSA_KV, NSA_DH, nc),
              (NSA_KV, S // Q_BLOCK, NSA_DH, NSA_HPG * Q_BLOCK)]
    return pl.pallas_call(
        functools.partial(_prep_kernel, S=S, rows=rows),
        grid=(B,),
        in_specs=[
            pl.BlockSpec((1, S, 4 * NSA_KVW), lambda b: (b, 0, P_KV // (4 * NSA_KVW))),
            pl.BlockSpec((1, S, NSA_Q), lambda b: (b, 0, P_NQ // NSA_Q)),
            pl.BlockSpec((1, nc, CMP_STRIDE * NSA_KVW), lambda b: (b, 0, 0)),
            pl.BlockSpec((1, nc, CMP_STRIDE * NSA_KVW), lambda b: (b, 0, 0)),
            full((1, LANES)), full((1, LANES)), full((1, LANES)), full(qgain.shape),
            full(pos8.shape), full(w1.shape), full(w1c.shape), full(b1.shape), full(w2p.shape),
        ],
        out_specs=[whole(s) for s in shapes],
        out_shape=[jax.ShapeDtypeStruct((B,) + s, BF16) for s in shapes],
        compiler_params=_cparams(("parallel",)),
        name="nsa_prep",
    )(proj3, proj3, kc2, vc2, gsel, gwin, gcmp, qgain, pos8, w1, w1c, b1, w2p)


def _sel_table_geometry():
    last_start = (REL_BUCKETS // 2) * (REL_MAX_DIST / (REL_BUCKETS // 2)) ** (
        (REL_BUCKETS - REL_BUCKETS // 2 - 1) / (REL_BUCKETS - REL_BUCKETS // 2))
    dsat = int(math.ceil((last_start + 16 + SEL_CHUNK) / LANES)) * LANES
    return dsat, dsat + SEL_CHUNK


def _attn_kernel(q_ref, gate_ref, kca_ref, vct_ref, ksa_ref, vst_ref, kwa_ref, vwt_ref, tabc_ref,
                 tabw_ref, tabs_ref, ovl_ref, o_ref, m_ref, acc_ref, sa_ref, sb_ref, sw_ref,
                 *, nsel, dsat):
    qi = pl.program_id(2)
    QB = Q_BLOCK
    R = NSA_HPG * QB
    DH = NSA_DH

    qt = q_ref[0, 0, 0]
    qta = jnp.concatenate([qt, jnp.zeros_like(qt)], axis=0)

    n_sub = QB // WIN_KB
    vts, kws, kvalid = [], [], []
    for r in range(WIN_BLOCKS):
        kb = qi * n_sub - WINDOW // WIN_KB + r
        kbc = jnp.maximum(kb, 0)
        kws.append(kwa_ref[0, 0, pl.ds(pl.multiple_of(kbc * WIN_KB, WIN_KB), WIN_KB), :])
        vts.append(vwt_ref[0, 0, kbc])
        kvalid.append(kb >= 0)
    for u in range(n_sub):
        q_u = jnp.concatenate([qta[:, h * QB + u * WIN_KB:h * QB + (u + 1) * WIN_KB] for h in range(NSA_HPG)],
                              axis=1)
        for r in range(WIN_SUB):
            row = (u * WIN_SUB + r) * WIN_KB
            sw_ref[row:row + WIN_KB, :] = jnp.dot(kws[u + r], q_u, preferred_element_type=F32)

    def hcols(h):
        return slice(h * QB, (h + 1) * QB)

    def scores(c, q_aug, h=None):
        k0 = pl.multiple_of(c * SEL_CHUNK, SEL_CHUNK)
        delta = qi * QB - c * SEL_CHUNK
        start = pl.multiple_of(dsat - jnp.minimum(delta, dsat), LANES)
        cs = slice(None) if h is None else hcols(h)
        return (jnp.dot(ksa_ref[0, 0, pl.ds(k0, SEL_CHUNK), :], q_aug[:, cs], preferred_element_type=F32)
                + tabs_ref[0, 0, pl.ds(start, SEL_CHUNK), cs])

    s = jnp.dot(kca_ref[0, 0], qta, preferred_element_type=F32) + tabc_ref[0, 0]
    mx = jnp.max(s, axis=0, keepdims=True)
    e = jnp.exp2(s - mx)
    inv = jnp.where(mx > 0.5 * NEG_BIG, 1.0 / jnp.sum(e, axis=0, keepdims=True), 0.0)
    p = e * inv
    o_c = jnp.dot(vct_ref[0, 0], p.astype(BF16), preferred_element_type=F32)

    psum = p[:, 0:QB] + p[:, QB:2 * QB] + p[:, 2 * QB:3 * QB] + p[:, 3 * QB:4 * QB]
    p_hi = psum.astype(BF16)
    p_lo = (psum - p_hi.astype(F32)).astype(BF16)
    imp = (jnp.dot(ovl_ref[...], p_hi, preferred_element_type=F32)
           + jnp.dot(ovl_ref[...], p_lo, preferred_element_type=F32))
    t = qi * QB + lax.broadcasted_iota(jnp.int32, (SEL_ROWS, QB), 1)
    jrow = lax.broadcasted_iota(jnp.int32, (SEL_ROWS, QB), 0)
    jcur = t >> SEL_SHIFT
    n_top = min(SEL_TOPK, nsel)
    forced = (jrow == 0) | (jrow == jcur) | (jrow == jcur - 1)
    work = jnp.where(forced, -1.0, jnp.where(jrow * SEL_BLOCK <= t, imp, -1.0))
    work = jnp.where(jrow < nsel, work, -5.0)
    jf = jrow.astype(F32)
    sel = jnp.where(forced, 1.0, 0.0)
    for _ in range(n_top - 3):
        best = jnp.max(work, axis=0, keepdims=True)
        first = jnp.min(jnp.where(work == best, jf, float(SEL_ROWS)), axis=0, keepdims=True)
        hit = jf == first
        sel = jnp.where(hit, 1.0, sel)
        work = jnp.where(hit, -3.0, work)
    sel = jnp.where(jcur < n_top, jnp.where(jrow < n_top, 1.0, 0.0), sel)
    selneg = jnp.concatenate([jnp.where(sel > 0.5, 0.0, NEG_BIG)] * NSA_HPG, axis=1)
    qts = jnp.concatenate([qt, selneg.astype(BF16)], axis=0)
    sa_ref[...] = scores(0, qts)

    o_wu = []
    for u in range(n_sub):
        tiles = []
        for r in range(WIN_SUB):
            row = (u * WIN_SUB + r) * WIN_KB
            slab = pl.multiple_of(jnp.where(kvalid[u + r], r, WIN_SUB) * WIN_KB, WIN_KB)
            tiles.append(sw_ref[row:row + WIN_KB, :] + tabw_ref[0, 0, pl.ds(slab, WIN_KB), :])
        sw = jnp.concatenate(tiles, axis=0)
        ew = jnp.exp2(sw - jnp.max(sw, axis=0, keepdims=True)).astype(BF16)
        o_wu.append(jnp.dot(jnp.concatenate(vts[u:u + WIN_SUB], axis=1), ew, preferred_element_type=F32))
    o_w = jnp.concatenate([o_wu[u][:, h * WIN_KB:(h + 1) * WIN_KB]
                           for h in range(NSA_HPG) for u in range(n_sub)], axis=1)

    m_ref[...] = jnp.full_like(m_ref, NEG_BIG)
    acc_ref[...] = jnp.zeros_like(acc_ref)

    def consume(s_ref, c, heads=range(NSA_HPG)):
        vt = vst_ref[0, 0, c]
        for h in heads:
            cs = hcols(h)
            sc = s_ref[:, cs]
            m_old = m_ref[:, cs]
            m_new = jnp.maximum(m_old, jnp.max(sc, axis=0, keepdims=True))
            alpha = jnp.exp2(m_old - m_new)
            pe = jnp.exp2(sc - m_new).astype(BF16)
            acc_ref[:, cs] = alpha * acc_ref[:, cs] + jnp.dot(vt, pe, preferred_element_type=F32)
            m_ref[:, cs] = m_new

    def fill_and_consume(dst_ref, c_next, src_ref, c):
        dst_ref[:, hcols(0)] = scores(c_next, qts, 0)
        for h in range(NSA_HPG):
            if h + 1 < NSA_HPG:
                dst_ref[:, hcols(h + 1)] = scores(c_next, qts, h + 1)
            consume(src_ref, c, [h])

    n_chunks = (qi * QB + QB - 1) // SEL_CHUNK + 1
    n_pairs = (n_chunks - 1) // 2

    def sel_pair(pr, carry):
        c = 2 * pr
        fill_and_consume(sb_ref, c + 1, sa_ref, c)
        fill_and_consume(sa_ref, c + 2, sb_ref, c + 1)
        return carry

    lax.fori_loop(0, n_pairs, sel_pair, 0)

    @pl.when(n_chunks % 2 == 0)
    def _():
        fill_and_consume(sb_ref, n_chunks - 1, sa_ref, n_chunks - 2)
        consume(sb_ref, n_chunks - 1)

    @pl.when(n_chunks % 2 == 1)
    def _():
        consume(sa_ref, n_chunks - 1)

    acc = acc_ref[...]

    gt = jax.nn.sigmoid(gate_ref[0].astype(F32)).T
    o_s = acc[0:DH] / acc[DH:DH + 1]
    o_w = o_w[0:DH] / o_w[DH:DH + 1]
    heads = []
    for h in range(NSA_HPG):
        cs = slice(h * QB, (h + 1) * QB)
        heads.append(gt[3 * h:3 * h + 1] * o_c[:, cs] + gt[3 * h + 1:3 * h + 2] * o_s[:, cs]
                     + gt[3 * h + 2:3 * h + 3] * o_w[:, cs])
    o_ref[0] = jnp.concatenate(heads, axis=0).T.astype(BF16)


def _nsa_attn(proj3, prep, tabc, tabw, tabs, ovl, dsat):
    B, S, _ = proj3.shape
    ksa, vst, kwa, vwt, kca, vct, qt = prep
    nq = S // Q_BLOCK
    nc = S // CMP_STRIDE
    nsel = S // SEL_BLOCK
    R = NSA_HPG * Q_BLOCK
    gw = NSA_HPG * NSA_DH
    per_bg = lambda a: pl.BlockSpec((1, 1) + a.shape[2:], lambda b, g, i: (b, g) + (0,) * (a.ndim - 2))
    return pl.pallas_call(
        functools.partial(_attn_kernel, nsel=nsel, dsat=dsat),
        grid=(B, NSA_KV, nq),
        in_specs=[
            pl.BlockSpec((1, 1, 1, NSA_DH, R), lambda b, g, i: (b, g, i, 0, 0)),
            pl.BlockSpec((1, Q_BLOCK, LANES), lambda b, g, i: (b, i, P_GATE // LANES + g)),
            per_bg(kca), per_bg(vct), per_bg(ksa), per_bg(vst), per_bg(kwa), per_bg(vwt),
            pl.BlockSpec((1, 1, nc, R), lambda b, g, i: (g, i, 0, 0)),
            pl.BlockSpec((1, 1) + tabw.shape[2:], lambda b, g, i: (g, 0, 0, 0)),
            pl.BlockSpec((1, 1) + tabs.shape[2:], lambda b, g, i: (g, 0, 0, 0)),
            pl.BlockSpec(ovl.shape, lambda b, g, i: (0, 0)),
        ],
        out_specs=pl.BlockSpec((1, Q_BLOCK, gw), lambda b, g, i: (b, i, g)),
        out_shape=jax.ShapeDtypeStruct((B, S, NSA_Q), BF16),
        scratch_shapes=[pltpu.VMEM((1, R), F32), pltpu.VMEM((V_ROWS, R), F32),
                        pltpu.VMEM((SEL_CHUNK, R), F32), pltpu.VMEM((SEL_CHUNK, R), F32),
                        pltpu.VMEM((Q_BLOCK // WIN_KB * WIN_SUB * WIN_KB, NSA_HPG * WIN_KB), F32)],
        compiler_params=_cparams(("parallel", "parallel", "arbitrary")),
        name="nsa_attn",
    )(qt, proj3, kca, vct, ksa, vst, kwa, vwt, tabc, tabw, tabs, ovl)


def _rel_bucket(d):
    max_exact = REL_BUCKETS // 2
    d = jnp.maximum(d, 0)
    df = jnp.maximum(d, 1).astype(F32)
    large = max_exact + (jnp.log(df / max_exact) / math.log(REL_MAX_DIST / max_exact)
                         * (REL_BUCKETS - max_exact)).astype(jnp.int32)
    return jnp.where(d < max_exact, d, jnp.minimum(large, REL_BUCKETS - 1))


def _bias_kernel(rb_ref, bkt_ref, o_ref, *, rows, cols):
    g = pl.program_id(0)
    step = 32
    for r0 in range(0, rows, step):
        bkt = bkt_ref[0, r0:r0 + step, :]
        acc = [jnp.full(bkt.shape, NEG_BIG, F32) for _ in range(NSA_HPG)]
        for b in range(REL_BUCKETS):
            hit = bkt == b
            for h in range(NSA_HPG):
                acc[h] = jnp.where(hit, rb_ref[b, g * NSA_HPG + h] * LOG2E, acc[h])
        for h in range(NSA_HPG):
            o_ref[0, 0, r0:r0 + step, h * cols:(h + 1) * cols] = acc[h]


def _bias_table(rel_bias, bucket):
    nt, rows, cols = bucket.shape
    return pl.pallas_call(
        functools.partial(_bias_kernel, rows=rows, cols=cols),
        grid=(NSA_KV, nt),
        in_specs=[
            pl.BlockSpec(memory_space=pltpu.SMEM),
            pl.BlockSpec((1, rows, cols), lambda g, t: (t, 0, 0)),
        ],
        out_specs=pl.BlockSpec((1, 1, rows, NSA_HPG * cols), lambda g, t: (g, t, 0, 0)),
        out_shape=jax.ShapeDtypeStruct((NSA_KV, nt, rows, NSA_HPG * cols), F32),
        compiler_params=_cparams(("parallel", "parallel")),
        name="bias_table",
    )(rel_bias, bucket)


def _bias_tables(rel_bias, S, dsat, strip_len):
    nq = S // Q_BLOCK
    nc = S // CMP_STRIDE
    n_cmp = (S - CMP_BLOCK) // CMP_STRIDE + 1
    ar = lambda n: jnp.arange(n, dtype=jnp.int32)
    bucket = lambda d, valid: jnp.where(valid, _rel_bucket(d), -1)
    i = ar(Q_BLOCK)[None, None, :]
    c = ar(nc)[None, :, None]
    d_c = ar(nq)[:, None, None] * Q_BLOCK + i - (c * CMP_STRIDE + CMP_BLOCK - 1)
    tabc = _bias_table(rel_bias, bucket(d_c, (d_c >= 0) & (c < n_cmp)))
    jw = ar((WIN_SUB + 1) * WIN_KB)[None, :, None]
    d_w = ar(WIN_KB)[None, None, :] + WINDOW - jw
    tabw = _bias_table(rel_bias, bucket(d_w, (d_w >= 0) & (d_w < WINDOW) & (jw < WIN_SUB * WIN_KB)))
    d_s = i - ar(strip_len)[None, :, None] + dsat
    tabs = _bias_table(rel_bias, bucket(d_s, d_s >= 0))
    return tabc, tabw, tabs


def _overlap_matrix(S):
    nc = S // CMP_STRIDE
    nsel = S // SEL_BLOCK
    n_cmp = (S - CMP_BLOCK) // CMP_STRIDE + 1
    c = np.arange(nc)[None, :]
    sel_start = (np.arange(SEL_ROWS) * SEL_BLOCK)[:, None]
    ovl = ((c * CMP_STRIDE < sel_start + SEL_BLOCK) & (c * CMP_STRIDE + CMP_BLOCK - 1 >= sel_start)
           & (c < n_cmp) & (np.arange(SEL_ROWS)[:, None] < nsel))
    return jnp.asarray(ovl, BF16)


def _layer_params(l, lower_bounds, norm_mix, w_in, hg_out_gain, nsa_qk_gain, cmp_pos, cmp_w1, cmp_b1,
                  cmp_w2, w_branch_a, w_branch_b, w_out, norm_mlp, w_up, w_down):
    splits = (HG_K, HG_K, HG_V, HG_V, NSA_Q, NSA_KVW, NSA_KVW, NSA_KVW, NSA_KVW, NSA_KVW, NSA_KVW,
              3 * NSA_HEADS, D_MODEL, D_MODEL)
    offs = np.concatenate([[0], np.cumsum(splits)])
    col = lambda k: w_in[l][:, offs[k]:offs[k + 1]]
    gate_w = col(11)
    zpad = jnp.zeros((D_MODEL, LANES - 3 * NSA_HPG), F32)
    gate_cols = []
    for g in range(NSA_KV):
        gate_cols += [gate_w[:, g * 3 * NSA_HPG:(g + 1) * 3 * NSA_HPG], zpad]
    w_main = jnp.concatenate([col(12), col(13), col(0), col(2), col(3),
                              col(7), col(8), col(9), col(10), col(4)] + gate_cols, axis=1).astype(BF16)
    w_cmp = jnp.concatenate([col(1), col(5), col(6)], axis=1).astype(BF16)

    lb = lower_bounds[l]
    lbp = jnp.stack([jnp.log(lb), jnp.log1p(-lb), 1.0 - lb])

    w1 = cmp_w1[l].reshape(2, 2, CMP_STRIDE, NSA_DH, CMP_HID)
    w1c = jnp.zeros((2, CMP_STRIDE, NSA_KV, NSA_DH, NSA_KV, 2, CMP_HID), F32)
    for g in range(NSA_KV):
        w1c = w1c.at[:, :, g, :, g, :, :].set(w1.transpose(0, 2, 3, 1, 4))
    w1c = w1c.reshape(2, CMP_STRIDE * NSA_KVW, NSA_KV * 2 * CMP_HID).astype(BF16)
    pos8 = jnp.zeros((2, 8, CMP_BLOCK * NSA_DH), F32).at[:, 0, :].set(cmp_pos[l].reshape(2, -1))
    w2p = jnp.zeros((2, CMP_HID, LANES), F32)
    w2p = w2p.at[0, :, :NSA_DH].set(cmp_w2[l, 0]).at[1, :NSA_DH, :].set(cmp_w2[l, 1].T).astype(BF16)
    qk = nsa_qk_gain[l]
    z64 = jnp.zeros((NSA_DH,), F32)
    return dict(
        norm_mix=norm_mix[l][None, :], w_main=w_main, w_cmp=w_cmp, lbp=lbp,
        hg_gain=hg_out_gain[l][None, :],
        gsel=jnp.concatenate([qk[2], qk[2]])[None, :], gwin=jnp.concatenate([qk[3], qk[3]])[None, :],
        gcmp=jnp.concatenate([qk[1], z64])[None, :],
        qgain=jnp.broadcast_to((qk[0] * (NSA_DH ** -0.5 * LOG2E))[:, None], (NSA_DH, Q_BLOCK)),
        pos8=pos8, w1=cmp_w1[l].astype(BF16), w1c=w1c, b1=cmp_b1[l][:, None, :], w2p=w2p,
        wa=w_branch_a[l].astype(BF16), wb=w_branch_b[l].astype(BF16), wo=w_out[l].astype(BF16),
        norm_mlp=norm_mlp[l][None, :], wu=w_up[l].astype(BF16), wd=w_down[l].astype(BF16))


def kernel(x, rel_bias, hg_lb_logits, norm_mix, w_in, hg_out_gain, nsa_qk_gain, cmp_pos, cmp_w1, cmp_b1, cmp_w2, w_branch_a, w_branch_b, w_out, norm_mlp, w_up, w_down):
    B, S, D = x.shape
    T = B * S
    depth = w_in.shape[0]
    assert D == D_MODEL and S % SEL_CHUNK == 0 and S >= WINDOW and S // SEL_BLOCK <= SEL_ROWS
    assert S % Q_BLOCK == 0 and Q_BLOCK % WIN_KB == 0 and WINDOW % WIN_KB == 0
    assert FORCE_BONUS > NSA_HPG and SEL_TOPK >= 3
    lb_cum = jnp.cumsum(jax.nn.softmax(hg_lb_logits.astype(F32), axis=0), axis=0)
    lower_bounds = lb_cum - lb_cum[0:1]

    dsat, strip_len = _sel_table_geometry()
    tabc, tabw, tabs = _bias_tables(rel_bias, S, dsat, strip_len)
    ovl = _overlap_matrix(S)
    nc = S // CMP_STRIDE
    tm = min(1024, T)

    x2 = x.reshape(T, D)
    for l in range(depth):
        p = _layer_params(l, lower_bounds, norm_mix, w_in, hg_out_gain, nsa_qk_gain, cmp_pos, cmp_w1,
                          cmp_b1, cmp_w2, w_branch_a, w_branch_b, w_out, norm_mlp, w_up, w_down)
        proj, z, kc, vc = _rms_proj(x2, p["norm_mix"], p["w_main"], p["w_cmp"], tm, P_TILE)
        proj3 = proj.reshape(B, S, P_COLS)
        kc2 = kc.reshape(B, nc, CMP_STRIDE * NSA_KVW)
        vc2 = vc.reshape(B, nc, CMP_STRIDE * NSA_KVW)
        o_a = _hgrn(proj3, z.reshape(B, S, HG_K), p["lbp"], p["hg_gain"])
        prep = _nsa_prep(proj3, kc2, vc2, p["gsel"], p["gwin"], p["gcmp"], p["qgain"], p["pos8"], p["w1"],
                         p["w1c"], p["b1"], p["w2p"])
        o_b = _nsa_attn(proj3, prep, tabc, tabw, tabs, ovl, dsat)
        x2 = _merge(x2, o_a.reshape(T, HG_V), o_b.reshape(T, NSA_Q), proj, p["wa"], p["wb"], p["wo"],
                    min(1024, T))
        x2 = _mlp(x2, p["norm_mlp"], p["wu"], p["wd"], tm, 1024)
    return x2.reshape(B, S, D)
```

```python
import functools
import math

import numpy as np
import jax
import jax.numpy as jnp
from jax import lax
from jax.experimental import pallas as pl
from jax.experimental.pallas import tpu as pltpu

F32 = jnp.float32
BF16 = jnp.bfloat16

D_MODEL = 1024
HG_HEADS = 4
HG_DK = 128
HG_DV = 128
HG_K = HG_HEADS * HG_DK
HG_V = HG_HEADS * HG_DV
NSA_HEADS = 8
NSA_KV = 2
NSA_DH = 64
NSA_HPG = NSA_HEADS // NSA_KV
NSA_Q = NSA_HEADS * NSA_DH
NSA_KVW = NSA_KV * NSA_DH
CMP_BLOCK = 32
CMP_STRIDE = 16
CMP_HID = 128
SEL_BLOCK = 64
SEL_TOPK = 8
WINDOW = 512
FORCE_BONUS = 1000.0
REL_BUCKETS = 32
REL_MAX_DIST = 1024
D_FF = 4 * D_MODEL
EPS = 1e-6
NEG_BIG = -1e30
LOG2E = math.log2(math.e)

LANES = 128
VMEM_LIMIT = 56 * 1024 * 1024

P_GAB = 0
P_HG = P_GAB + 2 * D_MODEL
P_KV = P_HG + 3 * HG_K
P_NQ = P_KV + 4 * NSA_KVW
P_GATE = P_NQ + NSA_Q
P_COLS = P_GATE + NSA_KV * LANES
P_TILE = P_COLS
ROW_BLOCKS = 4

HG_CHUNK = 128
HG_TS = 2048
Q_BLOCK = 256
WIN_KB = 128
SEL_CHUNK = 512
WIN_BLOCKS = (WINDOW + Q_BLOCK) // WIN_KB
WIN_SUB = WINDOW // WIN_KB + 1
SEL_SHIFT = int(math.log2(SEL_BLOCK))
SEL_ROWS = LANES - NSA_DH
V_ROWS = NSA_DH + 16


def _cparams(sem):
    return pltpu.CompilerParams(dimension_semantics=sem, vmem_limit_bytes=VMEM_LIMIT)


def _rms_rows(x, gain):
    ms = jnp.mean(x * x, axis=-1, keepdims=True)
    return x * lax.rsqrt(ms + EPS) * gain


def _proj_kernel(x_ref, g_ref, w_ref, wx_ref, o_ref, z_ref, kc_ref, vc_ref, h_ref, ck_ref, cv_ref):
    j = pl.program_id(1)
    rb = x_ref.shape[0] // ROW_BLOCKS

    @pl.when(j == 0)
    def _():
        for r in range(ROW_BLOCKS):
            rows = slice(r * rb, (r + 1) * rb)
            h = _rms_rows(x_ref[rows, :], g_ref[...]).astype(BF16)
            h_ref[rows, :] = h
            o_ref[rows, :] = jnp.dot(h, w_ref[...], preferred_element_type=F32).astype(BF16)
            c = jnp.dot(h, wx_ref[...], preferred_element_type=F32)
            z_ref[rows, :] = c[:, :HG_K]
            ck_ref[rows, :] = c[:, HG_K:HG_K + NSA_KVW]
            cv_ref[rows, :] = c[:, HG_K + NSA_KVW:]
        n16 = ck_ref.shape[0] // CMP_STRIDE
        for t in range(CMP_STRIDE):
            lanes = slice(t * NSA_KVW, (t + 1) * NSA_KVW)
            kc_ref[:, lanes] = ck_ref[pl.ds(t, n16, stride=CMP_STRIDE), :].astype(BF16)
            vc_ref[:, lanes] = cv_ref[pl.ds(t, n16, stride=CMP_STRIDE), :].astype(BF16)

    @pl.when(j != 0)
    def _():
        o_ref[...] = jnp.dot(h_ref[...], w_ref[...], preferred_element_type=F32).astype(BF16)


def _rms_proj(x2, gain, w, wx, tm, tn):
    T, D = x2.shape
    N = w.shape[1]
    cmp_sds = jax.ShapeDtypeStruct((T // CMP_STRIDE, CMP_STRIDE * NSA_KVW), BF16)
    return pl.pallas_call(
        _proj_kernel,
        grid=(T // tm, N // tn),
        in_specs=[
            pl.BlockSpec((tm, D), lambda i, j: (i, 0)),
            pl.BlockSpec((1, D), lambda i, j: (0, 0)),
            pl.BlockSpec((D, tn), lambda i, j: (0, j), pipeline_mode=pl.Buffered(1)),
            pl.BlockSpec(wx.shape, lambda i, j: (0, 0), pipeline_mode=pl.Buffered(1)),
        ],
        out_specs=[
            pl.BlockSpec((tm, tn), lambda i, j: (i, j)),
            pl.BlockSpec((tm, HG_K), lambda i, j: (i, 0)),
            pl.BlockSpec((tm // CMP_STRIDE, CMP_STRIDE * NSA_KVW), lambda i, j: (i, 0)),
            pl.BlockSpec((tm // CMP_STRIDE, CMP_STRIDE * NSA_KVW), lambda i, j: (i, 0)),
        ],
        out_shape=[jax.ShapeDtypeStruct((T, N), BF16), jax.ShapeDtypeStruct((T, HG_K), F32),
                   cmp_sds, cmp_sds],
        scratch_shapes=[pltpu.VMEM((tm, D), BF16), pltpu.VMEM((tm, NSA_KVW), F32),
                        pltpu.VMEM((tm, NSA_KVW), F32)],
        compiler_params=_cparams(("parallel", "arbitrary")),
        name="rms_proj",
    )(x2, gain, w, wx)


def _mlp_kernel(x_ref, g_ref, wu_ref, wd_ref, o_ref, h_ref, *, tf):
    rb = x_ref.shape[0] // ROW_BLOCKS
    n_f = wu_ref.shape[1] // tf

    def up_down(h, c):
        u = jnp.dot(h, wu_ref[:, c * tf:(c + 1) * tf], preferred_element_type=F32)
        u = jnp.square(jnp.maximum(u, 0.0)).astype(BF16)
        return jnp.dot(u, wd_ref[c * tf:(c + 1) * tf, :], preferred_element_type=F32)

    for r in range(ROW_BLOCKS):
        rows = slice(r * rb, (r + 1) * rb)
        x = x_ref[rows, :]
        h = _rms_rows(x, g_ref[...]).astype(BF16)
        h_ref[rows, :] = h
        o_ref[rows, :] = x + up_down(h, 0)
    for c in range(1, n_f):
        o_ref[...] += up_down(h_ref[...], c)


def _mlp(x2, gain, wu, wd, tm, tf):
    T, D = x2.shape
    FF = wu.shape[1]
    return pl.pallas_call(
        functools.partial(_mlp_kernel, tf=tf),
        grid=(T // tm,),
        in_specs=[
            pl.BlockSpec((tm, D), lambda i: (i, 0)),
            pl.BlockSpec((1, D), lambda i: (0, 0)),
            pl.BlockSpec((D, FF), lambda i: (0, 0), pipeline_mode=pl.Buffered(1)),
            pl.BlockSpec((FF, D), lambda i: (0, 0), pipeline_mode=pl.Buffered(1)),
        ],
        out_specs=pl.BlockSpec((tm, D), lambda i: (i, 0)),
        out_shape=jax.ShapeDtypeStruct((T, D), F32),
        scratch_shapes=[pltpu.VMEM((tm, D), BF16)],
        compiler_params=_cparams(("parallel",)),
        name="mlp",
    )(x2, gain, wu, wd)


def _merge_kernel(x_ref, oa_ref, ob_ref, ga_ref, gb_ref, wa_ref, wb_ref, wo_ref, o_ref):
    ya = jnp.dot(oa_ref[...], wa_ref[...], preferred_element_type=F32)
    yb = jnp.dot(ob_ref[...], wb_ref[...], preferred_element_type=F32)
    mixed = (jax.nn.sigmoid(ga_ref[...].astype(F32)) * ya
             + jax.nn.sigmoid(gb_ref[...].astype(F32)) * yb)
    o_ref[...] = x_ref[...] + jnp.dot(mixed.astype(BF16), wo_ref[...], preferred_element_type=F32)


def _merge(x2, oa, ob, proj, wa, wb, wo, tm):
    T, D = x2.shape
    ga_blk = P_GAB // D
    return pl.pallas_call(
        _merge_kernel,
        grid=(T // tm,),
        in_specs=[
            pl.BlockSpec((tm, D), lambda i: (i, 0)),
            pl.BlockSpec((tm, HG_V), lambda i: (i, 0)),
            pl.BlockSpec((tm, NSA_Q), lambda i: (i, 0)),
            pl.BlockSpec((tm, D), lambda i: (i, ga_blk)),
            pl.BlockSpec((tm, D), lambda i: (i, ga_blk + 1)),
            pl.BlockSpec((HG_V, D), lambda i: (0, 0)),
            pl.BlockSpec((NSA_Q, D), lambda i: (0, 0)),
            pl.BlockSpec((D, D), lambda i: (0, 0)),
        ],
        out_specs=pl.BlockSpec((tm, D), lambda i: (i, 0)),
        out_shape=jax.ShapeDtypeStruct((T, D), F32),
        compiler_params=_cparams(("parallel",)),
        name="merge",
    )(x2, oa, ob, proj, proj, wa, wb, wo)


def _hgrn_consts(C):
    L = int(math.log2(C))
    idx = np.arange(C)
    mats = [(idx[None, :] <= idx[:, None]),
            (idx[None, :] > idx[:, None])]
    bmask = []
    for l in range(L):
        m = 1 << l
        r = (idx & ~(2 * m - 1)) + m
        lo = np.minimum(idx, r)[:, None]
        hi = np.maximum(idx, r)[:, None]
        mats.append((idx[None, :] > lo) & (idx[None, :] <= hi))
        blk = idx >> (l + 1)
        bmask.append(blk[:, None] == blk[None, :])
    bmask.append(idx[:, None] == idx[None, :])
    seg = np.concatenate(mats, axis=0).astype(np.float32)
    seg = np.concatenate([seg, seg], axis=1)
    return seg, np.stack(bmask).astype(np.float32), L


def _hgrn_kernel(q_ref, f_ref, i_ref, g_ref, lbp_ref, gain_ref, seg_ref, bm_ref, o_ref, st_ref,
                 k0_ref, sg0_ref, k1_ref, sg1_ref, *, C, L, nchunk):
    @pl.when(pl.program_id(1) == 0)
    def _():
        st_ref[...] = jnp.zeros_like(st_ref)

    rowi = lax.broadcasted_iota(jnp.int32, (C, HG_DK), 0)
    nt = (((1,), (1,)), ((), ()))
    tn = (((0,), (0,)), ((), ()))

    H = range(HG_HEADS)
    css = [slice(h * HG_DK, (h + 1) * HG_DK) for h in H]

    def gates(c, k_ref, sg_ref):
        r0 = pl.multiple_of(c * C, C)
        z = f_ref[0, pl.ds(r0, C), :]
        ez = jnp.exp(-jnp.abs(z))
        y = lbp_ref[1:2, :] + (jnp.minimum(z, 0.0) - jnp.log(1.0 + ez))
        log_lb = lbp_ref[0:1, :]
        logf = (jnp.maximum(log_lb, y) + jnp.log(1.0 + jnp.exp(-jnp.abs(log_lb - y)))) * LOG2E
        k_ref[...] = lbp_ref[2:3, :] * (jnp.where(z >= 0.0, ez, 1.0) / (1.0 + ez))
        g_hi = logf.astype(BF16)
        g_lo = (logf - g_hi.astype(F32)).astype(BF16)
        sg_ref[...] = jnp.dot(seg_ref[...], jnp.concatenate([g_hi, g_lo], axis=0),
                              preferred_element_type=F32)

    def mix(c, k_ref, sg_ref):
        r0 = pl.multiple_of(c * C, C)
        qbs = [q_ref[0, pl.ds(r0, C), cs] for cs in css]
        qs = [qb.astype(F32) for qb in qbs]
        ks = [k_ref[:, cs] for cs in css]
        ss = [lax.dot_general(qbs[h], ks[h].astype(BF16), nt, preferred_element_type=F32) * bm_ref[L]
              for h in H]
        for l in range(L):
            second = (rowi & (1 << l)) != 0
            for h in H:
                e = jnp.exp2(sg_ref[(2 + l) * C:(3 + l) * C, css[h]])
                ql = jnp.where(second, qs[h] * e, 0.0).astype(BF16)
                kl = jnp.where(second, 0.0, ks[h] * e).astype(BF16)
                ss[h] = ss[h] + lax.dot_general(ql, kl, nt, preferred_element_type=F32) * bm_ref[l]
        return r0, qs, ks, ss

    def mix_out(scores, k_ref, sg_ref):
        r0, qs, ks, ss = scores
        vbs = [i_ref[0, pl.ds(r0, C), cs] for cs in css]
        bs = [sg_ref[0:C, cs] for cs in css]
        sts = [st_ref[h] for h in H]
        outs = []
        for h in H:
            qd = (qs[h] * jnp.exp2(bs[h])).astype(BF16)
            o = lax.dot_general(qd, sts[h].astype(BF16), nt, preferred_element_type=F32)
            outs.append(o + jnp.dot(ss[h].astype(BF16), vbs[h], preferred_element_type=F32))
        for h in H:
            kd = (ks[h] * jnp.exp2(sg_ref[C:2 * C, css[h]])).astype(BF16)
            st_ref[h] = (sts[h] * jnp.exp2(bs[h][C - 1:C, :])
                         + lax.dot_general(vbs[h], kd, tn, preferred_element_type=F32))
        for h in H:
            gg = g_ref[0, pl.ds(r0, C), css[h]].astype(F32)
            o = _rms_rows(outs[h], gain_ref[...]) * (gg * jax.nn.sigmoid(gg))
            o_ref[0, pl.ds(r0, C), css[h]] = o.astype(BF16)

    gates(0, k0_ref, sg0_ref)

    def pair(pr, carry):
        c = 2 * pr
        sc0 = mix(c, k0_ref, sg0_ref)
        gates(c + 1, k1_ref, sg1_ref)
        mix_out(sc0, k0_ref, sg0_ref)
        sc1 = mix(c + 1, k1_ref, sg1_ref)
        gates(jnp.minimum(c + 2, nchunk - 1), k0_ref, sg0_ref)
        mix_out(sc1, k1_ref, sg1_ref)
        return carry

    lax.fori_loop(0, nchunk // 2, pair, 0)


def _hgrn(proj3, z3, lbp, gain, C=HG_CHUNK, ts=HG_TS):
    B, S, _ = proj3.shape
    ts = min(ts, S)
    assert (ts // C) % 2 == 0
    seg, bm, L = _hgrn_consts(C)
    blk = lambda k: pl.BlockSpec((1, ts, HG_K), lambda b, s: (b, s, P_HG // HG_K + k))
    return pl.pallas_call(
        functools.partial(_hgrn_kernel, C=C, L=L, nchunk=ts // C),
        grid=(B, S // ts),
        in_specs=[
            blk(0), pl.BlockSpec((1, ts, HG_K), lambda b, s: (b, s, 0)), blk(1), blk(2),
            pl.BlockSpec((3, HG_K), lambda b, s: (0, 0)),
            pl.BlockSpec((1, HG_DV), lambda b, s: (0, 0)),
            pl.BlockSpec(seg.shape, lambda b, s: (0, 0)),
            pl.BlockSpec(bm.shape, lambda b, s: (0, 0, 0)),
        ],
        out_specs=pl.BlockSpec((1, ts, HG_V), lambda b, s: (b, s, 0)),
        out_shape=jax.ShapeDtypeStruct((B, S, HG_V), BF16),
        scratch_shapes=[pltpu.VMEM((HG_HEADS, HG_DV, HG_DK), F32)]
        + 2 * [pltpu.VMEM((C, HG_K), F32), pltpu.VMEM(((2 + L) * C, HG_K), F32)],
        compiler_params=_cparams(("parallel", "arbitrary")),
        name="hgrn2",
    )(proj3, z3, proj3, proj3, lbp, gain, jnp.asarray(seg, BF16), jnp.asarray(bm))


def _half_rms(x, gain2, lane):
    sq = x * x
    s_lo = jnp.sum(jnp.where(lane < NSA_DH, sq, 0.0), axis=-1, keepdims=True)
    s_hi = jnp.sum(jnp.where(lane < NSA_DH, 0.0, sq), axis=-1, keepdims=True)
    ms = jnp.where(lane < NSA_DH, s_lo, s_hi) * (1.0 / NSA_DH)
    return x * lax.rsqrt(ms + EPS) * gain2


def _gelu_tanh(x):
    return 0.5 * x * (1.0 + jnp.tanh(math.sqrt(2.0 / math.pi) * (x + 0.044715 * (x * x * x))))


def _prep_kernel(kv_ref, nq_ref, kc_ref, vc_ref, gsel_ref, gwin_ref, gcmp_ref, qg_ref, pos_ref, w1_ref,
                 w1c_ref, b1_ref, w2_ref, ksa_ref, vst_ref, kwa_ref, vwt_ref, kca_ref, vct_ref, qt_ref,
                 *, S, rows):
    lane = lax.broadcasted_iota(jnp.int32, (rows, LANES), 1)
    rowi = lax.broadcasted_iota(jnp.int32, (rows, LANES), 0)
    ones_row = (lax.broadcasted_iota(jnp.int32, (V_ROWS - NSA_DH, rows), 0) == 0).astype(F32)
    low = lane < NSA_DH

    def body(i, carry):
        r0 = pl.multiple_of(i * rows, rows)
        blk_hot = jnp.where(lane - NSA_DH == ((r0 + rowi) >> SEL_SHIFT), 1.0, 0.0)
        for src, gain_ref, aug, k_dst, v_dst, vrows in (
                (0, gsel_ref, blk_hot, ksa_ref, vst_ref, rows),
                (2, gwin_ref, jnp.zeros((rows, LANES), F32), kwa_ref, vwt_ref, WIN_KB)):
            kk = kv_ref[0, pl.ds(r0, rows), src * LANES:(src + 1) * LANES].astype(F32)
            vv = kv_ref[0, pl.ds(r0, rows), (src + 1) * LANES:(src + 2) * LANES].astype(F32)
            kn = _half_rms(kk, gain_ref[...], lane)
            k_dst[0, 0, pl.ds(r0, rows), :] = jnp.where(low, kn, aug).astype(BF16)
            k_dst[0, 1, pl.ds(r0, rows), :] = jnp.where(low, pltpu.roll(kn, NSA_DH, 1), aug).astype(BF16)
            vt = vv.T
            for g in range(NSA_KV):
                vg = jnp.concatenate([vt[g * NSA_DH:(g + 1) * NSA_DH], ones_row], axis=0).astype(BF16)
                for u in range(rows // vrows):
                    v_dst[0, g, i * (rows // vrows) + u] = vg[:, u * vrows:(u + 1) * vrows]
        for u in range(rows // Q_BLOCK):
            xt = nq_ref[0, pl.ds(r0 + u * Q_BLOCK, Q_BLOCK), :].astype(F32).T
            for g in range(NSA_KV):
                cols = []
                for h in range(NSA_HPG):
                    xh = xt[(g * NSA_HPG + h) * NSA_DH:(g * NSA_HPG + h + 1) * NSA_DH]
                    ms = jnp.mean(xh * xh, axis=0, keepdims=True)
                    cols.append(xh * lax.rsqrt(ms + EPS) * qg_ref[...])
                qt_ref[0, g, i * (rows // Q_BLOCK) + u] = jnp.concatenate(cols, axis=1).astype(BF16)
        return carry

    lax.fori_loop(0, S // rows, body, 0)

    nc = S // CMP_STRIDE
    outs = []
    for j, src_ref in ((0, kc_ref), (1, vc_ref)):
        bias = jnp.dot(pos_ref[j].astype(BF16), w1_ref[j], preferred_element_type=F32)[0:1] + b1_ref[j]
        pj = jnp.dot(src_ref[0], w1c_ref[j], preferred_element_type=F32)
        per_g = []
        for g in range(NSA_KV):
            first = pj[:, (2 * g) * CMP_HID:(2 * g + 1) * CMP_HID]
            second = pj[:, (2 * g + 1) * CMP_HID:(2 * g + 2) * CMP_HID]
            hid = _gelu_tanh(first + pltpu.roll(second, nc - 1, 0) + bias).astype(BF16)
            if j == 0:
                per_g.append(jnp.dot(hid, w2_ref[0], preferred_element_type=F32))
            else:
                per_g.append(lax.dot_general(w2_ref[1, :NSA_DH, :], hid, (((1,), (1,)), ((), ())),
                                             preferred_element_type=F32))
        outs.append(per_g)
    for g in range(NSA_KV):
        kc = outs[0][g]
        ms = jnp.sum(kc * kc, axis=-1, keepdims=True) * (1.0 / NSA_DH)
        kca_ref[0, g] = (kc * lax.rsqrt(ms + EPS) * gcmp_ref[...]).astype(BF16)
        vct_ref[0, g] = outs[1][g].astype(BF16)


def _nsa_prep(proj3, kc2, vc2, gsel, gwin, gcmp, qgain, pos8, w1, w1c, b1, w2p):
    B, S, _ = proj3.shape
    nc = S // CMP_STRIDE
    rows = SEL_CHUNK
    full = lambda shp: pl.BlockSpec(shp, lambda b: (0,) * len(shp))
    whole = lambda shp: pl.BlockSpec((1,) + shp, lambda b: (b,) + (0,) * len(shp))
    shapes = [(NSA_KV, S, LANES), (NSA_KV, S // SEL_CHUNK, V_ROWS, SEL_CHUNK),
              (NSA_KV, S, LANES), (NSA_KV, S // WIN_KB, V_ROWS, WIN_KB),
              (NSA_KV, nc, LANES), (NSA_KV, NSA_DH, nc),
              (NSA_KV, S // Q_BLOCK, NSA_DH, NSA_HPG * Q_BLOCK)]
    return pl.pallas_call(
        functools.partial(_prep_kernel, S=S, rows=rows),
        grid=(B,),
        in_specs=[
            pl.BlockSpec((1, S, 4 * NSA_KVW), lambda b: (b, 0, P_KV // (4 * NSA_KVW))),
            pl.BlockSpec((1, S, NSA_Q), lambda b: (b, 0, P_NQ // NSA_Q)),
            pl.BlockSpec((1, nc, CMP_STRIDE * NSA_KVW), lambda b: (b, 0, 0)),
            pl.BlockSpec((1, nc, CMP_STRIDE * NSA_KVW), lambda b: (b, 0, 0)),
            full((1, LANES)), full((1, LANES)), full((1, LANES)), full(qgain.shape),
            full(pos8.shape), full(w1.shape), full(w1c.shape), full(b1.shape), full(w2p.shape),
        ],
        out_specs=[whole(s) for s in shapes],
        out_shape=[jax.ShapeDtypeStruct((B,) + s, BF16) for s in shapes],
        compiler_params=_cparams(("parallel",)),
        name="nsa_prep",
    )(proj3, proj3, kc2, vc2, gsel, gwin, gcmp, qgain, pos8, w1, w1c, b1, w2p)


def _sel_table_geometry():
    last_start = (REL_BUCKETS // 2) * (REL_MAX_DIST / (REL_BUCKETS // 2)) ** (
        (REL_BUCKETS - REL_BUCKETS // 2 - 1) / (REL_BUCKETS - REL_BUCKETS // 2))
    dsat = int(math.ceil((last_start + 16 + SEL_CHUNK) / LANES)) * LANES
    return dsat, dsat + SEL_CHUNK


def _attn_kernel(q_ref, gate_ref, kca_ref, vct_ref, ksa_ref, vst_ref, kwa_ref, vwt_ref, tabc_ref,
                 tabw_ref, tabs_ref, ovl_ref, o_ref, m_ref, acc_ref, sa_ref, sb_ref, sw_ref,
                 *, nsel, dsat):
    qi = pl.program_id(2)
    QB = Q_BLOCK
    R = NSA_HPG * QB
    DH = NSA_DH

    qt = q_ref[0, 0, 0]
    qta = jnp.concatenate([qt, jnp.zeros_like(qt)], axis=0)

    n_sub = QB // WIN_KB
    vts, kws, kvalid = [], [], []
    for r in range(WIN_BLOCKS):
        kb = qi * n_sub - WINDOW // WIN_KB + r
        kbc = jnp.maximum(kb, 0)
        kws.append(kwa_ref[0, 0, pl.ds(pl.multiple_of(kbc * WIN_KB, WIN_KB), WIN_KB), :])
        vts.append(vwt_ref[0, 0, kbc])
        kvalid.append(kb >= 0)
    for u in range(n_sub):
        q_u = jnp.concatenate([qta[:, h * QB + u * WIN_KB:h * QB + (u + 1) * WIN_KB] for h in range(NSA_HPG)],
                              axis=1)
        for r in range(WIN_SUB):
            row = (u * WIN_SUB + r) * WIN_KB
            sw_ref[row:row + WIN_KB, :] = jnp.dot(kws[u + r], q_u, preferred_element_type=F32)

    def hcols(h):
        return slice(h * QB, (h + 1) * QB)

    def scores(c, q_aug, h=None):
        k0 = pl.multiple_of(c * SEL_CHUNK, SEL_CHUNK)
        delta = qi * QB - c * SEL_CHUNK
        start = pl.multiple_of(dsat - jnp.minimum(delta, dsat), LANES)
        cs = slice(None) if h is None else hcols(h)
        return (jnp.dot(ksa_ref[0, 0, pl.ds(k0, SEL_CHUNK), :], q_aug[:, cs], preferred_element_type=F32)
                + tabs_ref[0, 0, pl.ds(start, SEL_CHUNK), cs])

    s = jnp.dot(kca_ref[0, 0], qta, preferred_element_type=F32) + tabc_ref[0, 0]
    mx = jnp.max(s, axis=0, keepdims=True)
    e = jnp.exp2(s - mx)
    inv = jnp.where(mx > 0.5 * NEG_BIG, 1.0 / jnp.sum(e, axis=0, keepdims=True), 0.0)
    p = e * inv
    o_c = jnp.dot(vct_ref[0, 0], p.astype(BF16), preferred_element_type=F32)

    psum = p[:, 0:QB] + p[:, QB:2 * QB] + p[:, 2 * QB:3 * QB] + p[:, 3 * QB:4 * QB]
    p_hi = psum.astype(BF16)
    p_lo = (psum - p_hi.astype(F32)).astype(BF16)
    imp = (jnp.dot(ovl_ref[...], p_hi, preferred_element_type=F32)
           + jnp.dot(ovl_ref[...], p_lo, preferred_element_type=F32))
    t = qi * QB + lax.broadcasted_iota(jnp.int32, (SEL_ROWS, QB), 1)
    jrow = lax.broadcasted_iota(jnp.int32, (SEL_ROWS, QB), 0)
    jcur = t >> SEL_SHIFT
    n_top = min(SEL_TOPK, nsel)
    forced = (jrow == 0) | (jrow == jcur) | (jrow == jcur - 1)
    work = jnp.where(forced, -1.0, jnp.where(jrow * SEL_BLOCK <= t, imp, -1.0))
    work = jnp.where(jrow < nsel, work, -5.0)
    jf = jrow.astype(F32)
    sel = jnp.where(forced, 1.0, 0.0)
    for _ in range(n_top - 3):
        best = jnp.max(work, axis=0, keepdims=True)
        first = jnp.min(jnp.where(work == best, jf, float(SEL_ROWS)), axis=0, keepdims=True)
        hit = jf == first
        sel = jnp.where(hit, 1.0, sel)
        work = jnp.where(hit, -3.0, work)
    sel = jnp.where(jcur < n_top, jnp.where(jrow < n_top, 1.0, 0.0), sel)
    selneg = jnp.concatenate([jnp.where(sel > 0.5, 0.0, NEG_BIG)] * NSA_HPG, axis=1)
    qts = jnp.concatenate([qt, selneg.astype(BF16)], axis=0)
    sa_ref[...] = scores(0, qts)

    o_wu = []
    for u in range(n_sub):
        tiles = []
        for r in range(WIN_SUB):
            row = (u * WIN_SUB + r) * WIN_KB
            slab = pl.multiple_of(jnp.where(kvalid[u + r], r, WIN_SUB) * WIN_KB, WIN_KB)
            tiles.append(sw_ref[row:row + WIN_KB, :] + tabw_ref[0, 0, pl.ds(slab, WIN_KB), :])
        sw = jnp.concatenate(tiles, axis=0)
        ew = jnp.exp2(sw - jnp.max(sw, axis=0, keepdims=True)).astype(BF16)
        o_wu.append(jnp.dot(jnp.concatenate(vts[u:u + WIN_SUB], axis=1), ew, preferred_element_type=F32))
    o_w = jnp.concatenate([o_wu[u][:, h * WIN_KB:(h + 1) * WIN_KB]
                           for h in range(NSA_HPG) for u in range(n_sub)], axis=1)

    m_ref[...] = jnp.full_like(m_ref, NEG_BIG)
    acc_ref[...] = jnp.zeros_like(acc_ref)

    def consume(s_ref, c, heads=range(NSA_HPG)):
        vt = vst_ref[0, 0, c]
        for h in heads:
            cs = hcols(h)
            sc = s_ref[:, cs]
            m_old = m_ref[:, cs]
            m_new = jnp.maximum(m_old, jnp.max(sc, axis=0, keepdims=True))
            alpha = jnp.exp2(m_old - m_new)
            pe = jnp.exp2(sc - m_new).astype(BF16)
            acc_ref[:, cs] = alpha * acc_ref[:, cs] + jnp.dot(vt, pe, preferred_element_type=F32)
            m_ref[:, cs] = m_new

    def fill_and_consume(dst_ref, c_next, src_ref, c):
        dst_ref[:, hcols(0)] = scores(c_next, qts, 0)
        for h in range(NSA_HPG):
            if h + 1 < NSA_HPG:
                dst_ref[:, hcols(h + 1)] = scores(c_next, qts, h + 1)
            consume(src_ref, c, [h])

    n_chunks = (qi * QB + QB - 1) // SEL_CHUNK + 1
    n_pairs = (n_chunks - 1) // 2

    def sel_pair(pr, carry):
        c = 2 * pr
        fill_and_consume(sb_ref, c + 1, sa_ref, c)
        fill_and_consume(sa_ref, c + 2, sb_ref, c + 1)
        return carry

    lax.fori_loop(0, n_pairs, sel_pair, 0)

    @pl.when(n_chunks % 2 == 0)
    def _():
        fill_and_consume(sb_ref, n_chunks - 1, sa_ref, n_chunks - 2)
        consume(sb_ref, n_chunks - 1)

    @pl.when(n_chunks % 2 == 1)
    def _():
        consume(sa_ref, n_chunks - 1)

    acc = acc_ref[...]

    gt = jax.nn.sigmoid(gate_ref[0].astype(F32)).T
    o_s = acc[0:DH] / acc[DH:DH + 1]
    o_w = o_w[0:DH] / o_w[DH:DH + 1]
    heads = []
    for h in range(NSA_HPG):
        cs = slice(h * QB, (h + 1) * QB)
        heads.append(gt[3 * h:3 * h + 1] * o_c[:, cs] + gt[3 * h + 1:3 * h + 2] * o_s[:, cs]
                     + gt[3 * h + 2:3 * h + 3] * o_w[:, cs])
    o_ref[0] = jnp.concatenate(heads, axis=0).T.astype(BF16)


def _nsa_attn(proj3, prep, tabc, tabw, tabs, ovl, dsat):
    B, S, _ = proj3.shape
    ksa, vst, kwa, vwt, kca, vct, qt = prep
    nq = S // Q_BLOCK
    nc = S // CMP_STRIDE
    nsel = S // SEL_BLOCK
    R = NSA_HPG * Q_BLOCK
    gw = NSA_HPG * NSA_DH
    per_bg = lambda a: pl.BlockSpec((1, 1) + a.shape[2:], lambda b, g, i: (b, g) + (0,) * (a.ndim - 2))
    return pl.pallas_call(
        functools.partial(_attn_kernel, nsel=nsel, dsat=dsat),
        grid=(B, NSA_KV, nq),
        in_specs=[
            pl.BlockSpec((1, 1, 1, NSA_DH, R), lambda b, g, i: (b, g, i, 0, 0)),
            pl.BlockSpec((1, Q_BLOCK, LANES), lambda b, g, i: (b, i, P_GATE // LANES + g)),
            per_bg(kca), per_bg(vct), per_bg(ksa), per_bg(vst), per_bg(kwa), per_bg(vwt),
            pl.BlockSpec((1, 1, nc, R), lambda b, g, i: (g, i, 0, 0)),
            pl.BlockSpec((1, 1) + tabw.shape[2:], lambda b, g, i: (g, 0, 0, 0)),
            pl.BlockSpec((1, 1) + tabs.shape[2:], lambda b, g, i: (g, 0, 0, 0)),
            pl.BlockSpec(ovl.shape, lambda b, g, i: (0, 0)),
        ],
        out_specs=pl.BlockSpec((1, Q_BLOCK, gw), lambda b, g, i: (b, i, g)),
        out_shape=jax.ShapeDtypeStruct((B, S, NSA_Q), BF16),
        scratch_shapes=[pltpu.VMEM((1, R), F32), pltpu.VMEM((V_ROWS, R), F32),
                        pltpu.VMEM((SEL_CHUNK, R), F32), pltpu.VMEM((SEL_CHUNK, R), F32),
                        pltpu.VMEM((Q_BLOCK // WIN_KB * WIN_SUB * WIN_KB, NSA_HPG * WIN_KB), F32)],
        compiler_params=_cparams(("parallel", "parallel", "arbitrary")),
        name="nsa_attn",
    )(qt, proj3, kca, vct, ksa, vst, kwa, vwt, tabc, tabw, tabs, ovl)


def _rel_bucket(d):
    max_exact = REL_BUCKETS // 2
    d = jnp.maximum(d, 0)
    df = jnp.maximum(d, 1).astype(F32)
    large = max_exact + (jnp.log(df / max_exact) / math.log(REL_MAX_DIST / max_exact)
                         * (REL_BUCKETS - max_exact)).astype(jnp.int32)
    return jnp.where(d < max_exact, d, jnp.minimum(large, REL_BUCKETS - 1))


def _bias_kernel(rb_ref, bkt_ref, o_ref, *, rows, cols):
    g = pl.program_id(0)
    step = 32
    for r0 in range(0, rows, step):
        bkt = bkt_ref[0, r0:r0 + step, :]
        acc = [jnp.full(bkt.shape, NEG_BIG, F32) for _ in range(NSA_HPG)]
        for b in range(REL_BUCKETS):
            hit = bkt == b
            for h in range(NSA_HPG):
                acc[h] = jnp.where(hit, rb_ref[b, g * NSA_HPG + h] * LOG2E, acc[h])
        for h in range(NSA_HPG):
            o_ref[0, 0, r0:r0 + step, h * cols:(h + 1) * cols] = acc[h]


def _bias_table(rel_bias, bucket):
    nt, rows, cols = bucket.shape
    return pl.pallas_call(
        functools.partial(_bias_kernel, rows=rows, cols=cols),
        grid=(NSA_KV, nt),
        in_specs=[
            pl.BlockSpec(memory_space=pltpu.SMEM),
            pl.BlockSpec((1, rows, cols), lambda g, t: (t, 0, 0)),
        ],
        out_specs=pl.BlockSpec((1, 1, rows, NSA_HPG * cols), lambda g, t: (g, t, 0, 0)),
        out_shape=jax.ShapeDtypeStruct((NSA_KV, nt, rows, NSA_HPG * cols), F32),
        compiler_params=_cparams(("parallel", "parallel")),
        name="bias_table",
    )(rel_bias, bucket)


def _bias_tables(rel_bias, S, dsat, strip_len):
    nq = S // Q_BLOCK
    nc = S // CMP_STRIDE
    n_cmp = (S - CMP_BLOCK) // CMP_STRIDE + 1
    ar = lambda n: jnp.arange(n, dtype=jnp.int32)
    bucket = lambda d, valid: jnp.where(valid, _rel_bucket(d), -1)
    i = ar(Q_BLOCK)[None, None, :]
    c = ar(nc)[None, :, None]
    d_c = ar(nq)[:, None, None] * Q_BLOCK + i - (c * CMP_STRIDE + CMP_BLOCK - 1)
    tabc = _bias_table(rel_bias, bucket(d_c, (d_c >= 0) & (c < n_cmp)))
    jw = ar((WIN_SUB + 1) * WIN_KB)[None, :, None]
    d_w = ar(WIN_KB)[None, None, :] + WINDOW - jw
    tabw = _bias_table(rel_bias, bucket(d_w, (d_w >= 0) & (d_w < WINDOW) & (jw < WIN_SUB * WIN_KB)))
    d_s = i - ar(strip_len)[None, :, None] + dsat
    tabs = _bias_table(rel_bias, bucket(d_s, d_s >= 0))
    return tabc, tabw, tabs


def _overlap_matrix(S):
    nc = S // CMP_STRIDE
    nsel = S // SEL_BLOCK
    n_cmp = (S - CMP_BLOCK) // CMP_STRIDE + 1
    c = np.arange(nc)[None, :]
    sel_start = (np.arange(SEL_ROWS) * SEL_BLOCK)[:, None]
    ovl = ((c * CMP_STRIDE < sel_start + SEL_BLOCK) & (c * CMP_STRIDE + CMP_BLOCK - 1 >= sel_start)
           & (c < n_cmp) & (np.arange(SEL_ROWS)[:, None] < nsel))
    return jnp.asarray(ovl, BF16)


def _layer_params(l, lower_bounds, norm_mix, w_in, hg_out_gain, nsa_qk_gain, cmp_pos, cmp_w1, cmp_b1,
                  cmp_w2, w_branch_a, w_branch_b, w_out, norm_mlp, w_up, w_down):
    splits = (HG_K, HG_K, HG_V, HG_V, NSA_Q, NSA_KVW, NSA_KVW, NSA_KVW, NSA_KVW, NSA_KVW, NSA_KVW,
              3 * NSA_HEADS, D_MODEL, D_MODEL)
    offs = np.concatenate([[0], np.cumsum(splits)])
    col = lambda k: w_in[l][:, offs[k]:offs[k + 1]]
    gate_w = col(11)
    zpad = jnp.zeros((D_MODEL, LANES - 3 * NSA_HPG), F32)
    gate_cols = []
    for g in range(NSA_KV):
        gate_cols += [gate_w[:, g * 3 * NSA_HPG:(g + 1) * 3 * NSA_HPG], zpad]
    w_main = jnp.concatenate([col(12), col(13), col(0), col(2), col(3),
                              col(7), col(8), col(9), col(10), col(4)] + gate_cols, axis=1).astype(BF16)
    w_cmp = jnp.concatenate([col(1), col(5), col(6)], axis=1).astype(BF16)

    lb = lower_bounds[l]
    lbp = jnp.stack([jnp.log(lb), jnp.log1p(-lb), 1.0 - lb])

    w1 = cmp_w1[l].reshape(2, 2, CMP_STRIDE, NSA_DH, CMP_HID)
    w1c = jnp.zeros((2, CMP_STRIDE, NSA_KV, NSA_DH, NSA_KV, 2, CMP_HID), F32)
    for g in range(NSA_KV):
        w1c = w1c.at[:, :, g, :, g, :, :].set(w1.transpose(0, 2, 3, 1, 4))
    w1c = w1c.reshape(2, CMP_STRIDE * NSA_KVW, NSA_KV * 2 * CMP_HID).astype(BF16)
    pos8 = jnp.zeros((2, 8, CMP_BLOCK * NSA_DH), F32).at[:, 0, :].set(cmp_pos[l].reshape(2, -1))
    w2p = jnp.zeros((2, CMP_HID, LANES), F32)
    w2p = w2p.at[0, :, :NSA_DH].set(cmp_w2[l, 0]).at[1, :NSA_DH, :].set(cmp_w2[l, 1].T).astype(BF16)
    qk = nsa_qk_gain[l]
    z64 = jnp.zeros((NSA_DH,), F32)
    return dict(
        norm_mix=norm_mix[l][None, :], w_main=w_main, w_cmp=w_cmp, lbp=lbp,
        hg_gain=hg_out_gain[l][None, :],
        gsel=jnp.concatenate([qk[2], qk[2]])[None, :], gwin=jnp.concatenate([qk[3], qk[3]])[None, :],
        gcmp=jnp.concatenate([qk[1], z64])[None, :],
        qgain=jnp.broadcast_to((qk[0] * (NSA_DH ** -0.5 * LOG2E))[:, None], (NSA_DH, Q_BLOCK)),
        pos8=pos8, w1=cmp_w1[l].astype(BF16), w1c=w1c, b1=cmp_b1[l][:, None, :], w2p=w2p,
        wa=w_branch_a[l].astype(BF16), wb=w_branch_b[l].astype(BF16), wo=w_out[l].astype(BF16),
        norm_mlp=norm_mlp[l][None, :], wu=w_up[l].astype(BF16), wd=w_down[l].astype(BF16))


def kernel(x, rel_bias, hg_lb_logits, norm_mix, w_in, hg_out_gain, nsa_qk_gain, cmp_pos, cmp_w1, cmp_b1, cmp_w2, w_branch_a, w_branch_b, w_out, norm_mlp, w_up, w_down):
    B, S, D = x.shape
    T = B * S
    depth = w_in.shape[0]
    assert D == D_MODEL and S % SEL_CHUNK == 0 and S >= WINDOW and S // SEL_BLOCK <= SEL_ROWS
    assert S % Q_BLOCK == 0 and Q_BLOCK % WIN_KB == 0 and WINDOW % WIN_KB == 0
    assert FORCE_BONUS > NSA_HPG and SEL_TOPK >= 3
    lb_cum = jnp.cumsum(jax.nn.softmax(hg_lb_logits.astype(F32), axis=0), axis=0)
    lower_bounds = lb_cum - lb_cum[0:1]

    dsat, strip_len = _sel_table_geometry()
    tabc, tabw, tabs = _bias_tables(rel_bias, S, dsat, strip_len)
    ovl = _overlap_matrix(S)
    nc = S // CMP_STRIDE
    tm = min(1024, T)

    x2 = x.reshape(T, D)
    for l in range(depth):
        p = _layer_params(l, lower_bounds, norm_mix, w_in, hg_out_gain, nsa_qk_gain, cmp_pos, cmp_w1,
                          cmp_b1, cmp_w2, w_branch_a, w_branch_b, w_out, norm_mlp, w_up, w_down)
        proj, z, kc, vc = _rms_proj(x2, p["norm_mix"], p["w_main"], p["w_cmp"], tm, P_TILE)
        proj3 = proj.reshape(B, S, P_COLS)
        kc2 = kc.reshape(B, nc, CMP_STRIDE * NSA_KVW)
        vc2 = vc.reshape(B, nc, CMP_STRIDE * NSA_KVW)
        o_a = _hgrn(proj3, z.reshape(B, S, HG_K), p["lbp"], p["hg_gain"])
        prep = _nsa_prep(proj3, kc2, vc2, p["gsel"], p["gwin"], p["gcmp"], p["qgain"], p["pos8"], p["w1"],
                         p["w1c"], p["b1"], p["w2p"])
        o_b = _nsa_attn(proj3, prep, tabc, tabw, tabs, ovl, dsat)
        x2 = _merge(x2, o_a.reshape(T, HG_V), o_b.reshape(T, NSA_Q), proj, p["wa"], p["wb"], p["wo"],
                    min(1024, T))
        x2 = _mlp(x2, p["norm_mlp"], p["wu"], p["wd"], tm, 1024)
    return x2.reshape(B, S, D)
```

```python
import functools
import math

import numpy as np
import jax
import jax.numpy as jnp
from jax import lax
from jax.experimental import pallas as pl
from jax.experimental.pallas import tpu as pltpu

F32 = jnp.float32
BF16 = jnp.bfloat16

D_MODEL = 1024
HG_HEADS = 4
HG_DK = 128
HG_DV = 128
HG_K = HG_HEADS * HG_DK
HG_V = HG_HEADS * HG_DV
NSA_HEADS = 8
NSA_KV = 2
NSA_DH = 64
NSA_HPG = NSA_HEADS // NSA_KV
NSA_Q = NSA_HEADS * NSA_DH
NSA_KVW = NSA_KV * NSA_DH
CMP_BLOCK = 32
CMP_STRIDE = 16
CMP_HID = 128
SEL_BLOCK = 64
SEL_TOPK = 8
WINDOW = 512
FORCE_BONUS = 1000.0
REL_BUCKETS = 32
REL_MAX_DIST = 1024
D_FF = 4 * D_MODEL
EPS = 1e-6
NEG_BIG = -1e30
LOG2E = math.log2(math.e)

LANES = 128
VMEM_LIMIT = 56 * 1024 * 1024

P_GAB = 0
P_HG = P_GAB + 2 * D_MODEL
P_KV = P_HG + 3 * HG_K
P_NQ = P_KV + 4 * NSA_KVW
P_GATE = P_NQ + NSA_Q
P_COLS = P_GATE + NSA_KV * LANES
P_TILE = P_COLS
ROW_BLOCKS = 4

HG_CHUNK = 128
HG_TS = 2048
Q_BLOCK = 256
WIN_KB = 128
SEL_CHUNK = 512
WIN_BLOCKS = (WINDOW + Q_BLOCK) // WIN_KB
WIN_SUB = WINDOW // WIN_KB + 1
SEL_SHIFT = int(math.log2(SEL_BLOCK))
SEL_ROWS = LANES - NSA_DH
V_ROWS = NSA_DH + 16


def _cparams(sem):
    return pltpu.CompilerParams(dimension_semantics=sem, vmem_limit_bytes=VMEM_LIMIT)


def _rms_rows(x, gain):
    ms = jnp.mean(x * x, axis=-1, keepdims=True)
    return x * lax.rsqrt(ms + EPS) * gain


def _proj_kernel(x_ref, g_ref, w_ref, wx_ref, o_ref, z_ref, kc_ref, vc_ref, h_ref, ck_ref, cv_ref):
    j = pl.program_id(1)
    rb = x_ref.shape[0] // ROW_BLOCKS

    @pl.when(j == 0)
    def _():
        for r in range(ROW_BLOCKS):
            rows = slice(r * rb, (r + 1) * rb)
            h = _rms_rows(x_ref[rows, :], g_ref[...]).astype(BF16)
            h_ref[rows, :] = h
            o_ref[rows, :] = jnp.dot(h, w_ref[...], preferred_element_type=F32).astype(BF16)
            c = jnp.dot(h, wx_ref[...], preferred_element_type=F32)
            z_ref[rows, :] = c[:, :HG_K]
            ck_ref[rows, :] = c[:, HG_K:HG_K + NSA_KVW]
            cv_ref[rows, :] = c[:, HG_K + NSA_KVW:]
        n16 = ck_ref.shape[0] // CMP_STRIDE
        for t in range(CMP_STRIDE):
            lanes = slice(t * NSA_KVW, (t + 1) * NSA_KVW)
            kc_ref[:, lanes] = ck_ref[pl.ds(t, n16, stride=CMP_STRIDE), :].astype(BF16)
            vc_ref[:, lanes] = cv_ref[pl.ds(t, n16, stride=CMP_STRIDE), :].astype(BF16)

    @pl.when(j != 0)
    def _():
        o_ref[...] = jnp.dot(h_ref[...], w_ref[...], preferred_element_type=F32).astype(BF16)


def _rms_proj(x2, gain, w, wx, tm, tn):
    T, D = x2.shape
    N = w.shape[1]
    cmp_sds = jax.ShapeDtypeStruct((T // CMP_STRIDE, CMP_STRIDE * NSA_KVW), BF16)
    return pl.pallas_call(
        _proj_kernel,
        grid=(T // tm, N // tn),
        in_specs=[
            pl.BlockSpec((tm, D), lambda i, j: (i, 0)),
            pl.BlockSpec((1, D), lambda i, j: (0, 0)),
            pl.BlockSpec((D, tn), lambda i, j: (0, j), pipeline_mode=pl.Buffered(1)),
            pl.BlockSpec(wx.shape, lambda i, j: (0, 0), pipeline_mode=pl.Buffered(1)),
        ],
        out_specs=[
            pl.BlockSpec((tm, tn), lambda i, j: (i, j)),
            pl.BlockSpec((tm, HG_K), lambda i, j: (i, 0)),
            pl.BlockSpec((tm // CMP_STRIDE, CMP_STRIDE * NSA_KVW), lambda i, j: (i, 0)),
            pl.BlockSpec((tm // CMP_STRIDE, CMP_STRIDE * NSA_KVW), lambda i, j: (i, 0)),
        ],
        out_shape=[jax.ShapeDtypeStruct((T, N), BF16), jax.ShapeDtypeStruct((T, HG_K), F32),
                   cmp_sds, cmp_sds],
        scratch_shapes=[pltpu.VMEM((tm, D), BF16), pltpu.VMEM((tm, NSA_KVW), F32),
                        pltpu.VMEM((tm, NSA_KVW), F32)],
        compiler_params=_cparams(("parallel", "arbitrary")),
        name="rms_proj",
    )(x2, gain, w, wx)


def _mlp_kernel(x_ref, g_ref, wu_ref, wd_ref, o_ref, h_ref, *, tf):
    rb = x_ref.shape[0] // ROW_BLOCKS
    n_f = wu_ref.shape[1] // tf

    def up_down(h, c):
        u = jnp.dot(h, wu_ref[:, c * tf:(c + 1) * tf], preferred_element_type=F32)
        u = jnp.square(jnp.maximum(u, 0.0)).astype(BF16)
        return jnp.dot(u, wd_ref[c * tf:(c + 1) * tf, :], preferred_element_type=F32)

    for r in range(ROW_BLOCKS):
        rows = slice(r * rb, (r + 1) * rb)
        x = x_ref[rows, :]
        h = _rms_rows(x, g_ref[...]).astype(BF16)
        h_ref[rows, :] = h
        o_ref[rows, :] = x + up_down(h, 0)
    for c in range(1, n_f):
        o_ref[...] += up_down(h_ref[...], c)


def _mlp(x2, gain, wu, wd, tm, tf):
    T, D = x2.shape
    FF = wu.shape[1]
    return pl.pallas_call(
        functools.partial(_mlp_kernel, tf=tf),
        grid=(T // tm,),
        in_specs=[
            pl.BlockSpec((tm, D), lambda i: (i, 0)),
            pl.BlockSpec((1, D), lambda i: (0, 0)),
            pl.BlockSpec((D, FF), lambda i: (0, 0), pipeline_mode=pl.Buffered(1)),
            pl.BlockSpec((FF, D), lambda i: (0, 0), pipeline_mode=pl.Buffered(1)),
        ],
        out_specs=pl.BlockSpec((tm, D), lambda i: (i, 0)),
        out_shape=jax.ShapeDtypeStruct((T, D), F32),
        scratch_shapes=[pltpu.VMEM((tm, D), BF16)],
        compiler_params=_cparams(("parallel",)),
        name="mlp",
    )(x2, gain, wu, wd)


def _merge_kernel(x_ref, oa_ref, ob_ref, ga_ref, gb_ref, wa_ref, wb_ref, wo_ref, o_ref):
    ya = jnp.dot(oa_ref[...], wa_ref[...], preferred_element_type=F32)
    yb = jnp.dot(ob_ref[...], wb_ref[...], preferred_element_type=F32)
    mixed = (jax.nn.sigmoid(ga_ref[...].astype(F32)) * ya
             + jax.nn.sigmoid(gb_ref[...].astype(F32)) * yb)
    o_ref[...] = x_ref[...] + jnp.dot(mixed.astype(BF16), wo_ref[...], preferred_element_type=F32)


def _merge(x2, oa, ob, proj, wa, wb, wo, tm):
    T, D = x2.shape
    ga_blk = P_GAB // D
    return pl.pallas_call(
        _merge_kernel,
        grid=(T // tm,),
        in_specs=[
            pl.BlockSpec((tm, D), lambda i: (i, 0)),
            pl.BlockSpec((tm, HG_V), lambda i: (i, 0)),
            pl.BlockSpec((tm, NSA_Q), lambda i: (i, 0)),
            pl.BlockSpec((tm, D), lambda i: (i, ga_blk)),
            pl.BlockSpec((tm, D), lambda i: (i, ga_blk + 1)),
            pl.BlockSpec((HG_V, D), lambda i: (0, 0), pipeline_mode=pl.Buffered(1)),
            pl.BlockSpec((NSA_Q, D), lambda i: (0, 0), pipeline_mode=pl.Buffered(1)),
            pl.BlockSpec((D, D), lambda i: (0, 0), pipeline_mode=pl.Buffered(1)),
        ],
        out_specs=pl.BlockSpec((tm, D), lambda i: (i, 0)),
        out_shape=jax.ShapeDtypeStruct((T, D), F32),
        compiler_params=_cparams(("parallel",)),
        name="merge",
    )(x2, oa, ob, proj, proj, wa, wb, wo)


def _hgrn_consts(C):
    L = int(math.log2(C))
    idx = np.arange(C)
    mats = [(idx[None, :] <= idx[:, None]),
            (idx[None, :] > idx[:, None])]
    bmask = []
    for l in range(L):
        m = 1 << l
        r = (idx & ~(2 * m - 1)) + m
        lo = np.minimum(idx, r)[:, None]
        hi = np.maximum(idx, r)[:, None]
        mats.append((idx[None, :] > lo) & (idx[None, :] <= hi))
        blk = idx >> (l + 1)
        bmask.append(blk[:, None] == blk[None, :])
    bmask.append(idx[:, None] == idx[None, :])
    seg = np.concatenate(mats, axis=0).astype(np.float32)
    seg = np.concatenate([seg, seg], axis=1)
    return seg, np.stack(bmask).astype(np.float32), L


def _hgrn_kernel(q_ref, f_ref, i_ref, g_ref, lbp_ref, gain_ref, seg_ref, bm_ref, o_ref, st_ref,
                 k0_ref, sg0_ref, k1_ref, sg1_ref, *, C, L, nchunk):
    @pl.when(pl.program_id(1) == 0)
    def _():
        st_ref[...] = jnp.zeros_like(st_ref)

    rowi = lax.broadcasted_iota(jnp.int32, (C, HG_DK), 0)
    nt = (((1,), (1,)), ((), ()))
    tn = (((0,), (0,)), ((), ()))

    H = range(HG_HEADS)
    css = [slice(h * HG_DK, (h + 1) * HG_DK) for h in H]

    def gates(c, k_ref, sg_ref):
        r0 = pl.multiple_of(c * C, C)
        z = f_ref[0, pl.ds(r0, C), :]
        ez = jnp.exp(-jnp.abs(z))
        y = lbp_ref[1:2, :] + (jnp.minimum(z, 0.0) - jnp.log(1.0 + ez))
        log_lb = lbp_ref[0:1, :]
        logf = (jnp.maximum(log_lb, y) + jnp.log(1.0 + jnp.exp(-jnp.abs(log_lb - y)))) * LOG2E
        k_ref[...] = lbp_ref[2:3, :] * (jnp.where(z >= 0.0, ez, 1.0) / (1.0 + ez))
        g_hi = logf.astype(BF16)
        g_lo = (logf - g_hi.astype(F32)).astype(BF16)
        sg_ref[...] = jnp.dot(seg_ref[...], jnp.concatenate([g_hi, g_lo], axis=0),
                              preferred_element_type=F32)

    def mix(c, k_ref, sg_ref):
        r0 = pl.multiple_of(c * C, C)
        qbs = [q_ref[0, pl.ds(r0, C), cs] for cs in css]
        qs = [qb.astype(F32) for qb in qbs]
        ks = [k_ref[:, cs] for cs in css]
        ss = [lax.dot_general(qbs[h], ks[h].astype(BF16), nt, preferred_element_type=F32) * bm_ref[L]
              for h in H]
        for l in range(L):
            second = (rowi & (1 << l)) != 0
            for h in H:
                e = jnp.exp2(sg_ref[(2 + l) * C:(3 + l) * C, css[h]])
                ql = jnp.where(second, qs[h] * e, 0.0).astype(BF16)
                kl = jnp.where(second, 0.0, ks[h] * e).astype(BF16)
                ss[h] = ss[h] + lax.dot_general(ql, kl, nt, preferred_element_type=F32) * bm_ref[l]
        return r0, qs, ks, ss

    def mix_out(scores, k_ref, sg_ref):
        r0, qs, ks, ss = scores
        vbs = [i_ref[0, pl.ds(r0, C), cs] for cs in css]
        bs = [sg_ref[0:C, cs] for cs in css]
        sts = [st_ref[h] for h in H]
        outs = []
        for h in H:
            qd = (qs[h] * jnp.exp2(bs[h])).astype(BF16)
            o = lax.dot_general(qd, sts[h].astype(BF16), nt, preferred_element_type=F32)
            outs.append(o + jnp.dot(ss[h].astype(BF16), vbs[h], preferred_element_type=F32))
        for h in H:
            kd = (ks[h] * jnp.exp2(sg_ref[C:2 * C, css[h]])).astype(BF16)
            st_ref[h] = (sts[h] * jnp.exp2(bs[h][C - 1:C, :])
                         + lax.dot_general(vbs[h], kd, tn, preferred_element_type=F32))
        for h in H:
            gg = g_ref[0, pl.ds(r0, C), css[h]].astype(F32)
            o = _rms_rows(outs[h], gain_ref[...]) * (gg * jax.nn.sigmoid(gg))
            o_ref[0, pl.ds(r0, C), css[h]] = o.astype(BF16)

    gates(0, k0_ref, sg0_ref)

    def pair(pr, carry):
        c = 2 * pr
        sc0 = mix(c, k0_ref, sg0_ref)
        gates(c + 1, k1_ref, sg1_ref)
        mix_out(sc0, k0_ref, sg0_ref)
        sc1 = mix(c + 1, k1_ref, sg1_ref)
        gates(jnp.minimum(c + 2, nchunk - 1), k0_ref, sg0_ref)
        mix_out(sc1, k1_ref, sg1_ref)
        return carry

    lax.fori_loop(0, nchunk // 2, pair, 0)


def _hgrn(proj3, z3, lbp, gain, C=HG_CHUNK, ts=HG_TS):
    B, S, _ = proj3.shape
    ts = min(ts, S)
    assert (ts // C) % 2 == 0
    seg, bm, L = _hgrn_consts(C)
    blk = lambda k: pl.BlockSpec((1, ts, HG_K), lambda b, s: (b, s, P_HG // HG_K + k))
    return pl.pallas_call(
        functools.partial(_hgrn_kernel, C=C, L=L, nchunk=ts // C),
        grid=(B, S // ts),
        in_specs=[
            blk(0), pl.BlockSpec((1, ts, HG_K), lambda b, s: (b, s, 0)), blk(1), blk(2),
            pl.BlockSpec((3, HG_K), lambda b, s: (0, 0)),
            pl.BlockSpec((1, HG_DV), lambda b, s: (0, 0)),
            pl.BlockSpec(seg.shape, lambda b, s: (0, 0)),
            pl.BlockSpec(bm.shape, lambda b, s: (0, 0, 0)),
        ],
        out_specs=pl.BlockSpec((1, ts, HG_V), lambda b, s: (b, s, 0)),
        out_shape=jax.ShapeDtypeStruct((B, S, HG_V), BF16),
        scratch_shapes=[pltpu.VMEM((HG_HEADS, HG_DV, HG_DK), F32)]
        + 2 * [pltpu.VMEM((C, HG_K), F32), pltpu.VMEM(((2 + L) * C, HG_K), F32)],
        compiler_params=_cparams(("parallel", "arbitrary")),
        name="hgrn2",
    )(proj3, z3, proj3, proj3, lbp, gain, jnp.asarray(seg, BF16), jnp.asarray(bm))


def _half_rms(x, gain2, lane):
    sq = x * x
    s_lo = jnp.sum(jnp.where(lane < NSA_DH, sq, 0.0), axis=-1, keepdims=True)
    s_hi = jnp.sum(jnp.where(lane < NSA_DH, 0.0, sq), axis=-1, keepdims=True)
    ms = jnp.where(lane < NSA_DH, s_lo, s_hi) * (1.0 / NSA_DH)
    return x * lax.rsqrt(ms + EPS) * gain2


def _gelu_tanh(x):
    return 0.5 * x * (1.0 + jnp.tanh(math.sqrt(2.0 / math.pi) * (x + 0.044715 * (x * x * x))))


def _prep_kernel(kv_ref, nq_ref, kc_ref, vc_ref, gsel_ref, gwin_ref, gcmp_ref, qg_ref, pos_ref, w1_ref,
                 w1c_ref, b1_ref, w2_ref, ksa_ref, vst_ref, kwa_ref, vwt_ref, kca_ref, vct_ref, qt_ref,
                 *, S, rows):
    lane = lax.broadcasted_iota(jnp.int32, (rows, LANES), 1)
    rowi = lax.broadcasted_iota(jnp.int32, (rows, LANES), 0)
    ones_row = (lax.broadcasted_iota(jnp.int32, (V_ROWS - NSA_DH, rows), 0) == 0).astype(F32)
    low = lane < NSA_DH

    def body(i, carry):
        r0 = pl.multiple_of(i * rows, rows)
        blk_hot = jnp.where(lane - NSA_DH == ((r0 + rowi) >> SEL_SHIFT), 1.0, 0.0)
        for src, gain_ref, aug, k_dst, v_dst, vrows in (
                (0, gsel_ref, blk_hot, ksa_ref, vst_ref, rows),
                (2, gwin_ref, jnp.zeros((rows, LANES), F32), kwa_ref, vwt_ref, WIN_KB)):
            kk = kv_ref[0, pl.ds(r0, rows), src * LANES:(src + 1) * LANES].astype(F32)
            vv = kv_ref[0, pl.ds(r0, rows), (src + 1) * LANES:(src + 2) * LANES].astype(F32)
            kn = _half_rms(kk, gain_ref[...], lane)
            k_dst[0, 0, pl.ds(r0, rows), :] = jnp.where(low, kn, aug).astype(BF16)
            k_dst[0, 1, pl.ds(r0, rows), :] = jnp.where(low, pltpu.roll(kn, NSA_DH, 1), aug).astype(BF16)
            vt = vv.T
            for g in range(NSA_KV):
                vg = jnp.concatenate([vt[g * NSA_DH:(g + 1) * NSA_DH], ones_row], axis=0).astype(BF16)
                for u in range(rows // vrows):
                    v_dst[0, g, i * (rows // vrows) + u] = vg[:, u * vrows:(u + 1) * vrows]
        for u in range(rows // Q_BLOCK):
            xt = nq_ref[0, pl.ds(r0 + u * Q_BLOCK, Q_BLOCK), :].astype(F32).T
            for g in range(NSA_KV):
                cols = []
                for h in range(NSA_HPG):
                    xh = xt[(g * NSA_HPG + h) * NSA_DH:(g * NSA_HPG + h + 1) * NSA_DH]
                    ms = jnp.mean(xh * xh, axis=0, keepdims=True)
                    cols.append(xh * lax.rsqrt(ms + EPS) * qg_ref[...])
                qt_ref[0, g, i * (rows // Q_BLOCK) + u] = jnp.concatenate(cols, axis=1).astype(BF16)
        return carry

    lax.fori_loop(0, S // rows, body, 0)

    nc = S // CMP_STRIDE
    outs = []
    for j, src_ref in ((0, kc_ref), (1, vc_ref)):
        bias = jnp.dot(pos_ref[j].astype(BF16), w1_ref[j], preferred_element_type=F32)[0:1] + b1_ref[j]
        pj = jnp.dot(src_ref[0], w1c_ref[j], preferred_element_type=F32)
        per_g = []
        for g in range(NSA_KV):
            first = pj[:, (2 * g) * CMP_HID:(2 * g + 1) * CMP_HID]
            second = pj[:, (2 * g + 1) * CMP_HID:(2 * g + 2) * CMP_HID]
            hid = _gelu_tanh(first + pltpu.roll(second, nc - 1, 0) + bias).astype(BF16)
            if j == 0:
                per_g.append(jnp.dot(hid, w2_ref[0], preferred_element_type=F32))
            else:
                per_g.append(lax.dot_general(w2_ref[1, :NSA_DH, :], hid, (((1,), (1,)), ((), ())),
                                             preferred_element_type=F32))
        outs.append(per_g)
    for g in range(NSA_KV):
        kc = outs[0][g]
        ms = jnp.sum(kc * kc, axis=-1, keepdims=True) * (1.0 / NSA_DH)
        kca_ref[0, g] = (kc * lax.rsqrt(ms + EPS) * gcmp_ref[...]).astype(BF16)
        vct_ref[0, g] = outs[1][g].astype(BF16)


def _nsa_prep(proj3, kc2, vc2, gsel, gwin, gcmp, qgain, pos8, w1, w1c, b1, w2p):
    B, S, _ = proj3.shape
    nc = S // CMP_STRIDE
    rows = SEL_CHUNK
    full = lambda shp: pl.BlockSpec(shp, lambda b: (0,) * len(shp))
    whole = lambda shp: pl.BlockSpec((1,) + shp, lambda b: (b,) + (0,) * len(shp))
    shapes = [(NSA_KV, S, LANES), (NSA_KV, S // SEL_CHUNK, V_ROWS, SEL_CHUNK),
              (NSA_KV, S, LANES), (NSA_KV, S // WIN_KB, V_ROWS, WIN_KB),
              (NSA_KV, nc, LANES), (NSA_KV, NSA_DH, nc),
              (NSA_KV, S // Q_BLOCK, NSA_DH, NSA_HPG * Q_BLOCK)]
    return pl.pallas_call(
        functools.partial(_prep_kernel, S=S, rows=rows),
        grid=(B,),
        in_specs=[
            pl.BlockSpec((1, S, 4 * NSA_KVW), lambda b: (b, 0, P_KV // (4 * NSA_KVW))),
            pl.BlockSpec((1, S, NSA_Q), lambda b: (b, 0, P_NQ // NSA_Q)),
            pl.BlockSpec((1, nc, CMP_STRIDE * NSA_KVW), lambda b: (b, 0, 0)),
            pl.BlockSpec((1, nc, CMP_STRIDE * NSA_KVW), lambda b: (b, 0, 0)),
            full((1, LANES)), full((1, LANES)), full((1, LANES)), full(qgain.shape),
            full(pos8.shape), full(w1.shape), full(w1c.shape), full(b1.shape), full(w2p.shape),
        ],
        out_specs=[whole(s) for s in shapes],
        out_shape=[jax.ShapeDtypeStruct((B,) + s, BF16) for s in shapes],
        compiler_params=_cparams(("parallel",)),
        name="nsa_prep",
    )(proj3, proj3, kc2, vc2, gsel, gwin, gcmp, qgain, pos8, w1, w1c, b1, w2p)


def _sel_table_geometry():
    last_start = (REL_BUCKETS // 2) * (REL_MAX_DIST / (REL_BUCKETS // 2)) ** (
        (REL_BUCKETS - REL_BUCKETS // 2 - 1) / (REL_BUCKETS - REL_BUCKETS // 2))
    dsat = int(math.ceil((last_start + 16 + SEL_CHUNK) / LANES)) * LANES
    return dsat, dsat + SEL_CHUNK


def _attn_kernel(q_ref, gate_ref, kca_ref, vct_ref, ksa_ref, vst_ref, kwa_ref, vwt_ref, tabc_ref,
                 tabw_ref, tabs_ref, ovl_ref, o_ref, m_ref, acc_ref, sa_ref, sb_ref, sw_ref,
                 *, nsel, dsat):
    qi = pl.program_id(2)
    QB = Q_BLOCK
    R = NSA_HPG * QB
    DH = NSA_DH

    qt = q_ref[0, 0, 0]
    qta = jnp.concatenate([qt, jnp.zeros_like(qt)], axis=0)

    n_sub = QB // WIN_KB
    vts, kws, kvalid = [], [], []
    for r in range(WIN_BLOCKS):
        kb = qi * n_sub - WINDOW // WIN_KB + r
        kbc = jnp.maximum(kb, 0)
        kws.append(kwa_ref[0, 0, pl.ds(pl.multiple_of(kbc * WIN_KB, WIN_KB), WIN_KB), :])
        vts.append(vwt_ref[0, 0, kbc])
        kvalid.append(kb >= 0)
    for u in range(n_sub):
        q_u = jnp.concatenate([qta[:, h * QB + u * WIN_KB:h * QB + (u + 1) * WIN_KB] for h in range(NSA_HPG)],
                              axis=1)
        for r in range(WIN_SUB):
            row = (u * WIN_SUB + r) * WIN_KB
            sw_ref[row:row + WIN_KB, :] = jnp.dot(kws[u + r], q_u, preferred_element_type=F32)

    def hcols(h):
        return slice(h * QB, (h + 1) * QB)

    def scores(c, q_aug, h=None):
        k0 = pl.multiple_of(c * SEL_CHUNK, SEL_CHUNK)
        delta = qi * QB - c * SEL_CHUNK
        start = pl.multiple_of(dsat - jnp.minimum(delta, dsat), LANES)
        cs = slice(None) if h is None else hcols(h)
        return (jnp.dot(ksa_ref[0, 0, pl.ds(k0, SEL_CHUNK), :], q_aug[:, cs], preferred_element_type=F32)
                + tabs_ref[0, 0, pl.ds(start, SEL_CHUNK), cs])

    s = jnp.dot(kca_ref[0, 0], qta, preferred_element_type=F32) + tabc_ref[0, 0]
    mx = jnp.max(s, axis=0, keepdims=True)
    e = jnp.exp2(s - mx)
    inv = jnp.where(mx > 0.5 * NEG_BIG, 1.0 / jnp.sum(e, axis=0, keepdims=True), 0.0)
    p = e * inv
    o_c = jnp.dot(vct_ref[0, 0], p.astype(BF16), preferred_element_type=F32)

    psum = p[:, 0:QB] + p[:, QB:2 * QB] + p[:, 2 * QB:3 * QB] + p[:, 3 * QB:4 * QB]
    p_hi = psum.astype(BF16)
    p_lo = (psum - p_hi.astype(F32)).astype(BF16)
    imp = (jnp.dot(ovl_ref[...], p_hi, preferred_element_type=F32)
           + jnp.dot(ovl_ref[...], p_lo, preferred_element_type=F32))
    t = qi * QB + lax.broadcasted_iota(jnp.int32, (SEL_ROWS, QB), 1)
    jrow = lax.broadcasted_iota(jnp.int32, (SEL_ROWS, QB), 0)
    jcur = t >> SEL_SHIFT
    n_top = min(SEL_TOPK, nsel)
    forced = (jrow == 0) | (jrow == jcur) | (jrow == jcur - 1)
    work = jnp.where(forced, -1.0, jnp.where(jrow * SEL_BLOCK <= t, imp, -1.0))
    work = jnp.where(jrow < nsel, work, -5.0)
    jf = jrow.astype(F32)
    sel = jnp.where(forced, 1.0, 0.0)
    for _ in range(n_top - 3):
        best = jnp.max(work, axis=0, keepdims=True)
        first = jnp.min(jnp.where(work == best, jf, float(SEL_ROWS)), axis=0, keepdims=True)
        hit = jf == first
        sel = jnp.where(hit, 1.0, sel)
        work = jnp.where(hit, -3.0, work)
    sel = jnp.where(jcur < n_top, jnp.where(jrow < n_top, 1.0, 0.0), sel)
    selneg = jnp.concatenate([jnp.where(sel > 0.5, 0.0, NEG_BIG)] * NSA_HPG, axis=1)
    qts = jnp.concatenate([qt, selneg.astype(BF16)], axis=0)
    sa_ref[...] = scores(0, qts)

    o_wu = []
    for u in range(n_sub):
        tiles = []
        for r in range(WIN_SUB):
            row = (u * WIN_SUB + r) * WIN_KB
            slab = pl.multiple_of(jnp.where(kvalid[u + r], r, WIN_SUB) * WIN_KB, WIN_KB)
            tiles.append(sw_ref[row:row + WIN_KB, :] + tabw_ref[0, 0, pl.ds(slab, WIN_KB), :])
        sw = jnp.concatenate(tiles, axis=0)
        ew = jnp.exp2(sw - jnp.max(sw, axis=0, keepdims=True)).astype(BF16)
        o_wu.append(jnp.dot(jnp.concatenate(vts[u:u + WIN_SUB], axis=1), ew, preferred_element_type=F32))
    o_w = jnp.concatenate([o_wu[u][:, h * WIN_KB:(h + 1) * WIN_KB]
                           for h in range(NSA_HPG) for u in range(n_sub)], axis=1)

    m_ref[...] = jnp.full_like(m_ref, NEG_BIG)
    acc_ref[...] = jnp.zeros_like(acc_ref)

    def consume(s_ref, c, heads=range(NSA_HPG)):
        vt = vst_ref[0, 0, c]
        for h in heads:
            cs = hcols(h)
            sc = s_ref[:, cs]
            m_old = m_ref[:, cs]
            m_new = jnp.maximum(m_old, jnp.max(sc, axis=0, keepdims=True))
            alpha = jnp.exp2(m_old - m_new)
            pe = jnp.exp2(sc - m_new).astype(BF16)
            acc_ref[:, cs] = alpha * acc_ref[:, cs] + jnp.dot(vt, pe, preferred_element_type=F32)
            m_ref[:, cs] = m_new

    def fill_and_consume(dst_ref, c_next, src_ref, c):
        dst_ref[:, hcols(0)] = scores(c_next, qts, 0)
        for h in range(NSA_HPG):
            if h + 1 < NSA_HPG:
                dst_ref[:, hcols(h + 1)] = scores(c_next, qts, h + 1)
            consume(src_ref, c, [h])

    n_chunks = (qi * QB + QB - 1) // SEL_CHUNK + 1
    n_pairs = (n_chunks - 1) // 2

    def sel_pair(pr, carry):
        c = 2 * pr
        fill_and_consume(sb_ref, c + 1, sa_ref, c)
        fill_and_consume(sa_ref, c + 2, sb_ref, c + 1)
        return carry

    lax.fori_loop(0, n_pairs, sel_pair, 0)

    @pl.when(n_chunks % 2 == 0)
    def _():
        fill_and_consume(sb_ref, n_chunks - 1, sa_ref, n_chunks - 2)
        consume(sb_ref, n_chunks - 1)

    @pl.when(n_chunks % 2 == 1)
    def _():
        consume(sa_ref, n_chunks - 1)

    acc = acc_ref[...]

    gt = jax.nn.sigmoid(gate_ref[0].astype(F32)).T
    o_s = acc[0:DH] / acc[DH:DH + 1]
    o_w = o_w[0:DH] / o_w[DH:DH + 1]
    heads = []
    for h in range(NSA_HPG):
        cs = slice(h * QB, (h + 1) * QB)
        heads.append(gt[3 * h:3 * h + 1] * o_c[:, cs] + gt[3 * h + 1:3 * h + 2] * o_s[:, cs]
                     + gt[3 * h + 2:3 * h + 3] * o_w[:, cs])
    o_ref[0] = jnp.concatenate(heads, axis=0).T.astype(BF16)


def _nsa_attn(proj3, prep, tabc, tabw, tabs, ovl, dsat):
    B, S, _ = proj3.shape
    ksa, vst, kwa, vwt, kca, vct, qt = prep
    nq = S // Q_BLOCK
    nc = S // CMP_STRIDE
    nsel = S // SEL_BLOCK
    R = NSA_HPG * Q_BLOCK
    gw = NSA_HPG * NSA_DH
    per_bg = lambda a: pl.BlockSpec((1, 1) + a.shape[2:], lambda b, g, i: (b, g) + (0,) * (a.ndim - 2))
    return pl.pallas_call(
        functools.partial(_attn_kernel, nsel=nsel, dsat=dsat),
        grid=(B, NSA_KV, nq),
        in_specs=[
            pl.BlockSpec((1, 1, 1, NSA_DH, R), lambda b, g, i: (b, g, i, 0, 0)),
            pl.BlockSpec((1, Q_BLOCK, LANES), lambda b, g, i: (b, i, P_GATE // LANES + g)),
            per_bg(kca), per_bg(vct), per_bg(ksa), per_bg(vst), per_bg(kwa), per_bg(vwt),
            pl.BlockSpec((1, 1, nc, R), lambda b, g, i: (g, i, 0, 0)),
            pl.BlockSpec((1, 1) + tabw.shape[2:], lambda b, g, i: (g, 0, 0, 0), pipeline_mode=pl.Buffered(1)),
            pl.BlockSpec((1, 1) + tabs.shape[2:], lambda b, g, i: (g, 0, 0, 0), pipeline_mode=pl.Buffered(1)),
            pl.BlockSpec(ovl.shape, lambda b, g, i: (0, 0)),
        ],
        out_specs=pl.BlockSpec((1, Q_BLOCK, gw), lambda b, g, i: (b, i, g)),
        out_shape=jax.ShapeDtypeStruct((B, S, NSA_Q), BF16),
        scratch_shapes=[pltpu.VMEM((1, R), F32), pltpu.VMEM((V_ROWS, R), F32),
                        pltpu.VMEM((SEL_CHUNK, R), F32), pltpu.VMEM((SEL_CHUNK, R), F32),
                        pltpu.VMEM((Q_BLOCK // WIN_KB * WIN_SUB * WIN_KB, NSA_HPG * WIN_KB), F32)],
        compiler_params=_cparams(("parallel", "parallel", "arbitrary")),
        name="nsa_attn",
    )(qt, proj3, kca, vct, ksa, vst, kwa, vwt, tabc, tabw, tabs, ovl)


def _rel_bucket(d):
    max_exact = REL_BUCKETS // 2
    d = jnp.maximum(d, 0)
    df = jnp.maximum(d, 1).astype(F32)
    large = max_exact + (jnp.log(df / max_exact) / math.log(REL_MAX_DIST / max_exact)
                         * (REL_BUCKETS - max_exact)).astype(jnp.int32)
    return jnp.where(d < max_exact, d, jnp.minimum(large, REL_BUCKETS - 1))


def _bias_kernel(rb_ref, bkt_ref, o_ref, *, rows, cols):
    g = pl.program_id(0)
    step = 32
    for r0 in range(0, rows, step):
        bkt = bkt_ref[0, r0:r0 + step, :]
        acc = [jnp.full(bkt.shape, NEG_BIG, F32) for _ in range(NSA_HPG)]
        for b in range(REL_BUCKETS):
            hit = bkt == b
            for h in range(NSA_HPG):
                acc[h] = jnp.where(hit, rb_ref[b, g * NSA_HPG + h] * LOG2E, acc[h])
        for h in range(NSA_HPG):
            o_ref[0, 0, r0:r0 + step, h * cols:(h + 1) * cols] = acc[h]


def _bias_table(rel_bias, bucket):
    nt, rows, cols = bucket.shape
    return pl.pallas_call(
        functools.partial(_bias_kernel, rows=rows, cols=cols),
        grid=(NSA_KV, nt),
        in_specs=[
            pl.BlockSpec(memory_space=pltpu.SMEM),
            pl.BlockSpec((1, rows, cols), lambda g, t: (t, 0, 0)),
        ],
        out_specs=pl.BlockSpec((1, 1, rows, NSA_HPG * cols), lambda g, t: (g, t, 0, 0)),
        out_shape=jax.ShapeDtypeStruct((NSA_KV, nt, rows, NSA_HPG * cols), F32),
        compiler_params=_cparams(("parallel", "parallel")),
        name="bias_table",
    )(rel_bias, bucket)


def _bias_tables(rel_bias, S, dsat, strip_len):
    nq = S // Q_BLOCK
    nc = S // CMP_STRIDE
    n_cmp = (S - CMP_BLOCK) // CMP_STRIDE + 1
    ar = lambda n: jnp.arange(n, dtype=jnp.int32)
    bucket = lambda d, valid: jnp.where(valid, _rel_bucket(d), -1)
    i = ar(Q_BLOCK)[None, None, :]
    c = ar(nc)[None, :, None]
    d_c = ar(nq)[:, None, None] * Q_BLOCK + i - (c * CMP_STRIDE + CMP_BLOCK - 1)
    tabc = _bias_table(rel_bias, bucket(d_c, (d_c >= 0) & (c < n_cmp)))
    jw = ar((WIN_SUB + 1) * WIN_KB)[None, :, None]
    d_w = ar(WIN_KB)[None, None, :] + WINDOW - jw
    tabw = _bias_table(rel_bias, bucket(d_w, (d_w >= 0) & (d_w < WINDOW) & (jw < WIN_SUB * WIN_KB)))
    d_s = i - ar(strip_len)[None, :, None] + dsat
    tabs = _bias_table(rel_bias, bucket(d_s, d_s >= 0))
    return tabc, tabw, tabs


def _overlap_matrix(S):
    nc = S // CMP_STRIDE
    nsel = S // SEL_BLOCK
    n_cmp = (S - CMP_BLOCK) // CMP_STRIDE + 1
    c = np.arange(nc)[None, :]
    sel_start = (np.arange(SEL_ROWS) * SEL_BLOCK)[:, None]
    ovl = ((c * CMP_STRIDE < sel_start + SEL_BLOCK) & (c * CMP_STRIDE + CMP_BLOCK - 1 >= sel_start)
           & (c < n_cmp) & (np.arange(SEL_ROWS)[:, None] < nsel))
    return jnp.asarray(ovl, BF16)


def _layer_params(l, lower_bounds, norm_mix, w_in, hg_out_gain, nsa_qk_gain, cmp_pos, cmp_w1, cmp_b1,
                  cmp_w2, w_branch_a, w_branch_b, w_out, norm_mlp, w_up, w_down):
    splits = (HG_K, HG_K, HG_V, HG_V, NSA_Q, NSA_KVW, NSA_KVW, NSA_KVW, NSA_KVW, NSA_KVW, NSA_KVW,
              3 * NSA_HEADS, D_MODEL, D_MODEL)
    offs = np.concatenate([[0], np.cumsum(splits)])
    col = lambda k: w_in[l][:, offs[k]:offs[k + 1]]
    gate_w = col(11)
    zpad = jnp.zeros((D_MODEL, LANES - 3 * NSA_HPG), F32)
    gate_cols = []
    for g in range(NSA_KV):
        gate_cols += [gate_w[:, g * 3 * NSA_HPG:(g + 1) * 3 * NSA_HPG], zpad]
    w_main = jnp.concatenate([col(12), col(13), col(0), col(2), col(3),
                              col(7), col(8), col(9), col(10), col(4)] + gate_cols, axis=1).astype(BF16)
    w_cmp = jnp.concatenate([col(1), col(5), col(6)], axis=1).astype(BF16)

    lb = lower_bounds[l]
    lbp = jnp.stack([jnp.log(lb), jnp.log1p(-lb), 1.0 - lb])

    w1 = cmp_w1[l].reshape(2, 2, CMP_STRIDE, NSA_DH, CMP_HID)
    w1c = jnp.zeros((2, CMP_STRIDE, NSA_KV, NSA_DH, NSA_KV, 2, CMP_HID), F32)
    for g in range(NSA_KV):
        w1c = w1c.at[:, :, g, :, g, :, :].set(w1.transpose(0, 2, 3, 1, 4))
    w1c = w1c.reshape(2, CMP_STRIDE * NSA_KVW, NSA_KV * 2 * CMP_HID).astype(BF16)
    pos8 = jnp.zeros((2, 8, CMP_BLOCK * NSA_DH), F32).at[:, 0, :].set(cmp_pos[l].reshape(2, -1))
    w2p = jnp.zeros((2, CMP_HID, LANES), F32)
    w2p = w2p.at[0, :, :NSA_DH].set(cmp_w2[l, 0]).at[1, :NSA_DH, :].set(cmp_w2[l, 1].T).astype(BF16)
    qk = nsa_qk_gain[l]
    z64 = jnp.zeros((NSA_DH,), F32)
    return dict(
        norm_mix=norm_mix[l][None, :], w_main=w_main, w_cmp=w_cmp, lbp=lbp,
        hg_gain=hg_out_gain[l][None, :],
        gsel=jnp.concatenate([qk[2], qk[2]])[None, :], gwin=jnp.concatenate([qk[3], qk[3]])[None, :],
        gcmp=jnp.concatenate([qk[1], z64])[None, :],
        qgain=jnp.broadcast_to((qk[0] * (NSA_DH ** -0.5 * LOG2E))[:, None], (NSA_DH, Q_BLOCK)),
        pos8=pos8, w1=cmp_w1[l].astype(BF16), w1c=w1c, b1=cmp_b1[l][:, None, :], w2p=w2p,
        wa=w_branch_a[l].astype(BF16), wb=w_branch_b[l].astype(BF16), wo=w_out[l].astype(BF16),
        norm_mlp=norm_mlp[l][None, :], wu=w_up[l].astype(BF16), wd=w_down[l].astype(BF16))


def kernel(x, rel_bias, hg_lb_logits, norm_mix, w_in, hg_out_gain, nsa_qk_gain, cmp_pos, cmp_w1, cmp_b1, cmp_w2, w_branch_a, w_branch_b, w_out, norm_mlp, w_up, w_down):
    B, S, D = x.shape
    T = B * S
    depth = w_in.shape[0]
    assert D == D_MODEL and S % SEL_CHUNK == 0 and S >= WINDOW and S // SEL_BLOCK <= SEL_ROWS
    assert S % Q_BLOCK == 0 and Q_BLOCK % WIN_KB == 0 and WINDOW % WIN_KB == 0
    assert FORCE_BONUS > NSA_HPG and SEL_TOPK >= 3
    lb_cum = jnp.cumsum(jax.nn.softmax(hg_lb_logits.astype(F32), axis=0), axis=0)
    lower_bounds = lb_cum - lb_cum[0:1]

    dsat, strip_len = _sel_table_geometry()
    tabc, tabw, tabs = _bias_tables(rel_bias, S, dsat, strip_len)
    ovl = _overlap_matrix(S)
    nc = S // CMP_STRIDE
    tm = min(1024, T)

    x2 = x.reshape(T, D)
    for l in range(depth):
        p = _layer_params(l, lower_bounds, norm_mix, w_in, hg_out_gain, nsa_qk_gain, cmp_pos, cmp_w1,
                          cmp_b1, cmp_w2, w_branch_a, w_branch_b, w_out, norm_mlp, w_up, w_down)
        proj, z, kc, vc = _rms_proj(x2, p["norm_mix"], p["w_main"], p["w_cmp"], tm, P_TILE)
        proj3 = proj.reshape(B, S, P_COLS)
        kc2 = kc.reshape(B, nc, CMP_STRIDE * NSA_KVW)
        vc2 = vc.reshape(B, nc, CMP_STRIDE * NSA_KVW)
        o_a = _hgrn(proj3, z.reshape(B, S, HG_K), p["lbp"], p["hg_gain"])
        prep = _nsa_prep(proj3, kc2, vc2, p["gsel"], p["gwin"], p["gcmp"], p["qgain"], p["pos8"], p["w1"],
                         p["w1c"], p["b1"], p["w2p"])
        o_b = _nsa_attn(proj3, prep, tabc, tabw, tabs, ovl, dsat)
        x2 = _merge(x2, o_a.reshape(T, HG_V), o_b.reshape(T, NSA_Q), proj, p["wa"], p["wb"], p["wo"],
                    min(1024, T))
        x2 = _mlp(x2, p["norm_mlp"], p["wu"], p["wd"], tm, 1024)
    return x2.reshape(B, S, D)
```
